```python
import jax, jax.numpy as jnp
from jax import lax
import numpy as np

D_MODEL = 1024
BATCH = 2
SEQ = 8192
DEPTH = 1

MLA_HEADS = 8
QK_NOPE = 64
QK_ROPE = 32
QK_HEAD = QK_NOPE + QK_ROPE
V_HEAD = 64
Q_LORA = 256
KV_LORA = 128
ROPE_THETA = 10000.0
Q_BLOCK = 128
CONV_GROUPS = 8
CONV_WIDTH = 512
CONV_K = 3
N_EXPERTS = 32
TOP_K = 4
D_FF = D_MODEL
SWIGLU_LIMIT = 7.0
SWIGLU_ALPHA = 1.702
EXPERT_BLOCK = 256
EPS = 1e-6
IN_SIZES = (Q_LORA, KV_LORA, QK_ROPE, CONV_WIDTH, CONV_WIDTH, CONV_WIDTH, D_MODEL, D_MODEL)
IN_COLS = sum(IN_SIZES)
IN_SPLITS = tuple(int(s) for s in np.cumsum(IN_SIZES)[:-1])

kernel_name = 'hybrid_mla_shortconv_moe_encoder'


def rms_norm(x, g):
    xf = x.astype(jnp.float32)
    y = xf * lax.rsqrt(jnp.mean(xf * xf, axis=-1, keepdims=True) + EPS)
    return (y * g.astype(jnp.float32)).astype(x.dtype)


def rope_tables(positions, dtype):
    inv_freq = ROPE_THETA ** (-jnp.arange(0, QK_ROPE, 2, dtype=jnp.float32) / QK_ROPE)
    ang = positions.astype(jnp.float32)[..., None] * inv_freq
    return jnp.cos(ang)[:, :, None, :].astype(dtype), jnp.sin(ang)[:, :, None, :].astype(dtype)


def apply_rope(t, cos, sin):
    t1, t2 = t[..., :QK_ROPE // 2], t[..., QK_ROPE // 2:]
    return jnp.concatenate([t1 * cos - t2 * sin, t1 * sin + t2 * cos], axis=-1)


def dense_attention(q, k, v):
    B, S, H, _ = q.shape
    nb = S // Q_BLOCK
    qb = q.reshape(B, nb, Q_BLOCK, H, QK_HEAD).transpose(1, 0, 3, 2, 4)
    kt = k.transpose(0, 2, 1, 3)
    vt = v.transpose(0, 2, 1, 3)
    scale = QK_HEAD ** -0.5

    def one_block(q_blk):
        s = jnp.einsum('bhqd,bhkd->bhqk', q_blk, kt, preferred_element_type=jnp.float32) * scale
        p = jax.nn.softmax(s, axis=-1)
        return jnp.einsum('bhqk,bhkd->bhqd', p.astype(vt.dtype), vt)

    o = lax.map(one_block, qb)
    return o.transpose(1, 0, 3, 2, 4).reshape(B, S, H * V_HEAD)


def mla_branch(c_q, c_kv, k_rope, cos, sin, q_a_g, kv_a_g, w_uq, w_ukv, q_g, k_g, w_o_mla):
    B, S, _ = c_q.shape
    q = (rms_norm(c_q, q_a_g) @ w_uq).reshape(B, S, MLA_HEADS, QK_HEAD)
    kv = (rms_norm(c_kv, kv_a_g) @ w_ukv).reshape(B, S, MLA_HEADS, QK_NOPE + V_HEAD)
    k_nope, v = kv[..., :QK_NOPE], kv[..., QK_NOPE:]
    k_r = jnp.broadcast_to(k_rope[:, :, None, :], (B, S, MLA_HEADS, QK_ROPE))
    k = jnp.concatenate([k_nope, k_r], axis=-1)
    q = rms_norm(q, q_g)
    k = rms_norm(k, k_g)
    q = jnp.concatenate([q[..., :QK_NOPE], apply_rope(q[..., QK_NOPE:], cos, sin)], axis=-1)
    k = jnp.concatenate([k[..., :QK_NOPE], apply_rope(k[..., QK_NOPE:], cos, sin)], axis=-1)
    return dense_attention(q, k, v) @ w_o_mla


def short_conv_branch(u, c_gate, b_gate, conv_w, w_o_conv):
    cu = c_gate * u
    p = jnp.pad(cu, ((0, 0), (1, 1), (0, 0)))
    y = conv_w[0] * p[:, :-2] + conv_w[1] * p[:, 1:-1] + conv_w[2] * p[:, 2:]
    return (b_gate * y) @ w_o_conv


def moe(h, router_w, router_b, w1, b1, w2, b2):
    B, S, D = h.shape
    n_tok = B * S
    hf = h.reshape(n_tok, D)
    logits = (hf @ router_w + router_b).astype(jnp.float32)
    top_v, top_i = lax.top_k(logits, TOP_K)
    top_w = jax.nn.softmax(top_v, axis=-1).astype(h.dtype)

    nk = n_tok * TOP_K
    e_flat = top_i.reshape(nk)
    w_flat = top_w.reshape(nk)
    tok_flat = jnp.arange(nk, dtype=jnp.int32) // TOP_K
    order = jnp.argsort(e_flat)
    e_sorted = e_flat[order]
    counts = jnp.bincount(e_flat, length=N_EXPERTS)
    group_start = jnp.cumsum(counts) - counts
    padded = (counts + EXPERT_BLOCK - 1) // EXPERT_BLOCK * EXPERT_BLOCK
    pad_end = jnp.cumsum(padded)
    pad_start = pad_end - padded
    rank = jnp.arange(nk, dtype=jnp.int32) - group_start[e_sorted]
    dest = pad_start[e_sorted] + rank
    n_blocks = (nk + N_EXPERTS * (EXPERT_BLOCK - 1) + EXPERT_BLOCK - 1) // EXPERT_BLOCK
    rows = n_blocks * EXPERT_BLOCK
    row_tok = jnp.zeros((rows,), jnp.int32).at[dest].set(tok_flat[order])
    row_w = jnp.zeros((rows,), h.dtype).at[dest].set(w_flat[order])
    block_exp = jnp.clip(jnp.searchsorted(pad_end, jnp.arange(n_blocks, dtype=jnp.int32) * EXPERT_BLOCK,
                                          side='right'), 0, N_EXPERTS - 1)
    xs = hf[row_tok].reshape(n_blocks, EXPERT_BLOCK, D)

    def expert_block(args):
        xb, e = args
        hm = xb @ w1[e] + b1[e]
        gate, up = hm[:, :D_FF], hm[:, D_FF:]
        gate = jnp.minimum(gate, SWIGLU_LIMIT)
        up = jnp.clip(up, -SWIGLU_LIMIT, SWIGLU_LIMIT)
        glu = gate * jax.nn.sigmoid(SWIGLU_ALPHA * gate)
        return ((up + 1.0) * glu) @ w2[e] + b2[e]

    ys = lax.map(expert_block, (xs, block_exp)).reshape(rows, D) * row_w[:, None]
    out = jax.ops.segment_sum(ys, row_tok, num_segments=n_tok)
    return out.reshape(B, S, D)


def setup_inputs(seed: int = 0) -> dict:
    key = jax.random.key(seed)
    ks = jax.random.split(key, 24)
    L = DEPTH

    def nrm(k, shape, fan_in):
        return jax.random.normal(k, shape, jnp.float32) * fan_in ** -0.5

    def gain(k, n):
        return 1.0 + 0.02 * jax.random.normal(k, (L, n), jnp.float32)

    x = jax.random.normal(ks[0], (BATCH, SEQ, D_MODEL), jnp.float32)
    offs = jax.random.randint(ks[1], (BATCH, 1), 0, 1024, dtype=jnp.int32)
    positions = jnp.arange(SEQ, dtype=jnp.int32)[None, :] + offs
    return {
        'x': x,
        'positions': positions,
        'norm1_g': gain(ks[2], D_MODEL),
        'w_in': nrm(ks[3], (L, D_MODEL, IN_COLS), D_MODEL),
        'q_a_norm_g': gain(ks[4], Q_LORA),
        'kv_a_norm_g': gain(ks[5], KV_LORA),
        'w_uq': nrm(ks[6], (L, Q_LORA, MLA_HEADS * QK_HEAD), Q_LORA),
        'w_ukv': nrm(ks[7], (L, KV_LORA, MLA_HEADS * (QK_NOPE + V_HEAD)), KV_LORA),
        'q_norm_g': gain(ks[8], QK_HEAD),
        'k_norm_g': gain(ks[9], QK_HEAD),
        'conv_w': nrm(ks[10], (L, CONV_K, CONV_WIDTH), CONV_K),
        'w_o_mla': nrm(ks[11], (L, MLA_HEADS * V_HEAD, D_MODEL), MLA_HEADS * V_HEAD),
        'w_o_conv': nrm(ks[12], (L, CONV_WIDTH, D_MODEL), CONV_WIDTH),
        'w_o': nrm(ks[13], (L, D_MODEL, D_MODEL), D_MODEL),
        'norm2_g': gain(ks[14], D_MODEL),
        'router_w': nrm(ks[15], (L, D_MODEL, N_EXPERTS), D_MODEL),
        'router_b': 0.01 * jax.random.normal(ks[16], (L, N_EXPERTS), jnp.float32),
        'expert_w1': nrm(ks[17], (L, N_EXPERTS, D_MODEL, 2 * D_FF), D_MODEL),
        'expert_b1': 0.01 * jax.random.normal(ks[18], (L, N_EXPERTS, 2 * D_FF), jnp.float32),
        'expert_w2': nrm(ks[19], (L, N_EXPERTS, D_FF, D_MODEL), D_FF),
        'expert_b2': 0.01 * jax.random.normal(ks[20], (L, N_EXPERTS, D_MODEL), jnp.float32),
    }


def reference(x, positions, norm1_g, w_in, q_a_norm_g, kv_a_norm_g, w_uq, w_ukv, q_norm_g, k_norm_g,
              conv_w, w_o_mla, w_o_conv, w_o, norm2_g, router_w, router_b,
              expert_w1, expert_b1, expert_w2, expert_b2):
    cos, sin = rope_tables(positions, x.dtype)
    for l in range(DEPTH):
        h = rms_norm(x, norm1_g[l])
        proj = h @ w_in[l]
        c_q, c_kv, k_rope, u, c_gate, b_gate, g_a, g_b = jnp.split(proj, IN_SPLITS, axis=-1)
        y_a = mla_branch(c_q, c_kv, k_rope, cos, sin, q_a_norm_g[l], kv_a_norm_g[l], w_uq[l], w_ukv[l],
                         q_norm_g[l], k_norm_g[l], w_o_mla[l])
        y_b = short_conv_branch(u, c_gate, b_gate, conv_w[l], w_o_conv[l])
        merged = jax.nn.sigmoid(g_a) * y_a + jax.nn.sigmoid(g_b) * y_b
        x = x + merged @ w_o[l]
        h2 = rms_norm(x, norm2_g[l])
        x = x + moe(h2, router_w[l], router_b[l], expert_w1[l], expert_b1[l], expert_w2[l], expert_b2[l])
    return x
```

```python
import functools
import math

import jax
import jax.numpy as jnp
from jax import lax
from jax.experimental import pallas as pl
from jax.experimental.pallas import tpu as pltpu

F32 = jnp.float32
BF16 = jnp.bfloat16

MLA_HEADS = 8
QK_NOPE = 64
QK_ROPE = 32
QK_HEAD = QK_NOPE + QK_ROPE
V_HEAD = 64
Q_LORA = 256
KV_LORA = 128
ROPE_THETA = 10000.0
CONV_WIDTH = 512
N_EXPERTS = 32
TOP_K = 4
SWIGLU_LIMIT = 7.0
SWIGLU_ALPHA = 1.702
EPS = 1e-6

LANE = 128
HEAD_SLOT = LANE
HALF_ROPE = QK_ROPE // 2
V_ROWS = V_HEAD + 16
VMEM_LIMIT = 56 * 1024 * 1024

TM_PROJ = 512
TQ = 512
TKV = 512
TM_MIX = 512
TM_ROUTE = 256
EXPERT_BLOCK = 256
NEG_BIG = -1e30


def _rms(x, g):
    return x * lax.rsqrt(jnp.mean(x * x, axis=-1, keepdims=True) + EPS) * g


def _rope(t, cos_t, sin_p, sin_m):
    return t * cos_t + pltpu.roll(t, HALF_ROPE, 1) * sin_p + pltpu.roll(t, LANE - HALF_ROPE, 1) * sin_m


def _inproj_kernel(x_ref, g1_ref, w_ref, gq_ref, gkv_ref, wuq_ref, wukv_ref, qg_ref, kg_ref,
                   cos_ref, sinp_ref, sinm_ref,
                   qt_ref, k_ref, vt_ref, cu_ref, bg_ref, *, q_scale):
    x = x_ref[...]
    h = _rms(x, g1_ref[...]).astype(BF16)
    proj = jnp.dot(h, w_ref[...], preferred_element_type=F32)
    c_q = proj[:, 0:Q_LORA]
    c_kv = proj[:, Q_LORA:Q_LORA + KV_LORA]
    kr = proj[:, Q_LORA + KV_LORA:Q_LORA + KV_LORA + LANE]
    o = Q_LORA + KV_LORA + LANE
    u = proj[:, o:o + CONV_WIDTH]
    c_gate = proj[:, o + CONV_WIDTH:o + 2 * CONV_WIDTH]
    b_gate = proj[:, o + 2 * CONV_WIDTH:o + 3 * CONV_WIDTH]
    cu_ref[...] = c_gate * u
    bg_ref[...] = b_gate

    cos_t = cos_ref[...]
    sin_p = sinp_ref[...]
    sin_m = sinm_ref[...]
    qg = qg_ref[...]
    kg = kg_ref[...]
    tm = x.shape[0]
    lane = lax.broadcasted_iota(jnp.int32, (tm, LANE), 1)

    q = jnp.dot(_rms(c_q, gq_ref[...]).astype(BF16), wuq_ref[...], preferred_element_type=F32)
    kv = jnp.dot(_rms(c_kv, gkv_ref[...]).astype(BF16), wukv_ref[...], preferred_element_type=F32)

    ss_r = jnp.sum(kr * kr, axis=-1, keepdims=True)
    kr_roped = _rope(kr * kg, cos_t, sin_p, sin_m)
    ones = jnp.ones((V_ROWS - V_HEAD, tm), BF16)
    for hd in range(MLA_HEADS):
        qh = q[:, hd * HEAD_SLOT:(hd + 1) * HEAD_SLOT]
        r = lax.rsqrt(jnp.sum(qh * qh, axis=-1, keepdims=True) * (1.0 / QK_HEAD) + EPS)
        qn = _rope(qh * r * qg, cos_t, sin_p, sin_m) * q_scale
        qt_ref[0, hd] = qn.T.astype(BF16)

        kvh = kv[:, hd * HEAD_SLOT:(hd + 1) * HEAD_SLOT]
        knope = jnp.where(lane < QK_NOPE, kvh, 0.0)
        rk = lax.rsqrt((jnp.sum(knope * knope, axis=-1, keepdims=True) + ss_r) * (1.0 / QK_HEAD) + EPS)
        k_ref[0, hd] = ((knope * kg + kr_roped) * rk).astype(BF16)
        kvt = kvh.T
        vt_ref[0, hd, 0:V_HEAD, :] = kvt[QK_NOPE:QK_NOPE + V_HEAD].astype(BF16)
        vt_ref[0, hd, V_HEAD:V_ROWS, :] = ones


def _in_projection(x2, g1, w_a, gq, gkv, wuq, wukv, qg, kg, cos_t, sin_p, sin_m, batch, seq):
    n, d = x2.shape
    tm = TM_PROJ
    spb = seq // tm
    q_scale = (QK_HEAD ** -0.5) * math.log2(math.e)
    full = lambda shp: pl.BlockSpec(shp, lambda i: (0,) * len(shp))
    return pl.pallas_call(
        functools.partial(_inproj_kernel, q_scale=q_scale),
        grid=(n // tm,),
        in_specs=[
            pl.BlockSpec((tm, d), lambda i: (i, 0)),
            full(g1.shape), full(w_a.shape), full(gq.shape), full(gkv.shape), full(wuq.shape), full(wukv.shape),
            full(qg.shape), full(kg.shape),
            pl.BlockSpec((tm, LANE), lambda i: (i, 0)),
            pl.BlockSpec((tm, LANE), lambda i: (i, 0)),
            pl.BlockSpec((tm, LANE), lambda i: (i, 0)),
        ],
        out_specs=[
            pl.BlockSpec((1, MLA_HEADS, HEAD_SLOT, tm), lambda i: (i // spb, 0, 0, i % spb)),
            pl.BlockSpec((1, MLA_HEADS, tm, HEAD_SLOT), lambda i: (i // spb, 0, i % spb, 0)),
            pl.BlockSpec((1, MLA_HEADS, V_ROWS, tm), lambda i: (i // spb, 0, 0, i % spb)),
            pl.BlockSpec((tm, CONV_WIDTH), lambda i: (i, 0)),
            pl.BlockSpec((tm, CONV_WIDTH), lambda i: (i, 0)),
        ],
        out_shape=[
            jax.ShapeDtypeStruct((batch, MLA_HEADS, HEAD_SLOT, seq), BF16),
            jax.ShapeDtypeStruct((batch, MLA_HEADS, seq, HEAD_SLOT), BF16),
            jax.ShapeDtypeStruct((batch, MLA_HEADS, V_ROWS, seq), BF16),
            jax.ShapeDtypeStruct((n, CONV_WIDTH), F32),
            jax.ShapeDtypeStruct((n, CONV_WIDTH), F32),
        ],
        compiler_params=pltpu.CompilerParams(dimension_semantics=("parallel",), vmem_limit_bytes=VMEM_LIMIT),
        name="in_projection",
    )(x2, g1, w_a, gq, gkv, wuq, wukv, qg, kg, cos_t, sin_p, sin_m)


def _attn_kernel(qt_ref, k_ref, vt_ref, o_ref, *, tk):
    qt = qt_ref[0, 0]
    tq = qt.shape[1]
    nk = k_ref.shape[2] // tk

    def body(c, carry):
        m, acc = carry
        start = pl.multiple_of(c * tk, tk)
        ks = k_ref[0, 0, pl.ds(start, tk), :]
        s = jnp.dot(ks, qt, preferred_element_type=F32)
        m_new = jnp.maximum(m, jnp.max(s, axis=0, keepdims=True))
        alpha = jnp.exp2(m - m_new)
        p = jnp.exp2(s - m_new).astype(BF16)
        vs = vt_ref[0, 0, :, pl.ds(start, tk)]
        acc = alpha * acc + jnp.dot(vs, p, preferred_element_type=F32)
        return m_new, acc

    m0 = jnp.full((1, tq), NEG_BIG, F32)
    acc0 = jnp.zeros((V_ROWS, tq), F32)
    _, acc = lax.fori_loop(0, nk, body, (m0, acc0))
    o_ref[0] = (acc[0:V_HEAD] * (1.0 / acc[V_HEAD:V_HEAD + 1])).astype(BF16)


def _attention(qt, k, vt):
    batch, heads, _, seq = qt.shape
    return pl.pallas_call(
        functools.partial(_attn_kernel, tk=TKV),
        grid=(batch, heads, seq // TQ),
        in_specs=[
            pl.BlockSpec((1, 1, HEAD_SLOT, TQ), lambda b, h, i: (b, h, 0, i)),
            pl.BlockSpec((1, 1, seq, HEAD_SLOT), lambda b, h, i: (b, h, 0, 0)),
            pl.BlockSpec((1, 1, V_ROWS, seq), lambda b, h, i: (b, h, 0, 0)),
        ],
        out_specs=pl.BlockSpec((1, V_HEAD, TQ), lambda b, h, i: (b, h, i)),
        out_shape=jax.ShapeDtypeStruct((batch, heads * V_HEAD, seq), BF16),
        compiler_params=pltpu.CompilerParams(
            dimension_semantics=("parallel", "parallel", "parallel"), vmem_limit_bytes=VMEM_LIMIT),
        name="attention",
    )(qt, k, vt)


def _mix_kernel(x_ref, g1_ref, wg_ref, ot_ref, cu_ref, cup_ref, cun_ref, bg_ref, cw_ref,
                woa_ref, wob_ref, wo_ref, g2_ref, rwh_ref, rwl_ref, rb_ref,
                x1_ref, h2_ref, mi_ref, mw_ref, cnt_ref, tri_ref, carry_ref, *, steps_per_seq):
    i = pl.program_id(0)
    tm = x_ref.shape[0]

    @pl.when(i == 0)
    def _():
        r = lax.broadcasted_iota(jnp.int32, (tm, tm), 0)
        c = lax.broadcasted_iota(jnp.int32, (tm, tm), 1)
        tri_ref[...] = jnp.where(c < r, 1.0, 0.0).astype(BF16)
        carry_ref[...] = jnp.zeros_like(carry_ref)

    x = x_ref[...]
    h = _rms(x, g1_ref[...]).astype(BF16)
    gates = jnp.dot(h, wg_ref[...], preferred_element_type=F32)
    d = x.shape[1]
    sig_a = 1.0 / (1.0 + jnp.exp(-gates[:, 0:d]))
    sig_b = 1.0 / (1.0 + jnp.exp(-gates[:, d:2 * d]))

    y_a = lax.dot_general(ot_ref[0], woa_ref[...], (((0,), (0,)), ((), ())), preferred_element_type=F32)

    cu = cu_ref[...]
    row = lax.broadcasted_iota(jnp.int32, cu.shape, 0)
    s_in_seq = i % steps_per_seq
    prev_row = jnp.where(s_in_seq == 0, 0.0, cup_ref[7:8, :])
    next_row = jnp.where(s_in_seq == steps_per_seq - 1, 0.0, cun_ref[0:1, :])
    below = jnp.where(row == 0, prev_row, pltpu.roll(cu, 1, 0))
    above = jnp.where(row == tm - 1, next_row, pltpu.roll(cu, tm - 1, 0))
    cw = cw_ref[...]
    conv = cw[0:1, :] * below + cw[1:2, :] * cu + cw[2:3, :] * above
    y_b = jnp.dot((bg_ref[...] * conv).astype(BF16), wob_ref[...], preferred_element_type=F32)

    merged = (sig_a * y_a + sig_b * y_b).astype(BF16)
    x1 = x + jnp.dot(merged, wo_ref[...], preferred_element_type=F32)
    x1_ref[...] = x1
    h2 = _rms(x1, g2_ref[...])
    h2_ref[...] = h2

    h2_hi = h2.astype(BF16)
    h2_lo = (h2 - h2_hi.astype(F32)).astype(BF16)
    logits = (jnp.dot(h2_hi, rwh_ref[...], preferred_element_type=F32)
              + jnp.dot(h2_lo, rwh_ref[...], preferred_element_type=F32)
              + jnp.dot(h2_hi, rwl_ref[...], preferred_element_type=F32)
              + rb_ref[...])

    lane = lax.broadcasted_iota(jnp.int32, logits.shape, 1)
    lane_f = lane.astype(F32)
    work = logits
    vals, idxs, hits = [], [], []
    for _ in range(TOP_K):
        mx = jnp.max(work, axis=-1, keepdims=True)
        idx = jnp.min(jnp.where(work == mx, lane_f, float(LANE)), axis=-1, keepdims=True)
        hit = lane_f == idx
        work = jnp.where(hit, -jnp.inf, work)
        vals.append(mx)
        idxs.append(idx)
        hits.append(hit)
    exps = [jnp.exp(v - vals[0]) for v in vals]
    denom = exps[0] + exps[1] + exps[2] + exps[3]
    inv = 1.0 / denom

    sel = jnp.zeros(logits.shape, F32)
    for hit in hits:
        sel = sel + jnp.where(hit, 1.0, 0.0)
    carry = carry_ref[...]
    before = jnp.dot(tri_ref[...], sel.astype(BF16), preferred_element_type=F32) + carry
    carry_new = carry + jnp.sum(sel, axis=0, keepdims=True)
    carry_ref[...] = carry_new
    cnt_ref[...] = jnp.broadcast_to(carry_new, cnt_ref.shape)

    mi = jnp.zeros(logits.shape, F32)
    mw = jnp.zeros(logits.shape, F32)
    for k in range(TOP_K):
        rank = jnp.sum(jnp.where(hits[k], before, 0.0), axis=-1, keepdims=True)
        mi = jnp.where(lane == k, idxs[k], mi)
        mi = jnp.where(lane == TOP_K + k, rank, mi)
        mw = jnp.where(lane == k, exps[k] * inv, mw)
    mi_ref[...] = mi.astype(jnp.int32)
    mw_ref[...] = mw


def _mix(x2, g1, w_g, ot, cu, bg, conv_w, woa, wob, wo, g2, rw_hi, rw_lo, rb, batch, seq):
    n, d = x2.shape
    tm = TM_MIX
    spb = seq // tm
    r8 = tm // 8
    nsteps = n // tm
    full = lambda shp: pl.BlockSpec(shp, lambda i: (0,) * len(shp))
    return pl.pallas_call(
        functools.partial(_mix_kernel, steps_per_seq=spb),
        grid=(nsteps,),
        in_specs=[
            pl.BlockSpec((tm, d), lambda i: (i, 0)),
            full(g1.shape), full(w_g.shape),
            pl.BlockSpec((1, MLA_HEADS * V_HEAD, tm), lambda i: (i // spb, 0, i % spb)),
            pl.BlockSpec((tm, CONV_WIDTH), lambda i: (i, 0)),
            pl.BlockSpec((8, CONV_WIDTH), lambda i: (jnp.maximum(i * r8 - 1, 0), 0)),
            pl.BlockSpec((8, CONV_WIDTH), lambda i: (jnp.minimum((i + 1) * r8, nsteps * r8 - 1), 0)),
            pl.BlockSpec((tm, CONV_WIDTH), lambda i: (i, 0)),
            full(conv_w.shape), full(woa.shape), full(wob.shape), full(wo.shape), full(g2.shape),
            full(rw_hi.shape), full(rw_lo.shape), full(rb.shape),
        ],
        out_specs=[
            pl.BlockSpec((tm, d), lambda i: (i, 0)),
            pl.BlockSpec((tm, d), lambda i: (i, 0)),
            pl.BlockSpec((tm, LANE), lambda i: (i, 0)),
            pl.BlockSpec((tm, LANE), lambda i: (i, 0)),
            pl.BlockSpec((8, LANE), lambda i: (0, 0)),
        ],
        out_shape=[
            jax.ShapeDtypeStruct((n, d), F32),
            jax.ShapeDtypeStruct((n, d), F32),
            jax.ShapeDtypeStruct((n, LANE), jnp.int32),
            jax.ShapeDtypeStruct((n, LANE), F32),
            jax.ShapeDtypeStruct((8, LANE), F32),
        ],
        scratch_shapes=[pltpu.VMEM((tm, tm), BF16), pltpu.VMEM((1, LANE), F32)],
        compiler_params=pltpu.CompilerParams(dimension_semantics=("arbitrary",), vmem_limit_bytes=VMEM_LIMIT),
        name="mix_route",
    )(x2, g1, w_g, ot, cu, cu, cu, bg, conv_w, woa, wob, wo, g2, rw_hi, rw_lo, rb)


def _row_copy_wait(src_like, dst_like, sem, times):
    for _ in range(times):
        pltpu.make_async_copy(src_like, dst_like, sem).wait()


def _dispatch_kernel(e_ref, rank_ref, ps_ref, pe_ref, nused_ref, h2_ref, xs_ref, zero_ref, sem, zsem):
    tm = h2_ref.shape[0]
    n_blocks = xs_ref.shape[0] // EXPERT_BLOCK

    @pl.when(pl.program_id(0) == 0)
    def _():
        zero_ref[...] = jnp.zeros_like(zero_ref)

        def zero_copy(start):
            start = pl.multiple_of(start, EXPERT_BLOCK)
            return pltpu.make_async_copy(zero_ref, xs_ref.at[pl.ds(start, EXPERT_BLOCK)], zsem)

        def has_rows(e):
            return pe_ref[e] > jnp.where(e == 0, 0, pe_ref[jnp.maximum(e - 1, 0)])

        def start_expert(e, c):
            @pl.when(has_rows(e))
            def _():
                zero_copy(pe_ref[e] - EXPERT_BLOCK).start()
            return c

        def wait_expert(e, c):
            @pl.when(has_rows(e))
            def _():
                zero_copy(pe_ref[e] - EXPERT_BLOCK).wait()
            return c

        def start_tail(b, c):
            zero_copy(b * EXPERT_BLOCK).start()
            return c

        def wait_tail(b, c):
            zero_copy(b * EXPERT_BLOCK).wait()
            return c

        lax.fori_loop(0, N_EXPERTS, start_expert, 0)
        lax.fori_loop(nused_ref[0], n_blocks, start_tail, 0)
        lax.fori_loop(0, N_EXPERTS, wait_expert, 0)
        lax.fori_loop(nused_ref[0], n_blocks, wait_tail, 0)

    def issue(j, c):
        t = lax.shift_right_logical(j, 2)
        dest = ps_ref[e_ref[j]] + rank_ref[j]
        pltpu.make_async_copy(h2_ref.at[pl.ds(t, 1)], xs_ref.at[pl.ds(dest, 1)], sem).start()
        return c

    lax.fori_loop(0, tm * TOP_K, issue, 0)
    _row_copy_wait(h2_ref, xs_ref.at[pl.ds(0, tm)], sem, TOP_K)


def _dispatch(e_flat, rank_flat, pad_start, pad_end, n_used, h2, rows):
    n, d = h2.shape
    tm = TM_ROUTE
    return pl.pallas_call(
        _dispatch_kernel,
        grid=(n // tm,),
        in_specs=[
            pl.BlockSpec((tm * TOP_K,), lambda i: (i,), memory_space=pltpu.SMEM),
            pl.BlockSpec((tm * TOP_K,), lambda i: (i,), memory_space=pltpu.SMEM),
            pl.BlockSpec(memory_space=pltpu.SMEM),
            pl.BlockSpec(memory_space=pltpu.SMEM),
            pl.BlockSpec(memory_space=pltpu.SMEM),
            pl.BlockSpec((tm, d), lambda i: (i, 0)),
        ],
        out_specs=pl.BlockSpec(memory_space=pl.ANY),
        out_shape=jax.ShapeDtypeStruct((rows, d), F32),
        scratch_shapes=[pltpu.VMEM((EXPERT_BLOCK, d), F32), pltpu.SemaphoreType.DMA(()),
                        pltpu.SemaphoreType.DMA(())],
        compiler_params=pltpu.CompilerParams(dimension_semantics=("arbitrary",), vmem_limit_bytes=VMEM_LIMIT),
        name="dispatch",
    )(e_flat, rank_flat, pad_start, pad_end, n_used, h2)


def _expert_kernel(bexp_ref, nused_ref, xs_ref, w1_ref, b1_ref, w2_ref, b2_ref, ys_ref, w1b_ref, w2b_ref):
    i = pl.program_id(0)
    active = i < nused_ref[0]
    prev = bexp_ref[jnp.maximum(i - 1, 0)]
    fresh = jnp.logical_or(i == 0, bexp_ref[i] != prev)

    @pl.when(jnp.logical_and(active, fresh))
    def _():
        w1b_ref[...] = w1_ref[0].astype(BF16)
        w2b_ref[...] = w2_ref[0].astype(BF16)

    @pl.when(active)
    def _():
        dff = w2b_ref.shape[0]
        xb = xs_ref[...].astype(BF16)
        hm = jnp.dot(xb, w1b_ref[...], preferred_element_type=F32) + b1_ref[0]
        gate = jnp.minimum(hm[:, 0:dff], SWIGLU_LIMIT)
        up = jnp.clip(hm[:, dff:2 * dff], -SWIGLU_LIMIT, SWIGLU_LIMIT)
        glu = gate * (1.0 / (1.0 + jnp.exp(-SWIGLU_ALPHA * gate)))
        act = ((up + 1.0) * glu).astype(BF16)
        ys_ref[...] = jnp.dot(act, w2b_ref[...], preferred_element_type=F32) + b2_ref[0]

    @pl.when(jnp.logical_not(active))
    def _():
        ys_ref[...] = jnp.zeros_like(ys_ref)


def _experts(block_exp, n_used, xs, w1, b1, w2, b2):
    rows, d = xs.shape
    n_blocks = rows // EXPERT_BLOCK
    dff2 = w1.shape[2]
    dff = w2.shape[1]
    grid_spec = pltpu.PrefetchScalarGridSpec(
        num_scalar_prefetch=2,
        grid=(n_blocks,),
        in_specs=[
            pl.BlockSpec((EXPERT_BLOCK, d), lambda i, be, nu: (jnp.minimum(i, nu[0] - 1), 0)),
            pl.BlockSpec((1, d, dff2), lambda i, be, nu: (be[i], 0, 0)),
            pl.BlockSpec((1, 1, dff2), lambda i, be, nu: (be[i], 0, 0)),
            pl.BlockSpec((1, dff, d), lambda i, be, nu: (be[i], 0, 0)),
            pl.BlockSpec((1, 1, d), lambda i, be, nu: (be[i], 0, 0)),
        ],
        out_specs=pl.BlockSpec((EXPERT_BLOCK, d), lambda i, be, nu: (i, 0)),
        scratch_shapes=[pltpu.VMEM((d, dff2), BF16), pltpu.VMEM((dff, d), BF16)],
    )
    return pl.pallas_call(
        _expert_kernel,
        grid_spec=grid_spec,
        out_shape=jax.ShapeDtypeStruct((rows, d), F32),
        compiler_params=pltpu.CompilerParams(dimension_semantics=("arbitrary",), vmem_limit_bytes=VMEM_LIMIT),
        name="experts",
    )(block_exp, n_used, xs, w1, b1, w2, b2)


def _combine_kernel(e_ref, rank_ref, ps_ref, x1_ref, mw_ref, ys_ref, out_ref, buf_ref, sem):
    tm = x1_ref.shape[0]

    def issue(j, c):
        t = lax.shift_right_logical(j, 2)
        k = lax.bitwise_and(j, TOP_K - 1)
        dest = ps_ref[e_ref[j]] + rank_ref[j]
        pltpu.make_async_copy(ys_ref.at[pl.ds(dest, 1)], buf_ref.at[k, pl.ds(t, 1)], sem).start()
        return c

    lax.fori_loop(0, tm * TOP_K, issue, 0)
    _row_copy_wait(ys_ref.at[pl.ds(0, tm)], buf_ref.at[0], sem, TOP_K)
    acc = x1_ref[...]
    mw = mw_ref[...]
    for k in range(TOP_K):
        acc = acc + mw[:, k:k + 1] * buf_ref[k]
    out_ref[...] = acc


def _combine(e_flat, rank_flat, pad_start, x1, mw, ys):
    n, d = x1.shape
    tm = TM_ROUTE
    return pl.pallas_call(
        _combine_kernel,
        grid=(n // tm,),
        in_specs=[
            pl.BlockSpec((tm * TOP_K,), lambda i: (i,), memory_space=pltpu.SMEM),
            pl.BlockSpec((tm * TOP_K,), lambda i: (i,), memory_space=pltpu.SMEM),
            pl.BlockSpec(memory_space=pltpu.SMEM),
            pl.BlockSpec((tm, d), lambda i: (i, 0)),
            pl.BlockSpec((tm, LANE), lambda i: (i, 0)),
            pl.BlockSpec(memory_space=pl.ANY),
        ],
        out_specs=pl.BlockSpec((tm, d), lambda i: (i, 0)),
        out_shape=jax.ShapeDtypeStruct((n, d), F32),
        scratch_shapes=[pltpu.VMEM((TOP_K, tm, d), F32), pltpu.SemaphoreType.DMA(())],
        compiler_params=pltpu.CompilerParams(dimension_semantics=("arbitrary",), vmem_limit_bytes=VMEM_LIMIT),
        name="combine",
    )(e_flat, rank_flat, pad_start, x1, mw, ys)


def _pad_cols(w, width):
    return jnp.pad(w, ((0, 0), (0, width - w.shape[1])))


def _head_slots(w, per_head):
    rows = w.shape[0]
    w3 = w.reshape(rows, MLA_HEADS, per_head)
    return jnp.pad(w3, ((0, 0), (0, 0), (0, HEAD_SLOT - per_head))).reshape(rows, MLA_HEADS * HEAD_SLOT)


def _rope_tables(positions):
    inv_freq = ROPE_THETA ** (-jnp.arange(0, QK_ROPE, 2, dtype=F32) / QK_ROPE)
    ang = positions.astype(F32).reshape(-1, 1) * inv_freq
    cos, sin = jnp.cos(ang), jnp.sin(ang)
    n = ang.shape[0]
    z = lambda w: jnp.zeros((n, w), F32)
    cos_t = jnp.concatenate([jnp.ones((n, QK_NOPE), F32), cos, cos, z(LANE - QK_HEAD)], axis=1)
    sin_p = jnp.concatenate([z(QK_NOPE + HALF_ROPE), sin, z(LANE - QK_HEAD)], axis=1)
    sin_m = jnp.concatenate([z(QK_NOPE), -sin, z(LANE - QK_NOPE - HALF_ROPE)], axis=1)
    return cos_t, sin_p, sin_m


def _layer(x2, positions, norm1_g, w_in, q_a_norm_g, kv_a_norm_g, w_uq, w_ukv, q_norm_g, k_norm_g,
           conv_w, w_o_mla, w_o_conv, w_o, norm2_g, router_w, router_b,
           expert_w1, expert_b1, expert_w2, expert_b2, batch, seq):
    n, d = x2.shape
    o_kr = Q_LORA + KV_LORA
    o_u = o_kr + QK_ROPE
    o_g = o_u + 3 * CONV_WIDTH
    kr_cols = jnp.pad(w_in[:, o_kr:o_u], ((0, 0), (QK_NOPE, LANE - QK_HEAD)))
    w_a = jnp.concatenate([w_in[:, :o_kr], kr_cols, w_in[:, o_u:o_g]], axis=1).astype(BF16)
    w_g = w_in[:, o_g:].astype(BF16)
    row = lambda v: v.reshape(1, -1)
    cos_t, sin_p, sin_m = _rope_tables(positions)

    qt, k, vt, cu, bg = _in_projection(
        x2, row(norm1_g), w_a, row(q_a_norm_g), row(kv_a_norm_g),
        _head_slots(w_uq, QK_HEAD).astype(BF16), w_ukv.astype(BF16),
        _pad_cols(row(q_norm_g), LANE), _pad_cols(row(k_norm_g), LANE),
        cos_t, sin_p, sin_m, batch, seq)
    ot = _attention(qt, k, vt)

    rw = _pad_cols(router_w, LANE)
    rw_hi = rw.astype(BF16)
    rw_lo = (rw - rw_hi.astype(F32)).astype(BF16)
    rb = jnp.concatenate([row(router_b), jnp.full((1, LANE - N_EXPERTS), NEG_BIG, F32)], axis=1)
    x1, h2, mi, mw, cnt = _mix(
        x2, row(norm1_g), w_g, ot, cu, bg, conv_w, w_o_mla.astype(BF16), w_o_conv.astype(BF16),
        w_o.astype(BF16), row(norm2_g), rw_hi, rw_lo, rb, batch, seq)

    counts = cnt[0, :N_EXPERTS].astype(jnp.int32)
    padded = (counts + EXPERT_BLOCK - 1) // EXPERT_BLOCK * EXPERT_BLOCK
    pad_end = jnp.cumsum(padded)
    pad_start = (pad_end - padded).astype(jnp.int32)
    nk = n * TOP_K
    n_blocks = (nk + N_EXPERTS * (EXPERT_BLOCK - 1) + EXPERT_BLOCK - 1) // EXPERT_BLOCK
    rows = n_blocks * EXPERT_BLOCK
    block_exp = jnp.clip(jnp.searchsorted(pad_end, jnp.arange(n_blocks, dtype=jnp.int32) * EXPERT_BLOCK,
                                          side='right'), 0, N_EXPERTS - 1).astype(jnp.int32)
    n_used = (pad_end[-1:] // EXPERT_BLOCK).astype(jnp.int32)
    e_flat = mi[:, 0:TOP_K].reshape(nk)
    rank_flat = mi[:, TOP_K:2 * TOP_K].reshape(nk)

    xs = _dispatch(e_flat, rank_flat, pad_start, pad_end.astype(jnp.int32), n_used, h2, rows)
    ys = _experts(block_exp, n_used, xs, expert_w1, expert_b1.reshape(N_EXPERTS, 1, -1),
                  expert_w2, expert_b2.reshape(N_EXPERTS, 1, -1))
    return _combine(e_flat, rank_flat, pad_start, x1, mw, ys)


def kernel(x, positions, norm1_g, w_in, q_a_norm_g, kv_a_norm_g, w_uq, w_ukv, q_norm_g, k_norm_g, conv_w,
           w_o_mla, w_o_conv, w_o, norm2_g, router_w, router_b, expert_w1, expert_b1, expert_w2, expert_b2):
    batch, seq, d = x.shape
    depth = norm1_g.shape[0]
    x2 = x.reshape(batch * seq, d)
    for l in range(depth):
        x2 = _layer(x2, positions, norm1_g[l], w_in[l], q_a_norm_g[l], kv_a_norm_g[l], w_uq[l], w_ukv[l],
                    q_norm_g[l], k_norm_g[l], conv_w[l], w_o_mla[l], w_o_conv[l], w_o[l], norm2_g[l],
                    router_w[l], router_b[l], expert_w1[l], expert_b1[l], expert_w2[l], expert_b2[l], batch, seq)
    return x2.reshape(batch, seq, d)
```

```python
import functools
import math

import jax
import jax.numpy as jnp
from jax import lax
from jax.experimental import pallas as pl
from jax.experimental.pallas import tpu as pltpu

F32 = jnp.float32
BF16 = jnp.bfloat16

MLA_HEADS = 8
QK_NOPE = 64
QK_ROPE = 32
QK_HEAD = QK_NOPE + QK_ROPE
V_HEAD = 64
Q_LORA = 256
KV_LORA = 128
ROPE_THETA = 10000.0
CONV_WIDTH = 512
N_EXPERTS = 32
TOP_K = 4
SWIGLU_LIMIT = 7.0
SWIGLU_ALPHA = 1.702
EPS = 1e-6

LANE = 128
SUBLANE = 8
HEAD_SLOT = LANE
HALF_ROPE = QK_ROPE // 2
V_ROWS = V_HEAD + 16
VMEM_LIMIT = 56 * 1024 * 1024

TM_PROJ = 512
TQ = 512
TKV = 512
TM_MIX = 512
TM_ROUTE = 256
EXPERT_BLOCK = 256
ISSUE_GROUP = 16
NEG_BIG = -1e30


def _load_token_tiles(ref, lead, rows):
    return jnp.concatenate([ref[lead + (pl.ds(c, rows, stride=SUBLANE), slice(None))] for c in range(SUBLANE)],
                           axis=1)


def _store_token_tiles(ref, value):
    rows = value.shape[0]
    for c in range(SUBLANE):
        ref[pl.ds(c, rows, stride=SUBLANE), :] = value[:, c * LANE:(c + 1) * LANE]


def _rms(x, g):
    return x * lax.rsqrt(jnp.mean(x * x, axis=-1, keepdims=True) + EPS) * g


def _rope(t, cos_t, sin_p, sin_m):
    return t * cos_t + pltpu.roll(t, HALF_ROPE, 1) * sin_p + pltpu.roll(t, LANE - HALF_ROPE, 1) * sin_m


def _inproj_kernel(x_ref, g1_ref, w_ref, gq_ref, gkv_ref, wuq_ref, wukv_ref, qg_ref, kg_ref,
                   cos_ref, sinp_ref, sinm_ref,
                   qt_ref, k_ref, vt_ref, cu_ref, bg_ref, *, q_scale):
    x = x_ref[...]
    h = _rms(x, g1_ref[...]).astype(BF16)
    proj = jnp.dot(h, w_ref[...], preferred_element_type=F32)
    c_q = proj[:, 0:Q_LORA]
    c_kv = proj[:, Q_LORA:Q_LORA + KV_LORA]
    kr = proj[:, Q_LORA + KV_LORA:Q_LORA + KV_LORA + LANE]
    o = Q_LORA + KV_LORA + LANE
    u = proj[:, o:o + CONV_WIDTH]
    c_gate = proj[:, o + CONV_WIDTH:o + 2 * CONV_WIDTH]
    b_gate = proj[:, o + 2 * CONV_WIDTH:o + 3 * CONV_WIDTH]
    cu_ref[...] = c_gate * u
    bg_ref[...] = b_gate

    cos_t = cos_ref[...]
    sin_p = sinp_ref[...]
    sin_m = sinm_ref[...]
    qg = qg_ref[...]
    kg = kg_ref[...]
    tm = x.shape[0]
    lane = lax.broadcasted_iota(jnp.int32, (tm, LANE), 1)

    q = jnp.dot(_rms(c_q, gq_ref[...]).astype(BF16), wuq_ref[...], preferred_element_type=F32)
    kv = jnp.dot(_rms(c_kv, gkv_ref[...]).astype(BF16), wukv_ref[...], preferred_element_type=F32)

    ss_r = jnp.sum(kr * kr, axis=-1, keepdims=True)
    kr_roped = _rope(kr * kg, cos_t, sin_p, sin_m)
    ones = jnp.ones((V_ROWS - V_HEAD, tm), BF16)
    for hd in range(MLA_HEADS):
        qh = q[:, hd * HEAD_SLOT:(hd + 1) * HEAD_SLOT]
        r = lax.rsqrt(jnp.sum(qh * qh, axis=-1, keepdims=True) * (1.0 / QK_HEAD) + EPS)
        qn = _rope(qh * r * qg, cos_t, sin_p, sin_m) * q_scale
        qt_ref[0, hd] = qn.T.astype(BF16)

        kvh = kv[:, hd * HEAD_SLOT:(hd + 1) * HEAD_SLOT]
        knope = jnp.where(lane < QK_NOPE, kvh, 0.0)
        rk = lax.rsqrt((jnp.sum(knope * knope, axis=-1, keepdims=True) + ss_r) * (1.0 / QK_HEAD) + EPS)
        k_ref[0, hd] = ((knope * kg + kr_roped) * rk).astype(BF16)
        kvt = kvh.T
        vt_ref[0, hd, 0:V_HEAD, :] = kvt[QK_NOPE:QK_NOPE + V_HEAD].astype(BF16)
        vt_ref[0, hd, V_HEAD:V_ROWS, :] = ones


def _in_projection(x2, g1, w_a, gq, gkv, wuq, wukv, qg, kg, cos_t, sin_p, sin_m, batch, seq):
    n, d = x2.shape
    tm = TM_PROJ
    spb = seq // tm
    q_scale = (QK_HEAD ** -0.5) * math.log2(math.e)
    full = lambda shp: pl.BlockSpec(shp, lambda i: (0,) * len(shp))
    return pl.pallas_call(
        functools.partial(_inproj_kernel, q_scale=q_scale),
        grid=(n // tm,),
        in_specs=[
            pl.BlockSpec((tm, d), lambda i: (i, 0)),
            full(g1.shape), full(w_a.shape), full(gq.shape), full(gkv.shape), full(wuq.shape), full(wukv.shape),
            full(qg.shape), full(kg.shape),
            pl.BlockSpec((tm, LANE), lambda i: (i, 0)),
            pl.BlockSpec((tm, LANE), lambda i: (i, 0)),
            pl.BlockSpec((tm, LANE), lambda i: (i, 0)),
        ],
        out_specs=[
            pl.BlockSpec((1, MLA_HEADS, HEAD_SLOT, tm), lambda i: (i // spb, 0, 0, i % spb)),
            pl.BlockSpec((1, MLA_HEADS, tm, HEAD_SLOT), lambda i: (i // spb, 0, i % spb, 0)),
            pl.BlockSpec((1, MLA_HEADS, V_ROWS, tm), lambda i: (i // spb, 0, 0, i % spb)),
            pl.BlockSpec((tm, CONV_WIDTH), lambda i: (i, 0)),
            pl.BlockSpec((tm, CONV_WIDTH), lambda i: (i, 0)),
        ],
        out_shape=[
            jax.ShapeDtypeStruct((batch, MLA_HEADS, HEAD_SLOT, seq), BF16),
            jax.ShapeDtypeStruct((batch, MLA_HEADS, seq, HEAD_SLOT), BF16),
            jax.ShapeDtypeStruct((batch, MLA_HEADS, V_ROWS, seq), BF16),
            jax.ShapeDtypeStruct((n, CONV_WIDTH), F32),
            jax.ShapeDtypeStruct((n, CONV_WIDTH), F32),
        ],
        compiler_params=pltpu.CompilerParams(dimension_semantics=("parallel",), vmem_limit_bytes=VMEM_LIMIT),
        name="in_projection",
    )(x2, g1, w_a, gq, gkv, wuq, wukv, qg, kg, cos_t, sin_p, sin_m)


def _attn_kernel(qt_ref, k_ref, vt_ref, o_ref, sa_ref, sb_ref, m_ref, acc_ref, *, tk):
    qt = qt_ref[0, 0]
    nk = k_ref.shape[2] // tk

    def scores(c, s_ref):
        start = pl.multiple_of(c * tk, tk)
        s_ref[...] = jnp.dot(k_ref[0, 0, pl.ds(start, tk), :], qt, preferred_element_type=F32)

    def accumulate(c, s_ref):
        start = pl.multiple_of(c * tk, tk)
        s = s_ref[...]
        m = m_ref[...]
        m_new = jnp.maximum(m, jnp.max(s, axis=0, keepdims=True))
        m_ref[...] = m_new
        p = jnp.exp2(s - m_new).astype(BF16)
        vs = vt_ref[0, 0, :, pl.ds(start, tk)]
        acc_ref[...] = jnp.exp2(m - m_new) * acc_ref[...] + jnp.dot(vs, p, preferred_element_type=F32)

    m_ref[...] = jnp.full(m_ref.shape, NEG_BIG, F32)
    acc_ref[...] = jnp.zeros(acc_ref.shape, F32)
    scores(0, sa_ref)

    def body(j, carry):
        c = 2 * j
        scores(c + 1, sb_ref)
        accumulate(c, sa_ref)
        scores(c + 2, sa_ref)
        accumulate(c + 1, sb_ref)
        return carry

    lax.fori_loop(0, nk // 2 - 1, body, 0)
    scores(nk - 1, sb_ref)
    accumulate(nk - 2, sa_ref)
    accumulate(nk - 1, sb_ref)
    acc = acc_ref[...]
    o_ref[0] = (acc[0:V_HEAD] * (1.0 / acc[V_HEAD:V_HEAD + 1])).astype(BF16)


def _attention(qt, k, vt):
    batch, heads, _, seq = qt.shape
    assert (seq // TKV) % 2 == 0 and seq // TKV >= 2
    return pl.pallas_call(
        functools.partial(_attn_kernel, tk=TKV),
        grid=(batch, heads, seq // TQ),
        in_specs=[
            pl.BlockSpec((1, 1, HEAD_SLOT, TQ), lambda b, h, i: (b, h, 0, i)),
            pl.BlockSpec((1, 1, seq, HEAD_SLOT), lambda b, h, i: (b, h, 0, 0)),
            pl.BlockSpec((1, 1, V_ROWS, seq), lambda b, h, i: (b, h, 0, 0)),
        ],
        out_specs=pl.BlockSpec((1, V_HEAD, TQ), lambda b, h, i: (b, h, i)),
        out_shape=jax.ShapeDtypeStruct((batch, heads * V_HEAD, seq), BF16),
        scratch_shapes=[pltpu.VMEM((TKV, TQ), F32), pltpu.VMEM((TKV, TQ), F32),
                        pltpu.VMEM((1, TQ), F32), pltpu.VMEM((V_ROWS, TQ), F32)],
        compiler_params=pltpu.CompilerParams(
            dimension_semantics=("parallel", "parallel", "parallel"), vmem_limit_bytes=VMEM_LIMIT),
        name="attention",
    )(qt, k, vt)


def _mix_kernel(x_ref, g1_ref, wg_ref, ot_ref, cu_ref, cup_ref, cun_ref, bg_ref, cw_ref,
                woa_ref, wob_ref, wo_ref, g2_ref, rwh_ref, rwl_ref, rb_ref,
                x1_ref, h2_ref, mi_ref, mw_ref, cnt_ref, tri_ref, carry_ref, *, steps_per_seq):
    i = pl.program_id(0)
    tm = x_ref.shape[0]

    @pl.when(i == 0)
    def _():
        r = lax.broadcasted_iota(jnp.int32, (tm, tm), 0)
        c = lax.broadcasted_iota(jnp.int32, (tm, tm), 1)
        tri_ref[...] = jnp.where(c < r, 1.0, 0.0).astype(BF16)
        carry_ref[...] = jnp.zeros_like(carry_ref)

    x = x_ref[...]
    h = _rms(x, g1_ref[...]).astype(BF16)
    gates = jnp.dot(h, wg_ref[...], preferred_element_type=F32)
    d = x.shape[1]
    sig_a = 1.0 / (1.0 + jnp.exp(-gates[:, 0:d]))
    sig_b = 1.0 / (1.0 + jnp.exp(-gates[:, d:2 * d]))

    y_a = lax.dot_general(ot_ref[0], woa_ref[...], (((0,), (0,)), ((), ())), preferred_element_type=F32)

    cu = cu_ref[...]
    row = lax.broadcasted_iota(jnp.int32, cu.shape, 0)
    s_in_seq = i % steps_per_seq
    prev_row = jnp.where(s_in_seq == 0, 0.0, cup_ref[7:8, :])
    next_row = jnp.where(s_in_seq == steps_per_seq - 1, 0.0, cun_ref[0:1, :])
    below = jnp.where(row == 0, prev_row, pltpu.roll(cu, 1, 0))
    above = jnp.where(row == tm - 1, next_row, pltpu.roll(cu, tm - 1, 0))
    cw = cw_ref[...]
    conv = cw[0:1, :] * below + cw[1:2, :] * cu + cw[2:3, :] * above
    y_b = jnp.dot((bg_ref[...] * conv).astype(BF16), wob_ref[...], preferred_element_type=F32)

    merged = (sig_a * y_a + sig_b * y_b).astype(BF16)
    x1 = x + jnp.dot(merged, wo_ref[...], preferred_element_type=F32)
    x1_ref[...] = x1
    h2 = _rms(x1, g2_ref[...])
    _store_token_tiles(h2_ref, h2)

    h2_hi = h2.astype(BF16)
    h2_lo = (h2 - h2_hi.astype(F32)).astype(BF16)
    logits = (jnp.dot(h2_hi, rwh_ref[...], preferred_element_type=F32)
              + jnp.dot(h2_lo, rwh_ref[...], preferred_element_type=F32)
              + jnp.dot(h2_hi, rwl_ref[...], preferred_element_type=F32)
              + rb_ref[...])

    lane = lax.broadcasted_iota(jnp.int32, logits.shape, 1)
    lane_f = lane.astype(F32)
    work = logits
    vals, idxs, hits = [], [], []
    for _ in range(TOP_K):
        mx = jnp.max(work, axis=-1, keepdims=True)
        idx = jnp.min(jnp.where(work == mx, lane_f, float(LANE)), axis=-1, keepdims=True)
        hit = lane_f == idx
        work = jnp.where(hit, -jnp.inf, work)
        vals.append(mx)
        idxs.append(idx)
        hits.append(hit)
    exps = [jnp.exp(v - vals[0]) for v in vals]
    denom = exps[0] + exps[1] + exps[2] + exps[3]
    inv = 1.0 / denom

    sel = jnp.zeros(logits.shape, F32)
    for hit in hits:
        sel = sel + jnp.where(hit, 1.0, 0.0)
    carry = carry_ref[...]
    before = jnp.dot(tri_ref[...], sel.astype(BF16), preferred_element_type=F32) + carry
    carry_new = carry + jnp.sum(sel, axis=0, keepdims=True)
    carry_ref[...] = carry_new
    cnt_ref[...] = jnp.broadcast_to(carry_new, cnt_ref.shape)

    mi = jnp.zeros(logits.shape, F32)
    mw = jnp.zeros(logits.shape, F32)
    for k in range(TOP_K):
        rank = jnp.sum(jnp.where(hits[k], before, 0.0), axis=-1, keepdims=True)
        mi = jnp.where(lane == k, idxs[k], mi)
        mi = jnp.where(lane == TOP_K + k, rank, mi)
        mw = jnp.where(lane == k, exps[k] * inv, mw)
    mi_ref[...] = mi.astype(jnp.int32)
    mw_ref[...] = mw


def _mix(x2, g1, w_g, ot, cu, bg, conv_w, woa, wob, wo, g2, rw_hi, rw_lo, rb, batch, seq):
    n, d = x2.shape
    tm = TM_MIX
    spb = seq // tm
    r8 = tm // 8
    nsteps = n // tm
    full = lambda shp: pl.BlockSpec(shp, lambda i: (0,) * len(shp))
    return pl.pallas_call(
        functools.partial(_mix_kernel, steps_per_seq=spb),
        grid=(nsteps,),
        in_specs=[
            pl.BlockSpec((tm, d), lambda i: (i, 0)),
            full(g1.shape), full(w_g.shape),
            pl.BlockSpec((1, MLA_HEADS * V_HEAD, tm), lambda i: (i // spb, 0, i % spb)),
            pl.BlockSpec((tm, CONV_WIDTH), lambda i: (i, 0)),
            pl.BlockSpec((8, CONV_WIDTH), lambda i: (jnp.maximum(i * r8 - 1, 0), 0)),
            pl.BlockSpec((8, CONV_WIDTH), lambda i: (jnp.minimum((i + 1) * r8, nsteps * r8 - 1), 0)),
            pl.BlockSpec((tm, CONV_WIDTH), lambda i: (i, 0)),
            full(conv_w.shape), full(woa.shape), full(wob.shape), full(wo.shape), full(g2.shape),
            full(rw_hi.shape), full(rw_lo.shape), full(rb.shape),
        ],
        out_specs=[
            pl.BlockSpec((tm, d), lambda i: (i, 0)),
            pl.BlockSpec((tm * SUBLANE, LANE), lambda i: (i, 0)),
            pl.BlockSpec((tm, LANE), lambda i: (i, 0)),
            pl.BlockSpec((tm, LANE), lambda i: (i, 0)),
            pl.BlockSpec((8, LANE), lambda i: (0, 0)),
        ],
        out_shape=[
            jax.ShapeDtypeStruct((n, d), F32),
            jax.ShapeDtypeStruct((n * SUBLANE, LANE), F32),
            jax.ShapeDtypeStruct((n, LANE), jnp.int32),
            jax.ShapeDtypeStruct((n, LANE), F32),
            jax.ShapeDtypeStruct((8, LANE), F32),
        ],
        scratch_shapes=[pltpu.VMEM((tm, tm), BF16), pltpu.VMEM((1, LANE), F32)],
        compiler_params=pltpu.CompilerParams(dimension_semantics=("arbitrary",), vmem_limit_bytes=VMEM_LIMIT),
        name="mix_route",
    )(x2, g1, w_g, ot, cu, cu, cu, bg, conv_w, woa, wob, wo, g2, rw_hi, rw_lo, rb)


def _row_copy_wait(src_like, dst_like, sem, times):
    for _ in range(times):
        pltpu.make_async_copy(src_like, dst_like, sem).wait()


def _token(ref, idx):
    return ref.at[pl.ds(pl.multiple_of(idx * SUBLANE, SUBLANE), SUBLANE)]


def _dispatch_kernel(dest_ref, pe_ref, nused_ref, h2_ref, xs_ref, zero_ref, sem, zsem):
    tm = h2_ref.shape[0] // SUBLANE
    block_rows = EXPERT_BLOCK * SUBLANE
    n_blocks = xs_ref.shape[0] // block_rows

    @pl.when(pl.program_id(0) == 0)
    def _():
        zero_ref[...] = jnp.zeros_like(zero_ref)

        def zero_copy(first_token):
            start = pl.multiple_of(first_token * SUBLANE, block_rows)
            return pltpu.make_async_copy(zero_ref, xs_ref.at[pl.ds(start, block_rows)], zsem)

        def has_rows(e):
            return pe_ref[e] > jnp.where(e == 0, 0, pe_ref[jnp.maximum(e - 1, 0)])

        def start_expert(e, c):
            @pl.when(has_rows(e))
            def _():
                zero_copy(pe_ref[e] - EXPERT_BLOCK).start()
            return c

        def wait_expert(e, c):
            @pl.when(has_rows(e))
            def _():
                zero_copy(pe_ref[e] - EXPERT_BLOCK).wait()
            return c

        def start_tail(b, c):
            zero_copy(b * EXPERT_BLOCK).start()
            return c

        def wait_tail(b, c):
            zero_copy(b * EXPERT_BLOCK).wait()
            return c

        lax.fori_loop(0, N_EXPERTS, start_expert, 0)
        lax.fori_loop(nused_ref[0], n_blocks, start_tail, 0)
        lax.fori_loop(0, N_EXPERTS, wait_expert, 0)
        lax.fori_loop(nused_ref[0], n_blocks, wait_tail, 0)

    def issue(g, c):
        for u in range(ISSUE_GROUP):
            t = g * (ISSUE_GROUP // TOP_K) + u // TOP_K
            dest = dest_ref[g * ISSUE_GROUP + u]
            pltpu.make_async_copy(_token(h2_ref, t), _token(xs_ref, dest), sem).start(priority=u % 2)
        return c

    lax.fori_loop(0, tm * TOP_K // ISSUE_GROUP, issue, 0)
    _row_copy_wait(h2_ref, xs_ref.at[pl.ds(0, tm * SUBLANE)], sem, TOP_K)


def _dispatch(dest_flat, pad_end, n_used, h2t, rows):
    tm = TM_ROUTE
    n = h2t.shape[0] // SUBLANE
    return pl.pallas_call(
        _dispatch_kernel,
        grid=(n // tm,),
        in_specs=[
            pl.BlockSpec((tm * TOP_K,), lambda i: (i,), memory_space=pltpu.SMEM),
            pl.BlockSpec(memory_space=pltpu.SMEM),
            pl.BlockSpec(memory_space=pltpu.SMEM),
            pl.BlockSpec((tm * SUBLANE, LANE), lambda i: (i, 0)),
        ],
        out_specs=pl.BlockSpec(memory_space=pl.ANY),
        out_shape=jax.ShapeDtypeStruct((rows * SUBLANE, LANE), F32),
        scratch_shapes=[pltpu.VMEM((EXPERT_BLOCK * SUBLANE, LANE), F32), pltpu.SemaphoreType.DMA(()),
                        pltpu.SemaphoreType.DMA(())],
        compiler_params=pltpu.CompilerParams(dimension_semantics=("arbitrary",), vmem_limit_bytes=VMEM_LIMIT),
        name="dispatch",
    )(dest_flat, pad_end, n_used, h2t)


def _expert_kernel(bexp_ref, nused_ref, xs_ref, w1_ref, b1_ref, w2_ref, b2_ref, ys_ref, w1b_ref, w2b_ref):
    i = pl.program_id(0)
    active = i < nused_ref[0]
    prev = bexp_ref[jnp.maximum(i - 1, 0)]
    fresh = jnp.logical_or(i == 0, bexp_ref[i] != prev)

    @pl.when(jnp.logical_and(active, fresh))
    def _():
        w1b_ref[...] = w1_ref[0].astype(BF16)
        w2b_ref[...] = w2_ref[0].astype(BF16)

    @pl.when(active)
    def _():
        dff = w2b_ref.shape[0]
        xb = _load_token_tiles(xs_ref, (), EXPERT_BLOCK).astype(BF16)
        hm = jnp.dot(xb, w1b_ref[...], preferred_element_type=F32) + b1_ref[0]
        gate = jnp.minimum(hm[:, 0:dff], SWIGLU_LIMIT)
        up = jnp.clip(hm[:, dff:2 * dff], -SWIGLU_LIMIT, SWIGLU_LIMIT)
        glu = gate * (1.0 / (1.0 + jnp.exp(-SWIGLU_ALPHA * gate)))
        act = ((up + 1.0) * glu).astype(BF16)
        _store_token_tiles(ys_ref, jnp.dot(act, w2b_ref[...], preferred_element_type=F32) + b2_ref[0])

    @pl.when(jnp.logical_not(active))
    def _():
        ys_ref[...] = jnp.zeros_like(ys_ref)


def _experts(block_exp, n_used, xs, w1, b1, w2, b2):
    d = w1.shape[1]
    assert d == SUBLANE * LANE
    block_rows = EXPERT_BLOCK * SUBLANE
    n_blocks = xs.shape[0] // block_rows
    dff2 = w1.shape[2]
    dff = w2.shape[1]
    grid_spec = pltpu.PrefetchScalarGridSpec(
        num_scalar_prefetch=2,
        grid=(n_blocks,),
        in_specs=[
            pl.BlockSpec((block_rows, LANE), lambda i, be, nu: (jnp.minimum(i, nu[0] - 1), 0)),
            pl.BlockSpec((1, d, dff2), lambda i, be, nu: (be[i], 0, 0)),
            pl.BlockSpec((1, 1, dff2), lambda i, be, nu: (be[i], 0, 0)),
            pl.BlockSpec((1, dff, d), lambda i, be, nu: (be[i], 0, 0)),
            pl.BlockSpec((1, 1, d), lambda i, be, nu: (be[i], 0, 0)),
        ],
        out_specs=pl.BlockSpec((block_rows, LANE), lambda i, be, nu: (i, 0)),
        scratch_shapes=[pltpu.VMEM((d, dff2), BF16), pltpu.VMEM((dff, d), BF16)],
    )
    return pl.pallas_call(
        _expert_kernel,
        grid_spec=grid_spec,
        out_shape=jax.ShapeDtypeStruct(xs.shape, F32),
        compiler_params=pltpu.CompilerParams(dimension_semantics=("arbitrary",), vmem_limit_bytes=VMEM_LIMIT),
        name="experts",
    )(block_exp, n_used, xs, w1, b1, w2, b2)


def _combine_kernel(dest_ref, dest_next_ref, x1_ref, mw_ref, ys_ref, out_ref, buf_ref, sem):
    i = pl.program_id(0)
    tm = x1_ref.shape[0]
    slot = lax.rem(i, 2)

    def gather(idx_ref, s):
        def issue(g, c):
            for u in range(ISSUE_GROUP):
                t = g * (ISSUE_GROUP // TOP_K) + u // TOP_K
                dest = idx_ref[g * ISSUE_GROUP + u]
                pltpu.make_async_copy(_token(ys_ref, dest), _token(buf_ref.at[s, u % TOP_K], t),
                                      sem.at[s]).start(priority=u % 2)
            return c

        lax.fori_loop(0, tm * TOP_K // ISSUE_GROUP, issue, 0)

    @pl.when(i == 0)
    def _():
        gather(dest_ref, 0)

    @pl.when(i + 1 < pl.num_programs(0))
    def _():
        gather(dest_next_ref, 1 - slot)

    _row_copy_wait(ys_ref.at[pl.ds(0, tm * SUBLANE)], buf_ref.at[slot, 0], sem.at[slot], TOP_K)
    acc = x1_ref[...]
    mw = mw_ref[...]
    for k in range(TOP_K):
        acc = acc + mw[:, k:k + 1] * _load_token_tiles(buf_ref, (slot, k), tm)
    out_ref[...] = acc


def _combine(dest_flat, x1, mw, ys):
    n, d = x1.shape
    tm = TM_ROUTE
    nsteps = n // tm
    return pl.pallas_call(
        _combine_kernel,
        grid=(nsteps,),
        in_specs=[
            pl.BlockSpec((tm * TOP_K,), lambda i: (i,), memory_space=pltpu.SMEM),
            pl.BlockSpec((tm * TOP_K,), lambda i: (jnp.minimum(i + 1, nsteps - 1),), memory_space=pltpu.SMEM),
            pl.BlockSpec((tm, d), lambda i: (i, 0)),
            pl.BlockSpec((tm, LANE), lambda i: (i, 0)),
            pl.BlockSpec(memory_space=pl.ANY),
        ],
        out_specs=pl.BlockSpec((tm, d), lambda i: (i, 0)),
        out_shape=jax.ShapeDtypeStruct((n, d), F32),
        scratch_shapes=[pltpu.VMEM((2, TOP_K, tm * SUBLANE, LANE), F32), pltpu.SemaphoreType.DMA((2,))],
        compiler_params=pltpu.CompilerParams(dimension_semantics=("arbitrary",), vmem_limit_bytes=VMEM_LIMIT),
        name="combine",
    )(dest_flat, dest_flat, x1, mw, ys)


def _pad_cols(w, width):
    return jnp.pad(w, ((0, 0), (0, width - w.shape[1])))


def _head_slots(w, per_head):
    rows = w.shape[0]
    w3 = w.reshape(rows, MLA_HEADS, per_head)
    return jnp.pad(w3, ((0, 0), (0, 0), (0, HEAD_SLOT - per_head))).reshape(rows, MLA_HEADS * HEAD_SLOT)


def _rope_tables(positions):
    inv_freq = ROPE_THETA ** (-jnp.arange(0, QK_ROPE, 2, dtype=F32) / QK_ROPE)
    ang = positions.astype(F32).reshape(-1, 1) * inv_freq
    cos, sin = jnp.cos(ang), jnp.sin(ang)
    n = ang.shape[0]
    z = lambda w: jnp.zeros((n, w), F32)
    cos_t = jnp.concatenate([jnp.ones((n, QK_NOPE), F32), cos, cos, z(LANE - QK_HEAD)], axis=1)
    sin_p = jnp.concatenate([z(QK_NOPE + HALF_ROPE), sin, z(LANE - QK_HEAD)], axis=1)
    sin_m = jnp.concatenate([z(QK_NOPE), -sin, z(LANE - QK_NOPE - HALF_ROPE)], axis=1)
    return cos_t, sin_p, sin_m


def _layer(x2, positions, norm1_g, w_in, q_a_norm_g, kv_a_norm_g, w_uq, w_ukv, q_norm_g, k_norm_g,
           conv_w, w_o_mla, w_o_conv, w_o, norm2_g, router_w, router_b,
           expert_w1, expert_b1, expert_w2, expert_b2, batch, seq):
    n, d = x2.shape
    o_kr = Q_LORA + KV_LORA
    o_u = o_kr + QK_ROPE
    o_g = o_u + 3 * CONV_WIDTH
    kr_cols = jnp.pad(w_in[:, o_kr:o_u], ((0, 0), (QK_NOPE, LANE - QK_HEAD)))
    w_a = jnp.concatenate([w_in[:, :o_kr], kr_cols, w_in[:, o_u:o_g]], axis=1).astype(BF16)
    w_g = w_in[:, o_g:].astype(BF16)
    row = lambda v: v.reshape(1, -1)
    cos_t, sin_p, sin_m = _rope_tables(positions)

    qt, k, vt, cu, bg = _in_projection(
        x2, row(norm1_g), w_a, row(q_a_norm_g), row(kv_a_norm_g),
        _head_slots(w_uq, QK_HEAD).astype(BF16), w_ukv.astype(BF16),
        _pad_cols(row(q_norm_g), LANE), _pad_cols(row(k_norm_g), LANE),
        cos_t, sin_p, sin_m, batch, seq)
    ot = _attention(qt, k, vt)

    rw = _pad_cols(router_w, LANE)
    rw_hi = rw.astype(BF16)
    rw_lo = (rw - rw_hi.astype(F32)).astype(BF16)
    rb = jnp.concatenate([row(router_b), jnp.full((1, LANE - N_EXPERTS), NEG_BIG, F32)], axis=1)
    x1, h2, mi, mw, cnt = _mix(
        x2, row(norm1_g), w_g, ot, cu, bg, conv_w, w_o_mla.astype(BF16), w_o_conv.astype(BF16),
        w_o.astype(BF16), row(norm2_g), rw_hi, rw_lo, rb, batch, seq)

    counts = cnt[0, :N_EXPERTS].astype(jnp.int32)
    padded = (counts + EXPERT_BLOCK - 1) // EXPERT_BLOCK * EXPERT_BLOCK
    pad_end = jnp.cumsum(padded)
    pad_start = (pad_end - padded).astype(jnp.int32)
    nk = n * TOP_K
    n_blocks = (nk + N_EXPERTS * (EXPERT_BLOCK - 1) + EXPERT_BLOCK - 1) // EXPERT_BLOCK
    rows = n_blocks * EXPERT_BLOCK
    block_first_row = jnp.arange(n_blocks, dtype=jnp.int32) * EXPERT_BLOCK
    block_exp = jnp.minimum(jnp.sum(pad_end[None, :] <= block_first_row[:, None], axis=1),
                            N_EXPERTS - 1).astype(jnp.int32)
    n_used = (pad_end[-1:] // EXPERT_BLOCK).astype(jnp.int32)
    dest_flat = (pad_start[mi[:, 0:TOP_K]] + mi[:, TOP_K:2 * TOP_K]).reshape(nk)

    xs = _dispatch(dest_flat, pad_end.astype(jnp.int32), n_used, h2, rows)
    ys = _experts(block_exp, n_used, xs, expert_w1, expert_b1.reshape(N_EXPERTS, 1, -1),
                  expert_w2, expert_b2.reshape(N_EXPERTS, 1, -1))
    return _combine(dest_flat, x1, mw, ys)


def kernel(x, positions, norm1_g, w_in, q_a_norm_g, kv_a_norm_g, w_uq, w_ukv, q_norm_g, k_norm_g, conv_w,
           w_o_mla, w_o_conv, w_o, norm2_g, router_w, router_b, expert_w1, expert_b1, expert_w2, expert_b2):
    batch, seq, d = x.shape
    depth = norm1_g.shape[0]
    x2 = x.reshape(batch * seq, d)
    for l in range(depth):
        x2 = _layer(x2, positions, norm1_g[l], w_in[l], q_a_norm_g[l], kv_a_norm_g[l], w_uq[l], w_ukv[l],
                    q_norm_g[l], k_norm_g[l], conv_w[l], w_o_mla[l], w_o_conv[l], w_o[l], norm2_g[l],
                    router_w[l], router_b[l], expert_w1[l], expert_b1[l], expert_w2[l], expert_b2[l], batch, seq)
    return x2.reshape(batch, seq, d)
```

```python
import functools
import math

import jax
import jax.numpy as jnp
from jax import lax
from jax.experimental import pallas as pl
from jax.experimental.pallas import tpu as pltpu

F32 = jnp.float32
BF16 = jnp.bfloat16

MLA_HEADS = 8
QK_NOPE = 64
QK_ROPE = 32
QK_HEAD = QK_NOPE + QK_ROPE
V_HEAD = 64
Q_LORA = 256
KV_LORA = 128
ROPE_THETA = 10000.0
CONV_WIDTH = 512
N_EXPERTS = 32
TOP_K = 4
SWIGLU_LIMIT = 7.0
SWIGLU_ALPHA = 1.702
EPS = 1e-6

LANE = 128
SUBLANE = 8
HEAD_SLOT = LANE
HALF_ROPE = QK_ROPE // 2
V_ROWS = V_HEAD + 16
VMEM_LIMIT = 56 * 1024 * 1024

TM_PROJ = 512
TQ = 512
TKV = 512
ATTN_UNROLL = 4
TM_MIX = 512
TM_ROUTE = 256
EXPERT_BLOCK = 512
ISSUE_GROUP = 16
NEG_BIG = -1e30


def _load_token_tiles(ref, lead, rows):
    return jnp.concatenate([ref[lead + (pl.ds(c, rows, stride=SUBLANE), slice(None))] for c in range(SUBLANE)],
                           axis=1)


def _store_token_tiles(ref, value):
    rows = value.shape[0]
    for c in range(SUBLANE):
        ref[pl.ds(c, rows, stride=SUBLANE), :] = value[:, c * LANE:(c + 1) * LANE]


def _rms(x, g):
    return x * lax.rsqrt(jnp.mean(x * x, axis=-1, keepdims=True) + EPS) * g


def _rope(t, cos_t, sin_p, sin_m):
    return t * cos_t + pltpu.roll(t, HALF_ROPE, 1) * sin_p + pltpu.roll(t, LANE - HALF_ROPE, 1) * sin_m


def _inproj_kernel(x_ref, g1_ref, w_ref, gq_ref, gkv_ref, wuq_ref, wukv_ref, qg_ref, kg_ref,
                   cos_ref, sinp_ref, sinm_ref,
                   qt_ref, k_ref, vt_ref, cu_ref, bg_ref, *, q_scale):
    x = x_ref[...]
    h = _rms(x, g1_ref[...]).astype(BF16)
    proj = jnp.dot(h, w_ref[...], preferred_element_type=F32)
    c_q = proj[:, 0:Q_LORA]
    c_kv = proj[:, Q_LORA:Q_LORA + KV_LORA]
    kr = proj[:, Q_LORA + KV_LORA:Q_LORA + KV_LORA + LANE]
    o = Q_LORA + KV_LORA + LANE
    u = proj[:, o:o + CONV_WIDTH]
    c_gate = proj[:, o + CONV_WIDTH:o + 2 * CONV_WIDTH]
    b_gate = proj[:, o + 2 * CONV_WIDTH:o + 3 * CONV_WIDTH]
    cu_ref[...] = c_gate * u
    bg_ref[...] = b_gate

    cos_t = cos_ref[...]
    sin_p = sinp_ref[...]
    sin_m = sinm_ref[...]
    qg = qg_ref[...]
    kg = kg_ref[...]
    tm = x.shape[0]
    lane = lax.broadcasted_iota(jnp.int32, (tm, LANE), 1)

    q = jnp.dot(_rms(c_q, gq_ref[...]).astype(BF16), wuq_ref[...], preferred_element_type=F32)
    kv = jnp.dot(_rms(c_kv, gkv_ref[...]).astype(BF16), wukv_ref[...], preferred_element_type=F32)

    ss_r = jnp.sum(kr * kr, axis=-1, keepdims=True)
    kr_roped = _rope(kr * kg, cos_t, sin_p, sin_m)
    ones = jnp.ones((V_ROWS - V_HEAD, tm), BF16)
    for hd in range(MLA_HEADS):
        qh = q[:, hd * HEAD_SLOT:(hd + 1) * HEAD_SLOT]
        r = lax.rsqrt(jnp.sum(qh * qh, axis=-1, keepdims=True) * (1.0 / QK_HEAD) + EPS)
        qn = _rope(qh * r * qg, cos_t, sin_p, sin_m) * q_scale
        qt_ref[0, hd] = qn.T.astype(BF16)

        kvh = kv[:, hd * HEAD_SLOT:(hd + 1) * HEAD_SLOT]
        knope = jnp.where(lane < QK_NOPE, kvh, 0.0)
        rk = lax.rsqrt((jnp.sum(knope * knope, axis=-1, keepdims=True) + ss_r) * (1.0 / QK_HEAD) + EPS)
        k_ref[0, hd] = ((knope * kg + kr_roped) * rk).astype(BF16)
        kvt = kvh.T
        vt_ref[0, hd, 0:V_HEAD, :] = kvt[QK_NOPE:QK_NOPE + V_HEAD].astype(BF16)
        vt_ref[0, hd, V_HEAD:V_ROWS, :] = ones


def _in_projection(x2, g1, w_a, gq, gkv, wuq, wukv, qg, kg, cos_t, sin_p, sin_m, batch, seq):
    n, d = x2.shape
    tm = TM_PROJ
    spb = seq // tm
    q_scale = (QK_HEAD ** -0.5) * math.log2(math.e)
    full = lambda shp: pl.BlockSpec(shp, lambda i: (0,) * len(shp))
    return pl.pallas_call(
        functools.partial(_inproj_kernel, q_scale=q_scale),
        grid=(n // tm,),
        in_specs=[
            pl.BlockSpec((tm, d), lambda i: (i, 0)),
            full(g1.shape), full(w_a.shape), full(gq.shape), full(gkv.shape), full(wuq.shape), full(wukv.shape),
            full(qg.shape), full(kg.shape),
            pl.BlockSpec((tm, LANE), lambda i: (i, 0)),
            pl.BlockSpec((tm, LANE), lambda i: (i, 0)),
            pl.BlockSpec((tm, LANE), lambda i: (i, 0)),
        ],
        out_specs=[
            pl.BlockSpec((1, MLA_HEADS, HEAD_SLOT, tm), lambda i: (i // spb, 0, 0, i % spb)),
            pl.BlockSpec((1, MLA_HEADS, tm, HEAD_SLOT), lambda i: (i // spb, 0, i % spb, 0)),
            pl.BlockSpec((1, MLA_HEADS, V_ROWS, tm), lambda i: (i // spb, 0, 0, i % spb)),
            pl.BlockSpec((tm, CONV_WIDTH), lambda i: (i, 0)),
            pl.BlockSpec((tm, CONV_WIDTH), lambda i: (i, 0)),
        ],
        out_shape=[
            jax.ShapeDtypeStruct((batch, MLA_HEADS, HEAD_SLOT, seq), BF16),
            jax.ShapeDtypeStruct((batch, MLA_HEADS, seq, HEAD_SLOT), BF16),
            jax.ShapeDtypeStruct((batch, MLA_HEADS, V_ROWS, seq), BF16),
            jax.ShapeDtypeStruct((n, CONV_WIDTH), F32),
            jax.ShapeDtypeStruct((n, CONV_WIDTH), F32),
        ],
        compiler_params=pltpu.CompilerParams(dimension_semantics=("parallel",), vmem_limit_bytes=VMEM_LIMIT),
        name="in_projection",
    )(x2, g1, w_a, gq, gkv, wuq, wukv, qg, kg, cos_t, sin_p, sin_m)


def _attn_kernel(qt_ref, k_ref, vt_ref, o_ref, sa_ref, sb_ref, m_ref, acc_ref, *, tk):
    qt = qt_ref[0, 0]
    nk = k_ref.shape[2] // tk

    def scores(c, s_ref):
        start = pl.multiple_of(c * tk, tk)
        s_ref[...] = jnp.dot(k_ref[0, 0, pl.ds(start, tk), :], qt, preferred_element_type=F32)

    def accumulate(c, s_ref):
        start = pl.multiple_of(c * tk, tk)
        s = s_ref[...]
        m = m_ref[...]
        m_new = jnp.maximum(m, jnp.max(s, axis=0, keepdims=True))
        m_ref[...] = m_new
        p = jnp.exp2(s - m_new).astype(BF16)
        vs = vt_ref[0, 0, :, pl.ds(start, tk)]
        acc_ref[...] = jnp.exp2(m - m_new) * acc_ref[...] + jnp.dot(vs, p, preferred_element_type=F32)

    m_ref[...] = jnp.full(m_ref.shape, NEG_BIG, F32)
    acc_ref[...] = jnp.zeros(acc_ref.shape, F32)
    scores(0, sa_ref)

    bufs = (sa_ref, sb_ref)

    def body(j, carry):
        c = ATTN_UNROLL * j
        for u in range(ATTN_UNROLL):
            scores(c + u + 1, bufs[(u + 1) % 2])
            accumulate(c + u, bufs[u % 2])
        return carry

    lax.fori_loop(0, nk // ATTN_UNROLL - 1, body, 0)
    c = nk - ATTN_UNROLL
    for u in range(ATTN_UNROLL):
        if u + 1 < ATTN_UNROLL:
            scores(c + u + 1, bufs[(u + 1) % 2])
        accumulate(c + u, bufs[u % 2])
    acc = acc_ref[...]
    o_ref[0] = (acc[0:V_HEAD] * (1.0 / acc[V_HEAD:V_HEAD + 1])).astype(BF16)


def _attention(qt, k, vt):
    batch, heads, _, seq = qt.shape
    assert ATTN_UNROLL % 2 == 0 and (seq // TKV) % ATTN_UNROLL == 0
    return pl.pallas_call(
        functools.partial(_attn_kernel, tk=TKV),
        grid=(batch, heads, seq // TQ),
        in_specs=[
            pl.BlockSpec((1, 1, HEAD_SLOT, TQ), lambda b, h, i: (b, h, 0, i)),
            pl.BlockSpec((1, 1, seq, HEAD_SLOT), lambda b, h, i: (b, h, 0, 0)),
            pl.BlockSpec((1, 1, V_ROWS, seq), lambda b, h, i: (b, h, 0, 0)),
        ],
        out_specs=pl.BlockSpec((1, V_HEAD, TQ), lambda b, h, i: (b, h, i)),
        out_shape=jax.ShapeDtypeStruct((batch, heads * V_HEAD, seq), BF16),
        scratch_shapes=[pltpu.VMEM((TKV, TQ), F32), pltpu.VMEM((TKV, TQ), F32),
                        pltpu.VMEM((1, TQ), F32), pltpu.VMEM((V_ROWS, TQ), F32)],
        compiler_params=pltpu.CompilerParams(
            dimension_semantics=("parallel", "parallel", "parallel"), vmem_limit_bytes=VMEM_LIMIT),
        name="attention",
    )(qt, k, vt)


def _mix_kernel(x_ref, g1_ref, wg_ref, ot_ref, cu_ref, cup_ref, cun_ref, bg_ref, cw_ref,
                woa_ref, wob_ref, wo_ref, g2_ref, rwh_ref, rwl_ref, rb_ref,
                x1_ref, h2_ref, mi_ref, mw_ref, cnt_ref, tri_ref, carry_ref, *, steps_per_seq):
    i = pl.program_id(0)
    tm = x_ref.shape[0]

    @pl.when(i == 0)
    def _():
        r = lax.broadcasted_iota(jnp.int32, (tm, tm), 0)
        c = lax.broadcasted_iota(jnp.int32, (tm, tm), 1)
        tri_ref[...] = jnp.where(c < r, 1.0, 0.0).astype(BF16)
        carry_ref[...] = jnp.zeros_like(carry_ref)

    x = x_ref[...]
    h = _rms(x, g1_ref[...]).astype(BF16)
    gates = jnp.dot(h, wg_ref[...], preferred_element_type=F32)
    d = x.shape[1]
    sig_a = 1.0 / (1.0 + jnp.exp(-gates[:, 0:d]))
    sig_b = 1.0 / (1.0 + jnp.exp(-gates[:, d:2 * d]))

    y_a = lax.dot_general(ot_ref[0], woa_ref[...], (((0,), (0,)), ((), ())), preferred_element_type=F32)

    cu = cu_ref[...]
    row = lax.broadcasted_iota(jnp.int32, cu.shape, 0)
    s_in_seq = i % steps_per_seq
    prev_row = jnp.where(s_in_seq == 0, 0.0, cup_ref[7:8, :])
    next_row = jnp.where(s_in_seq == steps_per_seq - 1, 0.0, cun_ref[0:1, :])
    below = jnp.where(row == 0, prev_row, pltpu.roll(cu, 1, 0))
    above = jnp.where(row == tm - 1, next_row, pltpu.roll(cu, tm - 1, 0))
    cw = cw_ref[...]
    conv = cw[0:1, :] * below + cw[1:2, :] * cu + cw[2:3, :] * above
    y_b = jnp.dot((bg_ref[...] * conv).astype(BF16), wob_ref[...], preferred_element_type=F32)

    merged = (sig_a * y_a + sig_b * y_b).astype(BF16)
    x1 = x + jnp.dot(merged, wo_ref[...], preferred_element_type=F32)
    x1_ref[...] = x1
    h2 = _rms(x1, g2_ref[...])
    _store_token_tiles(h2_ref, h2)

    h2_hi = h2.astype(BF16)
    h2_lo = (h2 - h2_hi.astype(F32)).astype(BF16)
    logits = (jnp.dot(h2_hi, rwh_ref[...], preferred_element_type=F32)
              + jnp.dot(h2_lo, rwh_ref[...], preferred_element_type=F32)
              + jnp.dot(h2_hi, rwl_ref[...], preferred_element_type=F32)
              + rb_ref[...])

    lane = lax.broadcasted_iota(jnp.int32, logits.shape, 1)
    lane_f = lane.astype(F32)
    work = logits
    vals, idxs, hits = [], [], []
    for _ in range(TOP_K):
        mx = jnp.max(work, axis=-1, keepdims=True)
        idx = jnp.min(jnp.where(work == mx, lane_f, float(LANE)), axis=-1, keepdims=True)
        hit = lane_f == idx
        work = jnp.where(hit, -jnp.inf, work)
        vals.append(mx)
        idxs.append(idx)
        hits.append(hit)
    exps = [jnp.exp(v - vals[0]) for v in vals]
    denom = exps[0] + exps[1] + exps[2] + exps[3]
    inv = 1.0 / denom

    sel = jnp.zeros(logits.shape, F32)
    for hit in hits:
        sel = sel + jnp.where(hit, 1.0, 0.0)
    carry = carry_ref[...]
    before = jnp.dot(tri_ref[...], sel.astype(BF16), preferred_element_type=F32) + carry
    carry_new = carry + jnp.sum(sel, axis=0, keepdims=True)
    carry_ref[...] = carry_new
    cnt_ref[...] = jnp.broadcast_to(carry_new, cnt_ref.shape)

    mi = jnp.zeros(logits.shape, F32)
    mw = jnp.zeros(logits.shape, F32)
    for k in range(TOP_K):
        rank = jnp.sum(jnp.where(hits[k], before, 0.0), axis=-1, keepdims=True)
        mi = jnp.where(lane == k, idxs[k], mi)
        mi = jnp.where(lane == TOP_K + k, rank, mi)
        mw = jnp.where(lane == k, exps[k] * inv, mw)
    mi_ref[...] = mi.astype(jnp.int32)
    mw_ref[...] = mw


def _mix(x2, g1, w_g, ot, cu, bg, conv_w, woa, wob, wo, g2, rw_hi, rw_lo, rb, batch, seq):
    n, d = x2.shape
    tm = TM_MIX
    spb = seq // tm
    r8 = tm // 8
    nsteps = n // tm
    full = lambda shp: pl.BlockSpec(shp, lambda i: (0,) * len(shp))
    return pl.pallas_call(
        functools.partial(_mix_kernel, steps_per_seq=spb),
        grid=(nsteps,),
        in_specs=[
            pl.BlockSpec((tm, d), lambda i: (i, 0)),
            full(g1.shape), full(w_g.shape),
            pl.BlockSpec((1, MLA_HEADS * V_HEAD, tm), lambda i: (i // spb, 0, i % spb)),
            pl.BlockSpec((tm, CONV_WIDTH), lambda i: (i, 0)),
            pl.BlockSpec((8, CONV_WIDTH), lambda i: (jnp.maximum(i * r8 - 1, 0), 0)),
            pl.BlockSpec((8, CONV_WIDTH), lambda i: (jnp.minimum((i + 1) * r8, nsteps * r8 - 1), 0)),
            pl.BlockSpec((tm, CONV_WIDTH), lambda i: (i, 0)),
            full(conv_w.shape), full(woa.shape), full(wob.shape), full(wo.shape), full(g2.shape),
            full(rw_hi.shape), full(rw_lo.shape), full(rb.shape),
        ],
        out_specs=[
            pl.BlockSpec((tm, d), lambda i: (i, 0)),
            pl.BlockSpec((tm * SUBLANE, LANE), lambda i: (i, 0)),
            pl.BlockSpec((tm, LANE), lambda i: (i, 0)),
            pl.BlockSpec((tm, LANE), lambda i: (i, 0)),
            pl.BlockSpec((8, LANE), lambda i: (0, 0)),
        ],
        out_shape=[
            jax.ShapeDtypeStruct((n, d), F32),
            jax.ShapeDtypeStruct((n * SUBLANE, LANE), F32),
            jax.ShapeDtypeStruct((n, LANE), jnp.int32),
            jax.ShapeDtypeStruct((n, LANE), F32),
            jax.ShapeDtypeStruct((8, LANE), F32),
        ],
        scratch_shapes=[pltpu.VMEM((tm, tm), BF16), pltpu.VMEM((1, LANE), F32)],
        compiler_params=pltpu.CompilerParams(dimension_semantics=("arbitrary",), vmem_limit_bytes=VMEM_LIMIT),
        name="mix_route",
    )(x2, g1, w_g, ot, cu, cu, cu, bg, conv_w, woa, wob, wo, g2, rw_hi, rw_lo, rb)


def _row_copy_wait(src_like, dst_like, sem, times):
    for _ in range(times):
        pltpu.make_async_copy(src_like, dst_like, sem).wait()


def _token(ref, idx):
    return ref.at[pl.ds(pl.multiple_of(idx * SUBLANE, SUBLANE), SUBLANE)]


def _dispatch_kernel(dest_ref, pe_ref, nused_ref, h2_ref, xs_ref, zero_ref, sem, zsem):
    tm = h2_ref.shape[0] // SUBLANE
    block_rows = EXPERT_BLOCK * SUBLANE
    n_blocks = xs_ref.shape[0] // block_rows

    @pl.when(pl.program_id(0) == 0)
    def _():
        zero_ref[...] = jnp.zeros_like(zero_ref)

        def zero_copy(first_token):
            start = pl.multiple_of(first_token * SUBLANE, block_rows)
            return pltpu.make_async_copy(zero_ref, xs_ref.at[pl.ds(start, block_rows)], zsem)

        def has_rows(e):
            return pe_ref[e] > jnp.where(e == 0, 0, pe_ref[jnp.maximum(e - 1, 0)])

        def start_expert(e, c):
            @pl.when(has_rows(e))
            def _():
                zero_copy(pe_ref[e] - EXPERT_BLOCK).start()
            return c

        def wait_expert(e, c):
            @pl.when(has_rows(e))
            def _():
                zero_copy(pe_ref[e] - EXPERT_BLOCK).wait()
            return c

        def start_tail(b, c):
            zero_copy(b * EXPERT_BLOCK).start()
            return c

        def wait_tail(b, c):
            zero_copy(b * EXPERT_BLOCK).wait()
            return c

        lax.fori_loop(0, N_EXPERTS, start_expert, 0)
        lax.fori_loop(nused_ref[0], n_blocks, start_tail, 0)
        lax.fori_loop(0, N_EXPERTS, wait_expert, 0)
        lax.fori_loop(nused_ref[0], n_blocks, wait_tail, 0)

    def issue(g, c):
        for u in range(ISSUE_GROUP):
            t = g * (ISSUE_GROUP // TOP_K) + u // TOP_K
            dest = dest_ref[g * ISSUE_GROUP + u]
            pltpu.make_async_copy(_token(h2_ref, t), _token(xs_ref, dest), sem).start(priority=u % 2)
        return c

    lax.fori_loop(0, tm * TOP_K // ISSUE_GROUP, issue, 0)
    _row_copy_wait(h2_ref, xs_ref.at[pl.ds(0, tm * SUBLANE)], sem, TOP_K)


def _dispatch(dest_flat, pad_end, n_used, h2t, rows):
    tm = TM_ROUTE
    n = h2t.shape[0] // SUBLANE
    return pl.pallas_call(
        _dispatch_kernel,
        grid=(n // tm,),
        in_specs=[
            pl.BlockSpec((tm * TOP_K,), lambda i: (i,), memory_space=pltpu.SMEM),
            pl.BlockSpec(memory_space=pltpu.SMEM),
            pl.BlockSpec(memory_space=pltpu.SMEM),
            pl.BlockSpec((tm * SUBLANE, LANE), lambda i: (i, 0)),
        ],
        out_specs=pl.BlockSpec(memory_space=pl.ANY),
        out_shape=jax.ShapeDtypeStruct((rows * SUBLANE, LANE), F32),
        scratch_shapes=[pltpu.VMEM((EXPERT_BLOCK * SUBLANE, LANE), F32), pltpu.SemaphoreType.DMA(()),
                        pltpu.SemaphoreType.DMA(())],
        compiler_params=pltpu.CompilerParams(dimension_semantics=("arbitrary",), vmem_limit_bytes=VMEM_LIMIT),
        name="dispatch",
    )(dest_flat, pad_end, n_used, h2t)


def _expert_kernel(bexp_ref, nused_ref, xs_ref, w1_ref, b1_ref, w2_ref, b2_ref, ys_ref, w1b_ref, w2b_ref):
    i = pl.program_id(0)
    active = i < nused_ref[0]
    prev = bexp_ref[jnp.maximum(i - 1, 0)]
    fresh = jnp.logical_or(i == 0, bexp_ref[i] != prev)

    @pl.when(jnp.logical_and(active, fresh))
    def _():
        w1b_ref[...] = w1_ref[0].astype(BF16)
        w2b_ref[...] = w2_ref[0].astype(BF16)

    @pl.when(active)
    def _():
        dff = w2b_ref.shape[0]
        xb = _load_token_tiles(xs_ref, (), EXPERT_BLOCK).astype(BF16)
        hm = jnp.dot(xb, w1b_ref[...], preferred_element_type=F32) + b1_ref[0]
        gate = jnp.minimum(hm[:, 0:dff], SWIGLU_LIMIT)
        up = jnp.clip(hm[:, dff:2 * dff], -SWIGLU_LIMIT, SWIGLU_LIMIT)
        glu = gate * (1.0 / (1.0 + jnp.exp(-SWIGLU_ALPHA * gate)))
        act = ((up + 1.0) * glu).astype(BF16)
        _store_token_tiles(ys_ref, jnp.dot(act, w2b_ref[...], preferred_element_type=F32) + b2_ref[0])

    @pl.when(jnp.logical_not(active))
    def _():
        ys_ref[...] = jnp.zeros_like(ys_ref)


def _experts(block_exp, n_used, xs, w1, b1, w2, b2):
    d = w1.shape[1]
    assert d == SUBLANE * LANE
    block_rows = EXPERT_BLOCK * SUBLANE
    n_blocks = xs.shape[0] // block_rows
    dff2 = w1.shape[2]
    dff = w2.shape[1]
    grid_spec = pltpu.PrefetchScalarGridSpec(
        num_scalar_prefetch=2,
        grid=(n_blocks,),
        in_specs=[
            pl.BlockSpec((block_rows, LANE), lambda i, be, nu: (jnp.minimum(i, nu[0] - 1), 0)),
            pl.BlockSpec((1, d, dff2), lambda i, be, nu: (be[i], 0, 0)),
            pl.BlockSpec((1, 1, dff2), lambda i, be, nu: (be[i], 0, 0)),
            pl.BlockSpec((1, dff, d), lambda i, be, nu: (be[i], 0, 0)),
            pl.BlockSpec((1, 1, d), lambda i, be, nu: (be[i], 0, 0)),
        ],
        out_specs=pl.BlockSpec((block_rows, LANE), lambda i, be, nu: (i, 0)),
        scratch_shapes=[pltpu.VMEM((d, dff2), BF16), pltpu.VMEM((dff, d), BF16)],
    )
    return pl.pallas_call(
        _expert_kernel,
        grid_spec=grid_spec,
        out_shape=jax.ShapeDtypeStruct(xs.shape, F32),
        compiler_params=pltpu.CompilerParams(dimension_semantics=("arbitrary",), vmem_limit_bytes=VMEM_LIMIT),
        name="experts",
    )(block_exp, n_used, xs, w1, b1, w2, b2)


def _combine_kernel(dest_ref, dest_next_ref, x1_ref, mw_ref, ys_ref, out_ref, buf_ref, sem):
    i = pl.program_id(0)
    tm = x1_ref.shape[0]
    slot = lax.rem(i, 2)

    def gather(idx_ref, s):
        def issue(g, c):
            for u in range(ISSUE_GROUP):
                t = g * (ISSUE_GROUP // TOP_K) + u // TOP_K
                dest = idx_ref[g * ISSUE_GROUP + u]
                pltpu.make_async_copy(_token(ys_ref, dest), _token(buf_ref.at[s, u % TOP_K], t),
                                      sem.at[s]).start(priority=u % 2)
            return c

        lax.fori_loop(0, tm * TOP_K // ISSUE_GROUP, issue, 0)

    @pl.when(i == 0)
    def _():
        gather(dest_ref, 0)

    @pl.when(i + 1 < pl.num_programs(0))
    def _():
        gather(dest_next_ref, 1 - slot)

    _row_copy_wait(ys_ref.at[pl.ds(0, tm * SUBLANE)], buf_ref.at[slot, 0], sem.at[slot], TOP_K)
    acc = x1_ref[...]
    mw = mw_ref[...]
    for k in range(TOP_K):
        acc = acc + mw[:, k:k + 1] * _load_token_tiles(buf_ref, (slot, k), tm)
    out_ref[...] = acc


def _combine(dest_flat, x1, mw, ys):
    n, d = x1.shape
    tm = TM_ROUTE
    nsteps = n // tm
    return pl.pallas_call(
        _combine_kernel,
        grid=(nsteps,),
        in_specs=[
            pl.BlockSpec((tm * TOP_K,), lambda i: (i,), memory_space=pltpu.SMEM),
            pl.BlockSpec((tm * TOP_K,), lambda i: (jnp.minimum(i + 1, nsteps - 1),), memory_space=pltpu.SMEM),
            pl.BlockSpec((tm, d), lambda i: (i, 0)),
            pl.BlockSpec((tm, LANE), lambda i: (i, 0)),
            pl.BlockSpec(memory_space=pl.ANY),
        ],
        out_specs=pl.BlockSpec((tm, d), lambda i: (i, 0)),
        out_shape=jax.ShapeDtypeStruct((n, d), F32),
        scratch_shapes=[pltpu.VMEM((2, TOP_K, tm * SUBLANE, LANE), F32), pltpu.SemaphoreType.DMA((2,))],
        compiler_params=pltpu.CompilerParams(dimension_semantics=("arbitrary",), vmem_limit_bytes=VMEM_LIMIT),
        name="combine",
    )(dest_flat, dest_flat, x1, mw, ys)


def _pad_cols(w, width):
    return jnp.pad(w, ((0, 0), (0, width - w.shape[1])))


def _head_slots(w, per_head):
    rows = w.shape[0]
    w3 = w.reshape(rows, MLA_HEADS, per_head)
    return jnp.pad(w3, ((0, 0), (0, 0), (0, HEAD_SLOT - per_head))).reshape(rows, MLA_HEADS * HEAD_SLOT)


def _rope_tables(positions):
    inv_freq = ROPE_THETA ** (-jnp.arange(0, QK_ROPE, 2, dtype=F32) / QK_ROPE)
    ang = positions.astype(F32).reshape(-1, 1) * inv_freq
    cos, sin = jnp.cos(ang), jnp.sin(ang)
    n = ang.shape[0]
    z = lambda w: jnp.zeros((n, w), F32)
    cos_t = jnp.concatenate([jnp.ones((n, QK_NOPE), F32), cos, cos, z(LANE - QK_HEAD)], axis=1)
    sin_p = jnp.concatenate([z(QK_NOPE + HALF_ROPE), sin, z(LANE - QK_HEAD)], axis=1)
    sin_m = jnp.concatenate([z(QK_NOPE), -sin, z(LANE - QK_NOPE - HALF_ROPE)], axis=1)
    return cos_t, sin_p, sin_m


def _layer(x2, positions, norm1_g, w_in, q_a_norm_g, kv_a_norm_g, w_uq, w_ukv, q_norm_g, k_norm_g,
           conv_w, w_o_mla, w_o_conv, w_o, norm2_g, router_w, router_b,
           expert_w1, expert_b1, expert_w2, expert_b2, batch, seq):
    n, d = x2.shape
    o_kr = Q_LORA + KV_LORA
    o_u = o_kr + QK_ROPE
    o_g = o_u + 3 * CONV_WIDTH
    kr_cols = jnp.pad(w_in[:, o_kr:o_u], ((0, 0), (QK_NOPE, LANE - QK_HEAD)))
    w_a = jnp.concatenate([w_in[:, :o_kr], kr_cols, w_in[:, o_u:o_g]], axis=1).astype(BF16)
    w_g = w_in[:, o_g:].astype(BF16)
    row = lambda v: v.reshape(1, -1)
    cos_t, sin_p, sin_m = _rope_tables(positions)

    qt, k, vt, cu, bg = _in_projection(
        x2, row(norm1_g), w_a, row(q_a_norm_g), row(kv_a_norm_g),
        _head_slots(w_uq, QK_HEAD).astype(BF16), w_ukv.astype(BF16),
        _pad_cols(row(q_norm_g), LANE), _pad_cols(row(k_norm_g), LANE),
        cos_t, sin_p, sin_m, batch, seq)
    ot = _attention(qt, k, vt)

    rw = _pad_cols(router_w, LANE)
    rw_hi = rw.astype(BF16)
    rw_lo = (rw - rw_hi.astype(F32)).astype(BF16)
    rb = jnp.concatenate([row(router_b), jnp.full((1, LANE - N_EXPERTS), NEG_BIG, F32)], axis=1)
    x1, h2, mi, mw, cnt = _mix(
        x2, row(norm1_g), w_g, ot, cu, bg, conv_w, w_o_mla.astype(BF16), w_o_conv.astype(BF16),
        w_o.astype(BF16), row(norm2_g), rw_hi, rw_lo, rb, batch, seq)

    counts = cnt[0, :N_EXPERTS].astype(jnp.int32)
    padded = (counts + EXPERT_BLOCK - 1) // EXPERT_BLOCK * EXPERT_BLOCK
    pad_end = jnp.cumsum(padded)
    pad_start = (pad_end - padded).astype(jnp.int32)
    nk = n * TOP_K
    n_blocks = (nk + N_EXPERTS * (EXPERT_BLOCK - 1) + EXPERT_BLOCK - 1) // EXPERT_BLOCK
    rows = n_blocks * EXPERT_BLOCK
    block_first_row = jnp.arange(n_blocks, dtype=jnp.int32) * EXPERT_BLOCK
    block_exp = jnp.minimum(jnp.sum(pad_end[None, :] <= block_first_row[:, None], axis=1),
                            N_EXPERTS - 1).astype(jnp.int32)
    n_used = (pad_end[-1:] // EXPERT_BLOCK).astype(jnp.int32)
    e_sel = mi[:, 0:TOP_K, None] == jnp.arange(N_EXPERTS, dtype=jnp.int32)
    dest_flat = (jnp.sum(jnp.where(e_sel, pad_start, 0), axis=-1) + mi[:, TOP_K:2 * TOP_K]).reshape(nk)

    xs = _dispatch(dest_flat, pad_end.astype(jnp.int32), n_used, h2, rows)
    ys = _experts(block_exp, n_used, xs, expert_w1, expert_b1.reshape(N_EXPERTS, 1, -1),
                  expert_w2, expert_b2.reshape(N_EXPERTS, 1, -1))
    return _combine(dest_flat, x1, mw, ys)


def kernel(x, positions, norm1_g, w_in, q_a_norm_g, kv_a_norm_g, w_uq, w_ukv, q_norm_g, k_norm_g, conv_w,
           w_o_mla, w_o_conv, w_o, norm2_g, router_w, router_b, expert_w1, expert_b1, expert_w2, expert_b2):
    batch, seq, d = x.shape
    depth = norm1_g.shape[0]
    x2 = x.reshape(batch * seq, d)
    for l in range(depth):
        x2 = _layer(x2, positions, norm1_g[l], w_in[l], q_a_norm_g[l], kv_a_norm_g[l], w_uq[l], w_ukv[l],
                    q_norm_g[l], k_norm_g[l], conv_w[l], w_o_mla[l], w_o_conv[l], w_o[l], norm2_g[l],
                    router_w[l], router_b[l], expert_w1[l], expert_b1[l], expert_w2[l], expert_b2[l], batch, seq)
    return x2.reshape(batch, seq, d)
```

```python
import functools
import math

import jax
import jax.numpy as jnp
from jax import lax
from jax.experimental import pallas as pl
from jax.experimental.pallas import tpu as pltpu

F32 = jnp.float32
BF16 = jnp.bfloat16

MLA_HEADS = 8
QK_NOPE = 64
QK_ROPE = 32
QK_HEAD = QK_NOPE + QK_ROPE
V_HEAD = 64
Q_LORA = 256
KV_LORA = 128
ROPE_THETA = 10000.0
CONV_WIDTH = 512
N_EXPERTS = 32
TOP_K = 4
SWIGLU_LIMIT = 7.0
SWIGLU_ALPHA = 1.702
EPS = 1e-6

LANE = 128
SUBLANE = 8
HEAD_SLOT = LANE
HALF_ROPE = QK_ROPE // 2
V_ROWS = V_HEAD + 16
VMEM_LIMIT = 56 * 1024 * 1024

TM_PROJ = 512
PROJ_SUBTILES = 2
TQ = 512
TKV = 512
ATTN_UNROLL = 4
TM_MIX = 512
TM_ROUTE = 256
EXPERT_BLOCK = 512
ISSUE_GROUP = 16
NEG_BIG = -1e30


def _load_token_tiles(ref, lead, rows):
    return jnp.concatenate([ref[lead + (pl.ds(c, rows, stride=SUBLANE), slice(None))] for c in range(SUBLANE)],
                           axis=1)


def _store_token_tiles(ref, value):
    rows = value.shape[0]
    for c in range(SUBLANE):
        ref[pl.ds(c, rows, stride=SUBLANE), :] = value[:, c * LANE:(c + 1) * LANE]


def _rms(x, g):
    return x * lax.rsqrt(jnp.mean(x * x, axis=-1, keepdims=True) + EPS) * g


def _rope(t, cos_t, sin_p, sin_m):
    return t * cos_t + pltpu.roll(t, HALF_ROPE, 1) * sin_p + pltpu.roll(t, LANE - HALF_ROPE, 1) * sin_m


def _inproj_kernel(x_ref, g1_ref, w_ref, gq_ref, gkv_ref, wuq_ref, wukv_ref, qgt_ref, kg_ref,
                   cos_ref, sinp_ref, sinm_ref, cost_ref, sint_ref,
                   qt_ref, k_ref, vt_ref, cu_ref, bg_ref):
    tm = x_ref.shape[0] // PROJ_SUBTILES
    for part in range(PROJ_SUBTILES):
        _inproj_rows(slice(part * tm, (part + 1) * tm), x_ref, g1_ref, w_ref, gq_ref, gkv_ref, wuq_ref, wukv_ref,
                     qgt_ref, kg_ref, cos_ref, sinp_ref, sinm_ref, cost_ref, sint_ref,
                     qt_ref, k_ref, vt_ref, cu_ref, bg_ref)


def _inproj_rows(rows, x_ref, g1_ref, w_ref, gq_ref, gkv_ref, wuq_ref, wukv_ref, qgt_ref, kg_ref,
                 cos_ref, sinp_ref, sinm_ref, cost_ref, sint_ref, qt_ref, k_ref, vt_ref, cu_ref, bg_ref):
    x = x_ref[rows, :]
    h = _rms(x, g1_ref[...]).astype(BF16)
    proj = jnp.dot(h, w_ref[...], preferred_element_type=F32)
    c_q = proj[:, 0:Q_LORA]
    c_kv = proj[:, Q_LORA:Q_LORA + KV_LORA]
    kr = proj[:, Q_LORA + KV_LORA:Q_LORA + KV_LORA + LANE]
    o = Q_LORA + KV_LORA + LANE
    u = proj[:, o:o + CONV_WIDTH]
    c_gate = proj[:, o + CONV_WIDTH:o + 2 * CONV_WIDTH]
    b_gate = proj[:, o + 2 * CONV_WIDTH:o + 3 * CONV_WIDTH]
    cu_ref[rows, :] = c_gate * u
    bg_ref[rows, :] = b_gate

    kg = kg_ref[...]
    tm = x.shape[0]
    lane = lax.broadcasted_iota(jnp.int32, (tm, LANE), 1)

    q = jnp.dot(_rms(c_q, gq_ref[...]).astype(BF16), wuq_ref[...], preferred_element_type=F32)
    kv = jnp.dot(_rms(c_kv, gkv_ref[...]).astype(BF16), wukv_ref[...], preferred_element_type=F32)

    ss_r = jnp.sum(kr * kr, axis=-1, keepdims=True)
    kr_roped = _rope(kr * kg, cos_ref[rows, :], sinp_ref[rows, :], sinm_ref[rows, :])
    qgt = qgt_ref[...]
    cos_c = cost_ref[:, rows]
    sin_c = sint_ref[:, rows]
    r1 = slice(QK_NOPE, QK_NOPE + HALF_ROPE)
    r2 = slice(QK_NOPE + HALF_ROPE, QK_HEAD)
    ones = jnp.ones((V_ROWS - V_HEAD, tm), BF16)
    for hd in range(MLA_HEADS):
        qht = q[:, hd * HEAD_SLOT:(hd + 1) * HEAD_SLOT].T
        r = lax.rsqrt(jnp.sum(qht * qht, axis=0, keepdims=True) * (1.0 / QK_HEAD) + EPS)
        qn = qht * r * qgt
        t1, t2 = qn[r1], qn[r2]
        qn = jnp.concatenate([qn[0:QK_NOPE], t1 * cos_c - t2 * sin_c, t1 * sin_c + t2 * cos_c, qn[QK_HEAD:]],
                             axis=0)
        qt_ref[0, hd, :, rows] = qn.astype(BF16)

        kvh = kv[:, hd * HEAD_SLOT:(hd + 1) * HEAD_SLOT]
        knope = jnp.where(lane < QK_NOPE, kvh, 0.0)
        rk = lax.rsqrt((jnp.sum(knope * knope, axis=-1, keepdims=True) + ss_r) * (1.0 / QK_HEAD) + EPS)
        k_ref[0, hd, rows, :] = ((knope * kg + kr_roped) * rk).astype(BF16)
        kvt = kvh.T
        vt_ref[0, hd, 0:V_HEAD, rows] = kvt[QK_NOPE:QK_NOPE + V_HEAD].astype(BF16)
        vt_ref[0, hd, V_HEAD:V_ROWS, rows] = ones


def _in_projection(x2, g1, w_a, gq, gkv, wuq, wukv, qgt, kg, cos_t, sin_p, sin_m, cos_c, sin_c, batch, seq):
    n, d = x2.shape
    tm = TM_PROJ
    spb = seq // tm
    full = lambda shp: pl.BlockSpec(shp, lambda i: (0,) * len(shp))
    return pl.pallas_call(
        _inproj_kernel,
        grid=(n // tm,),
        in_specs=[
            pl.BlockSpec((tm, d), lambda i: (i, 0)),
            full(g1.shape), full(w_a.shape), full(gq.shape), full(gkv.shape), full(wuq.shape), full(wukv.shape),
            full(qgt.shape), full(kg.shape),
            pl.BlockSpec((tm, LANE), lambda i: (i, 0)),
            pl.BlockSpec((tm, LANE), lambda i: (i, 0)),
            pl.BlockSpec((tm, LANE), lambda i: (i, 0)),
            pl.BlockSpec((HALF_ROPE, tm), lambda i: (0, i)),
            pl.BlockSpec((HALF_ROPE, tm), lambda i: (0, i)),
        ],
        out_specs=[
            pl.BlockSpec((1, MLA_HEADS, HEAD_SLOT, tm), lambda i: (i // spb, 0, 0, i % spb)),
            pl.BlockSpec((1, MLA_HEADS, tm, HEAD_SLOT), lambda i: (i // spb, 0, i % spb, 0)),
            pl.BlockSpec((1, MLA_HEADS, V_ROWS, tm), lambda i: (i // spb, 0, 0, i % spb)),
            pl.BlockSpec((tm, CONV_WIDTH), lambda i: (i, 0)),
            pl.BlockSpec((tm, CONV_WIDTH), lambda i: (i, 0)),
        ],
        out_shape=[
            jax.ShapeDtypeStruct((batch, MLA_HEADS, HEAD_SLOT, seq), BF16),
            jax.ShapeDtypeStruct((batch, MLA_HEADS, seq, HEAD_SLOT), BF16),
            jax.ShapeDtypeStruct((batch, MLA_HEADS, V_ROWS, seq), BF16),
            jax.ShapeDtypeStruct((n, CONV_WIDTH), F32),
            jax.ShapeDtypeStruct((n, CONV_WIDTH), F32),
        ],
        compiler_params=pltpu.CompilerParams(dimension_semantics=("parallel",), vmem_limit_bytes=VMEM_LIMIT),
        name="in_projection",
    )(x2, g1, w_a, gq, gkv, wuq, wukv, qgt, kg, cos_t, sin_p, sin_m, cos_c, sin_c)


def _attn_kernel(qt_ref, k_ref, vt_ref, o_ref, sa_ref, sb_ref, m_ref, acc_ref, *, tq, tk):
    seq = k_ref.shape[2]
    nk = seq // tk
    bufs = (sa_ref, sb_ref)

    def query_tile(qi, carry):
        q0 = pl.multiple_of(qi * tq, tq)
        qt = qt_ref[0, 0, :, pl.ds(q0, tq)]

        def scores(c, s_ref):
            k0 = pl.multiple_of(c * tk, tk)
            s_ref[...] = jnp.dot(k_ref[0, 0, pl.ds(k0, tk), :], qt, preferred_element_type=F32)

        def accumulate(c, s_ref):
            k0 = pl.multiple_of(c * tk, tk)
            s = s_ref[...]
            m = m_ref[...]
            m_new = jnp.maximum(m, jnp.max(s, axis=0, keepdims=True))
            m_ref[...] = m_new
            p = jnp.exp2(s - m_new).astype(BF16)
            vs = vt_ref[0, 0, :, pl.ds(k0, tk)]
            acc_ref[...] = jnp.exp2(m - m_new) * acc_ref[...] + jnp.dot(vs, p, preferred_element_type=F32)

        m_ref[...] = jnp.full(m_ref.shape, NEG_BIG, F32)
        acc_ref[...] = jnp.zeros(acc_ref.shape, F32)
        scores(0, sa_ref)

        def group(j, c):
            base = ATTN_UNROLL * j
            for u in range(ATTN_UNROLL):
                scores(base + u + 1, bufs[(u + 1) % 2])
                accumulate(base + u, bufs[u % 2])
            return c

        lax.fori_loop(0, nk // ATTN_UNROLL - 1, group, 0)
        base = nk - ATTN_UNROLL
        for u in range(ATTN_UNROLL):
            if u + 1 < ATTN_UNROLL:
                scores(base + u + 1, bufs[(u + 1) % 2])
            accumulate(base + u, bufs[u % 2])
        acc = acc_ref[...]
        o_ref[0, :, pl.ds(q0, tq)] = (acc[0:V_HEAD] * (1.0 / acc[V_HEAD:V_HEAD + 1])).astype(BF16)
        return carry

    lax.fori_loop(0, seq // tq, query_tile, 0)


def _attention(qt, k, vt):
    batch, heads, _, seq = qt.shape
    assert ATTN_UNROLL % 2 == 0 and (seq // TKV) % ATTN_UNROLL == 0 and seq % TQ == 0
    return pl.pallas_call(
        functools.partial(_attn_kernel, tq=TQ, tk=TKV),
        grid=(batch, heads),
        in_specs=[
            pl.BlockSpec((1, 1, HEAD_SLOT, seq), lambda b, h: (b, h, 0, 0)),
            pl.BlockSpec((1, 1, seq, HEAD_SLOT), lambda b, h: (b, h, 0, 0)),
            pl.BlockSpec((1, 1, V_ROWS, seq), lambda b, h: (b, h, 0, 0)),
        ],
        out_specs=pl.BlockSpec((1, V_HEAD, seq), lambda b, h: (b, h, 0)),
        out_shape=jax.ShapeDtypeStruct((batch, heads * V_HEAD, seq), BF16),
        scratch_shapes=[pltpu.VMEM((TKV, TQ), F32), pltpu.VMEM((TKV, TQ), F32),
                        pltpu.VMEM((1, TQ), F32), pltpu.VMEM((V_ROWS, TQ), F32)],
        compiler_params=pltpu.CompilerParams(
            dimension_semantics=("parallel", "parallel"), vmem_limit_bytes=VMEM_LIMIT),
        name="attention",
    )(qt, k, vt)


def _mix_kernel(x_ref, g1_ref, wg_ref, ot_ref, cu_ref, cup_ref, cun_ref, bg_ref, cw_ref,
                woa_ref, wob_ref, wo_ref, g2_ref, rwh_ref, rwl_ref, rb_ref,
                x1_ref, h2_ref, mi_ref, mw_ref, cnt_ref, tri_ref, carry_ref, *, steps_per_seq):
    i = pl.program_id(0)
    tm = x_ref.shape[0]

    @pl.when(i == 0)
    def _():
        r = lax.broadcasted_iota(jnp.int32, (tm, tm), 0)
        c = lax.broadcasted_iota(jnp.int32, (tm, tm), 1)
        tri_ref[...] = jnp.where(c < r, 1.0, 0.0).astype(BF16)
        carry_ref[...] = jnp.zeros_like(carry_ref)

    x = x_ref[...]
    h = _rms(x, g1_ref[...]).astype(BF16)
    gates = jnp.dot(h, wg_ref[...], preferred_element_type=F32)
    d = x.shape[1]
    sig_a = 1.0 / (1.0 + jnp.exp(-gates[:, 0:d]))
    sig_b = 1.0 / (1.0 + jnp.exp(-gates[:, d:2 * d]))

    y_a = lax.dot_general(ot_ref[0], woa_ref[...], (((0,), (0,)), ((), ())), preferred_element_type=F32)

    cu = cu_ref[...]
    row = lax.broadcasted_iota(jnp.int32, cu.shape, 0)
    s_in_seq = i % steps_per_seq
    prev_row = jnp.where(s_in_seq == 0, 0.0, cup_ref[7:8, :])
    next_row = jnp.where(s_in_seq == steps_per_seq - 1, 0.0, cun_ref[0:1, :])
    below = jnp.where(row == 0, prev_row, pltpu.roll(cu, 1, 0))
    above = jnp.where(row == tm - 1, next_row, pltpu.roll(cu, tm - 1, 0))
    cw = cw_ref[...]
    conv = cw[0:1, :] * below + cw[1:2, :] * cu + cw[2:3, :] * above
    y_b = jnp.dot((bg_ref[...] * conv).astype(BF16), wob_ref[...], preferred_element_type=F32)

    merged = (sig_a * y_a + sig_b * y_b).astype(BF16)
    x1 = x + jnp.dot(merged, wo_ref[...], preferred_element_type=F32)
    x1_ref[...] = x1
    h2 = _rms(x1, g2_ref[...])
    _store_token_tiles(h2_ref, h2)

    h2_hi = h2.astype(BF16)
    h2_lo = (h2 - h2_hi.astype(F32)).astype(BF16)
    logits = (jnp.dot(h2_hi, rwh_ref[...], preferred_element_type=F32)
              + jnp.dot(h2_lo, rwh_ref[...], preferred_element_type=F32)
              + jnp.dot(h2_hi, rwl_ref[...], preferred_element_type=F32)
              + rb_ref[...])

    lane = lax.broadcasted_iota(jnp.int32, logits.shape, 1)
    lane_f = lane.astype(F32)
    work = logits
    vals, idxs, hits = [], [], []
    for _ in range(TOP_K):
        mx = jnp.max(work, axis=-1, keepdims=True)
        idx = jnp.min(jnp.where(work == mx, lane_f, float(LANE)), axis=-1, keepdims=True)
        hit = lane_f == idx
        work = jnp.where(hit, -jnp.inf, work)
        vals.append(mx)
        idxs.append(idx)
        hits.append(hit)
    exps = [jnp.exp(v - vals[0]) for v in vals]
    denom = exps[0] + exps[1] + exps[2] + exps[3]
    inv = 1.0 / denom

    sel = jnp.zeros(logits.shape, F32)
    for hit in hits:
        sel = sel + jnp.where(hit, 1.0, 0.0)
    carry = carry_ref[...]
    before = jnp.dot(tri_ref[...], sel.astype(BF16), preferred_element_type=F32) + carry
    carry_new = carry + jnp.sum(sel, axis=0, keepdims=True)
    carry_ref[...] = carry_new
    cnt_ref[...] = jnp.broadcast_to(carry_new, cnt_ref.shape)

    mi = jnp.zeros(logits.shape, F32)
    mw = jnp.zeros(logits.shape, F32)
    for k in range(TOP_K):
        rank = jnp.sum(jnp.where(hits[k], before, 0.0), axis=-1, keepdims=True)
        mi = jnp.where(lane == k, idxs[k], mi)
        mi = jnp.where(lane == TOP_K + k, rank, mi)
        mw = jnp.where(lane == k, exps[k] * inv, mw)
    mi_ref[...] = mi.astype(jnp.int32)
    mw_ref[...] = mw


def _mix(x2, g1, w_g, ot, cu, bg, conv_w, woa, wob, wo, g2, rw_hi, rw_lo, rb, batch, seq):
    n, d = x2.shape
    tm = TM_MIX
    spb = seq // tm
    r8 = tm // 8
    nsteps = n // tm
    full = lambda shp: pl.BlockSpec(shp, lambda i: (0,) * len(shp))
    return pl.pallas_call(
        functools.partial(_mix_kernel, steps_per_seq=spb),
        grid=(nsteps,),
        in_specs=[
            pl.BlockSpec((tm, d), lambda i: (i, 0)),
            full(g1.shape), full(w_g.shape),
            pl.BlockSpec((1, MLA_HEADS * V_HEAD, tm), lambda i: (i // spb, 0, i % spb)),
            pl.BlockSpec((tm, CONV_WIDTH), lambda i: (i, 0)),
            pl.BlockSpec((8, CONV_WIDTH), lambda i: (jnp.maximum(i * r8 - 1, 0), 0)),
            pl.BlockSpec((8, CONV_WIDTH), lambda i: (jnp.minimum((i + 1) * r8, nsteps * r8 - 1), 0)),
            pl.BlockSpec((tm, CONV_WIDTH), lambda i: (i, 0)),
            full(conv_w.shape), full(woa.shape), full(wob.shape), full(wo.shape), full(g2.shape),
            full(rw_hi.shape), full(rw_lo.shape), full(rb.shape),
        ],
        out_specs=[
            pl.BlockSpec((tm, d), lambda i: (i, 0)),
            pl.BlockSpec((tm * SUBLANE, LANE), lambda i: (i, 0)),
            pl.BlockSpec((tm, LANE), lambda i: (i, 0)),
            pl.BlockSpec((tm, LANE), lambda i: (i, 0)),
            pl.BlockSpec((8, LANE), lambda i: (0, 0)),
        ],
        out_shape=[
            jax.ShapeDtypeStruct((n, d), F32),
            jax.ShapeDtypeStruct((n * SUBLANE, LANE), F32),
            jax.ShapeDtypeStruct((n, LANE), jnp.int32),
            jax.ShapeDtypeStruct((n, LANE), F32),
            jax.ShapeDtypeStruct((8, LANE), F32),
        ],
        scratch_shapes=[pltpu.VMEM((tm, tm), BF16), pltpu.VMEM((1, LANE), F32)],
        compiler_params=pltpu.CompilerParams(dimension_semantics=("arbitrary",), vmem_limit_bytes=VMEM_LIMIT),
        name="mix_route",
    )(x2, g1, w_g, ot, cu, cu, cu, bg, conv_w, woa, wob, wo, g2, rw_hi, rw_lo, rb)


def _row_copy_wait(src_like, dst_like, sem, times):
    for _ in range(times):
        pltpu.make_async_copy(src_like, dst_like, sem).wait()


def _token(ref, idx):
    return ref.at[pl.ds(pl.multiple_of(idx * SUBLANE, SUBLANE), SUBLANE)]


def _dispatch_kernel(dest_ref, pe_ref, nused_ref, h2_ref, xs_ref, zero_ref, sem, zsem):
    tm = h2_ref.shape[0] // SUBLANE
    block_rows = EXPERT_BLOCK * SUBLANE
    n_blocks = xs_ref.shape[0] // block_rows

    @pl.when(pl.program_id(0) == 0)
    def _():
        zero_ref[...] = jnp.zeros_like(zero_ref)

        def zero_copy(first_token):
            start = pl.multiple_of(first_token * SUBLANE, block_rows)
            return pltpu.make_async_copy(zero_ref, xs_ref.at[pl.ds(start, block_rows)], zsem)

        def has_rows(e):
            return pe_ref[e] > jnp.where(e == 0, 0, pe_ref[jnp.maximum(e - 1, 0)])

        def start_expert(e, c):
            @pl.when(has_rows(e))
            def _():
                zero_copy(pe_ref[e] - EXPERT_BLOCK).start()
            return c

        def wait_expert(e, c):
            @pl.when(has_rows(e))
            def _():
                zero_copy(pe_ref[e] - EXPERT_BLOCK).wait()
            return c

        def start_tail(b, c):
            zero_copy(b * EXPERT_BLOCK).start()
            return c

        def wait_tail(b, c):
            zero_copy(b * EXPERT_BLOCK).wait()
            return c

        lax.fori_loop(0, N_EXPERTS, start_expert, 0)
        lax.fori_loop(nused_ref[0], n_blocks, start_tail, 0)
        lax.fori_loop(0, N_EXPERTS, wait_expert, 0)
        lax.fori_loop(nused_ref[0], n_blocks, wait_tail, 0)

    def issue(g, c):
        for u in range(ISSUE_GROUP):
            t = g * (ISSUE_GROUP // TOP_K) + u // TOP_K
            dest = dest_ref[g * ISSUE_GROUP + u]
            pltpu.make_async_copy(_token(h2_ref, t), _token(xs_ref, dest), sem).start(priority=u % 2)
        return c

    lax.fori_loop(0, tm * TOP_K // ISSUE_GROUP, issue, 0)
    _row_copy_wait(h2_ref, xs_ref.at[pl.ds(0, tm * SUBLANE)], sem, TOP_K)


def _dispatch(dest_flat, pad_end, n_used, h2t, rows):
    tm = TM_ROUTE
    n = h2t.shape[0] // SUBLANE
    return pl.pallas_call(
        _dispatch_kernel,
        grid=(n // tm,),
        in_specs=[
            pl.BlockSpec((tm * TOP_K,), lambda i: (i,), memory_space=pltpu.SMEM),
            pl.BlockSpec(memory_space=pltpu.SMEM),
            pl.BlockSpec(memory_space=pltpu.SMEM),
            pl.BlockSpec((tm * SUBLANE, LANE), lambda i: (i, 0)),
        ],
        out_specs=pl.BlockSpec(memory_space=pl.ANY),
        out_shape=jax.ShapeDtypeStruct((rows * SUBLANE, LANE), F32),
        scratch_shapes=[pltpu.VMEM((EXPERT_BLOCK * SUBLANE, LANE), F32), pltpu.SemaphoreType.DMA(()),
                        pltpu.SemaphoreType.DMA(())],
        compiler_params=pltpu.CompilerParams(dimension_semantics=("arbitrary",), vmem_limit_bytes=VMEM_LIMIT),
        name="dispatch",
    )(dest_flat, pad_end, n_used, h2t)


def _expert_kernel(bexp_ref, nused_ref, xs_ref, w1_ref, b1_ref, w2_ref, b2_ref, ys_ref, w1b_ref, w2b_ref):
    i = pl.program_id(0)
    active = i < nused_ref[0]
    prev = bexp_ref[jnp.maximum(i - 1, 0)]
    fresh = jnp.logical_or(i == 0, bexp_ref[i] != prev)

    @pl.when(jnp.logical_and(active, fresh))
    def _():
        w1b_ref[...] = w1_ref[0].astype(BF16)
        w2b_ref[...] = w2_ref[0].astype(BF16)

    @pl.when(active)
    def _():
        dff = w2b_ref.shape[0]
        xb = _load_token_tiles(xs_ref, (), EXPERT_BLOCK).astype(BF16)
        hm = jnp.dot(xb, w1b_ref[...], preferred_element_type=F32) + b1_ref[0]
        gate = jnp.minimum(hm[:, 0:dff], SWIGLU_LIMIT)
        up = jnp.clip(hm[:, dff:2 * dff], -SWIGLU_LIMIT, SWIGLU_LIMIT)
        glu = gate * (1.0 / (1.0 + jnp.exp(-SWIGLU_ALPHA * gate)))
        act = ((up + 1.0) * glu).astype(BF16)
        _store_token_tiles(ys_ref, jnp.dot(act, w2b_ref[...], preferred_element_type=F32) + b2_ref[0])

    @pl.when(jnp.logical_not(active))
    def _():
        ys_ref[...] = jnp.zeros_like(ys_ref)


def _experts(block_exp, n_used, xs, w1, b1, w2, b2):
    d = w1.shape[1]
    assert d == SUBLANE * LANE
    block_rows = EXPERT_BLOCK * SUBLANE
    n_blocks = xs.shape[0] // block_rows
    dff2 = w1.shape[2]
    dff = w2.shape[1]
    grid_spec = pltpu.PrefetchScalarGridSpec(
        num_scalar_prefetch=2,
        grid=(n_blocks,),
        in_specs=[
            pl.BlockSpec((block_rows, LANE), lambda i, be, nu: (jnp.minimum(i, nu[0] - 1), 0)),
            pl.BlockSpec((1, d, dff2), lambda i, be, nu: (be[i], 0, 0)),
            pl.BlockSpec((1, 1, dff2), lambda i, be, nu: (be[i], 0, 0)),
            pl.BlockSpec((1, dff, d), lambda i, be, nu: (be[i], 0, 0)),
            pl.BlockSpec((1, 1, d), lambda i, be, nu: (be[i], 0, 0)),
        ],
        out_specs=pl.BlockSpec((block_rows, LANE), lambda i, be, nu: (i, 0)),
        scratch_shapes=[pltpu.VMEM((d, dff2), BF16), pltpu.VMEM((dff, d), BF16)],
    )
    return pl.pallas_call(
        _expert_kernel,
        grid_spec=grid_spec,
        out_shape=jax.ShapeDtypeStruct(xs.shape, F32),
        compiler_params=pltpu.CompilerParams(dimension_semantics=("arbitrary",), vmem_limit_bytes=VMEM_LIMIT),
        name="experts",
    )(block_exp, n_used, xs, w1, b1, w2, b2)


def _combine_kernel(dest_ref, dest_next_ref, x1_ref, mw_ref, ys_ref, out_ref, buf_ref, sem):
    i = pl.program_id(0)
    tm = x1_ref.shape[0]
    slot = lax.rem(i, 2)

    def gather(idx_ref, s):
        def issue(g, c):
            for u in range(ISSUE_GROUP):
                t = g * (ISSUE_GROUP // TOP_K) + u // TOP_K
                dest = idx_ref[g * ISSUE_GROUP + u]
                pltpu.make_async_copy(_token(ys_ref, dest), _token(buf_ref.at[s, u % TOP_K], t),
                                      sem.at[s]).start(priority=u % 2)
            return c

        lax.fori_loop(0, tm * TOP_K // ISSUE_GROUP, issue, 0)

    @pl.when(i == 0)
    def _():
        gather(dest_ref, 0)

    @pl.when(i + 1 < pl.num_programs(0))
    def _():
        gather(dest_next_ref, 1 - slot)

    _row_copy_wait(ys_ref.at[pl.ds(0, tm * SUBLANE)], buf_ref.at[slot, 0], sem.at[slot], TOP_K)
    acc = x1_ref[...]
    mw = mw_ref[...]
    for k in range(TOP_K):
        acc = acc + mw[:, k:k + 1] * _load_token_tiles(buf_ref, (slot, k), tm)
    out_ref[...] = acc


def _combine(dest_flat, x1, mw, ys):
    n, d = x1.shape
    tm = TM_ROUTE
    nsteps = n // tm
    return pl.pallas_call(
        _combine_kernel,
        grid=(nsteps,),
        in_specs=[
            pl.BlockSpec((tm * TOP_K,), lambda i: (i,), memory_space=pltpu.SMEM),
            pl.BlockSpec((tm * TOP_K,), lambda i: (jnp.minimum(i + 1, nsteps - 1),), memory_space=pltpu.SMEM),
            pl.BlockSpec((tm, d), lambda i: (i, 0)),
            pl.BlockSpec((tm, LANE), lambda i: (i, 0)),
            pl.BlockSpec(memory_space=pl.ANY),
        ],
        out_specs=pl.BlockSpec((tm, d), lambda i: (i, 0)),
        out_shape=jax.ShapeDtypeStruct((n, d), F32),
        scratch_shapes=[pltpu.VMEM((2, TOP_K, tm * SUBLANE, LANE), F32), pltpu.SemaphoreType.DMA((2,))],
        compiler_params=pltpu.CompilerParams(dimension_semantics=("arbitrary",), vmem_limit_bytes=VMEM_LIMIT),
        name="combine",
    )(dest_flat, dest_flat, x1, mw, ys)


def _pad_cols(w, width):
    return jnp.pad(w, ((0, 0), (0, width - w.shape[1])))


def _head_slots(w, per_head):
    rows = w.shape[0]
    w3 = w.reshape(rows, MLA_HEADS, per_head)
    return jnp.pad(w3, ((0, 0), (0, 0), (0, HEAD_SLOT - per_head))).reshape(rows, MLA_HEADS * HEAD_SLOT)


def _rope_tables(positions):
    inv_freq = ROPE_THETA ** (-jnp.arange(0, QK_ROPE, 2, dtype=F32) / QK_ROPE)
    ang = positions.astype(F32).reshape(-1, 1) * inv_freq
    cos, sin = jnp.cos(ang), jnp.sin(ang)
    n = ang.shape[0]
    z = lambda w: jnp.zeros((n, w), F32)
    cos_t = jnp.concatenate([jnp.ones((n, QK_NOPE), F32), cos, cos, z(LANE - QK_HEAD)], axis=1)
    sin_p = jnp.concatenate([z(QK_NOPE + HALF_ROPE), sin, z(LANE - QK_HEAD)], axis=1)
    sin_m = jnp.concatenate([z(QK_NOPE), -sin, z(LANE - QK_NOPE - HALF_ROPE)], axis=1)
    return cos_t, sin_p, sin_m, cos.T, sin.T


def _layer(x2, positions, norm1_g, w_in, q_a_norm_g, kv_a_norm_g, w_uq, w_ukv, q_norm_g, k_norm_g,
           conv_w, w_o_mla, w_o_conv, w_o, norm2_g, router_w, router_b,
           expert_w1, expert_b1, expert_w2, expert_b2, batch, seq):
    n, d = x2.shape
    o_kr = Q_LORA + KV_LORA
    o_u = o_kr + QK_ROPE
    o_g = o_u + 3 * CONV_WIDTH
    kr_cols = jnp.pad(w_in[:, o_kr:o_u], ((0, 0), (QK_NOPE, LANE - QK_HEAD)))
    w_a = jnp.concatenate([w_in[:, :o_kr], kr_cols, w_in[:, o_u:o_g]], axis=1).astype(BF16)
    w_g = w_in[:, o_g:].astype(BF16)
    row = lambda v: v.reshape(1, -1)
    cos_t, sin_p, sin_m, cos_c, sin_c = _rope_tables(positions)
    q_scale = (QK_HEAD ** -0.5) * math.log2(math.e)
    qgt = jnp.broadcast_to(_pad_cols(row(q_norm_g) * q_scale, LANE).reshape(LANE, 1),
                           (LANE, TM_PROJ // PROJ_SUBTILES))

    qt, k, vt, cu, bg = _in_projection(
        x2, row(norm1_g), w_a, row(q_a_norm_g), row(kv_a_norm_g),
        _head_slots(w_uq, QK_HEAD).astype(BF16), w_ukv.astype(BF16),
        qgt, _pad_cols(row(k_norm_g), LANE),
        cos_t, sin_p, sin_m, cos_c, sin_c, batch, seq)
    ot = _attention(qt, k, vt)

    rw = _pad_cols(router_w, LANE)
    rw_hi = rw.astype(BF16)
    rw_lo = (rw - rw_hi.astype(F32)).astype(BF16)
    rb = jnp.concatenate([row(router_b), jnp.full((1, LANE - N_EXPERTS), NEG_BIG, F32)], axis=1)
    x1, h2, mi, mw, cnt = _mix(
        x2, row(norm1_g), w_g, ot, cu, bg, conv_w, w_o_mla.astype(BF16), w_o_conv.astype(BF16),
        w_o.astype(BF16), row(norm2_g), rw_hi, rw_lo, rb, batch, seq)

    counts = cnt[0, :N_EXPERTS].astype(jnp.int32)
    padded = (counts + EXPERT_BLOCK - 1) // EXPERT_BLOCK * EXPERT_BLOCK
    pad_end = jnp.cumsum(padded)
    pad_start = (pad_end - padded).astype(jnp.int32)
    nk = n * TOP_K
    n_blocks = (nk + N_EXPERTS * (EXPERT_BLOCK - 1) + EXPERT_BLOCK - 1) // EXPERT_BLOCK
    rows = n_blocks * EXPERT_BLOCK
    block_first_row = jnp.arange(n_blocks, dtype=jnp.int32) * EXPERT_BLOCK
    block_exp = jnp.minimum(jnp.sum(pad_end[None, :] <= block_first_row[:, None], axis=1),
                            N_EXPERTS - 1).astype(jnp.int32)
    n_used = (pad_end[-1:] // EXPERT_BLOCK).astype(jnp.int32)
    e_sel = mi[:, 0:TOP_K, None] == jnp.arange(N_EXPERTS, dtype=jnp.int32)
    dest_flat = (jnp.sum(jnp.where(e_sel, pad_start, 0), axis=-1) + mi[:, TOP_K:2 * TOP_K]).reshape(nk)

    xs = _dispatch(dest_flat, pad_end.astype(jnp.int32), n_used, h2, rows)
    ys = _experts(block_exp, n_used, xs, expert_w1, expert_b1.reshape(N_EXPERTS, 1, -1),
                  expert_w2, expert_b2.reshape(N_EXPERTS, 1, -1))
    return _combine(dest_flat, x1, mw, ys)


def kernel(x, positions, norm1_g, w_in, q_a_norm_g, kv_a_norm_g, w_uq, w_ukv, q_norm_g, k_norm_g, conv_w,
           w_o_mla, w_o_conv, w_o, norm2_g, router_w, router_b, expert_w1, expert_b1, expert_w2, expert_b2):
    batch, seq, d = x.shape
    depth = norm1_g.shape[0]
    x2 = x.reshape(batch * seq, d)
    for l in range(depth):
        x2 = _layer(x2, positions, norm1_g[l], w_in[l], q_a_norm_g[l], kv_a_norm_g[l], w_uq[l], w_ukv[l],
                    q_norm_g[l], k_norm_g[l], conv_w[l], w_o_mla[l], w_o_conv[l], w_o[l], norm2_g[l],
                    router_w[l], router_b[l], expert_w1[l], expert_b1[l], expert_w2[l], expert_b2[l], batch, seq)
    return x2.reshape(batch, seq, d)
```

```python
import functools
import math

import jax
import jax.numpy as jnp
from jax import lax
from jax.experimental import pallas as pl
from jax.experimental.pallas import tpu as pltpu

F32 = jnp.float32
BF16 = jnp.bfloat16

MLA_HEADS = 8
QK_NOPE = 64
QK_ROPE = 32
QK_HEAD = QK_NOPE + QK_ROPE
V_HEAD = 64
Q_LORA = 256
KV_LORA = 128
ROPE_THETA = 10000.0
CONV_WIDTH = 512
N_EXPERTS = 32
TOP_K = 4
SWIGLU_LIMIT = 7.0
SWIGLU_ALPHA = 1.702
EPS = 1e-6

LANE = 128
SUBLANE = 8
HEAD_SLOT = LANE
HALF_ROPE = QK_ROPE // 2
V_ROWS = V_HEAD + 16
VMEM_LIMIT = 56 * 1024 * 1024

TM_PROJ = 512
PROJ_SUBTILES = 2
TQ = 512
TKV = 512
ATTN_UNROLL = 4
TM_MIX = 512
TM_ROUTE = 256
EXPERT_BLOCK = 512
ISSUE_GROUP = 16
NEG_BIG = -1e30


def _load_token_tiles(ref, lead, rows):
    return jnp.concatenate([ref[lead + (pl.ds(c, rows, stride=SUBLANE), slice(None))] for c in range(SUBLANE)],
                           axis=1)


def _store_token_tiles(ref, value):
    rows = value.shape[0]
    for c in range(SUBLANE):
        ref[pl.ds(c, rows, stride=SUBLANE), :] = value[:, c * LANE:(c + 1) * LANE]


def _rms(x, g):
    return x * lax.rsqrt(jnp.mean(x * x, axis=-1, keepdims=True) + EPS) * g


def _inproj_kernel(x_ref, g1_ref, w_ref, gq_ref, gkv_ref, wuq_ref, wukv_ref, qgt_ref, kg_ref, kgt_ref,
                   cost_ref, sint_ref, qt_ref, k_ref, vt_ref, cu_ref, bg_ref):
    tm = x_ref.shape[0] // PROJ_SUBTILES
    for part in range(PROJ_SUBTILES):
        _inproj_rows(slice(part * tm, (part + 1) * tm), x_ref, g1_ref, w_ref, gq_ref, gkv_ref, wuq_ref, wukv_ref,
                     qgt_ref, kg_ref, kgt_ref, cost_ref, sint_ref, qt_ref, k_ref, vt_ref, cu_ref, bg_ref)


def _inproj_rows(rows, x_ref, g1_ref, w_ref, gq_ref, gkv_ref, wuq_ref, wukv_ref, qgt_ref, kg_ref, kgt_ref,
                 cost_ref, sint_ref, qt_ref, k_ref, vt_ref, cu_ref, bg_ref):
    x = x_ref[rows, :]
    h = _rms(x, g1_ref[...]).astype(BF16)
    proj = jnp.dot(h, w_ref[...], preferred_element_type=F32)
    c_q = proj[:, 0:Q_LORA]
    c_kv = proj[:, Q_LORA:Q_LORA + KV_LORA]
    kr = proj[:, Q_LORA + KV_LORA:Q_LORA + KV_LORA + LANE]
    o = Q_LORA + KV_LORA + LANE
    u = proj[:, o:o + CONV_WIDTH]
    c_gate = proj[:, o + CONV_WIDTH:o + 2 * CONV_WIDTH]
    b_gate = proj[:, o + 2 * CONV_WIDTH:o + 3 * CONV_WIDTH]
    cu_ref[rows, :] = c_gate * u
    bg_ref[rows, :] = b_gate

    kg = kg_ref[...]
    tm = x.shape[0]
    lane = lax.broadcasted_iota(jnp.int32, (tm, LANE), 1)

    q = jnp.dot(_rms(c_q, gq_ref[...]).astype(BF16), wuq_ref[...], preferred_element_type=F32)
    kv = jnp.dot(_rms(c_kv, gkv_ref[...]).astype(BF16), wukv_ref[...], preferred_element_type=F32)

    cos_c = cost_ref[:, rows]
    sin_c = sint_ref[:, rows]

    def rope_t(t):
        t1 = t[QK_NOPE:QK_NOPE + HALF_ROPE]
        t2 = t[QK_NOPE + HALF_ROPE:QK_HEAD]
        return jnp.concatenate([t[0:QK_NOPE], t1 * cos_c - t2 * sin_c, t1 * sin_c + t2 * cos_c, t[QK_HEAD:]], axis=0)

    ss_r = jnp.sum(kr * kr, axis=-1, keepdims=True)
    kr_roped = rope_t(kr.T * kgt_ref[...]).T
    qgt = qgt_ref[...]
    ones = jnp.ones((V_ROWS - V_HEAD, tm), BF16)
    for hd in range(MLA_HEADS):
        qht = q[:, hd * HEAD_SLOT:(hd + 1) * HEAD_SLOT].T
        r = lax.rsqrt(jnp.sum(qht * qht, axis=0, keepdims=True) * (1.0 / QK_HEAD) + EPS)
        qt_ref[0, hd, :, rows] = rope_t(qht * r * qgt).astype(BF16)

        kvh = kv[:, hd * HEAD_SLOT:(hd + 1) * HEAD_SLOT]
        knope = jnp.where(lane < QK_NOPE, kvh, 0.0)
        rk = lax.rsqrt((jnp.sum(knope * knope, axis=-1, keepdims=True) + ss_r) * (1.0 / QK_HEAD) + EPS)
        k_ref[0, hd, rows, :] = ((knope * kg + kr_roped) * rk).astype(BF16)
        kvt = kvh.T
        vt_ref[0, hd, 0:V_HEAD, rows] = kvt[QK_NOPE:QK_NOPE + V_HEAD].astype(BF16)
        vt_ref[0, hd, V_HEAD:V_ROWS, rows] = ones


def _in_projection(x2, g1, w_a, gq, gkv, wuq, wukv, qgt, kg, kgt, cos_c, sin_c, batch, seq):
    n, d = x2.shape
    tm = TM_PROJ
    spb = seq // tm
    full = lambda shp: pl.BlockSpec(shp, lambda i: (0,) * len(shp))
    return pl.pallas_call(
        _inproj_kernel,
        grid=(n // tm,),
        in_specs=[
            pl.BlockSpec((tm, d), lambda i: (i, 0)),
            full(g1.shape), full(w_a.shape), full(gq.shape), full(gkv.shape), full(wuq.shape), full(wukv.shape),
            full(qgt.shape), full(kg.shape), full(kgt.shape),
            pl.BlockSpec((HALF_ROPE, tm), lambda i: (0, i)),
            pl.BlockSpec((HALF_ROPE, tm), lambda i: (0, i)),
        ],
        out_specs=[
            pl.BlockSpec((1, MLA_HEADS, HEAD_SLOT, tm), lambda i: (i // spb, 0, 0, i % spb)),
            pl.BlockSpec((1, MLA_HEADS, tm, HEAD_SLOT), lambda i: (i // spb, 0, i % spb, 0)),
            pl.BlockSpec((1, MLA_HEADS, V_ROWS, tm), lambda i: (i // spb, 0, 0, i % spb)),
            pl.BlockSpec((tm, CONV_WIDTH), lambda i: (i, 0)),
            pl.BlockSpec((tm, CONV_WIDTH), lambda i: (i, 0)),
        ],
        out_shape=[
            jax.ShapeDtypeStruct((batch, MLA_HEADS, HEAD_SLOT, seq), BF16),
            jax.ShapeDtypeStruct((batch, MLA_HEADS, seq, HEAD_SLOT), BF16),
            jax.ShapeDtypeStruct((batch, MLA_HEADS, V_ROWS, seq), BF16),
            jax.ShapeDtypeStruct((n, CONV_WIDTH), F32),
            jax.ShapeDtypeStruct((n, CONV_WIDTH), F32),
        ],
        compiler_params=pltpu.CompilerParams(dimension_semantics=("parallel",), vmem_limit_bytes=VMEM_LIMIT),
        name="in_projection",
    )(x2, g1, w_a, gq, gkv, wuq, wukv, qgt, kg, kgt, cos_c, sin_c)


def _attn_kernel(qt_ref, k_ref, vt_ref, o_ref, sa_ref, sb_ref, m_ref, acc_ref, *, tq, tk):
    seq = k_ref.shape[2]
    nk = seq // tk
    bufs = (sa_ref, sb_ref)

    def query_tile(qi, carry):
        q0 = pl.multiple_of(qi * tq, tq)
        qt = qt_ref[0, 0, :, pl.ds(q0, tq)]

        def scores(c, s_ref):
            k0 = pl.multiple_of(c * tk, tk)
            s_ref[...] = jnp.dot(k_ref[0, 0, pl.ds(k0, tk), :], qt, preferred_element_type=F32)

        def accumulate(c, s_ref):
            k0 = pl.multiple_of(c * tk, tk)
            s = s_ref[...]
            m = m_ref[...]
            m_new = jnp.maximum(m, jnp.max(s, axis=0, keepdims=True))
            m_ref[...] = m_new
            p = jnp.exp2(s - m_new).astype(BF16)
            vs = vt_ref[0, 0, :, pl.ds(k0, tk)]
            acc_ref[...] = jnp.exp2(m - m_new) * acc_ref[...] + jnp.dot(vs, p, preferred_element_type=F32)

        m_ref[...] = jnp.full(m_ref.shape, NEG_BIG, F32)
        acc_ref[...] = jnp.zeros(acc_ref.shape, F32)
        scores(0, sa_ref)

        def group(j, c):
            base = ATTN_UNROLL * j
            for u in range(ATTN_UNROLL):
                scores(base + u + 1, bufs[(u + 1) % 2])
                accumulate(base + u, bufs[u % 2])
            return c

        lax.fori_loop(0, nk // ATTN_UNROLL - 1, group, 0)
        base = nk - ATTN_UNROLL
        for u in range(ATTN_UNROLL):
            if u + 1 < ATTN_UNROLL:
                scores(base + u + 1, bufs[(u + 1) % 2])
            accumulate(base + u, bufs[u % 2])
        acc = acc_ref[...]
        o_ref[0, :, pl.ds(q0, tq)] = (acc[0:V_HEAD] * (1.0 / acc[V_HEAD:V_HEAD + 1])).astype(BF16)
        return carry

    lax.fori_loop(0, seq // tq, query_tile, 0)


def _attention(qt, k, vt):
    batch, heads, _, seq = qt.shape
    assert ATTN_UNROLL % 2 == 0 and (seq // TKV) % ATTN_UNROLL == 0 and seq % TQ == 0
    return pl.pallas_call(
        functools.partial(_attn_kernel, tq=TQ, tk=TKV),
        grid=(batch, heads),
        in_specs=[
            pl.BlockSpec((1, 1, HEAD_SLOT, seq), lambda b, h: (b, h, 0, 0)),
            pl.BlockSpec((1, 1, seq, HEAD_SLOT), lambda b, h: (b, h, 0, 0)),
            pl.BlockSpec((1, 1, V_ROWS, seq), lambda b, h: (b, h, 0, 0)),
        ],
        out_specs=pl.BlockSpec((1, V_HEAD, seq), lambda b, h: (b, h, 0)),
        out_shape=jax.ShapeDtypeStruct((batch, heads * V_HEAD, seq), BF16),
        scratch_shapes=[pltpu.VMEM((TKV, TQ), F32), pltpu.VMEM((TKV, TQ), F32),
                        pltpu.VMEM((1, TQ), F32), pltpu.VMEM((V_ROWS, TQ), F32)],
        compiler_params=pltpu.CompilerParams(
            dimension_semantics=("parallel", "parallel"), vmem_limit_bytes=VMEM_LIMIT),
        name="attention",
    )(qt, k, vt)


def _mix_kernel(x_ref, g1_ref, wg_ref, ot_ref, cu_ref, cup_ref, cun_ref, bg_ref, cw_ref,
                woa_ref, wob_ref, wo_ref, g2_ref, rwh_ref, rwl_ref, rb_ref,
                x1_ref, h2_ref, mi_ref, mw_ref, cnt_ref, tri_ref, carry_ref, *, steps_per_seq):
    i = pl.program_id(0)
    tm = x_ref.shape[0]

    @pl.when(i == 0)
    def _():
        r = lax.broadcasted_iota(jnp.int32, (tm, tm), 0)
        c = lax.broadcasted_iota(jnp.int32, (tm, tm), 1)
        tri_ref[...] = jnp.where(c < r, 1.0, 0.0).astype(BF16)
        carry_ref[...] = jnp.zeros_like(carry_ref)

    x = x_ref[...]
    h = _rms(x, g1_ref[...]).astype(BF16)
    gates = jnp.dot(h, wg_ref[...], preferred_element_type=F32)
    d = x.shape[1]
    sig_a = 1.0 / (1.0 + jnp.exp(-gates[:, 0:d]))
    sig_b = 1.0 / (1.0 + jnp.exp(-gates[:, d:2 * d]))

    y_a = lax.dot_general(ot_ref[0], woa_ref[...], (((0,), (0,)), ((), ())), preferred_element_type=F32)

    cu = cu_ref[...]
    row = lax.broadcasted_iota(jnp.int32, cu.shape, 0)
    s_in_seq = i % steps_per_seq
    prev_row = jnp.where(s_in_seq == 0, 0.0, cup_ref[7:8, :])
    next_row = jnp.where(s_in_seq == steps_per_seq - 1, 0.0, cun_ref[0:1, :])
    below = jnp.where(row == 0, prev_row, pltpu.roll(cu, 1, 0))
    above = jnp.where(row == tm - 1, next_row, pltpu.roll(cu, tm - 1, 0))
    cw = cw_ref[...]
    conv = cw[0:1, :] * below + cw[1:2, :] * cu + cw[2:3, :] * above
    y_b = jnp.dot((bg_ref[...] * conv).astype(BF16), wob_ref[...], preferred_element_type=F32)

    merged = (sig_a * y_a + sig_b * y_b).astype(BF16)
    x1 = x + jnp.dot(merged, wo_ref[...], preferred_element_type=F32)
    x1_ref[...] = x1
    h2 = _rms(x1, g2_ref[...])
    _store_token_tiles(h2_ref, h2)

    h2_hi = h2.astype(BF16)
    h2_lo = (h2 - h2_hi.astype(F32)).astype(BF16)
    logits = (jnp.dot(h2_hi, rwh_ref[...], preferred_element_type=F32)
              + jnp.dot(h2_lo, rwh_ref[...], preferred_element_type=F32)
              + jnp.dot(h2_hi, rwl_ref[...], preferred_element_type=F32)
              + rb_ref[...])

    lane = lax.broadcasted_iota(jnp.int32, logits.shape, 1)
    lane_f = lane.astype(F32)
    work = logits
    vals, idxs, hits = [], [], []
    for _ in range(TOP_K):
        mx = jnp.max(work, axis=-1, keepdims=True)
        idx = jnp.min(jnp.where(work == mx, lane_f, float(LANE)), axis=-1, keepdims=True)
        hit = lane_f == idx
        work = jnp.where(hit, -jnp.inf, work)
        vals.append(mx)
        idxs.append(idx)
        hits.append(hit)
    exps = [jnp.exp(v - vals[0]) for v in vals]
    denom = exps[0] + exps[1] + exps[2] + exps[3]
    inv = 1.0 / denom

    sel = jnp.zeros(logits.shape, F32)
    for hit in hits:
        sel = sel + jnp.where(hit, 1.0, 0.0)
    carry = carry_ref[...]
    before = jnp.dot(tri_ref[...], sel.astype(BF16), preferred_element_type=F32) + carry
    carry_new = carry + jnp.sum(sel, axis=0, keepdims=True)
    carry_ref[...] = carry_new
    cnt_ref[...] = jnp.broadcast_to(carry_new, cnt_ref.shape)

    mi = jnp.zeros(logits.shape, F32)
    mw = jnp.zeros(logits.shape, F32)
    for k in range(TOP_K):
        rank = jnp.sum(jnp.where(hits[k], before, 0.0), axis=-1, keepdims=True)
        mi = jnp.where(lane == k, idxs[k], mi)
        mi = jnp.where(lane == TOP_K + k, rank, mi)
        mw = jnp.where(lane == k, exps[k] * inv, mw)
    mi_ref[...] = mi.astype(jnp.int32)
    mw_ref[...] = mw


def _mix(x2, g1, w_g, ot, cu, bg, conv_w, woa, wob, wo, g2, rw_hi, rw_lo, rb, batch, seq):
    n, d = x2.shape
    tm = TM_MIX
    spb = seq // tm
    r8 = tm // 8
    nsteps = n // tm
    full = lambda shp: pl.BlockSpec(shp, lambda i: (0,) * len(shp))
    return pl.pallas_call(
        functools.partial(_mix_kernel, steps_per_seq=spb),
        grid=(nsteps,),
        in_specs=[
            pl.BlockSpec((tm, d), lambda i: (i, 0)),
            full(g1.shape), full(w_g.shape),
            pl.BlockSpec((1, MLA_HEADS * V_HEAD, tm), lambda i: (i // spb, 0, i % spb)),
            pl.BlockSpec((tm, CONV_WIDTH), lambda i: (i, 0)),
            pl.BlockSpec((8, CONV_WIDTH), lambda i: (jnp.maximum(i * r8 - 1, 0), 0)),
            pl.BlockSpec((8, CONV_WIDTH), lambda i: (jnp.minimum((i + 1) * r8, nsteps * r8 - 1), 0)),
            pl.BlockSpec((tm, CONV_WIDTH), lambda i: (i, 0)),
            full(conv_w.shape), full(woa.shape), full(wob.shape), full(wo.shape), full(g2.shape),
            full(rw_hi.shape), full(rw_lo.shape), full(rb.shape),
        ],
        out_specs=[
            pl.BlockSpec((tm, d), lambda i: (i, 0)),
            pl.BlockSpec((tm * SUBLANE, LANE), lambda i: (i, 0)),
            pl.BlockSpec((tm, LANE), lambda i: (i, 0)),
            pl.BlockSpec((tm, LANE), lambda i: (i, 0)),
            pl.BlockSpec((8, LANE), lambda i: (0, 0)),
        ],
        out_shape=[
            jax.ShapeDtypeStruct((n, d), F32),
            jax.ShapeDtypeStruct((n * SUBLANE, LANE), F32),
            jax.ShapeDtypeStruct((n, LANE), jnp.int32),
            jax.ShapeDtypeStruct((n, LANE), F32),
            jax.ShapeDtypeStruct((8, LANE), F32),
        ],
        scratch_shapes=[pltpu.VMEM((tm, tm), BF16), pltpu.VMEM((1, LANE), F32)],
        compiler_params=pltpu.CompilerParams(dimension_semantics=("arbitrary",), vmem_limit_bytes=VMEM_LIMIT),
        name="mix_route",
    )(x2, g1, w_g, ot, cu, cu, cu, bg, conv_w, woa, wob, wo, g2, rw_hi, rw_lo, rb)


def _row_copy_wait(src_like, dst_like, sem, times):
    for _ in range(times):
        pltpu.make_async_copy(src_like, dst_like, sem).wait()


def _token(ref, idx):
    return ref.at[pl.ds(pl.multiple_of(idx * SUBLANE, SUBLANE), SUBLANE)]


def _dispatch_kernel(dest_ref, pe_ref, nused_ref, h2_ref, xs_ref, zero_ref, sem, zsem):
    tm = h2_ref.shape[0] // SUBLANE
    block_rows = EXPERT_BLOCK * SUBLANE
    n_blocks = xs_ref.shape[0] // block_rows

    @pl.when(pl.program_id(0) == 0)
    def _():
        zero_ref[...] = jnp.zeros_like(zero_ref)

        def zero_copy(first_token):
            start = pl.multiple_of(first_token * SUBLANE, block_rows)
            return pltpu.make_async_copy(zero_ref, xs_ref.at[pl.ds(start, block_rows)], zsem)

        def has_rows(e):
            return pe_ref[e] > jnp.where(e == 0, 0, pe_ref[jnp.maximum(e - 1, 0)])

        def start_expert(e, c):
            @pl.when(has_rows(e))
            def _():
                zero_copy(pe_ref[e] - EXPERT_BLOCK).start()
            return c

        def wait_expert(e, c):
            @pl.when(has_rows(e))
            def _():
                zero_copy(pe_ref[e] - EXPERT_BLOCK).wait()
            return c

        def start_tail(b, c):
            zero_copy(b * EXPERT_BLOCK).start()
            return c

        def wait_tail(b, c):
            zero_copy(b * EXPERT_BLOCK).wait()
            return c

        lax.fori_loop(0, N_EXPERTS, start_expert, 0)
        lax.fori_loop(nused_ref[0], n_blocks, start_tail, 0)
        lax.fori_loop(0, N_EXPERTS, wait_expert, 0)
        lax.fori_loop(nused_ref[0], n_blocks, wait_tail, 0)

    def issue(g, c):
        for u in range(ISSUE_GROUP):
            t = g * (ISSUE_GROUP // TOP_K) + u // TOP_K
            dest = dest_ref[g * ISSUE_GROUP + u]
            pltpu.make_async_copy(_token(h2_ref, t), _token(xs_ref, dest), sem).start(priority=u % 2)
        return c

    lax.fori_loop(0, tm * TOP_K // ISSUE_GROUP, issue, 0)
    _row_copy_wait(h2_ref, xs_ref.at[pl.ds(0, tm * SUBLANE)], sem, TOP_K)


def _dispatch(dest_flat, pad_end, n_used, h2t, rows):
    tm = TM_ROUTE
    n = h2t.shape[0] // SUBLANE
    return pl.pallas_call(
        _dispatch_kernel,
        grid=(n // tm,),
        in_specs=[
            pl.BlockSpec((tm * TOP_K,), lambda i: (i,), memory_space=pltpu.SMEM),
            pl.BlockSpec(memory_space=pltpu.SMEM),
            pl.BlockSpec(memory_space=pltpu.SMEM),
            pl.BlockSpec((tm * SUBLANE, LANE), lambda i: (i, 0)),
        ],
        out_specs=pl.BlockSpec(memory_space=pl.ANY),
        out_shape=jax.ShapeDtypeStruct((rows * SUBLANE, LANE), F32),
        scratch_shapes=[pltpu.VMEM((EXPERT_BLOCK * SUBLANE, LANE), F32), pltpu.SemaphoreType.DMA(()),
                        pltpu.SemaphoreType.DMA(())],
        compiler_params=pltpu.CompilerParams(dimension_semantics=("arbitrary",), vmem_limit_bytes=VMEM_LIMIT),
        name="dispatch",
    )(dest_flat, pad_end, n_used, h2t)


def _expert_kernel(bexp_ref, nused_ref, nexp_ref, xs_ref, w1_hbm, b1_ref, w2_hbm, b2_ref, ys_ref,
                   w1f_ref, w2f_ref, w1b_ref, w2b_ref, sem):
    i = pl.program_id(0)
    active = i < nused_ref[0]
    expert = bexp_ref[i]
    prev = bexp_ref[jnp.maximum(i - 1, 0)]
    fresh = jnp.logical_or(i == 0, expert != prev)

    def weight_copies(e):
        return (pltpu.make_async_copy(w1_hbm.at[e], w1f_ref, sem.at[0]),
                pltpu.make_async_copy(w2_hbm.at[e], w2f_ref, sem.at[1]))

    @pl.when(jnp.logical_and(active, i == 0))
    def _():
        for cp in weight_copies(expert):
            cp.start()

    @pl.when(jnp.logical_and(active, fresh))
    def _():
        for cp in weight_copies(expert):
            cp.wait()
        w1b_ref[...] = w1f_ref[...].astype(BF16)
        w2b_ref[...] = w2f_ref[...].astype(BF16)

        @pl.when(nexp_ref[i] != expert)
        def _():
            for cp in weight_copies(nexp_ref[i]):
                cp.start()

    @pl.when(active)
    def _():
        dff = w2b_ref.shape[0]
        xb = _load_token_tiles(xs_ref, (), EXPERT_BLOCK).astype(BF16)
        hm = jnp.dot(xb, w1b_ref[...], preferred_element_type=F32) + b1_ref[0]
        gate = jnp.minimum(hm[:, 0:dff], SWIGLU_LIMIT)
        up = jnp.clip(hm[:, dff:2 * dff], -SWIGLU_LIMIT, SWIGLU_LIMIT)
        glu = gate * (1.0 / (1.0 + jnp.exp(-SWIGLU_ALPHA * gate)))
        act = ((up + 1.0) * glu).astype(BF16)
        _store_token_tiles(ys_ref, jnp.dot(act, w2b_ref[...], preferred_element_type=F32) + b2_ref[0])

    @pl.when(jnp.logical_not(active))
    def _():
        ys_ref[...] = jnp.zeros_like(ys_ref)


def _experts(block_exp, n_used, next_exp, xs, w1, b1, w2, b2):
    d = w1.shape[1]
    assert d == SUBLANE * LANE
    block_rows = EXPERT_BLOCK * SUBLANE
    n_blocks = xs.shape[0] // block_rows
    dff2 = w1.shape[2]
    dff = w2.shape[1]
    grid_spec = pltpu.PrefetchScalarGridSpec(
        num_scalar_prefetch=3,
        grid=(n_blocks,),
        in_specs=[
            pl.BlockSpec((block_rows, LANE), lambda i, be, nu, ne: (jnp.minimum(i, nu[0] - 1), 0)),
            pl.BlockSpec(memory_space=pl.ANY),
            pl.BlockSpec((1, 1, dff2), lambda i, be, nu, ne: (be[i], 0, 0)),
            pl.BlockSpec(memory_space=pl.ANY),
            pl.BlockSpec((1, 1, d), lambda i, be, nu, ne: (be[i], 0, 0)),
        ],
        out_specs=pl.BlockSpec((block_rows, LANE), lambda i, be, nu, ne: (i, 0)),
        scratch_shapes=[pltpu.VMEM((d, dff2), F32), pltpu.VMEM((dff, d), F32),
                        pltpu.VMEM((d, dff2), BF16), pltpu.VMEM((dff, d), BF16),
                        pltpu.SemaphoreType.DMA((2,))],
    )
    return pl.pallas_call(
        _expert_kernel,
        grid_spec=grid_spec,
        out_shape=jax.ShapeDtypeStruct(xs.shape, F32),
        compiler_params=pltpu.CompilerParams(dimension_semantics=("arbitrary",), vmem_limit_bytes=VMEM_LIMIT),
        name="experts",
    )(block_exp, n_used, next_exp, xs, w1, b1, w2, b2)


def _combine_kernel(dest_ref, dest_next_ref, x1_ref, mw_ref, ys_ref, out_ref, buf_ref, sem):
    i = pl.program_id(0)
    tm = x1_ref.shape[0]
    slot = lax.rem(i, 2)

    def gather(idx_ref, s):
        def issue(g, c):
            for u in range(ISSUE_GROUP):
                t = g * (ISSUE_GROUP // TOP_K) + u // TOP_K
                dest = idx_ref[g * ISSUE_GROUP + u]
                pltpu.make_async_copy(_token(ys_ref, dest), _token(buf_ref.at[s, u % TOP_K], t),
                                      sem.at[s]).start(priority=u % 2)
            return c

        lax.fori_loop(0, tm * TOP_K // ISSUE_GROUP, issue, 0)

    @pl.when(i == 0)
    def _():
        gather(dest_ref, 0)

    @pl.when(i + 1 < pl.num_programs(0))
    def _():
        gather(dest_next_ref, 1 - slot)

    _row_copy_wait(ys_ref.at[pl.ds(0, tm * SUBLANE)], buf_ref.at[slot, 0], sem.at[slot], TOP_K)
    acc = x1_ref[...]
    mw = mw_ref[...]
    for k in range(TOP_K):
        acc = acc + mw[:, k:k + 1] * _load_token_tiles(buf_ref, (slot, k), tm)
    out_ref[...] = acc


def _combine(dest_flat, x1, mw, ys):
    n, d = x1.shape
    tm = TM_ROUTE
    nsteps = n // tm
    return pl.pallas_call(
        _combine_kernel,
        grid=(nsteps,),
        in_specs=[
            pl.BlockSpec((tm * TOP_K,), lambda i: (i,), memory_space=pltpu.SMEM),
            pl.BlockSpec((tm * TOP_K,), lambda i: (jnp.minimum(i + 1, nsteps - 1),), memory_space=pltpu.SMEM),
            pl.BlockSpec((tm, d), lambda i: (i, 0)),
            pl.BlockSpec((tm, LANE), lambda i: (i, 0)),
            pl.BlockSpec(memory_space=pl.ANY),
        ],
        out_specs=pl.BlockSpec((tm, d), lambda i: (i, 0)),
        out_shape=jax.ShapeDtypeStruct((n, d), F32),
        scratch_shapes=[pltpu.VMEM((2, TOP_K, tm * SUBLANE, LANE), F32), pltpu.SemaphoreType.DMA((2,))],
        compiler_params=pltpu.CompilerParams(dimension_semantics=("arbitrary",), vmem_limit_bytes=VMEM_LIMIT),
        name="combine",
    )(dest_flat, dest_flat, x1, mw, ys)


def _pad_cols(w, width):
    return jnp.pad(w, ((0, 0), (0, width - w.shape[1])))


def _head_slots(w, per_head):
    rows = w.shape[0]
    w3 = w.reshape(rows, MLA_HEADS, per_head)
    return jnp.pad(w3, ((0, 0), (0, 0), (0, HEAD_SLOT - per_head))).reshape(rows, MLA_HEADS * HEAD_SLOT)


def _rope_tables(positions):
    inv_freq = ROPE_THETA ** (-jnp.arange(0, QK_ROPE, 2, dtype=F32) / QK_ROPE)
    ang = positions.astype(F32).reshape(-1, 1) * inv_freq
    return jnp.cos(ang).T, jnp.sin(ang).T


def _layer(x2, positions, norm1_g, w_in, q_a_norm_g, kv_a_norm_g, w_uq, w_ukv, q_norm_g, k_norm_g,
           conv_w, w_o_mla, w_o_conv, w_o, norm2_g, router_w, router_b,
           expert_w1, expert_b1, expert_w2, expert_b2, batch, seq):
    n, d = x2.shape
    o_kr = Q_LORA + KV_LORA
    o_u = o_kr + QK_ROPE
    o_g = o_u + 3 * CONV_WIDTH
    kr_cols = jnp.pad(w_in[:, o_kr:o_u], ((0, 0), (QK_NOPE, LANE - QK_HEAD)))
    w_a = jnp.concatenate([w_in[:, :o_kr], kr_cols, w_in[:, o_u:o_g]], axis=1).astype(BF16)
    w_g = w_in[:, o_g:].astype(BF16)
    row = lambda v: v.reshape(1, -1)
    cos_c, sin_c = _rope_tables(positions)
    q_scale = (QK_HEAD ** -0.5) * math.log2(math.e)
    gain_t = lambda g: jnp.broadcast_to(g.reshape(LANE, 1), (LANE, TM_PROJ // PROJ_SUBTILES))
    kg = _pad_cols(row(k_norm_g), LANE)

    qt, k, vt, cu, bg = _in_projection(
        x2, row(norm1_g), w_a, row(q_a_norm_g), row(kv_a_norm_g),
        _head_slots(w_uq, QK_HEAD).astype(BF16), w_ukv.astype(BF16),
        gain_t(_pad_cols(row(q_norm_g) * q_scale, LANE)), kg, gain_t(kg),
        cos_c, sin_c, batch, seq)
    ot = _attention(qt, k, vt)

    rw = _pad_cols(router_w, LANE)
    rw_hi = rw.astype(BF16)
    rw_lo = (rw - rw_hi.astype(F32)).astype(BF16)
    rb = jnp.concatenate([row(router_b), jnp.full((1, LANE - N_EXPERTS), NEG_BIG, F32)], axis=1)
    x1, h2, mi, mw, cnt = _mix(
        x2, row(norm1_g), w_g, ot, cu, bg, conv_w, w_o_mla.astype(BF16), w_o_conv.astype(BF16),
        w_o.astype(BF16), row(norm2_g), rw_hi, rw_lo, rb, batch, seq)

    counts = cnt[0, :N_EXPERTS].astype(jnp.int32)
    padded = (counts + EXPERT_BLOCK - 1) // EXPERT_BLOCK * EXPERT_BLOCK
    pad_end = jnp.cumsum(padded)
    pad_start = (pad_end - padded).astype(jnp.int32)
    nk = n * TOP_K
    n_blocks = (nk + N_EXPERTS * (EXPERT_BLOCK - 1) + EXPERT_BLOCK - 1) // EXPERT_BLOCK
    rows = n_blocks * EXPERT_BLOCK
    block_first_row = jnp.arange(n_blocks, dtype=jnp.int32) * EXPERT_BLOCK
    block_exp = jnp.minimum(jnp.sum(pad_end[None, :] <= block_first_row[:, None], axis=1),
                            N_EXPERTS - 1).astype(jnp.int32)
    n_used = (pad_end[-1:] // EXPERT_BLOCK).astype(jnp.int32)
    after = pad_end[block_exp] // EXPERT_BLOCK
    next_exp = jnp.where(after < n_used[0], block_exp[jnp.minimum(after, n_blocks - 1)], block_exp).astype(jnp.int32)
    e_sel = mi[:, 0:TOP_K, None] == jnp.arange(N_EXPERTS, dtype=jnp.int32)
    dest_flat = (jnp.sum(jnp.where(e_sel, pad_start, 0), axis=-1) + mi[:, TOP_K:2 * TOP_K]).reshape(nk)

    xs = _dispatch(dest_flat, pad_end.astype(jnp.int32), n_used, h2, rows)
    ys = _experts(block_exp, n_used, next_exp, xs, expert_w1, expert_b1.reshape(N_EXPERTS, 1, -1),
                  expert_w2, expert_b2.reshape(N_EXPERTS, 1, -1))
    return _combine(dest_flat, x1, mw, ys)


def kernel(x, positions, norm1_g, w_in, q_a_norm_g, kv_a_norm_g, w_uq, w_ukv, q_norm_g, k_norm_g, conv_w,
           w_o_mla, w_o_conv, w_o, norm2_g, router_w, router_b, expert_w1, expert_b1, expert_w2, expert_b2):
    batch, seq, d = x.shape
    depth = norm1_g.shape[0]
    x2 = x.reshape(batch * seq, d)
    for l in range(depth):
        x2 = _layer(x2, positions, norm1_g[l], w_in[l], q_a_norm_g[l], kv_a_norm_g[l], w_uq[l], w_ukv[l],
                    q_norm_g[l], k_norm_g[l], conv_w[l], w_o_mla[l], w_o_conv[l], w_o[l], norm2_g[l],
                    router_w[l], router_b[l], expert_w1[l], expert_b1[l], expert_w2[l], expert_b2[l], batch, seq)
    return x2.reshape(batch, seq, d)
```

```python
import functools
import math

import jax
import jax.numpy as jnp
from jax import lax
from jax.experimental import pallas as pl
from jax.experimental.pallas import tpu as pltpu

F32 = jnp.float32
BF16 = jnp.bfloat16

MLA_HEADS = 8
QK_NOPE = 64
QK_ROPE = 32
QK_HEAD = QK_NOPE + QK_ROPE
V_HEAD = 64
Q_LORA = 256
KV_LORA = 128
ROPE_THETA = 10000.0
CONV_WIDTH = 512
N_EXPERTS = 32
TOP_K = 4
SWIGLU_LIMIT = 7.0
SWIGLU_ALPHA = 1.702
EPS = 1e-6

LANE = 128
SUBLANE = 8
HEAD_SLOT = LANE
HALF_ROPE = QK_ROPE // 2
V_ROWS = V_HEAD + 16
OFFSET_FEATURE = QK_HEAD
SAFE_SCORE_RANGE = 100.0
VMEM_LIMIT = 56 * 1024 * 1024

TM_PROJ = 512
PROJ_SUBTILES = 2
TQ = 512
TKV = 512
ATTN_UNROLL = 4
TKV_BOUNDED = 512
BOUNDED_UNROLL = 16
TM_MIX = 512
TM_ROUTE = 256
EXPERT_BLOCK = 512
ISSUE_GROUP = 16
NEG_BIG = -1e30


def _load_token_tiles(ref, lead, rows):
    return jnp.concatenate([ref[lead + (pl.ds(c, rows, stride=SUBLANE), slice(None))] for c in range(SUBLANE)],
                           axis=1)


def _store_token_tiles(ref, value):
    rows = value.shape[0]
    for c in range(SUBLANE):
        ref[pl.ds(c, rows, stride=SUBLANE), :] = value[:, c * LANE:(c + 1) * LANE]


def _rms(x, g):
    return x * lax.rsqrt(jnp.mean(x * x, axis=-1, keepdims=True) + EPS) * g


def _inproj_kernel(x_ref, g1_ref, w_ref, gq_ref, gkv_ref, wuq_ref, wukv_ref, qgt_ref, kg_ref, kgt_ref,
                   qoff_ref, koff_ref, cost_ref, sint_ref, qt_ref, k_ref, vt_ref, cu_ref, bg_ref):
    tm = x_ref.shape[0] // PROJ_SUBTILES
    for part in range(PROJ_SUBTILES):
        _inproj_rows(slice(part * tm, (part + 1) * tm), x_ref, g1_ref, w_ref, gq_ref, gkv_ref, wuq_ref, wukv_ref,
                     qgt_ref, kg_ref, kgt_ref, qoff_ref, koff_ref, cost_ref, sint_ref,
                     qt_ref, k_ref, vt_ref, cu_ref, bg_ref)


def _inproj_rows(rows, x_ref, g1_ref, w_ref, gq_ref, gkv_ref, wuq_ref, wukv_ref, qgt_ref, kg_ref, kgt_ref,
                 qoff_ref, koff_ref, cost_ref, sint_ref, qt_ref, k_ref, vt_ref, cu_ref, bg_ref):
    x = x_ref[rows, :]
    h = _rms(x, g1_ref[...]).astype(BF16)
    proj = jnp.dot(h, w_ref[...], preferred_element_type=F32)
    c_q = proj[:, 0:Q_LORA]
    c_kv = proj[:, Q_LORA:Q_LORA + KV_LORA]
    kr = proj[:, Q_LORA + KV_LORA:Q_LORA + KV_LORA + LANE]
    o = Q_LORA + KV_LORA + LANE
    u = proj[:, o:o + CONV_WIDTH]
    c_gate = proj[:, o + CONV_WIDTH:o + 2 * CONV_WIDTH]
    b_gate = proj[:, o + 2 * CONV_WIDTH:o + 3 * CONV_WIDTH]
    cu_ref[rows, :] = c_gate * u
    bg_ref[rows, :] = b_gate

    kg = kg_ref[...]
    tm = x.shape[0]
    lane = lax.broadcasted_iota(jnp.int32, (tm, LANE), 1)

    q = jnp.dot(_rms(c_q, gq_ref[...]).astype(BF16), wuq_ref[...], preferred_element_type=F32)
    kv = jnp.dot(_rms(c_kv, gkv_ref[...]).astype(BF16), wukv_ref[...], preferred_element_type=F32)

    cos_c = cost_ref[:, rows]
    sin_c = sint_ref[:, rows]

    def rope_t(t):
        t1 = t[QK_NOPE:QK_NOPE + HALF_ROPE]
        t2 = t[QK_NOPE + HALF_ROPE:QK_HEAD]
        return jnp.concatenate([t[0:QK_NOPE], t1 * cos_c - t2 * sin_c, t1 * sin_c + t2 * cos_c, t[QK_HEAD:]], axis=0)

    ss_r = jnp.sum(kr * kr, axis=-1, keepdims=True)
    kr_roped = rope_t(kr.T * kgt_ref[...]).T
    qgt = qgt_ref[...]
    qoff = qoff_ref[...]
    koff = koff_ref[...]
    ones = jnp.ones((V_ROWS - V_HEAD, tm), BF16)
    for hd in range(MLA_HEADS):
        qht = q[:, hd * HEAD_SLOT:(hd + 1) * HEAD_SLOT].T
        r = lax.rsqrt(jnp.sum(qht * qht, axis=0, keepdims=True) * (1.0 / QK_HEAD) + EPS)
        qt_ref[0, hd, :, rows] = (rope_t(qht * r * qgt) + qoff).astype(BF16)

        kvh = kv[:, hd * HEAD_SLOT:(hd + 1) * HEAD_SLOT]
        knope = jnp.where(lane < QK_NOPE, kvh, 0.0)
        rk = lax.rsqrt((jnp.sum(knope * knope, axis=-1, keepdims=True) + ss_r) * (1.0 / QK_HEAD) + EPS)
        k_ref[0, hd, rows, :] = ((knope * kg + kr_roped) * rk + koff).astype(BF16)
        kvt = kvh.T
        vt_ref[0, hd, 0:V_HEAD, rows] = kvt[QK_NOPE:QK_NOPE + V_HEAD].astype(BF16)
        vt_ref[0, hd, V_HEAD:V_ROWS, rows] = ones


def _in_projection(x2, g1, w_a, gq, gkv, wuq, wukv, qgt, kg, kgt, qoff, koff, cos_c, sin_c, batch, seq):
    n, d = x2.shape
    tm = TM_PROJ
    spb = seq // tm
    full = lambda shp: pl.BlockSpec(shp, lambda i: (0,) * len(shp))
    return pl.pallas_call(
        _inproj_kernel,
        grid=(n // tm,),
        in_specs=[
            pl.BlockSpec((tm, d), lambda i: (i, 0)),
            full(g1.shape), full(w_a.shape), full(gq.shape), full(gkv.shape), full(wuq.shape), full(wukv.shape),
            full(qgt.shape), full(kg.shape), full(kgt.shape), full(qoff.shape), full(koff.shape),
            pl.BlockSpec((HALF_ROPE, tm), lambda i: (0, i)),
            pl.BlockSpec((HALF_ROPE, tm), lambda i: (0, i)),
        ],
        out_specs=[
            pl.BlockSpec((1, MLA_HEADS, HEAD_SLOT, tm), lambda i: (i // spb, 0, 0, i % spb)),
            pl.BlockSpec((1, MLA_HEADS, tm, HEAD_SLOT), lambda i: (i // spb, 0, i % spb, 0)),
            pl.BlockSpec((1, MLA_HEADS, V_ROWS, tm), lambda i: (i // spb, 0, 0, i % spb)),
            pl.BlockSpec((tm, CONV_WIDTH), lambda i: (i, 0)),
            pl.BlockSpec((tm, CONV_WIDTH), lambda i: (i, 0)),
        ],
        out_shape=[
            jax.ShapeDtypeStruct((batch, MLA_HEADS, HEAD_SLOT, seq), BF16),
            jax.ShapeDtypeStruct((batch, MLA_HEADS, seq, HEAD_SLOT), BF16),
            jax.ShapeDtypeStruct((batch, MLA_HEADS, V_ROWS, seq), BF16),
            jax.ShapeDtypeStruct((n, CONV_WIDTH), F32),
            jax.ShapeDtypeStruct((n, CONV_WIDTH), F32),
        ],
        compiler_params=pltpu.CompilerParams(dimension_semantics=("parallel",), vmem_limit_bytes=VMEM_LIMIT),
        name="in_projection",
    )(x2, g1, w_a, gq, gkv, wuq, wukv, qgt, kg, kgt, qoff, koff, cos_c, sin_c)


def _attn_kernel(qt_ref, k_ref, vt_ref, o_ref, sa_ref, sb_ref, m_ref, acc_ref, *, tq, tk):
    seq = k_ref.shape[2]
    nk = seq // tk
    bufs = (sa_ref, sb_ref)

    def query_tile(qi, carry):
        q0 = pl.multiple_of(qi * tq, tq)
        qt = qt_ref[0, 0, :, pl.ds(q0, tq)]

        def scores(c, s_ref):
            k0 = pl.multiple_of(c * tk, tk)
            s_ref[...] = jnp.dot(k_ref[0, 0, pl.ds(k0, tk), :], qt, preferred_element_type=F32)

        def accumulate(c, s_ref):
            k0 = pl.multiple_of(c * tk, tk)
            s = s_ref[...]
            m = m_ref[...]
            m_new = jnp.maximum(m, jnp.max(s, axis=0, keepdims=True))
            m_ref[...] = m_new
            p = jnp.exp2(s - m_new).astype(BF16)
            vs = vt_ref[0, 0, :, pl.ds(k0, tk)]
            acc_ref[...] = jnp.exp2(m - m_new) * acc_ref[...] + jnp.dot(vs, p, preferred_element_type=F32)

        m_ref[...] = jnp.full(m_ref.shape, NEG_BIG, F32)
        acc_ref[...] = jnp.zeros(acc_ref.shape, F32)
        scores(0, sa_ref)

        def group(j, c):
            base = ATTN_UNROLL * j
            for u in range(ATTN_UNROLL):
                scores(base + u + 1, bufs[(u + 1) % 2])
                accumulate(base + u, bufs[u % 2])
            return c

        lax.fori_loop(0, nk // ATTN_UNROLL - 1, group, 0)
        base = nk - ATTN_UNROLL
        for u in range(ATTN_UNROLL):
            if u + 1 < ATTN_UNROLL:
                scores(base + u + 1, bufs[(u + 1) % 2])
            accumulate(base + u, bufs[u % 2])
        acc = acc_ref[...]
        o_ref[0, :, pl.ds(q0, tq)] = (acc[0:V_HEAD] * (1.0 / acc[V_HEAD:V_HEAD + 1])).astype(BF16)
        return carry

    lax.fori_loop(0, seq // tq, query_tile, 0)


def _attn_bounded_kernel(qt_ref, k_ref, vt_ref, o_ref, sa_ref, sb_ref, acc_ref, *, tq, tk):
    seq = k_ref.shape[2]
    nk = seq // tk
    bufs = (sa_ref, sb_ref)

    def query_tile(qi, carry):
        q0 = pl.multiple_of(qi * tq, tq)
        qt = qt_ref[0, 0, :, pl.ds(q0, tq)]

        def scores(c, s_ref):
            k0 = pl.multiple_of(c * tk, tk)
            s_ref[...] = jnp.dot(k_ref[0, 0, pl.ds(k0, tk), :], qt, preferred_element_type=F32)

        def weighted_values(c, s_ref):
            k0 = pl.multiple_of(c * tk, tk)
            p = jnp.exp2(s_ref[...]).astype(BF16)
            return jnp.dot(vt_ref[0, 0, :, pl.ds(k0, tk)], p, preferred_element_type=F32)

        acc_ref[...] = jnp.zeros(acc_ref.shape, F32)
        scores(0, sa_ref)

        def group(j, c):
            base = BOUNDED_UNROLL * j
            total = None
            for u in range(BOUNDED_UNROLL):
                scores(jnp.minimum(base + u + 1, nk - 1), bufs[(u + 1) % 2])
                part = weighted_values(base + u, bufs[u % 2])
                total = part if total is None else total + part
            acc_ref[...] += total
            return c

        lax.fori_loop(0, nk // BOUNDED_UNROLL, group, 0)
        acc = acc_ref[...]
        o_ref[0, :, pl.ds(q0, tq)] = (acc[0:V_HEAD] * (1.0 / acc[V_HEAD:V_HEAD + 1])).astype(BF16)
        return carry

    lax.fori_loop(0, seq // tq, query_tile, 0)


def _attention(qt, k, vt, bounded):
    batch, heads, _, seq = qt.shape
    assert ATTN_UNROLL % 2 == 0 and (seq // TKV) % ATTN_UNROLL == 0 and seq % TQ == 0
    in_specs = [
        pl.BlockSpec((1, 1, HEAD_SLOT, seq), lambda b, h: (b, h, 0, 0)),
        pl.BlockSpec((1, 1, seq, HEAD_SLOT), lambda b, h: (b, h, 0, 0)),
        pl.BlockSpec((1, 1, V_ROWS, seq), lambda b, h: (b, h, 0, 0)),
    ]
    out_spec = pl.BlockSpec((1, V_HEAD, seq), lambda b, h: (b, h, 0))
    out_shape = jax.ShapeDtypeStruct((batch, heads * V_HEAD, seq), BF16)
    params = pltpu.CompilerParams(dimension_semantics=("parallel", "parallel"), vmem_limit_bytes=VMEM_LIMIT)

    def bounded_path(qt, k, vt):
        return pl.pallas_call(
            functools.partial(_attn_bounded_kernel, tq=TQ, tk=TKV_BOUNDED),
            grid=(batch, heads), in_specs=in_specs, out_specs=out_spec, out_shape=out_shape,
            scratch_shapes=[pltpu.VMEM((TKV_BOUNDED, TQ), F32), pltpu.VMEM((TKV_BOUNDED, TQ), F32),
                            pltpu.VMEM((V_ROWS, TQ), F32)],
            compiler_params=params, name="attention_bounded",
        )(qt, k, vt)

    return lax.cond(bounded, bounded_path, _attention_online, qt, k, vt)


def _attention_online(qt, k, vt):
    batch, heads, _, seq = qt.shape
    return pl.pallas_call(
        functools.partial(_attn_kernel, tq=TQ, tk=TKV),
        grid=(batch, heads),
        in_specs=[
            pl.BlockSpec((1, 1, HEAD_SLOT, seq), lambda b, h: (b, h, 0, 0)),
            pl.BlockSpec((1, 1, seq, HEAD_SLOT), lambda b, h: (b, h, 0, 0)),
            pl.BlockSpec((1, 1, V_ROWS, seq), lambda b, h: (b, h, 0, 0)),
        ],
        out_specs=pl.BlockSpec((1, V_HEAD, seq), lambda b, h: (b, h, 0)),
        out_shape=jax.ShapeDtypeStruct((batch, heads * V_HEAD, seq), BF16),
        scratch_shapes=[pltpu.VMEM((TKV, TQ), F32), pltpu.VMEM((TKV, TQ), F32),
                        pltpu.VMEM((1, TQ), F32), pltpu.VMEM((V_ROWS, TQ), F32)],
        compiler_params=pltpu.CompilerParams(
            dimension_semantics=("parallel", "parallel"), vmem_limit_bytes=VMEM_LIMIT),
        name="attention",
    )(qt, k, vt)


def _mix_kernel(x_ref, g1_ref, wg_ref, ot_ref, cu_ref, cup_ref, cun_ref, bg_ref, cw_ref,
                woa_ref, wob_ref, wo_ref, g2_ref, rwh_ref, rwl_ref, rb_ref,
                x1_ref, h2_ref, mi_ref, mw_ref, cnt_ref, tri_ref, carry_ref, *, steps_per_seq):
    i = pl.program_id(0)
    tm = x_ref.shape[0]

    @pl.when(i == 0)
    def _():
        r = lax.broadcasted_iota(jnp.int32, (tm, tm), 0)
        c = lax.broadcasted_iota(jnp.int32, (tm, tm), 1)
        tri_ref[...] = jnp.where(c < r, 1.0, 0.0).astype(BF16)
        carry_ref[...] = jnp.zeros_like(carry_ref)

    x = x_ref[...]
    h = _rms(x, g1_ref[...]).astype(BF16)
    gates = jnp.dot(h, wg_ref[...], preferred_element_type=F32)
    d = x.shape[1]
    sig_a = 1.0 / (1.0 + jnp.exp(-gates[:, 0:d]))
    sig_b = 1.0 / (1.0 + jnp.exp(-gates[:, d:2 * d]))

    y_a = lax.dot_general(ot_ref[0], woa_ref[...], (((0,), (0,)), ((), ())), preferred_element_type=F32)

    cu = cu_ref[...]
    row = lax.broadcasted_iota(jnp.int32, cu.shape, 0)
    s_in_seq = i % steps_per_seq
    prev_row = jnp.where(s_in_seq == 0, 0.0, cup_ref[7:8, :])
    next_row = jnp.where(s_in_seq == steps_per_seq - 1, 0.0, cun_ref[0:1, :])
    below = jnp.where(row == 0, prev_row, pltpu.roll(cu, 1, 0))
    above = jnp.where(row == tm - 1, next_row, pltpu.roll(cu, tm - 1, 0))
    cw = cw_ref[...]
    conv = cw[0:1, :] * below + cw[1:2, :] * cu + cw[2:3, :] * above
    y_b = jnp.dot((bg_ref[...] * conv).astype(BF16), wob_ref[...], preferred_element_type=F32)

    merged = (sig_a * y_a + sig_b * y_b).astype(BF16)
    x1 = x + jnp.dot(merged, wo_ref[...], preferred_element_type=F32)
    x1_ref[...] = x1
    h2 = _rms(x1, g2_ref[...])
    _store_token_tiles(h2_ref, h2)

    h2_hi = h2.astype(BF16)
    h2_lo = (h2 - h2_hi.astype(F32)).astype(BF16)
    logits = (jnp.dot(h2_hi, rwh_ref[...], preferred_element_type=F32)
              + jnp.dot(h2_lo, rwh_ref[...], preferred_element_type=F32)
              + jnp.dot(h2_hi, rwl_ref[...], preferred_element_type=F32)
              + rb_ref[...])

    lane = lax.broadcasted_iota(jnp.int32, logits.shape, 1)
    lane_f = lane.astype(F32)
    work = logits
    vals, idxs, hits = [], [], []
    for _ in range(TOP_K):
        mx = jnp.max(work, axis=-1, keepdims=True)
        idx = jnp.min(jnp.where(work == mx, lane_f, float(LANE)), axis=-1, keepdims=True)
        hit = lane_f == idx
        work = jnp.where(hit, -jnp.inf, work)
        vals.append(mx)
        idxs.append(idx)
        hits.append(hit)
    exps = [jnp.exp(v - vals[0]) for v in vals]
    denom = exps[0] + exps[1] + exps[2] + exps[3]
    inv = 1.0 / denom

    sel = jnp.zeros(logits.shape, F32)
    for hit in hits:
        sel = sel + jnp.where(hit, 1.0, 0.0)
    carry = carry_ref[...]
    before = jnp.dot(tri_ref[...], sel.astype(BF16), preferred_element_type=F32) + carry
    carry_new = carry + jnp.sum(sel, axis=0, keepdims=True)
    carry_ref[...] = carry_new
    cnt_ref[...] = jnp.broadcast_to(carry_new, cnt_ref.shape)

    mi = jnp.zeros(logits.shape, F32)
    mw = jnp.zeros(logits.shape, F32)
    for k in range(TOP_K):
        rank = jnp.sum(jnp.where(hits[k], before, 0.0), axis=-1, keepdims=True)
        mi = jnp.where(lane == k, idxs[k], mi)
        mi = jnp.where(lane == TOP_K + k, rank, mi)
        mw = jnp.where(lane == k, exps[k] * inv, mw)
    mi_ref[...] = mi.astype(jnp.int32)
    mw_ref[...] = mw


def _mix(x2, g1, w_g, ot, cu, bg, conv_w, woa, wob, wo, g2, rw_hi, rw_lo, rb, batch, seq):
    n, d = x2.shape
    tm = TM_MIX
    spb = seq // tm
    r8 = tm // 8
    nsteps = n // tm
    full = lambda shp: pl.BlockSpec(shp, lambda i: (0,) * len(shp))
    return pl.pallas_call(
        functools.partial(_mix_kernel, steps_per_seq=spb),
        grid=(nsteps,),
        in_specs=[
            pl.BlockSpec((tm, d), lambda i: (i, 0)),
            full(g1.shape), full(w_g.shape),
            pl.BlockSpec((1, MLA_HEADS * V_HEAD, tm), lambda i: (i // spb, 0, i % spb)),
            pl.BlockSpec((tm, CONV_WIDTH), lambda i: (i, 0)),
            pl.BlockSpec((8, CONV_WIDTH), lambda i: (jnp.maximum(i * r8 - 1, 0), 0)),
            pl.BlockSpec((8, CONV_WIDTH), lambda i: (jnp.minimum((i + 1) * r8, nsteps * r8 - 1), 0)),
            pl.BlockSpec((tm, CONV_WIDTH), lambda i: (i, 0)),
            full(conv_w.shape), full(woa.shape), full(wob.shape), full(wo.shape), full(g2.shape),
            full(rw_hi.shape), full(rw_lo.shape), full(rb.shape),
        ],
        out_specs=[
            pl.BlockSpec((tm, d), lambda i: (i, 0)),
            pl.BlockSpec((tm * SUBLANE, LANE), lambda i: (i, 0)),
            pl.BlockSpec((tm, LANE), lambda i: (i, 0)),
            pl.BlockSpec((tm, LANE), lambda i: (i, 0)),
            pl.BlockSpec((8, LANE), lambda i: (0, 0)),
        ],
        out_shape=[
            jax.ShapeDtypeStruct((n, d), F32),
            jax.ShapeDtypeStruct((n * SUBLANE, LANE), F32),
            jax.ShapeDtypeStruct((n, LANE), jnp.int32),
            jax.ShapeDtypeStruct((n, LANE), F32),
            jax.ShapeDtypeStruct((8, LANE), F32),
        ],
        scratch_shapes=[pltpu.VMEM((tm, tm), BF16), pltpu.VMEM((1, LANE), F32)],
        compiler_params=pltpu.CompilerParams(dimension_semantics=("arbitrary",), vmem_limit_bytes=VMEM_LIMIT),
        name="mix_route",
    )(x2, g1, w_g, ot, cu, cu, cu, bg, conv_w, woa, wob, wo, g2, rw_hi, rw_lo, rb)


def _row_copy_wait(src_like, dst_like, sem, times):
    for _ in range(times):
        pltpu.make_async_copy(src_like, dst_like, sem).wait()


def _token(ref, idx):
    return ref.at[pl.ds(pl.multiple_of(idx * SUBLANE, SUBLANE), SUBLANE)]


def _dispatch_kernel(dest_ref, pe_ref, nused_ref, h2_ref, xs_ref, zero_ref, sem, zsem):
    tm = h2_ref.shape[0] // SUBLANE
    block_rows = EXPERT_BLOCK * SUBLANE
    n_blocks = xs_ref.shape[0] // block_rows

    @pl.when(pl.program_id(0) == 0)
    def _():
        zero_ref[...] = jnp.zeros_like(zero_ref)

        def zero_copy(first_token):
            start = pl.multiple_of(first_token * SUBLANE, block_rows)
            return pltpu.make_async_copy(zero_ref, xs_ref.at[pl.ds(start, block_rows)], zsem)

        def has_rows(e):
            return pe_ref[e] > jnp.where(e == 0, 0, pe_ref[jnp.maximum(e - 1, 0)])

        def start_expert(e, c):
            @pl.when(has_rows(e))
            def _():
                zero_copy(pe_ref[e] - EXPERT_BLOCK).start()
            return c

        def wait_expert(e, c):
            @pl.when(has_rows(e))
            def _():
                zero_copy(pe_ref[e] - EXPERT_BLOCK).wait()
            return c

        def start_tail(b, c):
            zero_copy(b * EXPERT_BLOCK).start()
            return c

        def wait_tail(b, c):
            zero_copy(b * EXPERT_BLOCK).wait()
            return c

        lax.fori_loop(0, N_EXPERTS, start_expert, 0)
        lax.fori_loop(nused_ref[0], n_blocks, start_tail, 0)
        lax.fori_loop(0, N_EXPERTS, wait_expert, 0)
        lax.fori_loop(nused_ref[0], n_blocks, wait_tail, 0)

    def issue(g, c):
        for u in range(ISSUE_GROUP):
            t = g * (ISSUE_GROUP // TOP_K) + u // TOP_K
            dest = dest_ref[g * ISSUE_GROUP + u]
            pltpu.make_async_copy(_token(h2_ref, t), _token(xs_ref, dest), sem).start(priority=u % 2)
        return c

    lax.fori_loop(0, tm * TOP_K // ISSUE_GROUP, issue, 0)
    _row_copy_wait(h2_ref, xs_ref.at[pl.ds(0, tm * SUBLANE)], sem, TOP_K)


def _dispatch(dest_flat, pad_end, n_used, h2t, rows):
    tm = TM_ROUTE
    n = h2t.shape[0] // SUBLANE
    return pl.pallas_call(
        _dispatch_kernel,
        grid=(n // tm,),
        in_specs=[
            pl.BlockSpec((tm * TOP_K,), lambda i: (i,), memory_space=pltpu.SMEM),
            pl.BlockSpec(memory_space=pltpu.SMEM),
            pl.BlockSpec(memory_space=pltpu.SMEM),
            pl.BlockSpec((tm * SUBLANE, LANE), lambda i: (i, 0)),
        ],
        out_specs=pl.BlockSpec(memory_space=pl.ANY),
        out_shape=jax.ShapeDtypeStruct((rows * SUBLANE, LANE), F32),
        scratch_shapes=[pltpu.VMEM((EXPERT_BLOCK * SUBLANE, LANE), F32), pltpu.SemaphoreType.DMA(()),
                        pltpu.SemaphoreType.DMA(())],
        compiler_params=pltpu.CompilerParams(dimension_semantics=("arbitrary",), vmem_limit_bytes=VMEM_LIMIT),
        name="dispatch",
    )(dest_flat, pad_end, n_used, h2t)


def _expert_kernel(bexp_ref, nused_ref, nexp_ref, xs_ref, w1_hbm, b1_ref, w2_hbm, b2_ref, ys_ref,
                   w1f_ref, w2f_ref, w1b_ref, w2b_ref, sem):
    i = pl.program_id(0)
    active = i < nused_ref[0]
    expert = bexp_ref[i]
    prev = bexp_ref[jnp.maximum(i - 1, 0)]
    fresh = jnp.logical_or(i == 0, expert != prev)

    def weight_copies(e):
        return (pltpu.make_async_copy(w1_hbm.at[e], w1f_ref, sem.at[0]),
                pltpu.make_async_copy(w2_hbm.at[e], w2f_ref, sem.at[1]))

    @pl.when(jnp.logical_and(active, i == 0))
    def _():
        for cp in weight_copies(expert):
            cp.start()

    @pl.when(jnp.logical_and(active, fresh))
    def _():
        for cp in weight_copies(expert):
            cp.wait()
        w1b_ref[...] = w1f_ref[...].astype(BF16)
        w2b_ref[...] = w2f_ref[...].astype(BF16)

        @pl.when(nexp_ref[i] != expert)
        def _():
            for cp in weight_copies(nexp_ref[i]):
                cp.start()

    @pl.when(active)
    def _():
        dff = w2b_ref.shape[0]
        xb = _load_token_tiles(xs_ref, (), EXPERT_BLOCK).astype(BF16)
        hm = jnp.dot(xb, w1b_ref[...], preferred_element_type=F32) + b1_ref[0]
        gate = jnp.minimum(hm[:, 0:dff], SWIGLU_LIMIT)
        up = jnp.clip(hm[:, dff:2 * dff], -SWIGLU_LIMIT, SWIGLU_LIMIT)
        glu = gate * (1.0 / (1.0 + jnp.exp(-SWIGLU_ALPHA * gate)))
        act = ((up + 1.0) * glu).astype(BF16)
        _store_token_tiles(ys_ref, jnp.dot(act, w2b_ref[...], preferred_element_type=F32) + b2_ref[0])

    @pl.when(jnp.logical_not(active))
    def _():
        ys_ref[...] = jnp.zeros_like(ys_ref)


def _experts(block_exp, n_used, next_exp, xs, w1, b1, w2, b2):
    d = w1.shape[1]
    assert d == SUBLANE * LANE
    block_rows = EXPERT_BLOCK * SUBLANE
    n_blocks = xs.shape[0] // block_rows
    dff2 = w1.shape[2]
    dff = w2.shape[1]
    grid_spec = pltpu.PrefetchScalarGridSpec(
        num_scalar_prefetch=3,
        grid=(n_blocks,),
        in_specs=[
            pl.BlockSpec((block_rows, LANE), lambda i, be, nu, ne: (jnp.minimum(i, nu[0] - 1), 0)),
            pl.BlockSpec(memory_space=pl.ANY),
            pl.BlockSpec((1, 1, dff2), lambda i, be, nu, ne: (be[i], 0, 0)),
            pl.BlockSpec(memory_space=pl.ANY),
            pl.BlockSpec((1, 1, d), lambda i, be, nu, ne: (be[i], 0, 0)),
        ],
        out_specs=pl.BlockSpec((block_rows, LANE), lambda i, be, nu, ne: (i, 0)),
        scratch_shapes=[pltpu.VMEM((d, dff2), F32), pltpu.VMEM((dff, d), F32),
                        pltpu.VMEM((d, dff2), BF16), pltpu.VMEM((dff, d), BF16),
                        pltpu.SemaphoreType.DMA((2,))],
    )
    return pl.pallas_call(
        _expert_kernel,
        grid_spec=grid_spec,
        out_shape=jax.ShapeDtypeStruct(xs.shape, F32),
        compiler_params=pltpu.CompilerParams(dimension_semantics=("arbitrary",), vmem_limit_bytes=VMEM_LIMIT),
        name="experts",
    )(block_exp, n_used, next_exp, xs, w1, b1, w2, b2)


def _combine_kernel(dest_ref, dest_next_ref, x1_ref, mw_ref, ys_ref, out_ref, buf_ref, sem):
    i = pl.program_id(0)
    tm = x1_ref.shape[0]
    slot = lax.rem(i, 2)

    def gather(idx_ref, s):
        def issue(g, c):
            for u in range(ISSUE_GROUP):
                t = g * (ISSUE_GROUP // TOP_K) + u // TOP_K
                dest = idx_ref[g * ISSUE_GROUP + u]
                pltpu.make_async_copy(_token(ys_ref, dest), _token(buf_ref.at[s, u % TOP_K], t),
                                      sem.at[s]).start(priority=u % 2)
            return c

        lax.fori_loop(0, tm * TOP_K // ISSUE_GROUP, issue, 0)

    @pl.when(i == 0)
    def _():
        gather(dest_ref, 0)

    @pl.when(i + 1 < pl.num_programs(0))
    def _():
        gather(dest_next_ref, 1 - slot)

    _row_copy_wait(ys_ref.at[pl.ds(0, tm * SUBLANE)], buf_ref.at[slot, 0], sem.at[slot], TOP_K)
    acc = x1_ref[...]
    mw = mw_ref[...]
    for k in range(TOP_K):
        acc = acc + mw[:, k:k + 1] * _load_token_tiles(buf_ref, (slot, k), tm)
    out_ref[...] = acc


def _combine(dest_flat, x1, mw, ys):
    n, d = x1.shape
    tm = TM_ROUTE
    nsteps = n // tm
    return pl.pallas_call(
        _combine_kernel,
        grid=(nsteps,),
        in_specs=[
            pl.BlockSpec((tm * TOP_K,), lambda i: (i,), memory_space=pltpu.SMEM),
            pl.BlockSpec((tm * TOP_K,), lambda i: (jnp.minimum(i + 1, nsteps - 1),), memory_space=pltpu.SMEM),
            pl.BlockSpec((tm, d), lambda i: (i, 0)),
            pl.BlockSpec((tm, LANE), lambda i: (i, 0)),
            pl.BlockSpec(memory_space=pl.ANY),
        ],
        out_specs=pl.BlockSpec((tm, d), lambda i: (i, 0)),
        out_shape=jax.ShapeDtypeStruct((n, d), F32),
        scratch_shapes=[pltpu.VMEM((2, TOP_K, tm * SUBLANE, LANE), F32), pltpu.SemaphoreType.DMA((2,))],
        compiler_params=pltpu.CompilerParams(dimension_semantics=("arbitrary",), vmem_limit_bytes=VMEM_LIMIT),
        name="combine",
    )(dest_flat, dest_flat, x1, mw, ys)


def _pad_cols(w, width):
    return jnp.pad(w, ((0, 0), (0, width - w.shape[1])))


def _head_slots(w, per_head):
    rows = w.shape[0]
    w3 = w.reshape(rows, MLA_HEADS, per_head)
    return jnp.pad(w3, ((0, 0), (0, 0), (0, HEAD_SLOT - per_head))).reshape(rows, MLA_HEADS * HEAD_SLOT)


def _rope_tables(positions):
    inv_freq = ROPE_THETA ** (-jnp.arange(0, QK_ROPE, 2, dtype=F32) / QK_ROPE)
    ang = positions.astype(F32).reshape(-1, 1) * inv_freq
    return jnp.cos(ang).T, jnp.sin(ang).T


def _layer(x2, positions, norm1_g, w_in, q_a_norm_g, kv_a_norm_g, w_uq, w_ukv, q_norm_g, k_norm_g,
           conv_w, w_o_mla, w_o_conv, w_o, norm2_g, router_w, router_b,
           expert_w1, expert_b1, expert_w2, expert_b2, batch, seq):
    n, d = x2.shape
    o_kr = Q_LORA + KV_LORA
    o_u = o_kr + QK_ROPE
    o_g = o_u + 3 * CONV_WIDTH
    kr_cols = jnp.pad(w_in[:, o_kr:o_u], ((0, 0), (QK_NOPE, LANE - QK_HEAD)))
    w_a = jnp.concatenate([w_in[:, :o_kr], kr_cols, w_in[:, o_u:o_g]], axis=1).astype(BF16)
    w_g = w_in[:, o_g:].astype(BF16)
    row = lambda v: v.reshape(1, -1)
    cos_c, sin_c = _rope_tables(positions)
    q_scale = (QK_HEAD ** -0.5) * math.log2(math.e)
    gain_t = lambda g: jnp.broadcast_to(g.reshape(LANE, 1), (LANE, TM_PROJ // PROJ_SUBTILES))
    kg = _pad_cols(row(k_norm_g), LANE)
    score_bound = 1.02 * q_scale * QK_HEAD * jnp.max(jnp.abs(q_norm_g)) * jnp.max(jnp.abs(k_norm_g))
    bounded = 2.0 * score_bound <= SAFE_SCORE_RANGE
    offset = jnp.where(bounded, score_bound, 0.0)
    feature = jnp.arange(LANE) == OFFSET_FEATURE
    qoff = gain_t(jnp.where(feature, -offset, 0.0).astype(F32))
    koff = jnp.where(feature, 1.0, 0.0).astype(F32).reshape(1, LANE)

    qt, k, vt, cu, bg = _in_projection(
        x2, row(norm1_g), w_a, row(q_a_norm_g), row(kv_a_norm_g),
        _head_slots(w_uq, QK_HEAD).astype(BF16), w_ukv.astype(BF16),
        gain_t(_pad_cols(row(q_norm_g) * q_scale, LANE)), kg, gain_t(kg), qoff, koff,
        cos_c, sin_c, batch, seq)
    ot = _attention(qt, k, vt, bounded)

    rw = _pad_cols(router_w, LANE)
    rw_hi = rw.astype(BF16)
    rw_lo = (rw - rw_hi.astype(F32)).astype(BF16)
    rb = jnp.concatenate([row(router_b), jnp.full((1, LANE - N_EXPERTS), NEG_BIG, F32)], axis=1)
    x1, h2, mi, mw, cnt = _mix(
        x2, row(norm1_g), w_g, ot, cu, bg, conv_w, w_o_mla.astype(BF16), w_o_conv.astype(BF16),
        w_o.astype(BF16), row(norm2_g), rw_hi, rw_lo, rb, batch, seq)

    counts = cnt[0, :N_EXPERTS].astype(jnp.int32)
    padded = (counts + EXPERT_BLOCK - 1) // EXPERT_BLOCK * EXPERT_BLOCK
    pad_end = jnp.cumsum(padded)
    pad_start = (pad_end - padded).astype(jnp.int32)
    nk = n * TOP_K
    n_blocks = (nk + N_EXPERTS * (EXPERT_BLOCK - 1) + EXPERT_BLOCK - 1) // EXPERT_BLOCK
    rows = n_blocks * EXPERT_BLOCK
    block_first_row = jnp.arange(n_blocks, dtype=jnp.int32) * EXPERT_BLOCK
    block_exp = jnp.minimum(jnp.sum(pad_end[None, :] <= block_first_row[:, None], axis=1),
                            N_EXPERTS - 1).astype(jnp.int32)
    n_used = (pad_end[-1:] // EXPERT_BLOCK).astype(jnp.int32)
    after = pad_end[block_exp] // EXPERT_BLOCK
    next_exp = jnp.where(after < n_used[0], block_exp[jnp.minimum(after, n_blocks - 1)], block_exp).astype(jnp.int32)
    e_sel = mi[:, 0:TOP_K, None] == jnp.arange(N_EXPERTS, dtype=jnp.int32)
    dest_flat = (jnp.sum(jnp.where(e_sel, pad_start, 0), axis=-1) + mi[:, TOP_K:2 * TOP_K]).reshape(nk)

    xs = _dispatch(dest_flat, pad_end.astype(jnp.int32), n_used, h2, rows)
    ys = _experts(block_exp, n_used, next_exp, xs, expert_w1, expert_b1.reshape(N_EXPERTS, 1, -1),
                  expert_w2, expert_b2.reshape(N_EXPERTS, 1, -1))
    return _combine(dest_flat, x1, mw, ys)


def kernel(x, positions, norm1_g, w_in, q_a_norm_g, kv_a_norm_g, w_uq, w_ukv, q_norm_g, k_norm_g, conv_w,
           w_o_mla, w_o_conv, w_o, norm2_g, router_w, router_b, expert_w1, expert_b1, expert_w2, expert_b2):
    batch, seq, d = x.shape
    depth = norm1_g.shape[0]
    x2 = x.reshape(batch * seq, d)
    for l in range(depth):
        x2 = _layer(x2, positions, norm1_g[l], w_in[l], q_a_norm_g[l], kv_a_norm_g[l], w_uq[l], w_ukv[l],
                    q_norm_g[l], k_norm_g[l], conv_w[l], w_o_mla[l], w_o_conv[l], w_o[l], norm2_g[l],
                    router_w[l], router_b[l], expert_w1[l], expert_b1[l], expert_w2[l], expert_b2[l], batch, seq)
    return x2.reshape(batch, seq, d)
```

```python
import functools
import math

import jax
import jax.numpy as jnp
from jax import lax
from jax.experimental import pallas as pl
from jax.experimental.pallas import tpu as pltpu

F32 = jnp.float32
BF16 = jnp.bfloat16

MLA_HEADS = 8
QK_NOPE = 64
QK_ROPE = 32
QK_HEAD = QK_NOPE + QK_ROPE
V_HEAD = 64
Q_LORA = 256
KV_LORA = 128
ROPE_THETA = 10000.0
CONV_WIDTH = 512
N_EXPERTS = 32
TOP_K = 4
SWIGLU_LIMIT = 7.0
SWIGLU_ALPHA = 1.702
EPS = 1e-6

LANE = 128
SUBLANE = 8
HEAD_SLOT = LANE
HALF_ROPE = QK_ROPE // 2
V_ROWS = V_HEAD + 16
OFFSET_FEATURE = QK_HEAD
SAFE_SCORE_RANGE = 100.0
VMEM_LIMIT = 56 * 1024 * 1024

TM_PROJ = 512
PROJ_SUBTILES = 2
TQ = 512
TKV = 512
ATTN_UNROLL = 4
TKV_BOUNDED = 512
BOUNDED_UNROLL = 16
TM_MIX = 512
TM_ROUTE = 256
EXPERT_BLOCK = 512
ISSUE_GROUP = 16
NEG_BIG = -1e30


def _load_token_tiles(ref, lead, rows):
    return jnp.concatenate([ref[lead + (pl.ds(c, rows, stride=SUBLANE), slice(None))] for c in range(SUBLANE)],
                           axis=1)


def _store_token_tiles(ref, value):
    rows = value.shape[0]
    for c in range(SUBLANE):
        ref[pl.ds(c, rows, stride=SUBLANE), :] = value[:, c * LANE:(c + 1) * LANE]


def _rms(x, g):
    return x * lax.rsqrt(jnp.mean(x * x, axis=-1, keepdims=True) + EPS) * g


def _inproj_kernel(x_ref, g1_ref, w_ref, gq_ref, gkv_ref, wuq_ref, wukv_ref, qgt_ref, kg_ref, kgt_ref,
                   qoff_ref, koff_ref, cost_ref, sint_ref, qt_ref, k_ref, vt_ref, cu_ref, bg_ref):
    tm = x_ref.shape[0] // PROJ_SUBTILES
    for part in range(PROJ_SUBTILES):
        _inproj_rows(slice(part * tm, (part + 1) * tm), x_ref, g1_ref, w_ref, gq_ref, gkv_ref, wuq_ref, wukv_ref,
                     qgt_ref, kg_ref, kgt_ref, qoff_ref, koff_ref, cost_ref, sint_ref,
                     qt_ref, k_ref, vt_ref, cu_ref, bg_ref)


def _inproj_rows(rows, x_ref, g1_ref, w_ref, gq_ref, gkv_ref, wuq_ref, wukv_ref, qgt_ref, kg_ref, kgt_ref,
                 qoff_ref, koff_ref, cost_ref, sint_ref, qt_ref, k_ref, vt_ref, cu_ref, bg_ref):
    x = x_ref[rows, :]
    h = _rms(x, g1_ref[...]).astype(BF16)
    proj = jnp.dot(h, w_ref[...], preferred_element_type=F32)
    c_q = proj[:, 0:Q_LORA]
    c_kv = proj[:, Q_LORA:Q_LORA + KV_LORA]
    kr = proj[:, Q_LORA + KV_LORA:Q_LORA + KV_LORA + LANE]
    o = Q_LORA + KV_LORA + LANE
    u = proj[:, o:o + CONV_WIDTH]
    c_gate = proj[:, o + CONV_WIDTH:o + 2 * CONV_WIDTH]
    b_gate = proj[:, o + 2 * CONV_WIDTH:o + 3 * CONV_WIDTH]
    cu_ref[rows, :] = c_gate * u
    bg_ref[rows, :] = b_gate

    kg = kg_ref[...]
    tm = x.shape[0]
    lane = lax.broadcasted_iota(jnp.int32, (tm, LANE), 1)

    q = jnp.dot(_rms(c_q, gq_ref[...]).astype(BF16), wuq_ref[...], preferred_element_type=F32)
    kv = jnp.dot(_rms(c_kv, gkv_ref[...]).astype(BF16), wukv_ref[...], preferred_element_type=F32)

    cos_c = cost_ref[:, rows]
    sin_c = sint_ref[:, rows]

    def rope_t(t):
        t1 = t[QK_NOPE:QK_NOPE + HALF_ROPE]
        t2 = t[QK_NOPE + HALF_ROPE:QK_HEAD]
        return jnp.concatenate([t[0:QK_NOPE], t1 * cos_c - t2 * sin_c, t1 * sin_c + t2 * cos_c, t[QK_HEAD:]], axis=0)

    ss_r = jnp.sum(kr * kr, axis=-1, keepdims=True)
    kr_roped = rope_t(kr.T * kgt_ref[...]).T
    qgt = qgt_ref[...]
    qoff = qoff_ref[...]
    koff = koff_ref[...]
    ones = jnp.ones((V_ROWS - V_HEAD, tm), BF16)
    for hd in range(MLA_HEADS):
        qht = q[:, hd * HEAD_SLOT:(hd + 1) * HEAD_SLOT].T
        r = lax.rsqrt(jnp.sum(qht * qht, axis=0, keepdims=True) * (1.0 / QK_HEAD) + EPS)
        qt_ref[0, hd, :, rows] = (rope_t(qht * r * qgt) + qoff).astype(BF16)

        kvh = kv[:, hd * HEAD_SLOT:(hd + 1) * HEAD_SLOT]
        knope = jnp.where(lane < QK_NOPE, kvh, 0.0)
        rk = lax.rsqrt((jnp.sum(knope * knope, axis=-1, keepdims=True) + ss_r) * (1.0 / QK_HEAD) + EPS)
        k_ref[0, hd, rows, :] = ((knope * kg + kr_roped) * rk + koff).astype(BF16)
        kvt = kvh.T
        vt_ref[0, hd, 0:V_HEAD, rows] = kvt[QK_NOPE:QK_NOPE + V_HEAD].astype(BF16)
        vt_ref[0, hd, V_HEAD:V_ROWS, rows] = ones


def _in_projection(x2, g1, w_a, gq, gkv, wuq, wukv, qgt, kg, kgt, qoff, koff, cos_c, sin_c, batch, seq):
    n, d = x2.shape
    tm = TM_PROJ
    spb = seq // tm
    full = lambda shp: pl.BlockSpec(shp, lambda i: (0,) * len(shp))
    return pl.pallas_call(
        _inproj_kernel,
        grid=(n // tm,),
        in_specs=[
            pl.BlockSpec((tm, d), lambda i: (i, 0)),
            full(g1.shape), full(w_a.shape), full(gq.shape), full(gkv.shape), full(wuq.shape), full(wukv.shape),
            full(qgt.shape), full(kg.shape), full(kgt.shape), full(qoff.shape), full(koff.shape),
            pl.BlockSpec((HALF_ROPE, tm), lambda i: (0, i)),
            pl.BlockSpec((HALF_ROPE, tm), lambda i: (0, i)),
        ],
        out_specs=[
            pl.BlockSpec((1, MLA_HEADS, HEAD_SLOT, tm), lambda i: (i // spb, 0, 0, i % spb)),
            pl.BlockSpec((1, MLA_HEADS, tm, HEAD_SLOT), lambda i: (i // spb, 0, i % spb, 0)),
            pl.BlockSpec((1, MLA_HEADS, V_ROWS, tm), lambda i: (i // spb, 0, 0, i % spb)),
            pl.BlockSpec((tm, CONV_WIDTH), lambda i: (i, 0)),
            pl.BlockSpec((tm, CONV_WIDTH), lambda i: (i, 0)),
        ],
        out_shape=[
            jax.ShapeDtypeStruct((batch, MLA_HEADS, HEAD_SLOT, seq), BF16),
            jax.ShapeDtypeStruct((batch, MLA_HEADS, seq, HEAD_SLOT), BF16),
            jax.ShapeDtypeStruct((batch, MLA_HEADS, V_ROWS, seq), BF16),
            jax.ShapeDtypeStruct((n, CONV_WIDTH), F32),
            jax.ShapeDtypeStruct((n, CONV_WIDTH), F32),
        ],
        compiler_params=pltpu.CompilerParams(dimension_semantics=("parallel",), vmem_limit_bytes=VMEM_LIMIT),
        name="in_projection",
    )(x2, g1, w_a, gq, gkv, wuq, wukv, qgt, kg, kgt, qoff, koff, cos_c, sin_c)


def _attn_kernel(qt_ref, k_ref, vt_ref, o_ref, sa_ref, sb_ref, m_ref, acc_ref, *, tq, tk):
    seq = k_ref.shape[2]
    nk = seq // tk
    bufs = (sa_ref, sb_ref)

    def query_tile(qi, carry):
        q0 = pl.multiple_of(qi * tq, tq)
        qt = qt_ref[0, 0, :, pl.ds(q0, tq)]

        def scores(c, s_ref):
            k0 = pl.multiple_of(c * tk, tk)
            s_ref[...] = jnp.dot(k_ref[0, 0, pl.ds(k0, tk), :], qt, preferred_element_type=F32)

        def accumulate(c, s_ref):
            k0 = pl.multiple_of(c * tk, tk)
            s = s_ref[...]
            m = m_ref[...]
            m_new = jnp.maximum(m, jnp.max(s, axis=0, keepdims=True))
            m_ref[...] = m_new
            p = jnp.exp2(s - m_new).astype(BF16)
            vs = vt_ref[0, 0, :, pl.ds(k0, tk)]
            acc_ref[...] = jnp.exp2(m - m_new) * acc_ref[...] + jnp.dot(vs, p, preferred_element_type=F32)

        m_ref[...] = jnp.full(m_ref.shape, NEG_BIG, F32)
        acc_ref[...] = jnp.zeros(acc_ref.shape, F32)
        scores(0, sa_ref)

        def group(j, c):
            base = ATTN_UNROLL * j
            for u in range(ATTN_UNROLL):
                scores(base + u + 1, bufs[(u + 1) % 2])
                accumulate(base + u, bufs[u % 2])
            return c

        lax.fori_loop(0, nk // ATTN_UNROLL - 1, group, 0)
        base = nk - ATTN_UNROLL
        for u in range(ATTN_UNROLL):
            if u + 1 < ATTN_UNROLL:
                scores(base + u + 1, bufs[(u + 1) % 2])
            accumulate(base + u, bufs[u % 2])
        acc = acc_ref[...]
        o_ref[0, :, pl.ds(q0, tq)] = (acc[0:V_HEAD] * (1.0 / acc[V_HEAD:V_HEAD + 1])).astype(BF16)
        return carry

    lax.fori_loop(0, seq // tq, query_tile, 0)


def _attn_bounded_kernel(qt_ref, k_ref, vt_ref, o_ref, sa_ref, sb_ref, acc_ref, *, tq, tk):
    seq = k_ref.shape[2]
    nk = seq // tk
    bufs = (sa_ref, sb_ref)

    def query_tile(qi, carry):
        q0 = pl.multiple_of(qi * tq, tq)
        qt = qt_ref[0, 0, :, pl.ds(q0, tq)]

        def scores(c, s_ref):
            k0 = pl.multiple_of(c * tk, tk)
            s_ref[...] = jnp.dot(k_ref[0, 0, pl.ds(k0, tk), :], qt, preferred_element_type=F32)

        def weighted_values(c, s_ref):
            k0 = pl.multiple_of(c * tk, tk)
            p = jnp.exp2(s_ref[...]).astype(BF16)
            return jnp.dot(vt_ref[0, 0, :, pl.ds(k0, tk)], p, preferred_element_type=F32)

        acc_ref[...] = jnp.zeros(acc_ref.shape, F32)
        scores(0, sa_ref)

        def group(j, c):
            base = BOUNDED_UNROLL * j
            total = None
            for u in range(BOUNDED_UNROLL):
                scores(jnp.minimum(base + u + 1, nk - 1), bufs[(u + 1) % 2])
                part = weighted_values(base + u, bufs[u % 2])
                total = part if total is None else total + part
            acc_ref[...] += total
            return c

        lax.fori_loop(0, nk // BOUNDED_UNROLL, group, 0)
        acc = acc_ref[...]
        o_ref[0, :, pl.ds(q0, tq)] = (acc[0:V_HEAD] * (1.0 / acc[V_HEAD:V_HEAD + 1])).astype(BF16)
        return carry

    lax.fori_loop(0, seq // tq, query_tile, 0)


def _attention(qt, k, vt, bounded):
    batch, heads, _, seq = qt.shape
    assert ATTN_UNROLL % 2 == 0 and (seq // TKV) % ATTN_UNROLL == 0 and seq % TQ == 0
    in_specs = [
        pl.BlockSpec((1, 1, HEAD_SLOT, seq), lambda b, h, flag: (b, h, 0, 0)),
        pl.BlockSpec((1, 1, seq, HEAD_SLOT), lambda b, h, flag: (b, h, 0, 0)),
        pl.BlockSpec((1, 1, V_ROWS, seq), lambda b, h, flag: (b, h, 0, 0)),
    ]
    out_spec = pl.BlockSpec((1, V_HEAD, seq), lambda b, h, flag: (b, h, 0))
    out_shape = jax.ShapeDtypeStruct((batch, heads * V_HEAD, seq), BF16)
    grid_spec = pltpu.PrefetchScalarGridSpec(
        num_scalar_prefetch=1, grid=(batch, heads), in_specs=in_specs, out_specs=out_spec,
        scratch_shapes=[pltpu.VMEM((max(TKV, TKV_BOUNDED), TQ), F32), pltpu.VMEM((max(TKV, TKV_BOUNDED), TQ), F32),
                        pltpu.VMEM((1, TQ), F32), pltpu.VMEM((V_ROWS, TQ), F32)])
    return pl.pallas_call(
        _attn_select_kernel, grid_spec=grid_spec, out_shape=out_shape,
        compiler_params=pltpu.CompilerParams(dimension_semantics=("parallel", "parallel"),
                                             vmem_limit_bytes=VMEM_LIMIT),
        name="attention",
    )(bounded.astype(jnp.int32).reshape(1), qt, k, vt)


def _attn_select_kernel(bounded_ref, qt_ref, k_ref, vt_ref, o_ref, sa_ref, sb_ref, m_ref, acc_ref):
    @pl.when(bounded_ref[0] != 0)
    def _():
        _attn_bounded_kernel(qt_ref, k_ref, vt_ref, o_ref, sa_ref.at[0:TKV_BOUNDED], sb_ref.at[0:TKV_BOUNDED],
                             acc_ref, tq=TQ, tk=TKV_BOUNDED)

    @pl.when(bounded_ref[0] == 0)
    def _():
        _attn_kernel(qt_ref, k_ref, vt_ref, o_ref, sa_ref.at[0:TKV], sb_ref.at[0:TKV], m_ref, acc_ref,
                     tq=TQ, tk=TKV)


def _mix_kernel(x_ref, g1_ref, wg_ref, ot_ref, cu_ref, cup_ref, cun_ref, bg_ref, cw_ref,
                woa_ref, wob_ref, wo_ref, g2_ref, rwh_ref, rwl_ref, rb_ref,
                x1_ref, h2_ref, mi_ref, mw_ref, cnt_ref, tri_ref, carry_ref, *, steps_per_seq):
    i = pl.program_id(0)
    tm = x_ref.shape[0]

    @pl.when(i == 0)
    def _():
        r = lax.broadcasted_iota(jnp.int32, (tm, tm), 0)
        c = lax.broadcasted_iota(jnp.int32, (tm, tm), 1)
        tri_ref[...] = jnp.where(c < r, 1.0, 0.0).astype(BF16)
        carry_ref[...] = jnp.zeros_like(carry_ref)

    x = x_ref[...]
    h = _rms(x, g1_ref[...]).astype(BF16)
    gates = jnp.dot(h, wg_ref[...], preferred_element_type=F32)
    d = x.shape[1]
    sig_a = 1.0 / (1.0 + jnp.exp(-gates[:, 0:d]))
    sig_b = 1.0 / (1.0 + jnp.exp(-gates[:, d:2 * d]))

    y_a = lax.dot_general(ot_ref[0], woa_ref[...], (((0,), (0,)), ((), ())), preferred_element_type=F32)

    cu = cu_ref[...]
    row = lax.broadcasted_iota(jnp.int32, cu.shape, 0)
    s_in_seq = i % steps_per_seq
    prev_row = jnp.where(s_in_seq == 0, 0.0, cup_ref[7:8, :])
    next_row = jnp.where(s_in_seq == steps_per_seq - 1, 0.0, cun_ref[0:1, :])
    below = jnp.where(row == 0, prev_row, pltpu.roll(cu, 1, 0))
    above = jnp.where(row == tm - 1, next_row, pltpu.roll(cu, tm - 1, 0))
    cw = cw_ref[...]
    conv = cw[0:1, :] * below + cw[1:2, :] * cu + cw[2:3, :] * above
    y_b = jnp.dot((bg_ref[...] * conv).astype(BF16), wob_ref[...], preferred_element_type=F32)

    merged = (sig_a * y_a + sig_b * y_b).astype(BF16)
    x1 = x + jnp.dot(merged, wo_ref[...], preferred_element_type=F32)
    x1_ref[...] = x1
    h2 = _rms(x1, g2_ref[...])
    _store_token_tiles(h2_ref, h2)

    h2_hi = h2.astype(BF16)
    h2_lo = (h2 - h2_hi.astype(F32)).astype(BF16)
    logits = (jnp.dot(h2_hi, rwh_ref[...], preferred_element_type=F32)
              + jnp.dot(h2_lo, rwh_ref[...], preferred_element_type=F32)
              + jnp.dot(h2_hi, rwl_ref[...], preferred_element_type=F32)
              + rb_ref[...])

    lane = lax.broadcasted_iota(jnp.int32, logits.shape, 1)
    lane_f = lane.astype(F32)
    work = logits
    vals, idxs, hits = [], [], []
    for _ in range(TOP_K):
        mx = jnp.max(work, axis=-1, keepdims=True)
        idx = jnp.min(jnp.where(work == mx, lane_f, float(LANE)), axis=-1, keepdims=True)
        hit = lane_f == idx
        work = jnp.where(hit, -jnp.inf, work)
        vals.append(mx)
        idxs.append(idx)
        hits.append(hit)
    exps = [jnp.exp(v - vals[0]) for v in vals]
    denom = exps[0] + exps[1] + exps[2] + exps[3]
    inv = 1.0 / denom

    sel = jnp.zeros(logits.shape, F32)
    for hit in hits:
        sel = sel + jnp.where(hit, 1.0, 0.0)
    carry = carry_ref[...]
    before = jnp.dot(tri_ref[...], sel.astype(BF16), preferred_element_type=F32) + carry
    carry_new = carry + jnp.sum(sel, axis=0, keepdims=True)
    carry_ref[...] = carry_new
    cnt_ref[...] = jnp.broadcast_to(carry_new, cnt_ref.shape)

    mi = jnp.zeros(logits.shape, F32)
    mw = jnp.zeros(logits.shape, F32)
    for k in range(TOP_K):
        rank = jnp.sum(jnp.where(hits[k], before, 0.0), axis=-1, keepdims=True)
        mi = jnp.where(lane == k, idxs[k], mi)
        mi = jnp.where(lane == TOP_K + k, rank, mi)
        mw = jnp.where(lane == k, exps[k] * inv, mw)
    mi_ref[...] = mi.T[0:2 * TOP_K].astype(jnp.int32)
    mw_ref[...] = mw


def _mix(x2, g1, w_g, ot, cu, bg, conv_w, woa, wob, wo, g2, rw_hi, rw_lo, rb, batch, seq):
    n, d = x2.shape
    tm = TM_MIX
    spb = seq // tm
    r8 = tm // 8
    nsteps = n // tm
    full = lambda shp: pl.BlockSpec(shp, lambda i: (0,) * len(shp))
    return pl.pallas_call(
        functools.partial(_mix_kernel, steps_per_seq=spb),
        grid=(nsteps,),
        in_specs=[
            pl.BlockSpec((tm, d), lambda i: (i, 0)),
            full(g1.shape), full(w_g.shape),
            pl.BlockSpec((1, MLA_HEADS * V_HEAD, tm), lambda i: (i // spb, 0, i % spb)),
            pl.BlockSpec((tm, CONV_WIDTH), lambda i: (i, 0)),
            pl.BlockSpec((8, CONV_WIDTH), lambda i: (jnp.maximum(i * r8 - 1, 0), 0)),
            pl.BlockSpec((8, CONV_WIDTH), lambda i: (jnp.minimum((i + 1) * r8, nsteps * r8 - 1), 0)),
            pl.BlockSpec((tm, CONV_WIDTH), lambda i: (i, 0)),
            full(conv_w.shape), full(woa.shape), full(wob.shape), full(wo.shape), full(g2.shape),
            full(rw_hi.shape), full(rw_lo.shape), full(rb.shape),
        ],
        out_specs=[
            pl.BlockSpec((tm, d), lambda i: (i, 0)),
            pl.BlockSpec((tm * SUBLANE, LANE), lambda i: (i, 0)),
            pl.BlockSpec((2 * TOP_K, tm), lambda i: (0, i)),
            pl.BlockSpec((tm, LANE), lambda i: (i, 0)),
            pl.BlockSpec((8, LANE), lambda i: (0, 0)),
        ],
        out_shape=[
            jax.ShapeDtypeStruct((n, d), F32),
            jax.ShapeDtypeStruct((n * SUBLANE, LANE), F32),
            jax.ShapeDtypeStruct((2 * TOP_K, n), jnp.int32),
            jax.ShapeDtypeStruct((n, LANE), F32),
            jax.ShapeDtypeStruct((8, LANE), F32),
        ],
        scratch_shapes=[pltpu.VMEM((tm, tm), BF16), pltpu.VMEM((1, LANE), F32)],
        compiler_params=pltpu.CompilerParams(dimension_semantics=("arbitrary",), vmem_limit_bytes=VMEM_LIMIT),
        name="mix_route",
    )(x2, g1, w_g, ot, cu, cu, cu, bg, conv_w, woa, wob, wo, g2, rw_hi, rw_lo, rb)


def _row_copy_wait(src_like, dst_like, sem, times):
    for _ in range(times):
        pltpu.make_async_copy(src_like, dst_like, sem).wait()


def _token(ref, idx):
    return ref.at[pl.ds(pl.multiple_of(idx * SUBLANE, SUBLANE), SUBLANE)]


def _dispatch_kernel(dest_ref, pe_ref, nused_ref, h2_ref, xs_ref, zero_ref, sem, zsem):
    tm = h2_ref.shape[0] // SUBLANE
    block_rows = EXPERT_BLOCK * SUBLANE
    n_blocks = xs_ref.shape[0] // block_rows

    @pl.when(pl.program_id(0) == 0)
    def _():
        zero_ref[...] = jnp.zeros_like(zero_ref)

        def zero_copy(first_token):
            start = pl.multiple_of(first_token * SUBLANE, block_rows)
            return pltpu.make_async_copy(zero_ref, xs_ref.at[pl.ds(start, block_rows)], zsem)

        def has_rows(e):
            return pe_ref[e] > jnp.where(e == 0, 0, pe_ref[jnp.maximum(e - 1, 0)])

        def start_expert(e, c):
            @pl.when(has_rows(e))
            def _():
                zero_copy(pe_ref[e] - EXPERT_BLOCK).start()
            return c

        def wait_expert(e, c):
            @pl.when(has_rows(e))
            def _():
                zero_copy(pe_ref[e] - EXPERT_BLOCK).wait()
            return c

        def start_tail(b, c):
            zero_copy(b * EXPERT_BLOCK).start()
            return c

        def wait_tail(b, c):
            zero_copy(b * EXPERT_BLOCK).wait()
            return c

        lax.fori_loop(0, N_EXPERTS, start_expert, 0)
        lax.fori_loop(nused_ref[0], n_blocks, start_tail, 0)
        lax.fori_loop(0, N_EXPERTS, wait_expert, 0)
        lax.fori_loop(nused_ref[0], n_blocks, wait_tail, 0)

    def issue(g, c):
        for u in range(ISSUE_GROUP):
            t = g * (ISSUE_GROUP // TOP_K) + u // TOP_K
            dest = dest_ref[(u % TOP_K) * tm + t]
            pltpu.make_async_copy(_token(h2_ref, t), _token(xs_ref, dest), sem).start(priority=u % 2)
        return c

    lax.fori_loop(0, tm * TOP_K // ISSUE_GROUP, issue, 0)
    _row_copy_wait(h2_ref, xs_ref.at[pl.ds(0, tm * SUBLANE)], sem, TOP_K)


def _dispatch(dest_flat, pad_end, n_used, h2t, rows):
    tm = TM_ROUTE
    n = h2t.shape[0] // SUBLANE
    return pl.pallas_call(
        _dispatch_kernel,
        grid=(n // tm,),
        in_specs=[
            pl.BlockSpec((tm * TOP_K,), lambda i: (i,), memory_space=pltpu.SMEM),
            pl.BlockSpec(memory_space=pltpu.SMEM),
            pl.BlockSpec(memory_space=pltpu.SMEM),
            pl.BlockSpec((tm * SUBLANE, LANE), lambda i: (i, 0)),
        ],
        out_specs=pl.BlockSpec(memory_space=pl.ANY),
        out_shape=jax.ShapeDtypeStruct((rows * SUBLANE, LANE), F32),
        scratch_shapes=[pltpu.VMEM((EXPERT_BLOCK * SUBLANE, LANE), F32), pltpu.SemaphoreType.DMA(()),
                        pltpu.SemaphoreType.DMA(())],
        compiler_params=pltpu.CompilerParams(dimension_semantics=("arbitrary",), vmem_limit_bytes=VMEM_LIMIT),
        name="dispatch",
    )(dest_flat, pad_end, n_used, h2t)


def _expert_kernel(bexp_ref, nused_ref, nexp_ref, xs_ref, w1_hbm, b1_ref, w2_hbm, b2_ref, ys_ref,
                   w1f_ref, w2f_ref, w1b_ref, w2b_ref, sem):
    i = pl.program_id(0)
    active = i < nused_ref[0]
    expert = bexp_ref[i]
    prev = bexp_ref[jnp.maximum(i - 1, 0)]
    fresh = jnp.logical_or(i == 0, expert != prev)

    def weight_copies(e):
        return (pltpu.make_async_copy(w1_hbm.at[e], w1f_ref, sem.at[0]),
                pltpu.make_async_copy(w2_hbm.at[e], w2f_ref, sem.at[1]))

    @pl.when(jnp.logical_and(active, i == 0))
    def _():
        for cp in weight_copies(expert):
            cp.start()

    @pl.when(jnp.logical_and(active, fresh))
    def _():
        for cp in weight_copies(expert):
            cp.wait()
        w1b_ref[...] = w1f_ref[...].astype(BF16)
        w2b_ref[...] = w2f_ref[...].astype(BF16)

        @pl.when(nexp_ref[i] != expert)
        def _():
            for cp in weight_copies(nexp_ref[i]):
                cp.start()

    @pl.when(active)
    def _():
        dff = w2b_ref.shape[0]
        xb = _load_token_tiles(xs_ref, (), EXPERT_BLOCK).astype(BF16)
        hm = jnp.dot(xb, w1b_ref[...], preferred_element_type=F32) + b1_ref[0]
        gate = jnp.minimum(hm[:, 0:dff], SWIGLU_LIMIT)
        up = jnp.clip(hm[:, dff:2 * dff], -SWIGLU_LIMIT, SWIGLU_LIMIT)
        glu = gate * (1.0 / (1.0 + jnp.exp(-SWIGLU_ALPHA * gate)))
        act = ((up + 1.0) * glu).astype(BF16)
        _store_token_tiles(ys_ref, jnp.dot(act, w2b_ref[...], preferred_element_type=F32) + b2_ref[0])

    @pl.when(jnp.logical_not(active))
    def _():
        ys_ref[...] = jnp.zeros_like(ys_ref)


def _experts(block_exp, n_used, next_exp, xs, w1, b1, w2, b2):
    d = w1.shape[1]
    assert d == SUBLANE * LANE
    block_rows = EXPERT_BLOCK * SUBLANE
    n_blocks = xs.shape[0] // block_rows
    dff2 = w1.shape[2]
    dff = w2.shape[1]
    grid_spec = pltpu.PrefetchScalarGridSpec(
        num_scalar_prefetch=3,
        grid=(n_blocks,),
        in_specs=[
            pl.BlockSpec((block_rows, LANE), lambda i, be, nu, ne: (jnp.minimum(i, nu[0] - 1), 0)),
            pl.BlockSpec(memory_space=pl.ANY),
            pl.BlockSpec((1, 1, dff2), lambda i, be, nu, ne: (be[i], 0, 0)),
            pl.BlockSpec(memory_space=pl.ANY),
            pl.BlockSpec((1, 1, d), lambda i, be, nu, ne: (be[i], 0, 0)),
        ],
        out_specs=pl.BlockSpec((block_rows, LANE), lambda i, be, nu, ne: (i, 0)),
        scratch_shapes=[pltpu.VMEM((d, dff2), F32), pltpu.VMEM((dff, d), F32),
                        pltpu.VMEM((d, dff2), BF16), pltpu.VMEM((dff, d), BF16),
                        pltpu.SemaphoreType.DMA((2,))],
    )
    return pl.pallas_call(
        _expert_kernel,
        grid_spec=grid_spec,
        out_shape=jax.ShapeDtypeStruct(xs.shape, F32),
        compiler_params=pltpu.CompilerParams(dimension_semantics=("arbitrary",), vmem_limit_bytes=VMEM_LIMIT),
        name="experts",
    )(block_exp, n_used, next_exp, xs, w1, b1, w2, b2)


def _combine_kernel(dest_ref, dest_next_ref, x1_ref, mw_ref, ys_ref, out_ref, buf_ref, sem):
    i = pl.program_id(0)
    tm = x1_ref.shape[0]
    slot = lax.rem(i, 2)

    def gather(idx_ref, s):
        def issue(g, c):
            for u in range(ISSUE_GROUP):
                t = g * (ISSUE_GROUP // TOP_K) + u // TOP_K
                dest = idx_ref[(u % TOP_K) * tm + t]
                pltpu.make_async_copy(_token(ys_ref, dest), _token(buf_ref.at[s, u % TOP_K], t),
                                      sem.at[s]).start(priority=u % 2)
            return c

        lax.fori_loop(0, tm * TOP_K // ISSUE_GROUP, issue, 0)

    @pl.when(i == 0)
    def _():
        gather(dest_ref, 0)

    @pl.when(i + 1 < pl.num_programs(0))
    def _():
        gather(dest_next_ref, 1 - slot)

    _row_copy_wait(ys_ref.at[pl.ds(0, tm * SUBLANE)], buf_ref.at[slot, 0], sem.at[slot], TOP_K)
    acc = x1_ref[...]
    mw = mw_ref[...]
    for k in range(TOP_K):
        acc = acc + mw[:, k:k + 1] * _load_token_tiles(buf_ref, (slot, k), tm)
    out_ref[...] = acc


def _combine(dest_flat, x1, mw, ys):
    n, d = x1.shape
    tm = TM_ROUTE
    nsteps = n // tm
    return pl.pallas_call(
        _combine_kernel,
        grid=(nsteps,),
        in_specs=[
            pl.BlockSpec((tm * TOP_K,), lambda i: (i,), memory_space=pltpu.SMEM),
            pl.BlockSpec((tm * TOP_K,), lambda i: (jnp.minimum(i + 1, nsteps - 1),), memory_space=pltpu.SMEM),
            pl.BlockSpec((tm, d), lambda i: (i, 0)),
            pl.BlockSpec((tm, LANE), lambda i: (i, 0)),
            pl.BlockSpec(memory_space=pl.ANY),
        ],
        out_specs=pl.BlockSpec((tm, d), lambda i: (i, 0)),
        out_shape=jax.ShapeDtypeStruct((n, d), F32),
        scratch_shapes=[pltpu.VMEM((2, TOP_K, tm * SUBLANE, LANE), F32), pltpu.SemaphoreType.DMA((2,))],
        compiler_params=pltpu.CompilerParams(dimension_semantics=("arbitrary",), vmem_limit_bytes=VMEM_LIMIT),
        name="combine",
    )(dest_flat, dest_flat, x1, mw, ys)


def _pad_cols(w, width):
    return jnp.pad(w, ((0, 0), (0, width - w.shape[1])))


def _head_slots(w, per_head):
    rows = w.shape[0]
    w3 = w.reshape(rows, MLA_HEADS, per_head)
    return jnp.pad(w3, ((0, 0), (0, 0), (0, HEAD_SLOT - per_head))).reshape(rows, MLA_HEADS * HEAD_SLOT)


def _rope_tables(positions):
    inv_freq = ROPE_THETA ** (-jnp.arange(0, QK_ROPE, 2, dtype=F32) / QK_ROPE)
    ang = positions.astype(F32).reshape(-1, 1) * inv_freq
    return jnp.cos(ang).T, jnp.sin(ang).T


def _layer(x2, positions, norm1_g, w_in, q_a_norm_g, kv_a_norm_g, w_uq, w_ukv, q_norm_g, k_norm_g,
           conv_w, w_o_mla, w_o_conv, w_o, norm2_g, router_w, router_b,
           expert_w1, expert_b1, expert_w2, expert_b2, batch, seq):
    n, d = x2.shape
    o_kr = Q_LORA + KV_LORA
    o_u = o_kr + QK_ROPE
    o_g = o_u + 3 * CONV_WIDTH
    kr_cols = jnp.pad(w_in[:, o_kr:o_u], ((0, 0), (QK_NOPE, LANE - QK_HEAD)))
    w_a = jnp.concatenate([w_in[:, :o_kr], kr_cols, w_in[:, o_u:o_g]], axis=1).astype(BF16)
    w_g = w_in[:, o_g:].astype(BF16)
    row = lambda v: v.reshape(1, -1)
    cos_c, sin_c = _rope_tables(positions)
    q_scale = (QK_HEAD ** -0.5) * math.log2(math.e)
    gain_t = lambda g: jnp.broadcast_to(g.reshape(LANE, 1), (LANE, TM_PROJ // PROJ_SUBTILES))
    kg = _pad_cols(row(k_norm_g), LANE)
    score_bound = 1.02 * q_scale * QK_HEAD * jnp.max(jnp.abs(q_norm_g)) * jnp.max(jnp.abs(k_norm_g))
    bounded = 2.0 * score_bound <= SAFE_SCORE_RANGE
    offset = jnp.where(bounded, score_bound, 0.0)
    feature = jnp.arange(LANE) == OFFSET_FEATURE
    qoff = gain_t(jnp.where(feature, -offset, 0.0).astype(F32))
    koff = jnp.where(feature, 1.0, 0.0).astype(F32).reshape(1, LANE)

    qt, k, vt, cu, bg = _in_projection(
        x2, row(norm1_g), w_a, row(q_a_norm_g), row(kv_a_norm_g),
        _head_slots(w_uq, QK_HEAD).astype(BF16), w_ukv.astype(BF16),
        gain_t(_pad_cols(row(q_norm_g) * q_scale, LANE)), kg, gain_t(kg), qoff, koff,
        cos_c, sin_c, batch, seq)
    ot = _attention(qt, k, vt, bounded)

    rw = _pad_cols(router_w, LANE)
    rw_hi = rw.astype(BF16)
    rw_lo = (rw - rw_hi.astype(F32)).astype(BF16)
    rb = jnp.concatenate([row(router_b), jnp.full((1, LANE - N_EXPERTS), NEG_BIG, F32)], axis=1)
    x1, h2, mi, mw, cnt = _mix(
        x2, row(norm1_g), w_g, ot, cu, bg, conv_w, w_o_mla.astype(BF16), w_o_conv.astype(BF16),
        w_o.astype(BF16), row(norm2_g), rw_hi, rw_lo, rb, batch, seq)

    counts = cnt[0, :N_EXPERTS].astype(jnp.int32)
    padded = (counts + EXPERT_BLOCK - 1) // EXPERT_BLOCK * EXPERT_BLOCK
    pad_end = jnp.cumsum(padded)
    pad_start = (pad_end - padded).astype(jnp.int32)
    nk = n * TOP_K
    n_blocks = (nk + N_EXPERTS * (EXPERT_BLOCK - 1) + EXPERT_BLOCK - 1) // EXPERT_BLOCK
    rows = n_blocks * EXPERT_BLOCK
    block_first_row = jnp.arange(n_blocks, dtype=jnp.int32) * EXPERT_BLOCK
    block_exp = jnp.minimum(jnp.sum(pad_end[None, :] <= block_first_row[:, None], axis=1),
                            N_EXPERTS - 1).astype(jnp.int32)
    n_used = (pad_end[-1:] // EXPERT_BLOCK).astype(jnp.int32)
    after = pad_end[block_exp] // EXPERT_BLOCK
    next_exp = jnp.where(after < n_used[0], block_exp[jnp.minimum(after, n_blocks - 1)], block_exp).astype(jnp.int32)
    e_sel = mi[None, 0:TOP_K] == jnp.arange(N_EXPERTS, dtype=jnp.int32)[:, None, None]
    dest = jnp.sum(jnp.where(e_sel, pad_start[:, None, None], 0), axis=0) + mi[TOP_K:2 * TOP_K]
    dest_flat = dest.reshape(TOP_K, n // TM_ROUTE, TM_ROUTE).transpose(1, 0, 2).reshape(nk)

    xs = _dispatch(dest_flat, pad_end.astype(jnp.int32), n_used, h2, rows)
    ys = _experts(block_exp, n_used, next_exp, xs, expert_w1, expert_b1.reshape(N_EXPERTS, 1, -1),
                  expert_w2, expert_b2.reshape(N_EXPERTS, 1, -1))
    return _combine(dest_flat, x1, mw, ys)


def kernel(x, positions, norm1_g, w_in, q_a_norm_g, kv_a_norm_g, w_uq, w_ukv, q_norm_g, k_norm_g, conv_w,
           w_o_mla, w_o_conv, w_o, norm2_g, router_w, router_b, expert_w1, expert_b1, expert_w2, expert_b2):
    batch, seq, d = x.shape
    depth = norm1_g.shape[0]
    x2 = x.reshape(batch * seq, d)
    for l in range(depth):
        x2 = _layer(x2, positions, norm1_g[l], w_in[l], q_a_norm_g[l], kv_a_norm_g[l], w_uq[l], w_ukv[l],
                    q_norm_g[l], k_norm_g[l], conv_w[l], w_o_mla[l], w_o_conv[l], w_o[l], norm2_g[l],
                    router_w[l], router_b[l], expert_w1[l], expert_b1[l], expert_w2[l], expert_b2[l], batch, seq)
    return x2.reshape(batch, seq, d)
```

```python
import functools
import math

import jax
import jax.numpy as jnp
from jax import lax
from jax.experimental import pallas as pl
from jax.experimental.pallas import tpu as pltpu

F32 = jnp.float32
BF16 = jnp.bfloat16

MLA_HEADS = 8
QK_NOPE = 64
QK_ROPE = 32
QK_HEAD = QK_NOPE + QK_ROPE
V_HEAD = 64
Q_LORA = 256
KV_LORA = 128
ROPE_THETA = 10000.0
CONV_WIDTH = 512
N_EXPERTS = 32
TOP_K = 4
SWIGLU_LIMIT = 7.0
SWIGLU_ALPHA = 1.702
EPS = 1e-6

LANE = 128
SUBLANE = 8
HEAD_SLOT = LANE
HALF_ROPE = QK_ROPE // 2
V_ROWS = V_HEAD + 16
OFFSET_FEATURE = QK_HEAD
SAFE_SCORE_RANGE = 100.0
VMEM_LIMIT = 56 * 1024 * 1024

TM_PROJ = 512
PROJ_SUBTILES = 2
TQ = 512
TKV = 512
ATTN_UNROLL = 4
TKV_BOUNDED = 512
BOUNDED_UNROLL = 16
TM_MIX = 512
TM_ROUTE = 256
EXPERT_BLOCK = 512
ISSUE_GROUP = 16
NEG_BIG = -1e30


def _load_token_tiles(ref, lead, rows):
    return jnp.concatenate([ref[lead + (pl.ds(c, rows, stride=SUBLANE), slice(None))] for c in range(SUBLANE)],
                           axis=1)


def _store_token_tiles(ref, value):
    rows = value.shape[0]
    for c in range(SUBLANE):
        ref[pl.ds(c, rows, stride=SUBLANE), :] = value[:, c * LANE:(c + 1) * LANE]


def _rms(x, g):
    return x * lax.rsqrt(jnp.mean(x * x, axis=-1, keepdims=True) + EPS) * g


def _inproj_kernel(x_ref, g1_ref, w_ref, gq_ref, gkv_ref, wuq_ref, wukv_ref, qgt_ref, kg_ref, kgt_ref,
                   qoff_ref, koff_ref, cost_ref, sint_ref, qt_ref, k_ref, vt_ref, cu_ref, bg_ref):
    tm = x_ref.shape[0] // PROJ_SUBTILES
    for part in range(PROJ_SUBTILES):
        _inproj_rows(slice(part * tm, (part + 1) * tm), x_ref, g1_ref, w_ref, gq_ref, gkv_ref, wuq_ref, wukv_ref,
                     qgt_ref, kg_ref, kgt_ref, qoff_ref, koff_ref, cost_ref, sint_ref,
                     qt_ref, k_ref, vt_ref, cu_ref, bg_ref)


def _inproj_rows(rows, x_ref, g1_ref, w_ref, gq_ref, gkv_ref, wuq_ref, wukv_ref, qgt_ref, kg_ref, kgt_ref,
                 qoff_ref, koff_ref, cost_ref, sint_ref, qt_ref, k_ref, vt_ref, cu_ref, bg_ref):
    x = x_ref[rows, :]
    h = _rms(x, g1_ref[...]).astype(BF16)
    proj = jnp.dot(h, w_ref[...], preferred_element_type=F32)
    c_q = proj[:, 0:Q_LORA]
    c_kv = proj[:, Q_LORA:Q_LORA + KV_LORA]
    kr = proj[:, Q_LORA + KV_LORA:Q_LORA + KV_LORA + LANE]
    o = Q_LORA + KV_LORA + LANE
    u = proj[:, o:o + CONV_WIDTH]
    c_gate = proj[:, o + CONV_WIDTH:o + 2 * CONV_WIDTH]
    b_gate = proj[:, o + 2 * CONV_WIDTH:o + 3 * CONV_WIDTH]
    cu_ref[rows, :] = c_gate * u
    bg_ref[rows, :] = b_gate

    kg = kg_ref[...]
    tm = x.shape[0]
    lane = lax.broadcasted_iota(jnp.int32, (tm, LANE), 1)

    q = jnp.dot(_rms(c_q, gq_ref[...]).astype(BF16), wuq_ref[...], preferred_element_type=F32)
    kv = jnp.dot(_rms(c_kv, gkv_ref[...]).astype(BF16), wukv_ref[...], preferred_element_type=F32)

    cos_c = cost_ref[:, rows]
    sin_c = sint_ref[:, rows]

    def rope_t(t):
        t1 = t[QK_NOPE:QK_NOPE + HALF_ROPE]
        t2 = t[QK_NOPE + HALF_ROPE:QK_HEAD]
        return jnp.concatenate([t[0:QK_NOPE], t1 * cos_c - t2 * sin_c, t1 * sin_c + t2 * cos_c, t[QK_HEAD:]], axis=0)

    ss_r = jnp.sum(kr * kr, axis=-1, keepdims=True)
    kr_roped = rope_t(kr.T * kgt_ref[...]).T
    qgt = qgt_ref[...]
    qoff = qoff_ref[...]
    koff = koff_ref[...]
    ones = jnp.ones((V_ROWS - V_HEAD, tm), BF16)
    for hd in range(MLA_HEADS):
        qht = q[:, hd * HEAD_SLOT:(hd + 1) * HEAD_SLOT].T
        r = lax.rsqrt(jnp.sum(qht * qht, axis=0, keepdims=True) * (1.0 / QK_HEAD) + EPS)
        qt_ref[0, hd, :, rows] = (rope_t(qht * r * qgt) + qoff).astype(BF16)

        kvh = kv[:, hd * HEAD_SLOT:(hd + 1) * HEAD_SLOT]
        knope = jnp.where(lane < QK_NOPE, kvh, 0.0)
        rk = lax.rsqrt((jnp.sum(knope * knope, axis=-1, keepdims=True) + ss_r) * (1.0 / QK_HEAD) + EPS)
        k_ref[0, hd, rows, :] = ((knope * kg + kr_roped) * rk + koff).astype(BF16)
        kvt = kvh.T
        vt_ref[0, hd, 0:V_HEAD, rows] = kvt[QK_NOPE:QK_NOPE + V_HEAD].astype(BF16)
        vt_ref[0, hd, V_HEAD:V_ROWS, rows] = ones


def _in_projection(x2, g1, w_a, gq, gkv, wuq, wukv, qgt, kg, kgt, qoff, koff, cos_c, sin_c, batch, seq):
    n, d = x2.shape
    tm = TM_PROJ
    spb = seq // tm
    full = lambda shp: pl.BlockSpec(shp, lambda i: (0,) * len(shp))
    return pl.pallas_call(
        _inproj_kernel,
        grid=(n // tm,),
        in_specs=[
            pl.BlockSpec((tm, d), lambda i: (i, 0)),
            full(g1.shape), full(w_a.shape), full(gq.shape), full(gkv.shape), full(wuq.shape), full(wukv.shape),
            full(qgt.shape), full(kg.shape), full(kgt.shape), full(qoff.shape), full(koff.shape),
            pl.BlockSpec((HALF_ROPE, tm), lambda i: (0, i)),
            pl.BlockSpec((HALF_ROPE, tm), lambda i: (0, i)),
        ],
        out_specs=[
            pl.BlockSpec((1, MLA_HEADS, HEAD_SLOT, tm), lambda i: (i // spb, 0, 0, i % spb)),
            pl.BlockSpec((1, MLA_HEADS, tm, HEAD_SLOT), lambda i: (i // spb, 0, i % spb, 0)),
            pl.BlockSpec((1, MLA_HEADS, V_ROWS, tm), lambda i: (i // spb, 0, 0, i % spb)),
            pl.BlockSpec((tm, CONV_WIDTH), lambda i: (i, 0)),
            pl.BlockSpec((tm, CONV_WIDTH), lambda i: (i, 0)),
        ],
        out_shape=[
            jax.ShapeDtypeStruct((batch, MLA_HEADS, HEAD_SLOT, seq), BF16),
            jax.ShapeDtypeStruct((batch, MLA_HEADS, seq, HEAD_SLOT), BF16),
            jax.ShapeDtypeStruct((batch, MLA_HEADS, V_ROWS, seq), BF16),
            jax.ShapeDtypeStruct((n, CONV_WIDTH), F32),
            jax.ShapeDtypeStruct((n, CONV_WIDTH), F32),
        ],
        compiler_params=pltpu.CompilerParams(dimension_semantics=("parallel",), vmem_limit_bytes=VMEM_LIMIT),
        name="in_projection",
    )(x2, g1, w_a, gq, gkv, wuq, wukv, qgt, kg, kgt, qoff, koff, cos_c, sin_c)


def _attn_kernel(qt_ref, k_ref, vt_ref, o_ref, sa_ref, sb_ref, m_ref, acc_ref, *, tq, tk):
    seq = k_ref.shape[2]
    nk = seq // tk
    bufs = (sa_ref, sb_ref)

    def query_tile(qi, carry):
        q0 = pl.multiple_of(qi * tq, tq)
        qt = qt_ref[0, 0, :, pl.ds(q0, tq)]

        def scores(c, s_ref):
            k0 = pl.multiple_of(c * tk, tk)
            s_ref[...] = jnp.dot(k_ref[0, 0, pl.ds(k0, tk), :], qt, preferred_element_type=F32)

        def accumulate(c, s_ref):
            k0 = pl.multiple_of(c * tk, tk)
            s = s_ref[...]
            m = m_ref[...]
            m_new = jnp.maximum(m, jnp.max(s, axis=0, keepdims=True))
            m_ref[...] = m_new
            p = jnp.exp2(s - m_new).astype(BF16)
            vs = vt_ref[0, 0, :, pl.ds(k0, tk)]
            acc_ref[...] = jnp.exp2(m - m_new) * acc_ref[...] + jnp.dot(vs, p, preferred_element_type=F32)

        m_ref[...] = jnp.full(m_ref.shape, NEG_BIG, F32)
        acc_ref[...] = jnp.zeros(acc_ref.shape, F32)
        scores(0, sa_ref)

        def group(j, c):
            base = ATTN_UNROLL * j
            for u in range(ATTN_UNROLL):
                scores(base + u + 1, bufs[(u + 1) % 2])
                accumulate(base + u, bufs[u % 2])
            return c

        lax.fori_loop(0, nk // ATTN_UNROLL - 1, group, 0)
        base = nk - ATTN_UNROLL
        for u in range(ATTN_UNROLL):
            if u + 1 < ATTN_UNROLL:
                scores(base + u + 1, bufs[(u + 1) % 2])
            accumulate(base + u, bufs[u % 2])
        acc = acc_ref[...]
        o_ref[0, :, pl.ds(q0, tq)] = (acc[0:V_HEAD] * (1.0 / acc[V_HEAD:V_HEAD + 1])).astype(BF16)
        return carry

    lax.fori_loop(0, seq // tq, query_tile, 0)


def _attn_bounded_kernel(qt_ref, k_ref, vt_ref, o_ref, sa_ref, sb_ref, acc_ref, *, tq, tk):
    seq = k_ref.shape[2]
    nk = seq // tk
    n_tiles = seq // tq
    n_groups = nk // BOUNDED_UNROLL
    bufs = (sa_ref, sb_ref)

    def load_qt(qi):
        return qt_ref[0, 0, :, pl.ds(pl.multiple_of(qi * tq, tq), tq)]

    def scores(c, qt, s_ref):
        k0 = pl.multiple_of(c * tk, tk)
        s_ref[...] = jnp.dot(k_ref[0, 0, pl.ds(k0, tk), :], qt, preferred_element_type=F32)

    def weighted_values(c, s_ref):
        k0 = pl.multiple_of(c * tk, tk)
        p = jnp.exp2(s_ref[...]).astype(BF16)
        return jnp.dot(vt_ref[0, 0, :, pl.ds(k0, tk)], p, preferred_element_type=F32)

    scores(0, load_qt(0), sa_ref)

    def query_tile(qi, carry):
        qt = load_qt(qi)
        qt_next = load_qt(jnp.minimum(qi + 1, n_tiles - 1))
        acc_ref[...] = jnp.zeros(acc_ref.shape, F32)

        def group(j, c):
            base = BOUNDED_UNROLL * j
            total = None
            for u in range(BOUNDED_UNROLL):
                if u + 1 < BOUNDED_UNROLL:
                    scores(base + u + 1, qt, bufs[(u + 1) % 2])
                else:
                    wraps = j == n_groups - 1
                    scores(jnp.where(wraps, 0, base + u + 1), jnp.where(wraps, qt_next, qt), bufs[(u + 1) % 2])
                part = weighted_values(base + u, bufs[u % 2])
                total = part if total is None else total + part
            acc_ref[...] += total
            return c

        lax.fori_loop(0, n_groups, group, 0)
        acc = acc_ref[...]
        o_ref[0, :, pl.ds(pl.multiple_of(qi * tq, tq), tq)] = (
            acc[0:V_HEAD] * (1.0 / acc[V_HEAD:V_HEAD + 1])).astype(BF16)
        return carry

    lax.fori_loop(0, n_tiles, query_tile, 0)


def _attention(qt, k, vt, bounded):
    batch, heads, _, seq = qt.shape
    assert ATTN_UNROLL % 2 == 0 and (seq // TKV) % ATTN_UNROLL == 0 and seq % TQ == 0
    in_specs = [
        pl.BlockSpec((1, 1, HEAD_SLOT, seq), lambda b, h, flag: (b, h, 0, 0)),
        pl.BlockSpec((1, 1, seq, HEAD_SLOT), lambda b, h, flag: (b, h, 0, 0)),
        pl.BlockSpec((1, 1, V_ROWS, seq), lambda b, h, flag: (b, h, 0, 0)),
    ]
    out_spec = pl.BlockSpec((1, V_HEAD, seq), lambda b, h, flag: (b, h, 0))
    out_shape = jax.ShapeDtypeStruct((batch, heads * V_HEAD, seq), BF16)
    grid_spec = pltpu.PrefetchScalarGridSpec(
        num_scalar_prefetch=1, grid=(batch, heads), in_specs=in_specs, out_specs=out_spec,
        scratch_shapes=[pltpu.VMEM((max(TKV, TKV_BOUNDED), TQ), F32), pltpu.VMEM((max(TKV, TKV_BOUNDED), TQ), F32),
                        pltpu.VMEM((1, TQ), F32), pltpu.VMEM((V_ROWS, TQ), F32)])
    return pl.pallas_call(
        _attn_select_kernel, grid_spec=grid_spec, out_shape=out_shape,
        compiler_params=pltpu.CompilerParams(dimension_semantics=("parallel", "parallel"),
                                             vmem_limit_bytes=VMEM_LIMIT),
        name="attention",
    )(bounded.astype(jnp.int32).reshape(1), qt, k, vt)


def _attn_select_kernel(bounded_ref, qt_ref, k_ref, vt_ref, o_ref, sa_ref, sb_ref, m_ref, acc_ref):
    @pl.when(bounded_ref[0] != 0)
    def _():
        _attn_bounded_kernel(qt_ref, k_ref, vt_ref, o_ref, sa_ref.at[0:TKV_BOUNDED], sb_ref.at[0:TKV_BOUNDED],
                             acc_ref, tq=TQ, tk=TKV_BOUNDED)

    @pl.when(bounded_ref[0] == 0)
    def _():
        _attn_kernel(qt_ref, k_ref, vt_ref, o_ref, sa_ref.at[0:TKV], sb_ref.at[0:TKV], m_ref, acc_ref,
                     tq=TQ, tk=TKV)


def _mix_kernel(x_ref, g1_ref, wg_ref, ot_ref, cu_ref, cup_ref, cun_ref, bg_ref, cw_ref,
                woa_ref, wob_ref, wo_ref, g2_ref, rwh_ref, rwl_ref, rb_ref,
                x1_ref, h2_ref, mi_ref, mw_ref, cnt_ref, tri_ref, carry_ref, *, steps_per_seq):
    i = pl.program_id(0)
    tm = x_ref.shape[0]

    @pl.when(i == 0)
    def _():
        r = lax.broadcasted_iota(jnp.int32, (tm, tm), 0)
        c = lax.broadcasted_iota(jnp.int32, (tm, tm), 1)
        tri_ref[...] = jnp.where(c < r, 1.0, 0.0).astype(BF16)
        carry_ref[...] = jnp.zeros_like(carry_ref)

    x = x_ref[...]
    h = _rms(x, g1_ref[...]).astype(BF16)
    gates = jnp.dot(h, wg_ref[...], preferred_element_type=F32)
    d = x.shape[1]
    sig_a = 1.0 / (1.0 + jnp.exp(-gates[:, 0:d]))
    sig_b = 1.0 / (1.0 + jnp.exp(-gates[:, d:2 * d]))

    y_a = lax.dot_general(ot_ref[0], woa_ref[...], (((0,), (0,)), ((), ())), preferred_element_type=F32)

    cu = cu_ref[...]
    row = lax.broadcasted_iota(jnp.int32, cu.shape, 0)
    s_in_seq = i % steps_per_seq
    prev_row = jnp.where(s_in_seq == 0, 0.0, cup_ref[7:8, :])
    next_row = jnp.where(s_in_seq == steps_per_seq - 1, 0.0, cun_ref[0:1, :])
    below = jnp.where(row == 0, prev_row, pltpu.roll(cu, 1, 0))
    above = jnp.where(row == tm - 1, next_row, pltpu.roll(cu, tm - 1, 0))
    cw = cw_ref[...]
    conv = cw[0:1, :] * below + cw[1:2, :] * cu + cw[2:3, :] * above
    y_b = jnp.dot((bg_ref[...] * conv).astype(BF16), wob_ref[...], preferred_element_type=F32)

    merged = (sig_a * y_a + sig_b * y_b).astype(BF16)
    x1 = x + jnp.dot(merged, wo_ref[...], preferred_element_type=F32)
    x1_ref[...] = x1
    h2 = _rms(x1, g2_ref[...])
    _store_token_tiles(h2_ref, h2)

    h2_hi = h2.astype(BF16)
    h2_lo = (h2 - h2_hi.astype(F32)).astype(BF16)
    logits = (jnp.dot(h2_hi, rwh_ref[...], preferred_element_type=F32)
              + jnp.dot(h2_lo, rwh_ref[...], preferred_element_type=F32)
              + jnp.dot(h2_hi, rwl_ref[...], preferred_element_type=F32)
              + rb_ref[...])

    lane = lax.broadcasted_iota(jnp.int32, logits.shape, 1)
    lane_f = lane.astype(F32)
    work = logits
    vals, idxs, hits = [], [], []
    for _ in range(TOP_K):
        mx = jnp.max(work, axis=-1, keepdims=True)
        idx = jnp.min(jnp.where(work == mx, lane_f, float(LANE)), axis=-1, keepdims=True)
        hit = lane_f == idx
        work = jnp.where(hit, -jnp.inf, work)
        vals.append(mx)
        idxs.append(idx)
        hits.append(hit)
    exps = [jnp.exp(v - vals[0]) for v in vals]
    denom = exps[0] + exps[1] + exps[2] + exps[3]
    inv = 1.0 / denom

    sel = jnp.zeros(logits.shape, F32)
    for hit in hits:
        sel = sel + jnp.where(hit, 1.0, 0.0)
    carry = carry_ref[...]
    before = jnp.dot(tri_ref[...], sel.astype(BF16), preferred_element_type=F32) + carry
    carry_new = carry + jnp.sum(sel, axis=0, keepdims=True)
    carry_ref[...] = carry_new
    cnt_ref[...] = jnp.broadcast_to(carry_new, cnt_ref.shape)

    mi = jnp.zeros(logits.shape, F32)
    mw = jnp.zeros(logits.shape, F32)
    for k in range(TOP_K):
        rank = jnp.sum(jnp.where(hits[k], before, 0.0), axis=-1, keepdims=True)
        mi = jnp.where(lane == k, idxs[k], mi)
        mi = jnp.where(lane == TOP_K + k, rank, mi)
        mw = jnp.where(lane == k, exps[k] * inv, mw)
    mi_ref[...] = mi.T[0:2 * TOP_K].astype(jnp.int32)
    mw_ref[...] = mw


def _mix(x2, g1, w_g, ot, cu, bg, conv_w, woa, wob, wo, g2, rw_hi, rw_lo, rb, batch, seq):
    n, d = x2.shape
    tm = TM_MIX
    spb = seq // tm
    r8 = tm // 8
    nsteps = n // tm
    full = lambda shp: pl.BlockSpec(shp, lambda i: (0,) * len(shp))
    return pl.pallas_call(
        functools.partial(_mix_kernel, steps_per_seq=spb),
        grid=(nsteps,),
        in_specs=[
            pl.BlockSpec((tm, d), lambda i: (i, 0)),
            full(g1.shape), full(w_g.shape),
            pl.BlockSpec((1, MLA_HEADS * V_HEAD, tm), lambda i: (i // spb, 0, i % spb)),
            pl.BlockSpec((tm, CONV_WIDTH), lambda i: (i, 0)),
            pl.BlockSpec((8, CONV_WIDTH), lambda i: (jnp.maximum(i * r8 - 1, 0), 0)),
            pl.BlockSpec((8, CONV_WIDTH), lambda i: (jnp.minimum((i + 1) * r8, nsteps * r8 - 1), 0)),
            pl.BlockSpec((tm, CONV_WIDTH), lambda i: (i, 0)),
            full(conv_w.shape), full(woa.shape), full(wob.shape), full(wo.shape), full(g2.shape),
            full(rw_hi.shape), full(rw_lo.shape), full(rb.shape),
        ],
        out_specs=[
            pl.BlockSpec((tm, d), lambda i: (i, 0)),
            pl.BlockSpec((tm * SUBLANE, LANE), lambda i: (i, 0)),
            pl.BlockSpec((2 * TOP_K, tm), lambda i: (0, i)),
            pl.BlockSpec((tm, LANE), lambda i: (i, 0)),
            pl.BlockSpec((8, LANE), lambda i: (0, 0)),
        ],
        out_shape=[
            jax.ShapeDtypeStruct((n, d), F32),
            jax.ShapeDtypeStruct((n * SUBLANE, LANE), F32),
            jax.ShapeDtypeStruct((2 * TOP_K, n), jnp.int32),
            jax.ShapeDtypeStruct((n, LANE), F32),
            jax.ShapeDtypeStruct((8, LANE), F32),
        ],
        scratch_shapes=[pltpu.VMEM((tm, tm), BF16), pltpu.VMEM((1, LANE), F32)],
        compiler_params=pltpu.CompilerParams(dimension_semantics=("arbitrary",), vmem_limit_bytes=VMEM_LIMIT),
        name="mix_route",
    )(x2, g1, w_g, ot, cu, cu, cu, bg, conv_w, woa, wob, wo, g2, rw_hi, rw_lo, rb)


def _row_copy_wait(src_like, dst_like, sem, times):
    for _ in range(times):
        pltpu.make_async_copy(src_like, dst_like, sem).wait()


def _token(ref, idx):
    return ref.at[pl.ds(pl.multiple_of(idx * SUBLANE, SUBLANE), SUBLANE)]


def _dispatch_kernel(dest_ref, pe_ref, nused_ref, h2_ref, xs_ref, zero_ref, sem, zsem):
    tm = h2_ref.shape[0] // SUBLANE
    block_rows = EXPERT_BLOCK * SUBLANE
    n_blocks = xs_ref.shape[0] // block_rows

    @pl.when(pl.program_id(0) == 0)
    def _():
        zero_ref[...] = jnp.zeros_like(zero_ref)

        def zero_copy(first_token):
            start = pl.multiple_of(first_token * SUBLANE, block_rows)
            return pltpu.make_async_copy(zero_ref, xs_ref.at[pl.ds(start, block_rows)], zsem)

        def has_rows(e):
            return pe_ref[e] > jnp.where(e == 0, 0, pe_ref[jnp.maximum(e - 1, 0)])

        def start_expert(e, c):
            @pl.when(has_rows(e))
            def _():
                zero_copy(pe_ref[e] - EXPERT_BLOCK).start()
            return c

        def wait_expert(e, c):
            @pl.when(has_rows(e))
            def _():
                zero_copy(pe_ref[e] - EXPERT_BLOCK).wait()
            return c

        def start_tail(b, c):
            zero_copy(b * EXPERT_BLOCK).start()
            return c

        def wait_tail(b, c):
            zero_copy(b * EXPERT_BLOCK).wait()
            return c

        lax.fori_loop(0, N_EXPERTS, start_expert, 0)
        lax.fori_loop(nused_ref[0], n_blocks, start_tail, 0)
        lax.fori_loop(0, N_EXPERTS, wait_expert, 0)
        lax.fori_loop(nused_ref[0], n_blocks, wait_tail, 0)

    def issue(g, c):
        for u in range(ISSUE_GROUP):
            t = g * (ISSUE_GROUP // TOP_K) + u // TOP_K
            dest = dest_ref[(u % TOP_K) * tm + t]
            pltpu.make_async_copy(_token(h2_ref, t), _token(xs_ref, dest), sem).start(priority=u % 2)
        return c

    lax.fori_loop(0, tm * TOP_K // ISSUE_GROUP, issue, 0)
    _row_copy_wait(h2_ref, xs_ref.at[pl.ds(0, tm * SUBLANE)], sem, TOP_K)


def _dispatch(dest_flat, pad_end, n_used, h2t, rows):
    tm = TM_ROUTE
    n = h2t.shape[0] // SUBLANE
    return pl.pallas_call(
        _dispatch_kernel,
        grid=(n // tm,),
        in_specs=[
            pl.BlockSpec((tm * TOP_K,), lambda i: (i,), memory_space=pltpu.SMEM),
            pl.BlockSpec(memory_space=pltpu.SMEM),
            pl.BlockSpec(memory_space=pltpu.SMEM),
            pl.BlockSpec((tm * SUBLANE, LANE), lambda i: (i, 0)),
        ],
        out_specs=pl.BlockSpec(memory_space=pl.ANY),
        out_shape=jax.ShapeDtypeStruct((rows * SUBLANE, LANE), F32),
        scratch_shapes=[pltpu.VMEM((EXPERT_BLOCK * SUBLANE, LANE), F32), pltpu.SemaphoreType.DMA(()),
                        pltpu.SemaphoreType.DMA(())],
        compiler_params=pltpu.CompilerParams(dimension_semantics=("arbitrary",), vmem_limit_bytes=VMEM_LIMIT),
        name="dispatch",
    )(dest_flat, pad_end, n_used, h2t)


def _expert_kernel(bexp_ref, nused_ref, nexp_ref, xs_ref, w1_hbm, b1_ref, w2_hbm, b2_ref, ys_ref,
                   w1f_ref, w2f_ref, w1b_ref, w2b_ref, sem):
    i = pl.program_id(0)
    active = i < nused_ref[0]
    expert = bexp_ref[i]
    prev = bexp_ref[jnp.maximum(i - 1, 0)]
    fresh = jnp.logical_or(i == 0, expert != prev)

    def weight_copies(e):
        return (pltpu.make_async_copy(w1_hbm.at[e], w1f_ref, sem.at[0]),
                pltpu.make_async_copy(w2_hbm.at[e], w2f_ref, sem.at[1]))

    @pl.when(jnp.logical_and(active, i == 0))
    def _():
        for cp in weight_copies(expert):
            cp.start()

    @pl.when(jnp.logical_and(active, fresh))
    def _():
        for cp in weight_copies(expert):
            cp.wait()
        w1b_ref[...] = w1f_ref[...].astype(BF16)
        w2b_ref[...] = w2f_ref[...].astype(BF16)

        @pl.when(nexp_ref[i] != expert)
        def _():
            for cp in weight_copies(nexp_ref[i]):
                cp.start()

    @pl.when(active)
    def _():
        dff = w2b_ref.shape[0]
        xb = _load_token_tiles(xs_ref, (), EXPERT_BLOCK).astype(BF16)
        hm = jnp.dot(xb, w1b_ref[...], preferred_element_type=F32) + b1_ref[0]
        gate = jnp.minimum(hm[:, 0:dff], SWIGLU_LIMIT)
        up = jnp.clip(hm[:, dff:2 * dff], -SWIGLU_LIMIT, SWIGLU_LIMIT)
        glu = gate * (1.0 / (1.0 + jnp.exp(-SWIGLU_ALPHA * gate)))
        act = ((up + 1.0) * glu).astype(BF16)
        _store_token_tiles(ys_ref, jnp.dot(act, w2b_ref[...], preferred_element_type=F32) + b2_ref[0])

    @pl.when(jnp.logical_not(active))
    def _():
        ys_ref[...] = jnp.zeros_like(ys_ref)


def _experts(block_exp, n_used, next_exp, xs, w1, b1, w2, b2):
    d = w1.shape[1]
    assert d == SUBLANE * LANE
    block_rows = EXPERT_BLOCK * SUBLANE
    n_blocks = xs.shape[0] // block_rows
    dff2 = w1.shape[2]
    dff = w2.shape[1]
    grid_spec = pltpu.PrefetchScalarGridSpec(
        num_scalar_prefetch=3,
        grid=(n_blocks,),
        in_specs=[
            pl.BlockSpec((block_rows, LANE), lambda i, be, nu, ne: (jnp.minimum(i, nu[0] - 1), 0)),
            pl.BlockSpec(memory_space=pl.ANY),
            pl.BlockSpec((1, 1, dff2), lambda i, be, nu, ne: (be[i], 0, 0)),
            pl.BlockSpec(memory_space=pl.ANY),
            pl.BlockSpec((1, 1, d), lambda i, be, nu, ne: (be[i], 0, 0)),
        ],
        out_specs=pl.BlockSpec((block_rows, LANE), lambda i, be, nu, ne: (i, 0)),
        scratch_shapes=[pltpu.VMEM((d, dff2), F32), pltpu.VMEM((dff, d), F32),
                        pltpu.VMEM((d, dff2), BF16), pltpu.VMEM((dff, d), BF16),
                        pltpu.SemaphoreType.DMA((2,))],
    )
    return pl.pallas_call(
        _expert_kernel,
        grid_spec=grid_spec,
        out_shape=jax.ShapeDtypeStruct(xs.shape, F32),
        compiler_params=pltpu.CompilerParams(dimension_semantics=("arbitrary",), vmem_limit_bytes=VMEM_LIMIT),
        name="experts",
    )(block_exp, n_used, next_exp, xs, w1, b1, w2, b2)


def _combine_kernel(dest_ref, dest_next_ref, x1_ref, mw_ref, ys_ref, out_ref, buf_ref, sem):
    i = pl.program_id(0)
    tm = x1_ref.shape[0]
    slot = lax.rem(i, 2)

    def gather(idx_ref, s):
        def issue(g, c):
            for u in range(ISSUE_GROUP):
                t = g * (ISSUE_GROUP // TOP_K) + u // TOP_K
                dest = idx_ref[(u % TOP_K) * tm + t]
                pltpu.make_async_copy(_token(ys_ref, dest), _token(buf_ref.at[s, u % TOP_K], t),
                                      sem.at[s]).start(priority=u % 2)
            return c

        lax.fori_loop(0, tm * TOP_K // ISSUE_GROUP, issue, 0)

    @pl.when(i == 0)
    def _():
        gather(dest_ref, 0)

    @pl.when(i + 1 < pl.num_programs(0))
    def _():
        gather(dest_next_ref, 1 - slot)

    _row_copy_wait(ys_ref.at[pl.ds(0, tm * SUBLANE)], buf_ref.at[slot, 0], sem.at[slot], TOP_K)
    acc = x1_ref[...]
    mw = mw_ref[...]
    for k in range(TOP_K):
        acc = acc + mw[:, k:k + 1] * _load_token_tiles(buf_ref, (slot, k), tm)
    out_ref[...] = acc


def _combine(dest_flat, x1, mw, ys):
    n, d = x1.shape
    tm = TM_ROUTE
    nsteps = n // tm
    return pl.pallas_call(
        _combine_kernel,
        grid=(nsteps,),
        in_specs=[
            pl.BlockSpec((tm * TOP_K,), lambda i: (i,), memory_space=pltpu.SMEM),
            pl.BlockSpec((tm * TOP_K,), lambda i: (jnp.minimum(i + 1, nsteps - 1),), memory_space=pltpu.SMEM),
            pl.BlockSpec((tm, d), lambda i: (i, 0)),
            pl.BlockSpec((tm, LANE), lambda i: (i, 0)),
            pl.BlockSpec(memory_space=pl.ANY),
        ],
        out_specs=pl.BlockSpec((tm, d), lambda i: (i, 0)),
        out_shape=jax.ShapeDtypeStruct((n, d), F32),
        scratch_shapes=[pltpu.VMEM((2, TOP_K, tm * SUBLANE, LANE), F32), pltpu.SemaphoreType.DMA((2,))],
        compiler_params=pltpu.CompilerParams(dimension_semantics=("arbitrary",), vmem_limit_bytes=VMEM_LIMIT),
        name="combine",
    )(dest_flat, dest_flat, x1, mw, ys)


def _pad_cols(w, width):
    return jnp.pad(w, ((0, 0), (0, width - w.shape[1])))


def _head_slots(w, per_head):
    rows = w.shape[0]
    w3 = w.reshape(rows, MLA_HEADS, per_head)
    return jnp.pad(w3, ((0, 0), (0, 0), (0, HEAD_SLOT - per_head))).reshape(rows, MLA_HEADS * HEAD_SLOT)


def _rope_tables(positions):
    inv_freq = ROPE_THETA ** (-jnp.arange(0, QK_ROPE, 2, dtype=F32) / QK_ROPE)
    ang = positions.astype(F32).reshape(-1, 1) * inv_freq
    return jnp.cos(ang).T, jnp.sin(ang).T


def _layer(x2, positions, norm1_g, w_in, q_a_norm_g, kv_a_norm_g, w_uq, w_ukv, q_norm_g, k_norm_g,
           conv_w, w_o_mla, w_o_conv, w_o, norm2_g, router_w, router_b,
           expert_w1, expert_b1, expert_w2, expert_b2, batch, seq):
    n, d = x2.shape
    o_kr = Q_LORA + KV_LORA
    o_u = o_kr + QK_ROPE
    o_g = o_u + 3 * CONV_WIDTH
    kr_cols = jnp.pad(w_in[:, o_kr:o_u], ((0, 0), (QK_NOPE, LANE - QK_HEAD)))
    w_a = jnp.concatenate([w_in[:, :o_kr], kr_cols, w_in[:, o_u:o_g]], axis=1).astype(BF16)
    w_g = w_in[:, o_g:].astype(BF16)
    row = lambda v: v.reshape(1, -1)
    cos_c, sin_c = _rope_tables(positions)
    q_scale = (QK_HEAD ** -0.5) * math.log2(math.e)
    gain_t = lambda g: jnp.broadcast_to(g.reshape(LANE, 1), (LANE, TM_PROJ // PROJ_SUBTILES))
    kg = _pad_cols(row(k_norm_g), LANE)
    score_bound = 1.02 * q_scale * QK_HEAD * jnp.max(jnp.abs(q_norm_g)) * jnp.max(jnp.abs(k_norm_g))
    bounded = 2.0 * score_bound <= SAFE_SCORE_RANGE
    offset = jnp.where(bounded, score_bound, 0.0)
    feature = jnp.arange(LANE) == OFFSET_FEATURE
    qoff = gain_t(jnp.where(feature, -offset, 0.0).astype(F32))
    koff = jnp.where(feature, 1.0, 0.0).astype(F32).reshape(1, LANE)

    qt, k, vt, cu, bg = _in_projection(
        x2, row(norm1_g), w_a, row(q_a_norm_g), row(kv_a_norm_g),
        _head_slots(w_uq, QK_HEAD).astype(BF16), w_ukv.astype(BF16),
        gain_t(_pad_cols(row(q_norm_g) * q_scale, LANE)), kg, gain_t(kg), qoff, koff,
        cos_c, sin_c, batch, seq)
    ot = _attention(qt, k, vt, bounded)

    rw = _pad_cols(router_w, LANE)
    rw_hi = rw.astype(BF16)
    rw_lo = (rw - rw_hi.astype(F32)).astype(BF16)
    rb = jnp.concatenate([row(router_b), jnp.full((1, LANE - N_EXPERTS), NEG_BIG, F32)], axis=1)
    x1, h2, mi, mw, cnt = _mix(
        x2, row(norm1_g), w_g, ot, cu, bg, conv_w, w_o_mla.astype(BF16), w_o_conv.astype(BF16),
        w_o.astype(BF16), row(norm2_g), rw_hi, rw_lo, rb, batch, seq)

    counts = cnt[0, :N_EXPERTS].astype(jnp.int32)
    padded = (counts + EXPERT_BLOCK - 1) // EXPERT_BLOCK * EXPERT_BLOCK
    experts = jnp.arange(N_EXPERTS, dtype=jnp.int32)
    pad_end = jnp.sum(jnp.where(experts[None, :] <= experts[:, None], padded[None, :], 0), axis=1)
    pad_start = (pad_end - padded).astype(jnp.int32)
    nk = n * TOP_K
    n_blocks = (nk + N_EXPERTS * (EXPERT_BLOCK - 1) + EXPERT_BLOCK - 1) // EXPERT_BLOCK
    rows = n_blocks * EXPERT_BLOCK
    block_first_row = jnp.arange(n_blocks, dtype=jnp.int32) * EXPERT_BLOCK
    block_exp = jnp.minimum(jnp.sum(pad_end[None, :] <= block_first_row[:, None], axis=1),
                            N_EXPERTS - 1).astype(jnp.int32)
    n_used = (pad_end[-1:] // EXPERT_BLOCK).astype(jnp.int32)
    after = pad_end[block_exp] // EXPERT_BLOCK
    next_exp = jnp.where(after < n_used[0], block_exp[jnp.minimum(after, n_blocks - 1)], block_exp).astype(jnp.int32)
    e_sel = mi[None, 0:TOP_K] == jnp.arange(N_EXPERTS, dtype=jnp.int32)[:, None, None]
    dest = jnp.sum(jnp.where(e_sel, pad_start[:, None, None], 0), axis=0) + mi[TOP_K:2 * TOP_K]
    dest_flat = dest.reshape(TOP_K, n // TM_ROUTE, TM_ROUTE).transpose(1, 0, 2).reshape(nk)

    xs = _dispatch(dest_flat, pad_end.astype(jnp.int32), n_used, h2, rows)
    ys = _experts(block_exp, n_used, next_exp, xs, expert_w1, expert_b1.reshape(N_EXPERTS, 1, -1),
                  expert_w2, expert_b2.reshape(N_EXPERTS, 1, -1))
    return _combine(dest_flat, x1, mw, ys)


def kernel(x, positions, norm1_g, w_in, q_a_norm_g, kv_a_norm_g, w_uq, w_ukv, q_norm_g, k_norm_g, conv_w,
           w_o_mla, w_o_conv, w_o, norm2_g, router_w, router_b, expert_w1, expert_b1, expert_w2, expert_b2):
    batch, seq, d = x.shape
    depth = norm1_g.shape[0]
    x2 = x.reshape(batch * seq, d)
    for l in range(depth):
        x2 = _layer(x2, positions, norm1_g[l], w_in[l], q_a_norm_g[l], kv_a_norm_g[l], w_uq[l], w_ukv[l],
                    q_norm_g[l], k_norm_g[l], conv_w[l], w_o_mla[l], w_o_conv[l], w_o[l], norm2_g[l],
                    router_w[l], router_b[l], expert_w1[l], expert_b1[l], expert_w2[l], expert_b2[l], batch, seq)
    return x2.reshape(batch, seq, d)
```

```python
import functools
import math

import jax
import jax.numpy as jnp
from jax import lax
from jax.experimental import pallas as pl
from jax.experimental.pallas import tpu as pltpu

F32 = jnp.float32
BF16 = jnp.bfloat16

MLA_HEADS = 8
QK_NOPE = 64
QK_ROPE = 32
QK_HEAD = QK_NOPE + QK_ROPE
V_HEAD = 64
Q_LORA = 256
KV_LORA = 128
ROPE_THETA = 10000.0
CONV_WIDTH = 512
N_EXPERTS = 32
TOP_K = 4
SWIGLU_LIMIT = 7.0
SWIGLU_ALPHA = 1.702
EPS = 1e-6

LANE = 128
SUBLANE = 8
HEAD_SLOT = LANE
HALF_ROPE = QK_ROPE // 2
V_ROWS = V_HEAD + 16
OFFSET_FEATURE = QK_HEAD
SAFE_SCORE_RANGE = 100.0
VMEM_LIMIT = 56 * 1024 * 1024

TM_PROJ = 512
PROJ_SUBTILES = 2
TQ = 512
TKV = 512
ATTN_UNROLL = 4
TKV_BOUNDED = 512
BOUNDED_UNROLL = 16
TM_MIX = 512
TM_ROUTE = 512
EXPERT_BLOCK = 512
ISSUE_GROUP = 16
NEG_BIG = -1e30


def _load_token_tiles(ref, lead, rows):
    return jnp.concatenate([ref[lead + (pl.ds(c, rows, stride=SUBLANE), slice(None))] for c in range(SUBLANE)],
                           axis=1)


def _store_token_tiles(ref, value):
    rows = value.shape[0]
    for c in range(SUBLANE):
        ref[pl.ds(c, rows, stride=SUBLANE), :] = value[:, c * LANE:(c + 1) * LANE]


def _rms(x, g):
    return x * lax.rsqrt(jnp.mean(x * x, axis=-1, keepdims=True) + EPS) * g


def _inproj_kernel(x_ref, g1_ref, w_ref, gq_ref, gkv_ref, wuq_ref, wukv_ref, qgt_ref, kg_ref, kgt_ref,
                   qoff_ref, koff_ref, cost_ref, sint_ref, qt_ref, k_ref, vt_ref, cu_ref, bg_ref):
    tm = x_ref.shape[0] // PROJ_SUBTILES
    for part in range(PROJ_SUBTILES):
        _inproj_rows(slice(part * tm, (part + 1) * tm), x_ref, g1_ref, w_ref, gq_ref, gkv_ref, wuq_ref, wukv_ref,
                     qgt_ref, kg_ref, kgt_ref, qoff_ref, koff_ref, cost_ref, sint_ref,
                     qt_ref, k_ref, vt_ref, cu_ref, bg_ref)


def _inproj_rows(rows, x_ref, g1_ref, w_ref, gq_ref, gkv_ref, wuq_ref, wukv_ref, qgt_ref, kg_ref, kgt_ref,
                 qoff_ref, koff_ref, cost_ref, sint_ref, qt_ref, k_ref, vt_ref, cu_ref, bg_ref):
    x = x_ref[rows, :]
    h = _rms(x, g1_ref[...]).astype(BF16)
    proj = jnp.dot(h, w_ref[...], preferred_element_type=F32)
    c_q = proj[:, 0:Q_LORA]
    c_kv = proj[:, Q_LORA:Q_LORA + KV_LORA]
    kr = proj[:, Q_LORA + KV_LORA:Q_LORA + KV_LORA + LANE]
    o = Q_LORA + KV_LORA + LANE
    u = proj[:, o:o + CONV_WIDTH]
    c_gate = proj[:, o + CONV_WIDTH:o + 2 * CONV_WIDTH]
    b_gate = proj[:, o + 2 * CONV_WIDTH:o + 3 * CONV_WIDTH]
    cu_ref[rows, :] = c_gate * u
    bg_ref[rows, :] = b_gate

    kg = kg_ref[...]
    tm = x.shape[0]
    lane = lax.broadcasted_iota(jnp.int32, (tm, LANE), 1)

    q = jnp.dot(_rms(c_q, gq_ref[...]).astype(BF16), wuq_ref[...], preferred_element_type=F32)
    kv = jnp.dot(_rms(c_kv, gkv_ref[...]).astype(BF16), wukv_ref[...], preferred_element_type=F32)

    cos_c = cost_ref[:, rows]
    sin_c = sint_ref[:, rows]

    def rope_t(t):
        t1 = t[QK_NOPE:QK_NOPE + HALF_ROPE]
        t2 = t[QK_NOPE + HALF_ROPE:QK_HEAD]
        return jnp.concatenate([t[0:QK_NOPE], t1 * cos_c - t2 * sin_c, t1 * sin_c + t2 * cos_c, t[QK_HEAD:]], axis=0)

    ss_r = jnp.sum(kr * kr, axis=-1, keepdims=True)
    kr_roped = rope_t(kr.T * kgt_ref[...]).T
    qgt = qgt_ref[...]
    qoff = qoff_ref[...]
    koff = koff_ref[...]
    ones = jnp.ones((V_ROWS - V_HEAD, tm), BF16)
    for hd in range(MLA_HEADS):
        qht = q[:, hd * HEAD_SLOT:(hd + 1) * HEAD_SLOT].T
        r = lax.rsqrt(jnp.sum(qht * qht, axis=0, keepdims=True) * (1.0 / QK_HEAD) + EPS)
        qt_ref[0, hd, :, rows] = (rope_t(qht * r * qgt) + qoff).astype(BF16)

        kvh = kv[:, hd * HEAD_SLOT:(hd + 1) * HEAD_SLOT]
        knope = jnp.where(lane < QK_NOPE, kvh, 0.0)
        rk = lax.rsqrt((jnp.sum(knope * knope, axis=-1, keepdims=True) + ss_r) * (1.0 / QK_HEAD) + EPS)
        k_ref[0, hd, rows, :] = ((knope * kg + kr_roped) * rk + koff).astype(BF16)
        kvt = kvh.T
        vt_ref[0, hd, 0:V_HEAD, rows] = kvt[QK_NOPE:QK_NOPE + V_HEAD].astype(BF16)
        vt_ref[0, hd, V_HEAD:V_ROWS, rows] = ones


def _in_projection(x2, g1, w_a, gq, gkv, wuq, wukv, qgt, kg, kgt, qoff, koff, cos_c, sin_c, batch, seq):
    n, d = x2.shape
    tm = TM_PROJ
    spb = seq // tm
    full = lambda shp: pl.BlockSpec(shp, lambda i: (0,) * len(shp))
    return pl.pallas_call(
        _inproj_kernel,
        grid=(n // tm,),
        in_specs=[
            pl.BlockSpec((tm, d), lambda i: (i, 0)),
            full(g1.shape), full(w_a.shape), full(gq.shape), full(gkv.shape), full(wuq.shape), full(wukv.shape),
            full(qgt.shape), full(kg.shape), full(kgt.shape), full(qoff.shape), full(koff.shape),
            pl.BlockSpec((HALF_ROPE, tm), lambda i: (0, i)),
            pl.BlockSpec((HALF_ROPE, tm), lambda i: (0, i)),
        ],
        out_specs=[
            pl.BlockSpec((1, MLA_HEADS, HEAD_SLOT, tm), lambda i: (i // spb, 0, 0, i % spb)),
            pl.BlockSpec((1, MLA_HEADS, tm, HEAD_SLOT), lambda i: (i // spb, 0, i % spb, 0)),
            pl.BlockSpec((1, MLA_HEADS, V_ROWS, tm), lambda i: (i // spb, 0, 0, i % spb)),
            pl.BlockSpec((tm, CONV_WIDTH), lambda i: (i, 0)),
            pl.BlockSpec((tm, CONV_WIDTH), lambda i: (i, 0)),
        ],
        out_shape=[
            jax.ShapeDtypeStruct((batch, MLA_HEADS, HEAD_SLOT, seq), BF16),
            jax.ShapeDtypeStruct((batch, MLA_HEADS, seq, HEAD_SLOT), BF16),
            jax.ShapeDtypeStruct((batch, MLA_HEADS, V_ROWS, seq), BF16),
            jax.ShapeDtypeStruct((n, CONV_WIDTH), F32),
            jax.ShapeDtypeStruct((n, CONV_WIDTH), F32),
        ],
        compiler_params=pltpu.CompilerParams(dimension_semantics=("parallel",), vmem_limit_bytes=VMEM_LIMIT),
        name="in_projection",
    )(x2, g1, w_a, gq, gkv, wuq, wukv, qgt, kg, kgt, qoff, koff, cos_c, sin_c)


def _attn_kernel(qt_ref, k_ref, vt_ref, o_ref, sa_ref, sb_ref, m_ref, acc_ref, *, tq, tk):
    seq = k_ref.shape[2]
    nk = seq // tk
    bufs = (sa_ref, sb_ref)

    def query_tile(qi, carry):
        q0 = pl.multiple_of(qi * tq, tq)
        qt = qt_ref[0, 0, :, pl.ds(q0, tq)]

        def scores(c, s_ref):
            k0 = pl.multiple_of(c * tk, tk)
            s_ref[...] = jnp.dot(k_ref[0, 0, pl.ds(k0, tk), :], qt, preferred_element_type=F32)

        def accumulate(c, s_ref):
            k0 = pl.multiple_of(c * tk, tk)
            s = s_ref[...]
            m = m_ref[...]
            m_new = jnp.maximum(m, jnp.max(s, axis=0, keepdims=True))
            m_ref[...] = m_new
            p = jnp.exp2(s - m_new).astype(BF16)
            vs = vt_ref[0, 0, :, pl.ds(k0, tk)]
            acc_ref[...] = jnp.exp2(m - m_new) * acc_ref[...] + jnp.dot(vs, p, preferred_element_type=F32)

        m_ref[...] = jnp.full(m_ref.shape, NEG_BIG, F32)
        acc_ref[...] = jnp.zeros(acc_ref.shape, F32)
        scores(0, sa_ref)

        def group(j, c):
            base = ATTN_UNROLL * j
            for u in range(ATTN_UNROLL):
                scores(base + u + 1, bufs[(u + 1) % 2])
                accumulate(base + u, bufs[u % 2])
            return c

        lax.fori_loop(0, nk // ATTN_UNROLL - 1, group, 0)
        base = nk - ATTN_UNROLL
        for u in range(ATTN_UNROLL):
            if u + 1 < ATTN_UNROLL:
                scores(base + u + 1, bufs[(u + 1) % 2])
            accumulate(base + u, bufs[u % 2])
        acc = acc_ref[...]
        o_ref[0, :, pl.ds(q0, tq)] = (acc[0:V_HEAD] * (1.0 / acc[V_HEAD:V_HEAD + 1])).astype(BF16)
        return carry

    lax.fori_loop(0, seq // tq, query_tile, 0)


def _attn_bounded_kernel(qt_ref, k_ref, vt_ref, o_ref, sa_ref, sb_ref, acc_ref, *, tq, tk):
    seq = k_ref.shape[2]
    nk = seq // tk
    n_tiles = seq // tq
    n_groups = nk // BOUNDED_UNROLL
    bufs = (sa_ref, sb_ref)

    def load_qt(qi):
        return qt_ref[0, 0, :, pl.ds(pl.multiple_of(qi * tq, tq), tq)]

    def scores(c, qt, s_ref):
        k0 = pl.multiple_of(c * tk, tk)
        s_ref[...] = jnp.dot(k_ref[0, 0, pl.ds(k0, tk), :], qt, preferred_element_type=F32)

    def weighted_values(c, s_ref):
        k0 = pl.multiple_of(c * tk, tk)
        p = jnp.exp2(s_ref[...]).astype(BF16)
        return jnp.dot(vt_ref[0, 0, :, pl.ds(k0, tk)], p, preferred_element_type=F32)

    scores(0, load_qt(0), sa_ref)

    def query_tile(qi, carry):
        qt = load_qt(qi)
        qt_next = load_qt(jnp.minimum(qi + 1, n_tiles - 1))
        acc_ref[...] = jnp.zeros(acc_ref.shape, F32)

        def group(j, c):
            base = BOUNDED_UNROLL * j
            total = None
            for u in range(BOUNDED_UNROLL):
                if u + 1 < BOUNDED_UNROLL:
                    scores(base + u + 1, qt, bufs[(u + 1) % 2])
                else:
                    wraps = j == n_groups - 1
                    scores(jnp.where(wraps, 0, base + u + 1), jnp.where(wraps, qt_next, qt), bufs[(u + 1) % 2])
                part = weighted_values(base + u, bufs[u % 2])
                total = part if total is None else total + part
            acc_ref[...] += total
            return c

        lax.fori_loop(0, n_groups, group, 0)
        acc = acc_ref[...]
        o_ref[0, :, pl.ds(pl.multiple_of(qi * tq, tq), tq)] = (
            acc[0:V_HEAD] * (1.0 / acc[V_HEAD:V_HEAD + 1])).astype(BF16)
        return carry

    lax.fori_loop(0, n_tiles, query_tile, 0)


def _attention(qt, k, vt, bounded):
    batch, heads, _, seq = qt.shape
    assert ATTN_UNROLL % 2 == 0 and (seq // TKV) % ATTN_UNROLL == 0 and seq % TQ == 0
    in_specs = [
        pl.BlockSpec((1, 1, HEAD_SLOT, seq), lambda b, h, flag: (b, h, 0, 0)),
        pl.BlockSpec((1, 1, seq, HEAD_SLOT), lambda b, h, flag: (b, h, 0, 0)),
        pl.BlockSpec((1, 1, V_ROWS, seq), lambda b, h, flag: (b, h, 0, 0)),
    ]
    out_spec = pl.BlockSpec((1, V_HEAD, seq), lambda b, h, flag: (b, h, 0))
    out_shape = jax.ShapeDtypeStruct((batch, heads * V_HEAD, seq), BF16)
    grid_spec = pltpu.PrefetchScalarGridSpec(
        num_scalar_prefetch=1, grid=(batch, heads), in_specs=in_specs, out_specs=out_spec,
        scratch_shapes=[pltpu.VMEM((max(TKV, TKV_BOUNDED), TQ), F32), pltpu.VMEM((max(TKV, TKV_BOUNDED), TQ), F32),
                        pltpu.VMEM((1, TQ), F32), pltpu.VMEM((V_ROWS, TQ), F32)])
    return pl.pallas_call(
        _attn_select_kernel, grid_spec=grid_spec, out_shape=out_shape,
        compiler_params=pltpu.CompilerParams(dimension_semantics=("parallel", "parallel"),
                                             vmem_limit_bytes=VMEM_LIMIT),
        name="attention",
    )(bounded.astype(jnp.int32).reshape(1), qt, k, vt)


def _attn_select_kernel(bounded_ref, qt_ref, k_ref, vt_ref, o_ref, sa_ref, sb_ref, m_ref, acc_ref):
    @pl.when(bounded_ref[0] != 0)
    def _():
        _attn_bounded_kernel(qt_ref, k_ref, vt_ref, o_ref, sa_ref.at[0:TKV_BOUNDED], sb_ref.at[0:TKV_BOUNDED],
                             acc_ref, tq=TQ, tk=TKV_BOUNDED)

    @pl.when(bounded_ref[0] == 0)
    def _():
        _attn_kernel(qt_ref, k_ref, vt_ref, o_ref, sa_ref.at[0:TKV], sb_ref.at[0:TKV], m_ref, acc_ref,
                     tq=TQ, tk=TKV)


def _mix_kernel(x_ref, g1_ref, wg_ref, ot_ref, cu_ref, cup_ref, cun_ref, bg_ref, cw_ref,
                woa_ref, wob_ref, wo_ref, g2_ref, rw_ref, rb_ref,
                x1_ref, h2_ref, mi_ref, mw_ref, cnt_ref, tri_ref, carry_ref, *, steps_per_seq):
    i = pl.program_id(0)
    tm = x_ref.shape[0]

    @pl.when(i == 0)
    def _():
        r = lax.broadcasted_iota(jnp.int32, (tm, tm), 0)
        c = lax.broadcasted_iota(jnp.int32, (tm, tm), 1)
        tri_ref[...] = jnp.where(c < r, 1.0, 0.0).astype(BF16)
        carry_ref[...] = jnp.zeros_like(carry_ref)

    x = x_ref[...]
    h = _rms(x, g1_ref[...]).astype(BF16)
    gates = jnp.dot(h, wg_ref[...], preferred_element_type=F32)
    d = x.shape[1]
    sig_a = 1.0 / (1.0 + jnp.exp(-gates[:, 0:d]))
    sig_b = 1.0 / (1.0 + jnp.exp(-gates[:, d:2 * d]))

    y_a = lax.dot_general(ot_ref[0], woa_ref[...], (((0,), (0,)), ((), ())), preferred_element_type=F32)

    cu = cu_ref[...]
    row = lax.broadcasted_iota(jnp.int32, cu.shape, 0)
    s_in_seq = i % steps_per_seq
    prev_row = jnp.where(s_in_seq == 0, 0.0, cup_ref[7:8, :])
    next_row = jnp.where(s_in_seq == steps_per_seq - 1, 0.0, cun_ref[0:1, :])
    below = jnp.where(row == 0, prev_row, pltpu.roll(cu, 1, 0))
    above = jnp.where(row == tm - 1, next_row, pltpu.roll(cu, tm - 1, 0))
    cw = cw_ref[...]
    conv = cw[0:1, :] * below + cw[1:2, :] * cu + cw[2:3, :] * above
    y_b = jnp.dot((bg_ref[...] * conv).astype(BF16), wob_ref[...], preferred_element_type=F32)

    merged = (sig_a * y_a + sig_b * y_b).astype(BF16)
    x1 = x + jnp.dot(merged, wo_ref[...], preferred_element_type=F32)
    x1_ref[...] = x1
    h2 = _rms(x1, g2_ref[...])
    _store_token_tiles(h2_ref, h2)

    h2_hi = h2.astype(BF16)
    h2_lo = (h2 - h2_hi.astype(F32)).astype(BF16)
    rw = rw_ref[...]
    hi_terms = jnp.dot(h2_hi, rw, preferred_element_type=F32)
    logits = (hi_terms[:, 0:LANE] + hi_terms[:, LANE:2 * LANE]
              + jnp.dot(h2_lo, rw[:, 0:LANE], preferred_element_type=F32)
              + rb_ref[...])

    lane = lax.broadcasted_iota(jnp.int32, logits.shape, 1)
    lane_f = lane.astype(F32)
    work = logits
    vals, idxs, hits = [], [], []
    for _ in range(TOP_K):
        mx = jnp.max(work, axis=-1, keepdims=True)
        idx = jnp.min(jnp.where(work == mx, lane_f, float(LANE)), axis=-1, keepdims=True)
        hit = lane_f == idx
        work = jnp.where(hit, -jnp.inf, work)
        vals.append(mx)
        idxs.append(idx)
        hits.append(hit)
    exps = [jnp.exp(v - vals[0]) for v in vals]
    denom = exps[0] + exps[1] + exps[2] + exps[3]
    inv = 1.0 / denom

    sel = jnp.zeros(logits.shape, F32)
    for hit in hits:
        sel = sel + jnp.where(hit, 1.0, 0.0)
    carry = carry_ref[...]
    before = jnp.dot(tri_ref[...], sel.astype(BF16), preferred_element_type=F32) + carry
    carry_new = carry + jnp.sum(sel, axis=0, keepdims=True)
    carry_ref[...] = carry_new
    cnt_ref[...] = jnp.broadcast_to(carry_new, cnt_ref.shape)

    mi = jnp.zeros(logits.shape, F32)
    mw = jnp.zeros(logits.shape, F32)
    for k in range(TOP_K):
        rank = jnp.sum(jnp.where(hits[k], before, 0.0), axis=-1, keepdims=True)
        mi = jnp.where(lane == k, idxs[k], mi)
        mi = jnp.where(lane == TOP_K + k, rank, mi)
        mw = jnp.where(lane == k, exps[k] * inv, mw)
    mi_ref[...] = mi.T[0:2 * TOP_K].astype(jnp.int32)
    mw_ref[...] = mw


def _mix(x2, g1, w_g, ot, cu, bg, conv_w, woa, wob, wo, g2, rw_split, rb, batch, seq):
    n, d = x2.shape
    tm = TM_MIX
    spb = seq // tm
    r8 = tm // 8
    nsteps = n // tm
    full = lambda shp: pl.BlockSpec(shp, lambda i: (0,) * len(shp))
    return pl.pallas_call(
        functools.partial(_mix_kernel, steps_per_seq=spb),
        grid=(nsteps,),
        in_specs=[
            pl.BlockSpec((tm, d), lambda i: (i, 0)),
            full(g1.shape), full(w_g.shape),
            pl.BlockSpec((1, MLA_HEADS * V_HEAD, tm), lambda i: (i // spb, 0, i % spb)),
            pl.BlockSpec((tm, CONV_WIDTH), lambda i: (i, 0)),
            pl.BlockSpec((8, CONV_WIDTH), lambda i: (jnp.maximum(i * r8 - 1, 0), 0)),
            pl.BlockSpec((8, CONV_WIDTH), lambda i: (jnp.minimum((i + 1) * r8, nsteps * r8 - 1), 0)),
            pl.BlockSpec((tm, CONV_WIDTH), lambda i: (i, 0)),
            full(conv_w.shape), full(woa.shape), full(wob.shape), full(wo.shape), full(g2.shape),
            full(rw_split.shape), full(rb.shape),
        ],
        out_specs=[
            pl.BlockSpec((tm, d), lambda i: (i, 0)),
            pl.BlockSpec((tm * SUBLANE, LANE), lambda i: (i, 0)),
            pl.BlockSpec((2 * TOP_K, tm), lambda i: (0, i)),
            pl.BlockSpec((tm, LANE), lambda i: (i, 0)),
            pl.BlockSpec((8, LANE), lambda i: (0, 0)),
        ],
        out_shape=[
            jax.ShapeDtypeStruct((n, d), F32),
            jax.ShapeDtypeStruct((n * SUBLANE, LANE), F32),
            jax.ShapeDtypeStruct((2 * TOP_K, n), jnp.int32),
            jax.ShapeDtypeStruct((n, LANE), F32),
            jax.ShapeDtypeStruct((8, LANE), F32),
        ],
        scratch_shapes=[pltpu.VMEM((tm, tm), BF16), pltpu.VMEM((1, LANE), F32)],
        compiler_params=pltpu.CompilerParams(dimension_semantics=("arbitrary",), vmem_limit_bytes=VMEM_LIMIT),
        name="mix_route",
    )(x2, g1, w_g, ot, cu, cu, cu, bg, conv_w, woa, wob, wo, g2, rw_split, rb)


def _row_copy_wait(src_like, dst_like, sem, times):
    for _ in range(times):
        pltpu.make_async_copy(src_like, dst_like, sem).wait()


def _token(ref, idx):
    return ref.at[pl.ds(pl.multiple_of(idx * SUBLANE, SUBLANE), SUBLANE)]


def _dispatch_kernel(dest_ref, pe_ref, nused_ref, h2_ref, xs_ref, zero_ref, sem, zsem):
    tm = h2_ref.shape[0] // SUBLANE
    block_rows = EXPERT_BLOCK * SUBLANE
    n_blocks = xs_ref.shape[0] // block_rows

    @pl.when(pl.program_id(0) == 0)
    def _():
        zero_ref[...] = jnp.zeros_like(zero_ref)

        def zero_copy(first_token):
            start = pl.multiple_of(first_token * SUBLANE, block_rows)
            return pltpu.make_async_copy(zero_ref, xs_ref.at[pl.ds(start, block_rows)], zsem)

        def has_rows(e):
            return pe_ref[e] > jnp.where(e == 0, 0, pe_ref[jnp.maximum(e - 1, 0)])

        def start_expert(e, c):
            @pl.when(has_rows(e))
            def _():
                zero_copy(pe_ref[e] - EXPERT_BLOCK).start()
            return c

        def wait_expert(e, c):
            @pl.when(has_rows(e))
            def _():
                zero_copy(pe_ref[e] - EXPERT_BLOCK).wait()
            return c

        def start_tail(b, c):
            zero_copy(b * EXPERT_BLOCK).start()
            return c

        def wait_tail(b, c):
            zero_copy(b * EXPERT_BLOCK).wait()
            return c

        lax.fori_loop(0, N_EXPERTS, start_expert, 0)
        lax.fori_loop(nused_ref[0], n_blocks, start_tail, 0)
        lax.fori_loop(0, N_EXPERTS, wait_expert, 0)
        lax.fori_loop(nused_ref[0], n_blocks, wait_tail, 0)

    def issue(g, c):
        for u in range(ISSUE_GROUP):
            t = g * (ISSUE_GROUP // TOP_K) + u // TOP_K
            dest = dest_ref[(u % TOP_K) * tm + t]
            pltpu.make_async_copy(_token(h2_ref, t), _token(xs_ref, dest), sem).start(priority=u % 2)
        return c

    lax.fori_loop(0, tm * TOP_K // ISSUE_GROUP, issue, 0)
    _row_copy_wait(h2_ref, xs_ref.at[pl.ds(0, tm * SUBLANE)], sem, TOP_K)


def _dispatch(dest_flat, pad_end, n_used, h2t, rows):
    tm = TM_ROUTE
    n = h2t.shape[0] // SUBLANE
    return pl.pallas_call(
        _dispatch_kernel,
        grid=(n // tm,),
        in_specs=[
            pl.BlockSpec((tm * TOP_K,), lambda i: (i,), memory_space=pltpu.SMEM),
            pl.BlockSpec(memory_space=pltpu.SMEM),
            pl.BlockSpec(memory_space=pltpu.SMEM),
            pl.BlockSpec((tm * SUBLANE, LANE), lambda i: (i, 0)),
        ],
        out_specs=pl.BlockSpec(memory_space=pl.ANY),
        out_shape=jax.ShapeDtypeStruct((rows * SUBLANE, LANE), F32),
        scratch_shapes=[pltpu.VMEM((EXPERT_BLOCK * SUBLANE, LANE), F32), pltpu.SemaphoreType.DMA(()),
                        pltpu.SemaphoreType.DMA(())],
        compiler_params=pltpu.CompilerParams(dimension_semantics=("arbitrary",), vmem_limit_bytes=VMEM_LIMIT),
        name="dispatch",
    )(dest_flat, pad_end, n_used, h2t)


def _expert_kernel(bexp_ref, nused_ref, nexp_ref, xs_ref, w1_hbm, b1_ref, w2_hbm, b2_ref, ys_ref,
                   w1f_ref, w2f_ref, w1b_ref, w2b_ref, sem):
    i = pl.program_id(0)
    active = i < nused_ref[0]
    expert = bexp_ref[i]
    prev = bexp_ref[jnp.maximum(i - 1, 0)]
    fresh = jnp.logical_or(i == 0, expert != prev)

    def weight_copies(e):
        return (pltpu.make_async_copy(w1_hbm.at[e], w1f_ref, sem.at[0]),
                pltpu.make_async_copy(w2_hbm.at[e], w2f_ref, sem.at[1]))

    @pl.when(jnp.logical_and(active, i == 0))
    def _():
        for cp in weight_copies(expert):
            cp.start()

    @pl.when(jnp.logical_and(active, fresh))
    def _():
        for cp in weight_copies(expert):
            cp.wait()
        w1b_ref[...] = w1f_ref[...].astype(BF16)
        w2b_ref[...] = w2f_ref[...].astype(BF16)

        @pl.when(nexp_ref[i] != expert)
        def _():
            for cp in weight_copies(nexp_ref[i]):
                cp.start()

    @pl.when(active)
    def _():
        dff = w2b_ref.shape[0]
        xb = _load_token_tiles(xs_ref, (), EXPERT_BLOCK).astype(BF16)
        hm = jnp.dot(xb, w1b_ref[...], preferred_element_type=F32) + b1_ref[0]
        gate = jnp.minimum(hm[:, 0:dff], SWIGLU_LIMIT)
        up = jnp.clip(hm[:, dff:2 * dff], -SWIGLU_LIMIT, SWIGLU_LIMIT)
        glu = gate * (1.0 / (1.0 + jnp.exp(-SWIGLU_ALPHA * gate)))
        act = ((up + 1.0) * glu).astype(BF16)
        _store_token_tiles(ys_ref, jnp.dot(act, w2b_ref[...], preferred_element_type=F32) + b2_ref[0])

    @pl.when(jnp.logical_not(active))
    def _():
        ys_ref[...] = jnp.zeros_like(ys_ref)


def _experts(block_exp, n_used, next_exp, xs, w1, b1, w2, b2):
    d = w1.shape[1]
    assert d == SUBLANE * LANE
    block_rows = EXPERT_BLOCK * SUBLANE
    n_blocks = xs.shape[0] // block_rows
    dff2 = w1.shape[2]
    dff = w2.shape[1]
    grid_spec = pltpu.PrefetchScalarGridSpec(
        num_scalar_prefetch=3,
        grid=(n_blocks,),
        in_specs=[
            pl.BlockSpec((block_rows, LANE), lambda i, be, nu, ne: (jnp.minimum(i, nu[0] - 1), 0)),
            pl.BlockSpec(memory_space=pl.ANY),
            pl.BlockSpec((1, 1, dff2), lambda i, be, nu, ne: (be[i], 0, 0)),
            pl.BlockSpec(memory_space=pl.ANY),
            pl.BlockSpec((1, 1, d), lambda i, be, nu, ne: (be[i], 0, 0)),
        ],
        out_specs=pl.BlockSpec((block_rows, LANE), lambda i, be, nu, ne: (i, 0)),
        scratch_shapes=[pltpu.VMEM((d, dff2), F32), pltpu.VMEM((dff, d), F32),
                        pltpu.VMEM((d, dff2), BF16), pltpu.VMEM((dff, d), BF16),
                        pltpu.SemaphoreType.DMA((2,))],
    )
    return pl.pallas_call(
        _expert_kernel,
        grid_spec=grid_spec,
        out_shape=jax.ShapeDtypeStruct(xs.shape, F32),
        compiler_params=pltpu.CompilerParams(dimension_semantics=("arbitrary",), vmem_limit_bytes=VMEM_LIMIT),
        name="experts",
    )(block_exp, n_used, next_exp, xs, w1, b1, w2, b2)


def _combine_kernel(dest_ref, dest_next_ref, x1_ref, mw_ref, ys_ref, out_ref, buf_ref, sem):
    i = pl.program_id(0)
    tm = x1_ref.shape[0]
    slot = lax.rem(i, 2)

    def gather(idx_ref, s):
        def issue(g, c):
            for u in range(ISSUE_GROUP):
                t = g * (ISSUE_GROUP // TOP_K) + u // TOP_K
                dest = idx_ref[(u % TOP_K) * tm + t]
                pltpu.make_async_copy(_token(ys_ref, dest), _token(buf_ref.at[s, u % TOP_K], t),
                                      sem.at[s]).start(priority=u % 2)
            return c

        lax.fori_loop(0, tm * TOP_K // ISSUE_GROUP, issue, 0)

    @pl.when(i == 0)
    def _():
        gather(dest_ref, 0)

    @pl.when(i + 1 < pl.num_programs(0))
    def _():
        gather(dest_next_ref, 1 - slot)

    _row_copy_wait(ys_ref.at[pl.ds(0, tm * SUBLANE)], buf_ref.at[slot, 0], sem.at[slot], TOP_K)
    acc = x1_ref[...]
    mw = mw_ref[...]
    for k in range(TOP_K):
        acc = acc + mw[:, k:k + 1] * _load_token_tiles(buf_ref, (slot, k), tm)
    out_ref[...] = acc


def _combine(dest_flat, x1, mw, ys):
    n, d = x1.shape
    tm = TM_ROUTE
    nsteps = n // tm
    return pl.pallas_call(
        _combine_kernel,
        grid=(nsteps,),
        in_specs=[
            pl.BlockSpec((tm * TOP_K,), lambda i: (i,), memory_space=pltpu.SMEM),
            pl.BlockSpec((tm * TOP_K,), lambda i: (jnp.minimum(i + 1, nsteps - 1),), memory_space=pltpu.SMEM),
            pl.BlockSpec((tm, d), lambda i: (i, 0)),
            pl.BlockSpec((tm, LANE), lambda i: (i, 0)),
            pl.BlockSpec(memory_space=pl.ANY),
        ],
        out_specs=pl.BlockSpec((tm, d), lambda i: (i, 0)),
        out_shape=jax.ShapeDtypeStruct((n, d), F32),
        scratch_shapes=[pltpu.VMEM((2, TOP_K, tm * SUBLANE, LANE), F32), pltpu.SemaphoreType.DMA((2,))],
        compiler_params=pltpu.CompilerParams(dimension_semantics=("arbitrary",), vmem_limit_bytes=VMEM_LIMIT),
        name="combine",
    )(dest_flat, dest_flat, x1, mw, ys)


def _pad_cols(w, width):
    return jnp.pad(w, ((0, 0), (0, width - w.shape[1])))


def _head_slots(w, per_head):
    rows = w.shape[0]
    w3 = w.reshape(rows, MLA_HEADS, per_head)
    return jnp.pad(w3, ((0, 0), (0, 0), (0, HEAD_SLOT - per_head))).reshape(rows, MLA_HEADS * HEAD_SLOT)


def _rope_tables(positions):
    inv_freq = ROPE_THETA ** (-jnp.arange(0, QK_ROPE, 2, dtype=F32) / QK_ROPE)
    ang = positions.astype(F32).reshape(-1, 1) * inv_freq
    return jnp.cos(ang).T, jnp.sin(ang).T


def _layer(x2, positions, norm1_g, w_in, q_a_norm_g, kv_a_norm_g, w_uq, w_ukv, q_norm_g, k_norm_g,
           conv_w, w_o_mla, w_o_conv, w_o, norm2_g, router_w, router_b,
           expert_w1, expert_b1, expert_w2, expert_b2, batch, seq):
    n, d = x2.shape
    o_kr = Q_LORA + KV_LORA
    o_u = o_kr + QK_ROPE
    o_g = o_u + 3 * CONV_WIDTH
    kr_cols = jnp.pad(w_in[:, o_kr:o_u], ((0, 0), (QK_NOPE, LANE - QK_HEAD)))
    w_a = jnp.concatenate([w_in[:, :o_kr], kr_cols, w_in[:, o_u:o_g]], axis=1).astype(BF16)
    w_g = w_in[:, o_g:].astype(BF16)
    row = lambda v: v.reshape(1, -1)
    cos_c, sin_c = _rope_tables(positions)
    q_scale = (QK_HEAD ** -0.5) * math.log2(math.e)
    gain_t = lambda g: jnp.broadcast_to(g.reshape(LANE, 1), (LANE, TM_PROJ // PROJ_SUBTILES))
    kg = _pad_cols(row(k_norm_g), LANE)
    score_bound = 1.02 * q_scale * QK_HEAD * jnp.max(jnp.abs(q_norm_g)) * jnp.max(jnp.abs(k_norm_g))
    bounded = 2.0 * score_bound <= SAFE_SCORE_RANGE
    offset = jnp.where(bounded, score_bound, 0.0)
    feature = jnp.arange(LANE) == OFFSET_FEATURE
    qoff = gain_t(jnp.where(feature, -offset, 0.0).astype(F32))
    koff = jnp.where(feature, 1.0, 0.0).astype(F32).reshape(1, LANE)

    qt, k, vt, cu, bg = _in_projection(
        x2, row(norm1_g), w_a, row(q_a_norm_g), row(kv_a_norm_g),
        _head_slots(w_uq, QK_HEAD).astype(BF16), w_ukv.astype(BF16),
        gain_t(_pad_cols(row(q_norm_g) * q_scale, LANE)), kg, gain_t(kg), qoff, koff,
        cos_c, sin_c, batch, seq)
    ot = _attention(qt, k, vt, bounded)

    rw = _pad_cols(router_w, LANE)
    rw_hi = rw.astype(BF16)
    rw_lo = (rw - rw_hi.astype(F32)).astype(BF16)
    rb = jnp.concatenate([row(router_b), jnp.full((1, LANE - N_EXPERTS), NEG_BIG, F32)], axis=1)
    x1, h2, mi, mw, cnt = _mix(
        x2, row(norm1_g), w_g, ot, cu, bg, conv_w, w_o_mla.astype(BF16), w_o_conv.astype(BF16),
        w_o.astype(BF16), row(norm2_g), jnp.concatenate([rw_hi, rw_lo], axis=1), rb, batch, seq)

    counts = cnt[0, :N_EXPERTS].astype(jnp.int32)
    padded = (counts + EXPERT_BLOCK - 1) // EXPERT_BLOCK * EXPERT_BLOCK
    experts = jnp.arange(N_EXPERTS, dtype=jnp.int32)
    pad_end = jnp.sum(jnp.where(experts[None, :] <= experts[:, None], padded[None, :], 0), axis=1)
    pad_start = (pad_end - padded).astype(jnp.int32)
    nk = n * TOP_K
    n_blocks = (nk + N_EXPERTS * (EXPERT_BLOCK - 1) + EXPERT_BLOCK - 1) // EXPERT_BLOCK
    rows = n_blocks * EXPERT_BLOCK
    block_first_row = jnp.arange(n_blocks, dtype=jnp.int32) * EXPERT_BLOCK
    block_exp = jnp.minimum(jnp.sum(pad_end[None, :] <= block_first_row[:, None], axis=1),
                            N_EXPERTS - 1).astype(jnp.int32)
    n_used = (pad_end[-1:] // EXPERT_BLOCK).astype(jnp.int32)
    group_end = jnp.sum(jnp.where(experts[None, :] == block_exp[:, None], pad_end[None, :], 0), axis=1)
    following = jnp.minimum(jnp.sum(pad_end[None, :] <= group_end[:, None], axis=1), N_EXPERTS - 1)
    next_exp = jnp.where(group_end < pad_end[-1], following, block_exp).astype(jnp.int32)
    e_sel = mi[None, 0:TOP_K] == jnp.arange(N_EXPERTS, dtype=jnp.int32)[:, None, None]
    dest = jnp.sum(jnp.where(e_sel, pad_start[:, None, None], 0), axis=0) + mi[TOP_K:2 * TOP_K]
    dest_flat = dest.reshape(TOP_K, n // TM_ROUTE, TM_ROUTE).transpose(1, 0, 2).reshape(nk)

    xs = _dispatch(dest_flat, pad_end.astype(jnp.int32), n_used, h2, rows)
    ys = _experts(block_exp, n_used, next_exp, xs, expert_w1, expert_b1.reshape(N_EXPERTS, 1, -1),
                  expert_w2, expert_b2.reshape(N_EXPERTS, 1, -1))
    return _combine(dest_flat, x1, mw, ys)


def kernel(x, positions, norm1_g, w_in, q_a_norm_g, kv_a_norm_g, w_uq, w_ukv, q_norm_g, k_norm_g, conv_w,
           w_o_mla, w_o_conv, w_o, norm2_g, router_w, router_b, expert_w1, expert_b1, expert_w2, expert_b2):
    batch, seq, d = x.shape
    depth = norm1_g.shape[0]
    x2 = x.reshape(batch * seq, d)
    for l in range(depth):
        x2 = _layer(x2, positions, norm1_g[l], w_in[l], q_a_norm_g[l], kv_a_norm_g[l], w_uq[l], w_ukv[l],
                    q_norm_g[l], k_norm_g[l], conv_w[l], w_o_mla[l], w_o_conv[l], w_o[l], norm2_g[l],
                    router_w[l], router_b[l], expert_w1[l], expert_b1[l], expert_w2[l], expert_b2[l], batch, seq)
    return x2.reshape(batch, seq, d)
```

```python
import functools
import math

import jax
import jax.numpy as jnp
from jax import lax
from jax.experimental import pallas as pl
from jax.experimental.pallas import tpu as pltpu

F32 = jnp.float32
BF16 = jnp.bfloat16

MLA_HEADS = 8
QK_NOPE = 64
QK_ROPE = 32
QK_HEAD = QK_NOPE + QK_ROPE
V_HEAD = 64
Q_LORA = 256
KV_LORA = 128
ROPE_THETA = 10000.0
CONV_WIDTH = 512
N_EXPERTS = 32
TOP_K = 4
SWIGLU_LIMIT = 7.0
SWIGLU_ALPHA = 1.702
EPS = 1e-6

LANE = 128
SUBLANE = 8
HEAD_SLOT = LANE
HALF_ROPE = QK_ROPE // 2
V_ROWS = V_HEAD + 16
OFFSET_FEATURE = QK_HEAD
SAFE_SCORE_RANGE = 100.0
VMEM_LIMIT = 56 * 1024 * 1024

TM_PROJ = 512
PROJ_SUBTILES = 2
TQ = 512
TKV = 512
ATTN_UNROLL = 4
TKV_BOUNDED = 512
BOUNDED_UNROLL = 16
TM_MIX = 512
MIX_SUBTILES = 1
TM_ROUTE = 512
EXPERT_BLOCK = 512
ISSUE_GROUP = 16
NEG_BIG = -1e30


def _load_token_tiles(ref, lead, rows):
    return jnp.concatenate([ref[lead + (pl.ds(c, rows, stride=SUBLANE), slice(None))] for c in range(SUBLANE)],
                           axis=1)


def _store_token_tiles(ref, value):
    rows = value.shape[0]
    for c in range(SUBLANE):
        ref[pl.ds(c, rows, stride=SUBLANE), :] = value[:, c * LANE:(c + 1) * LANE]


def _rms(x, g):
    return x * lax.rsqrt(jnp.mean(x * x, axis=-1, keepdims=True) + EPS) * g


def _inproj_kernel(x_ref, g1_ref, w_ref, gq_ref, gkv_ref, wuq_ref, wukv_ref, qgt_ref, kg_ref, kgt_ref,
                   qoff_ref, koff_ref, cost_ref, sint_ref, qt_ref, k_ref, vt_ref, cu_ref, bg_ref):
    tm = x_ref.shape[0] // PROJ_SUBTILES
    for part in range(PROJ_SUBTILES):
        _inproj_rows(slice(part * tm, (part + 1) * tm), x_ref, g1_ref, w_ref, gq_ref, gkv_ref, wuq_ref, wukv_ref,
                     qgt_ref, kg_ref, kgt_ref, qoff_ref, koff_ref, cost_ref, sint_ref,
                     qt_ref, k_ref, vt_ref, cu_ref, bg_ref)


def _inproj_rows(rows, x_ref, g1_ref, w_ref, gq_ref, gkv_ref, wuq_ref, wukv_ref, qgt_ref, kg_ref, kgt_ref,
                 qoff_ref, koff_ref, cost_ref, sint_ref, qt_ref, k_ref, vt_ref, cu_ref, bg_ref):
    x = x_ref[rows, :]
    h = _rms(x, g1_ref[...]).astype(BF16)
    proj = jnp.dot(h, w_ref[...], preferred_element_type=F32)
    c_q = proj[:, 0:Q_LORA]
    c_kv = proj[:, Q_LORA:Q_LORA + KV_LORA]
    kr = proj[:, Q_LORA + KV_LORA:Q_LORA + KV_LORA + LANE]
    o = Q_LORA + KV_LORA + LANE
    u = proj[:, o:o + CONV_WIDTH]
    c_gate = proj[:, o + CONV_WIDTH:o + 2 * CONV_WIDTH]
    b_gate = proj[:, o + 2 * CONV_WIDTH:o + 3 * CONV_WIDTH]
    cu_ref[rows, :] = c_gate * u
    bg_ref[rows, :] = b_gate

    kg = kg_ref[...]
    tm = x.shape[0]
    lane = lax.broadcasted_iota(jnp.int32, (tm, LANE), 1)

    q = jnp.dot(_rms(c_q, gq_ref[...]).astype(BF16), wuq_ref[...], preferred_element_type=F32)
    kv = jnp.dot(_rms(c_kv, gkv_ref[...]).astype(BF16), wukv_ref[...], preferred_element_type=F32)

    cos_c = cost_ref[:, rows]
    sin_c = sint_ref[:, rows]

    def rope_t(t):
        t1 = t[QK_NOPE:QK_NOPE + HALF_ROPE]
        t2 = t[QK_NOPE + HALF_ROPE:QK_HEAD]
        return jnp.concatenate([t[0:QK_NOPE], t1 * cos_c - t2 * sin_c, t1 * sin_c + t2 * cos_c, t[QK_HEAD:]], axis=0)

    ss_r = jnp.sum(kr * kr, axis=-1, keepdims=True)
    kr_roped = rope_t(kr.T * kgt_ref[...]).T
    qgt = qgt_ref[...]
    qoff = qoff_ref[...]
    koff = koff_ref[...]
    ones = jnp.ones((V_ROWS - V_HEAD, tm), BF16)
    for hd in range(MLA_HEADS):
        qht = q[:, hd * HEAD_SLOT:(hd + 1) * HEAD_SLOT].T
        r = lax.rsqrt(jnp.sum(qht * qht, axis=0, keepdims=True) * (1.0 / QK_HEAD) + EPS)
        qt_ref[0, hd, :, rows] = (rope_t(qht * r * qgt) + qoff).astype(BF16)

        kvh = kv[:, hd * HEAD_SLOT:(hd + 1) * HEAD_SLOT]
        knope = jnp.where(lane < QK_NOPE, kvh, 0.0)
        rk = lax.rsqrt((jnp.sum(knope * knope, axis=-1, keepdims=True) + ss_r) * (1.0 / QK_HEAD) + EPS)
        k_ref[0, hd, rows, :] = ((knope * kg + kr_roped) * rk + koff).astype(BF16)
        kvt = kvh.T
        vt_ref[0, hd, 0:V_HEAD, rows] = kvt[QK_NOPE:QK_NOPE + V_HEAD].astype(BF16)
        vt_ref[0, hd, V_HEAD:V_ROWS, rows] = ones


def _in_projection(x2, g1, w_a, gq, gkv, wuq, wukv, qgt, kg, kgt, qoff, koff, cos_c, sin_c, batch, seq):
    n, d = x2.shape
    tm = TM_PROJ
    spb = seq // tm
    full = lambda shp: pl.BlockSpec(shp, lambda i: (0,) * len(shp))
    return pl.pallas_call(
        _inproj_kernel,
        grid=(n // tm,),
        in_specs=[
            pl.BlockSpec((tm, d), lambda i: (i, 0)),
            full(g1.shape), full(w_a.shape), full(gq.shape), full(gkv.shape), full(wuq.shape), full(wukv.shape),
            full(qgt.shape), full(kg.shape), full(kgt.shape), full(qoff.shape), full(koff.shape),
            pl.BlockSpec((HALF_ROPE, tm), lambda i: (0, i)),
            pl.BlockSpec((HALF_ROPE, tm), lambda i: (0, i)),
        ],
        out_specs=[
            pl.BlockSpec((1, MLA_HEADS, HEAD_SLOT, tm), lambda i: (i // spb, 0, 0, i % spb)),
            pl.BlockSpec((1, MLA_HEADS, tm, HEAD_SLOT), lambda i: (i // spb, 0, i % spb, 0)),
            pl.BlockSpec((1, MLA_HEADS, V_ROWS, tm), lambda i: (i // spb, 0, 0, i % spb)),
            pl.BlockSpec((tm, CONV_WIDTH), lambda i: (i, 0)),
            pl.BlockSpec((tm, CONV_WIDTH), lambda i: (i, 0)),
        ],
        out_shape=[
            jax.ShapeDtypeStruct((batch, MLA_HEADS, HEAD_SLOT, seq), BF16),
            jax.ShapeDtypeStruct((batch, MLA_HEADS, seq, HEAD_SLOT), BF16),
            jax.ShapeDtypeStruct((batch, MLA_HEADS, V_ROWS, seq), BF16),
            jax.ShapeDtypeStruct((n, CONV_WIDTH), F32),
            jax.ShapeDtypeStruct((n, CONV_WIDTH), F32),
        ],
        compiler_params=pltpu.CompilerParams(dimension_semantics=("parallel",), vmem_limit_bytes=VMEM_LIMIT),
        name="in_projection",
    )(x2, g1, w_a, gq, gkv, wuq, wukv, qgt, kg, kgt, qoff, koff, cos_c, sin_c)


def _attn_kernel(qt_ref, k_ref, vt_ref, o_ref, sa_ref, sb_ref, m_ref, acc_ref, *, tq, tk):
    seq = k_ref.shape[2]
    nk = seq // tk
    bufs = (sa_ref, sb_ref)

    def query_tile(qi, carry):
        q0 = pl.multiple_of(qi * tq, tq)
        qt = qt_ref[0, 0, :, pl.ds(q0, tq)]

        def scores(c, s_ref):
            k0 = pl.multiple_of(c * tk, tk)
            s_ref[...] = jnp.dot(k_ref[0, 0, pl.ds(k0, tk), :], qt, preferred_element_type=F32)

        def accumulate(c, s_ref):
            k0 = pl.multiple_of(c * tk, tk)
            s = s_ref[...]
            m = m_ref[...]
            m_new = jnp.maximum(m, jnp.max(s, axis=0, keepdims=True))
            m_ref[...] = m_new
            p = jnp.exp2(s - m_new).astype(BF16)
            vs = vt_ref[0, 0, :, pl.ds(k0, tk)]
            acc_ref[...] = jnp.exp2(m - m_new) * acc_ref[...] + jnp.dot(vs, p, preferred_element_type=F32)

        m_ref[...] = jnp.full(m_ref.shape, NEG_BIG, F32)
        acc_ref[...] = jnp.zeros(acc_ref.shape, F32)
        scores(0, sa_ref)

        def group(j, c):
            base = ATTN_UNROLL * j
            for u in range(ATTN_UNROLL):
                scores(base + u + 1, bufs[(u + 1) % 2])
                accumulate(base + u, bufs[u % 2])
            return c

        lax.fori_loop(0, nk // ATTN_UNROLL - 1, group, 0)
        base = nk - ATTN_UNROLL
        for u in range(ATTN_UNROLL):
            if u + 1 < ATTN_UNROLL:
                scores(base + u + 1, bufs[(u + 1) % 2])
            accumulate(base + u, bufs[u % 2])
        acc = acc_ref[...]
        o_ref[0, :, pl.ds(q0, tq)] = (acc[0:V_HEAD] * (1.0 / acc[V_HEAD:V_HEAD + 1])).astype(BF16)
        return carry

    lax.fori_loop(0, seq // tq, query_tile, 0)


def _attn_bounded_kernel(qt_ref, k_ref, vt_ref, o_ref, sa_ref, sb_ref, acc_ref, *, tq, tk):
    seq = k_ref.shape[2]
    nk = seq // tk
    n_tiles = seq // tq
    n_groups = nk // BOUNDED_UNROLL
    bufs = (sa_ref, sb_ref)

    def load_qt(qi):
        return qt_ref[0, 0, :, pl.ds(pl.multiple_of(qi * tq, tq), tq)]

    def scores(c, qt, s_ref):
        k0 = pl.multiple_of(c * tk, tk)
        s_ref[...] = jnp.dot(k_ref[0, 0, pl.ds(k0, tk), :], qt, preferred_element_type=F32)

    def weighted_values(c, s_ref):
        k0 = pl.multiple_of(c * tk, tk)
        p = jnp.exp2(s_ref[...]).astype(BF16)
        return jnp.dot(vt_ref[0, 0, :, pl.ds(k0, tk)], p, preferred_element_type=F32)

    scores(0, load_qt(0), sa_ref)

    def query_tile(qi, carry):
        qt = load_qt(qi)
        qt_next = load_qt(jnp.minimum(qi + 1, n_tiles - 1))
        acc_ref[...] = jnp.zeros(acc_ref.shape, F32)

        def group(j, c):
            base = BOUNDED_UNROLL * j
            total = None
            for u in range(BOUNDED_UNROLL):
                if u + 1 < BOUNDED_UNROLL:
                    scores(base + u + 1, qt, bufs[(u + 1) % 2])
                else:
                    wraps = j == n_groups - 1
                    scores(jnp.where(wraps, 0, base + u + 1), jnp.where(wraps, qt_next, qt), bufs[(u + 1) % 2])
                part = weighted_values(base + u, bufs[u % 2])
                total = part if total is None else total + part
            acc_ref[...] += total
            return c

        lax.fori_loop(0, n_groups, group, 0)
        acc = acc_ref[...]
        o_ref[0, :, pl.ds(pl.multiple_of(qi * tq, tq), tq)] = (
            acc[0:V_HEAD] * (1.0 / acc[V_HEAD:V_HEAD + 1])).astype(BF16)
        return carry

    lax.fori_loop(0, n_tiles, query_tile, 0)


def _attention(qt, k, vt, bounded):
    batch, heads, _, seq = qt.shape
    assert ATTN_UNROLL % 2 == 0 and (seq // TKV) % ATTN_UNROLL == 0 and seq % TQ == 0
    in_specs = [
        pl.BlockSpec((1, 1, HEAD_SLOT, seq), lambda b, h, flag: (b, h, 0, 0)),
        pl.BlockSpec((1, 1, seq, HEAD_SLOT), lambda b, h, flag: (b, h, 0, 0)),
        pl.BlockSpec((1, 1, V_ROWS, seq), lambda b, h, flag: (b, h, 0, 0)),
    ]
    out_spec = pl.BlockSpec((1, V_HEAD, seq), lambda b, h, flag: (b, h, 0))
    out_shape = jax.ShapeDtypeStruct((batch, heads * V_HEAD, seq), BF16)
    grid_spec = pltpu.PrefetchScalarGridSpec(
        num_scalar_prefetch=1, grid=(batch, heads), in_specs=in_specs, out_specs=out_spec,
        scratch_shapes=[pltpu.VMEM((max(TKV, TKV_BOUNDED), TQ), F32), pltpu.VMEM((max(TKV, TKV_BOUNDED), TQ), F32),
                        pltpu.VMEM((1, TQ), F32), pltpu.VMEM((V_ROWS, TQ), F32)])
    return pl.pallas_call(
        _attn_select_kernel, grid_spec=grid_spec, out_shape=out_shape,
        compiler_params=pltpu.CompilerParams(dimension_semantics=("parallel", "parallel"),
                                             vmem_limit_bytes=VMEM_LIMIT),
        name="attention",
    )(bounded.astype(jnp.int32).reshape(1), qt, k, vt)


def _attn_select_kernel(bounded_ref, qt_ref, k_ref, vt_ref, o_ref, sa_ref, sb_ref, m_ref, acc_ref):
    @pl.when(bounded_ref[0] != 0)
    def _():
        _attn_bounded_kernel(qt_ref, k_ref, vt_ref, o_ref, sa_ref.at[0:TKV_BOUNDED], sb_ref.at[0:TKV_BOUNDED],
                             acc_ref, tq=TQ, tk=TKV_BOUNDED)

    @pl.when(bounded_ref[0] == 0)
    def _():
        _attn_kernel(qt_ref, k_ref, vt_ref, o_ref, sa_ref.at[0:TKV], sb_ref.at[0:TKV], m_ref, acc_ref,
                     tq=TQ, tk=TKV)


def _mix_kernel(x_ref, g1_ref, wg_ref, ot_ref, cu_ref, cup_ref, cun_ref, bg_ref, cw_ref,
                woa_ref, wob_ref, wo_ref, g2_ref, rw_ref, rb_ref,
                x1_ref, h2_ref, mi_ref, mw_ref, cnt_ref, tri_ref, carry_ref, *, steps_per_seq):
    i = pl.program_id(0)
    tm = x_ref.shape[0]
    ts = tm // MIX_SUBTILES

    @pl.when(i == 0)
    def _():
        r = lax.broadcasted_iota(jnp.int32, (ts, ts), 0)
        c = lax.broadcasted_iota(jnp.int32, (ts, ts), 1)
        tri_ref[...] = jnp.where(c < r, 1.0, 0.0).astype(BF16)
        carry_ref[...] = jnp.zeros_like(carry_ref)

    cu = cu_ref[...]
    row = lax.broadcasted_iota(jnp.int32, cu.shape, 0)
    s_in_seq = i % steps_per_seq
    prev_row = jnp.where(s_in_seq == 0, 0.0, cup_ref[7:8, :])
    next_row = jnp.where(s_in_seq == steps_per_seq - 1, 0.0, cun_ref[0:1, :])
    below = jnp.where(row == 0, prev_row, pltpu.roll(cu, 1, 0))
    above = jnp.where(row == tm - 1, next_row, pltpu.roll(cu, tm - 1, 0))
    cw = cw_ref[...]
    gated_conv = (bg_ref[...] * (cw[0:1, :] * below + cw[1:2, :] * cu + cw[2:3, :] * above)).astype(BF16)

    carry = carry_ref[...]
    for part in range(MIX_SUBTILES):
        carry = _mix_rows(part * ts, ts, gated_conv[part * ts:(part + 1) * ts], carry, x_ref, g1_ref, wg_ref,
                          ot_ref, woa_ref, wob_ref, wo_ref, g2_ref, rw_ref, rb_ref, tri_ref,
                          x1_ref, h2_ref, mi_ref, mw_ref)
    carry_ref[...] = carry
    cnt_ref[...] = jnp.broadcast_to(carry, cnt_ref.shape)


def _mix_rows(r0, ts, gated_conv, carry, x_ref, g1_ref, wg_ref, ot_ref, woa_ref, wob_ref, wo_ref, g2_ref,
              rw_ref, rb_ref, tri_ref, x1_ref, h2_ref, mi_ref, mw_ref):
    rows = slice(r0, r0 + ts)
    y_a = lax.dot_general(ot_ref[0, :, rows], woa_ref[...], (((0,), (0,)), ((), ())), preferred_element_type=F32)
    y_b = jnp.dot(gated_conv, wob_ref[...], preferred_element_type=F32)

    x = x_ref[rows, :]
    h = _rms(x, g1_ref[...]).astype(BF16)
    gates = jnp.dot(h, wg_ref[...], preferred_element_type=F32)
    d = x.shape[1]
    sig_a = 1.0 / (1.0 + jnp.exp(-gates[:, 0:d]))
    sig_b = 1.0 / (1.0 + jnp.exp(-gates[:, d:2 * d]))

    merged = (sig_a * y_a + sig_b * y_b).astype(BF16)
    x1 = x + jnp.dot(merged, wo_ref[...], preferred_element_type=F32)
    x1_ref[rows, :] = x1
    h2 = _rms(x1, g2_ref[...])
    _store_token_tiles(h2_ref.at[r0 * SUBLANE:(r0 + ts) * SUBLANE], h2)

    h2_hi = h2.astype(BF16)
    h2_lo = (h2 - h2_hi.astype(F32)).astype(BF16)
    rw = rw_ref[...]
    hi_terms = jnp.dot(h2_hi, rw, preferred_element_type=F32)
    logits = (hi_terms[:, 0:LANE] + hi_terms[:, LANE:2 * LANE]
              + jnp.dot(h2_lo, rw[:, 0:LANE], preferred_element_type=F32)
              + rb_ref[...])

    lane = lax.broadcasted_iota(jnp.int32, logits.shape, 1)
    lane_f = lane.astype(F32)
    work = logits
    vals, idxs, hits = [], [], []
    for _ in range(TOP_K):
        mx = jnp.max(work, axis=-1, keepdims=True)
        idx = jnp.min(jnp.where(work == mx, lane_f, float(LANE)), axis=-1, keepdims=True)
        hit = lane_f == idx
        work = jnp.where(hit, -jnp.inf, work)
        vals.append(mx)
        idxs.append(idx)
        hits.append(hit)
    exps = [jnp.exp(v - vals[0]) for v in vals]
    denom = exps[0] + exps[1] + exps[2] + exps[3]
    inv = 1.0 / denom

    sel = jnp.zeros(logits.shape, F32)
    for hit in hits:
        sel = sel + jnp.where(hit, 1.0, 0.0)
    before = jnp.dot(tri_ref[...], sel.astype(BF16), preferred_element_type=F32) + carry

    mi = jnp.zeros(logits.shape, F32)
    mw = jnp.zeros(logits.shape, F32)
    for k in range(TOP_K):
        rank = jnp.sum(jnp.where(hits[k], before, 0.0), axis=-1, keepdims=True)
        mi = jnp.where(lane == k, idxs[k], mi)
        mi = jnp.where(lane == TOP_K + k, rank, mi)
        mw = jnp.where(lane == k, exps[k] * inv, mw)
    mi_ref[:, rows] = mi.T[0:2 * TOP_K].astype(jnp.int32)
    mw_ref[rows, :] = mw
    return carry + jnp.sum(sel, axis=0, keepdims=True)


def _mix(x2, g1, w_g, ot, cu, bg, conv_w, woa, wob, wo, g2, rw_split, rb, batch, seq):
    n, d = x2.shape
    tm = TM_MIX
    spb = seq // tm
    r8 = tm // 8
    nsteps = n // tm
    full = lambda shp: pl.BlockSpec(shp, lambda i: (0,) * len(shp))
    return pl.pallas_call(
        functools.partial(_mix_kernel, steps_per_seq=spb),
        grid=(nsteps,),
        in_specs=[
            pl.BlockSpec((tm, d), lambda i: (i, 0)),
            full(g1.shape), full(w_g.shape),
            pl.BlockSpec((1, MLA_HEADS * V_HEAD, tm), lambda i: (i // spb, 0, i % spb)),
            pl.BlockSpec((tm, CONV_WIDTH), lambda i: (i, 0)),
            pl.BlockSpec((8, CONV_WIDTH), lambda i: (jnp.maximum(i * r8 - 1, 0), 0)),
            pl.BlockSpec((8, CONV_WIDTH), lambda i: (jnp.minimum((i + 1) * r8, nsteps * r8 - 1), 0)),
            pl.BlockSpec((tm, CONV_WIDTH), lambda i: (i, 0)),
            full(conv_w.shape), full(woa.shape), full(wob.shape), full(wo.shape), full(g2.shape),
            full(rw_split.shape), full(rb.shape),
        ],
        out_specs=[
            pl.BlockSpec((tm, d), lambda i: (i, 0)),
            pl.BlockSpec((tm * SUBLANE, LANE), lambda i: (i, 0)),
            pl.BlockSpec((2 * TOP_K, tm), lambda i: (0, i)),
            pl.BlockSpec((tm, LANE), lambda i: (i, 0)),
            pl.BlockSpec((8, LANE), lambda i: (0, 0)),
        ],
        out_shape=[
            jax.ShapeDtypeStruct((n, d), F32),
            jax.ShapeDtypeStruct((n * SUBLANE, LANE), F32),
            jax.ShapeDtypeStruct((2 * TOP_K, n), jnp.int32),
            jax.ShapeDtypeStruct((n, LANE), F32),
            jax.ShapeDtypeStruct((8, LANE), F32),
        ],
        scratch_shapes=[pltpu.VMEM((tm // MIX_SUBTILES, tm // MIX_SUBTILES), BF16), pltpu.VMEM((1, LANE), F32)],
        compiler_params=pltpu.CompilerParams(dimension_semantics=("arbitrary",), vmem_limit_bytes=VMEM_LIMIT),
        name="mix_route",
    )(x2, g1, w_g, ot, cu, cu, cu, bg, conv_w, woa, wob, wo, g2, rw_split, rb)


def _row_copy_wait(src_like, dst_like, sem, times):
    for _ in range(times):
        pltpu.make_async_copy(src_like, dst_like, sem).wait()


def _token(ref, idx):
    return ref.at[pl.ds(pl.multiple_of(idx * SUBLANE, SUBLANE), SUBLANE)]


def _dispatch_kernel(dest_ref, pe_ref, nused_ref, h2_ref, xs_ref, zero_ref, sem, zsem):
    tm = h2_ref.shape[0] // SUBLANE
    block_rows = EXPERT_BLOCK * SUBLANE
    n_blocks = xs_ref.shape[0] // block_rows

    @pl.when(pl.program_id(0) == 0)
    def _():
        zero_ref[...] = jnp.zeros_like(zero_ref)

        def zero_copy(first_token):
            start = pl.multiple_of(first_token * SUBLANE, block_rows)
            return pltpu.make_async_copy(zero_ref, xs_ref.at[pl.ds(start, block_rows)], zsem)

        def has_rows(e):
            return pe_ref[e] > jnp.where(e == 0, 0, pe_ref[jnp.maximum(e - 1, 0)])

        def start_expert(e, c):
            @pl.when(has_rows(e))
            def _():
                zero_copy(pe_ref[e] - EXPERT_BLOCK).start()
            return c

        def wait_expert(e, c):
            @pl.when(has_rows(e))
            def _():
                zero_copy(pe_ref[e] - EXPERT_BLOCK).wait()
            return c

        def start_tail(b, c):
            zero_copy(b * EXPERT_BLOCK).start()
            return c

        def wait_tail(b, c):
            zero_copy(b * EXPERT_BLOCK).wait()
            return c

        lax.fori_loop(0, N_EXPERTS, start_expert, 0)
        lax.fori_loop(nused_ref[0], n_blocks, start_tail, 0)
        lax.fori_loop(0, N_EXPERTS, wait_expert, 0)
        lax.fori_loop(nused_ref[0], n_blocks, wait_tail, 0)

    def issue(g, c):
        for u in range(ISSUE_GROUP):
            t = g * (ISSUE_GROUP // TOP_K) + u // TOP_K
            dest = dest_ref[(u % TOP_K) * tm + t]
            pltpu.make_async_copy(_token(h2_ref, t), _token(xs_ref, dest), sem).start(priority=u % 2)
        return c

    lax.fori_loop(0, tm * TOP_K // ISSUE_GROUP, issue, 0)
    _row_copy_wait(h2_ref, xs_ref.at[pl.ds(0, tm * SUBLANE)], sem, TOP_K)


def _dispatch(dest_flat, pad_end, n_used, h2t, rows):
    tm = TM_ROUTE
    n = h2t.shape[0] // SUBLANE
    return pl.pallas_call(
        _dispatch_kernel,
        grid=(n // tm,),
        in_specs=[
            pl.BlockSpec((tm * TOP_K,), lambda i: (i,), memory_space=pltpu.SMEM),
            pl.BlockSpec(memory_space=pltpu.SMEM),
            pl.BlockSpec(memory_space=pltpu.SMEM),
            pl.BlockSpec((tm * SUBLANE, LANE), lambda i: (i, 0)),
        ],
        out_specs=pl.BlockSpec(memory_space=pl.ANY),
        out_shape=jax.ShapeDtypeStruct((rows * SUBLANE, LANE), F32),
        scratch_shapes=[pltpu.VMEM((EXPERT_BLOCK * SUBLANE, LANE), F32), pltpu.SemaphoreType.DMA(()),
                        pltpu.SemaphoreType.DMA(())],
        compiler_params=pltpu.CompilerParams(dimension_semantics=("arbitrary",), vmem_limit_bytes=VMEM_LIMIT),
        name="dispatch",
    )(dest_flat, pad_end, n_used, h2t)


def _expert_kernel(bexp_ref, nused_ref, nexp_ref, slot_ref, xs_ref, w1_hbm, b1_ref, w2_hbm, b2_ref, ys_ref,
                   w1_ref, w2_ref, sem):
    i = pl.program_id(0)
    active = i < nused_ref[0]
    expert = bexp_ref[i]
    slot = slot_ref[i]
    prev = bexp_ref[jnp.maximum(i - 1, 0)]
    fresh = jnp.logical_or(i == 0, expert != prev)

    def weight_copies(e, s):
        return (pltpu.make_async_copy(w1_hbm.at[e], w1_ref.at[s], sem.at[0, s]),
                pltpu.make_async_copy(w2_hbm.at[e], w2_ref.at[s], sem.at[1, s]))

    @pl.when(jnp.logical_and(active, i == 0))
    def _():
        for cp in weight_copies(expert, slot):
            cp.start()

    @pl.when(jnp.logical_and(active, fresh))
    def _():
        for cp in weight_copies(expert, slot):
            cp.wait()

        @pl.when(nexp_ref[i] != expert)
        def _():
            for cp in weight_copies(nexp_ref[i], 1 - slot):
                cp.start()

    @pl.when(active)
    def _():
        dff = w2_ref.shape[1]
        xb = _load_token_tiles(xs_ref, (), EXPERT_BLOCK).astype(BF16)
        hm = jnp.dot(xb, w1_ref[slot].astype(BF16), preferred_element_type=F32) + b1_ref[0]
        gate = jnp.minimum(hm[:, 0:dff], SWIGLU_LIMIT)
        up = jnp.clip(hm[:, dff:2 * dff], -SWIGLU_LIMIT, SWIGLU_LIMIT)
        glu = gate * (1.0 / (1.0 + jnp.exp(-SWIGLU_ALPHA * gate)))
        act = ((up + 1.0) * glu).astype(BF16)
        _store_token_tiles(ys_ref, jnp.dot(act, w2_ref[slot].astype(BF16), preferred_element_type=F32) + b2_ref[0])

    @pl.when(jnp.logical_not(active))
    def _():
        ys_ref[...] = jnp.zeros_like(ys_ref)


def _experts(block_exp, n_used, next_exp, weight_slot, xs, w1, b1, w2, b2):
    d = w1.shape[1]
    assert d == SUBLANE * LANE
    block_rows = EXPERT_BLOCK * SUBLANE
    n_blocks = xs.shape[0] // block_rows
    dff2 = w1.shape[2]
    dff = w2.shape[1]
    grid_spec = pltpu.PrefetchScalarGridSpec(
        num_scalar_prefetch=4,
        grid=(n_blocks,),
        in_specs=[
            pl.BlockSpec((block_rows, LANE), lambda i, be, nu, ne, ws: (jnp.minimum(i, nu[0] - 1), 0)),
            pl.BlockSpec(memory_space=pl.ANY),
            pl.BlockSpec((1, 1, dff2), lambda i, be, nu, ne, ws: (be[i], 0, 0)),
            pl.BlockSpec(memory_space=pl.ANY),
            pl.BlockSpec((1, 1, d), lambda i, be, nu, ne, ws: (be[i], 0, 0)),
        ],
        out_specs=pl.BlockSpec((block_rows, LANE), lambda i, be, nu, ne, ws: (i, 0)),
        scratch_shapes=[pltpu.VMEM((2, d, dff2), F32), pltpu.VMEM((2, dff, d), F32),
                        pltpu.SemaphoreType.DMA((2, 2))],
    )
    return pl.pallas_call(
        _expert_kernel,
        grid_spec=grid_spec,
        out_shape=jax.ShapeDtypeStruct(xs.shape, F32),
        compiler_params=pltpu.CompilerParams(dimension_semantics=("arbitrary",), vmem_limit_bytes=VMEM_LIMIT),
        name="experts",
    )(block_exp, n_used, next_exp, weight_slot, xs, w1, b1, w2, b2)


def _combine_kernel(dest_ref, dest_next_ref, x1_ref, mw_ref, ys_ref, out_ref, buf_ref, sem):
    i = pl.program_id(0)
    tm = x1_ref.shape[0]
    slot = lax.rem(i, 2)

    def gather(idx_ref, s):
        def issue(g, c):
            for u in range(ISSUE_GROUP):
                t = g * (ISSUE_GROUP // TOP_K) + u // TOP_K
                dest = idx_ref[(u % TOP_K) * tm + t]
                pltpu.make_async_copy(_token(ys_ref, dest), _token(buf_ref.at[s, u % TOP_K], t),
                                      sem.at[s]).start(priority=u % 2)
            return c

        lax.fori_loop(0, tm * TOP_K // ISSUE_GROUP, issue, 0)

    @pl.when(i == 0)
    def _():
        gather(dest_ref, 0)

    @pl.when(i + 1 < pl.num_programs(0))
    def _():
        gather(dest_next_ref, 1 - slot)

    _row_copy_wait(ys_ref.at[pl.ds(0, tm * SUBLANE)], buf_ref.at[slot, 0], sem.at[slot], TOP_K)
    acc = x1_ref[...]
    mw = mw_ref[...]
    for k in range(TOP_K):
        acc = acc + mw[:, k:k + 1] * _load_token_tiles(buf_ref, (slot, k), tm)
    out_ref[...] = acc


def _combine(dest_flat, x1, mw, ys):
    n, d = x1.shape
    tm = TM_ROUTE
    nsteps = n // tm
    return pl.pallas_call(
        _combine_kernel,
        grid=(nsteps,),
        in_specs=[
            pl.BlockSpec((tm * TOP_K,), lambda i: (i,), memory_space=pltpu.SMEM),
            pl.BlockSpec((tm * TOP_K,), lambda i: (jnp.minimum(i + 1, nsteps - 1),), memory_space=pltpu.SMEM),
            pl.BlockSpec((tm, d), lambda i: (i, 0)),
            pl.BlockSpec((tm, LANE), lambda i: (i, 0)),
            pl.BlockSpec(memory_space=pl.ANY),
        ],
        out_specs=pl.BlockSpec((tm, d), lambda i: (i, 0)),
        out_shape=jax.ShapeDtypeStruct((n, d), F32),
        scratch_shapes=[pltpu.VMEM((2, TOP_K, tm * SUBLANE, LANE), F32), pltpu.SemaphoreType.DMA((2,))],
        compiler_params=pltpu.CompilerParams(dimension_semantics=("arbitrary",), vmem_limit_bytes=VMEM_LIMIT),
        name="combine",
    )(dest_flat, dest_flat, x1, mw, ys)


def _pad_cols(w, width):
    return jnp.pad(w, ((0, 0), (0, width - w.shape[1])))


def _head_slots(w, per_head):
    rows = w.shape[0]
    w3 = w.reshape(rows, MLA_HEADS, per_head)
    return jnp.pad(w3, ((0, 0), (0, 0), (0, HEAD_SLOT - per_head))).reshape(rows, MLA_HEADS * HEAD_SLOT)


def _rope_tables(positions):
    inv_freq = ROPE_THETA ** (-jnp.arange(0, QK_ROPE, 2, dtype=F32) / QK_ROPE)
    ang = positions.astype(F32).reshape(-1, 1) * inv_freq
    return jnp.cos(ang).T, jnp.sin(ang).T


def _layer(x2, positions, norm1_g, w_in, q_a_norm_g, kv_a_norm_g, w_uq, w_ukv, q_norm_g, k_norm_g,
           conv_w, w_o_mla, w_o_conv, w_o, norm2_g, router_w, router_b,
           expert_w1, expert_b1, expert_w2, expert_b2, batch, seq):
    n, d = x2.shape
    o_kr = Q_LORA + KV_LORA
    o_u = o_kr + QK_ROPE
    o_g = o_u + 3 * CONV_WIDTH
    kr_cols = jnp.pad(w_in[:, o_kr:o_u], ((0, 0), (QK_NOPE, LANE - QK_HEAD)))
    w_a = jnp.concatenate([w_in[:, :o_kr], kr_cols, w_in[:, o_u:o_g]], axis=1).astype(BF16)
    w_g = w_in[:, o_g:].astype(BF16)
    row = lambda v: v.reshape(1, -1)
    cos_c, sin_c = _rope_tables(positions)
    q_scale = (QK_HEAD ** -0.5) * math.log2(math.e)
    gain_t = lambda g: jnp.broadcast_to(g.reshape(LANE, 1), (LANE, TM_PROJ // PROJ_SUBTILES))
    kg = _pad_cols(row(k_norm_g), LANE)
    score_bound = 1.02 * q_scale * QK_HEAD * jnp.max(jnp.abs(q_norm_g)) * jnp.max(jnp.abs(k_norm_g))
    bounded = 2.0 * score_bound <= SAFE_SCORE_RANGE
    offset = jnp.where(bounded, score_bound, 0.0)
    feature = jnp.arange(LANE) == OFFSET_FEATURE
    qoff = gain_t(jnp.where(feature, -offset, 0.0).astype(F32))
    koff = jnp.where(feature, 1.0, 0.0).astype(F32).reshape(1, LANE)

    qt, k, vt, cu, bg = _in_projection(
        x2, row(norm1_g), w_a, row(q_a_norm_g), row(kv_a_norm_g),
        _head_slots(w_uq, QK_HEAD).astype(BF16), w_ukv.astype(BF16),
        gain_t(_pad_cols(row(q_norm_g) * q_scale, LANE)), kg, gain_t(kg), qoff, koff,
        cos_c, sin_c, batch, seq)
    ot = _attention(qt, k, vt, bounded)

    rw = _pad_cols(router_w, LANE)
    rw_hi = rw.astype(BF16)
    rw_lo = (rw - rw_hi.astype(F32)).astype(BF16)
    rb = jnp.concatenate([row(router_b), jnp.full((1, LANE - N_EXPERTS), NEG_BIG, F32)], axis=1)
    x1, h2, mi, mw, cnt = _mix(
        x2, row(norm1_g), w_g, ot, cu, bg, conv_w, w_o_mla.astype(BF16), w_o_conv.astype(BF16),
        w_o.astype(BF16), row(norm2_g), jnp.concatenate([rw_hi, rw_lo], axis=1), rb, batch, seq)

    counts = cnt[0, :N_EXPERTS].astype(jnp.int32)
    padded = (counts + EXPERT_BLOCK - 1) // EXPERT_BLOCK * EXPERT_BLOCK
    experts = jnp.arange(N_EXPERTS, dtype=jnp.int32)
    pad_end = jnp.sum(jnp.where(experts[None, :] <= experts[:, None], padded[None, :], 0), axis=1)
    pad_start = (pad_end - padded).astype(jnp.int32)
    nk = n * TOP_K
    n_blocks = (nk + N_EXPERTS * (EXPERT_BLOCK - 1) + EXPERT_BLOCK - 1) // EXPERT_BLOCK
    rows = n_blocks * EXPERT_BLOCK
    block_first_row = jnp.arange(n_blocks, dtype=jnp.int32) * EXPERT_BLOCK
    block_exp = jnp.minimum(jnp.sum(pad_end[None, :] <= block_first_row[:, None], axis=1),
                            N_EXPERTS - 1).astype(jnp.int32)
    n_used = (pad_end[-1:] // EXPERT_BLOCK).astype(jnp.int32)
    group_end = jnp.sum(jnp.where(experts[None, :] == block_exp[:, None], pad_end[None, :], 0), axis=1)
    following = jnp.minimum(jnp.sum(pad_end[None, :] <= group_end[:, None], axis=1), N_EXPERTS - 1)
    next_exp = jnp.where(group_end < pad_end[-1], following, block_exp).astype(jnp.int32)
    ordinal = jnp.sum(jnp.where(experts[None, :] < experts[:, None], (padded > 0)[None, :], False), axis=1)
    weight_slot = jnp.sum(jnp.where(experts[None, :] == block_exp[:, None], (ordinal % 2)[None, :], 0),
                          axis=1).astype(jnp.int32)
    e_sel = mi[None, 0:TOP_K] == jnp.arange(N_EXPERTS, dtype=jnp.int32)[:, None, None]
    dest = jnp.sum(jnp.where(e_sel, pad_start[:, None, None], 0), axis=0) + mi[TOP_K:2 * TOP_K]
    dest_flat = dest.reshape(TOP_K, n // TM_ROUTE, TM_ROUTE).transpose(1, 0, 2).reshape(nk)

    xs = _dispatch(dest_flat, pad_end.astype(jnp.int32), n_used, h2, rows)
    ys = _experts(block_exp, n_used, next_exp, weight_slot, xs, expert_w1, expert_b1.reshape(N_EXPERTS, 1, -1),
                  expert_w2, expert_b2.reshape(N_EXPERTS, 1, -1))
    return _combine(dest_flat, x1, mw, ys)


def kernel(x, positions, norm1_g, w_in, q_a_norm_g, kv_a_norm_g, w_uq, w_ukv, q_norm_g, k_norm_g, conv_w,
           w_o_mla, w_o_conv, w_o, norm2_g, router_w, router_b, expert_w1, expert_b1, expert_w2, expert_b2):
    batch, seq, d = x.shape
    depth = norm1_g.shape[0]
    x2 = x.reshape(batch * seq, d)
    for l in range(depth):
        x2 = _layer(x2, positions, norm1_g[l], w_in[l], q_a_norm_g[l], kv_a_norm_g[l], w_uq[l], w_ukv[l],
                    q_norm_g[l], k_norm_g[l], conv_w[l], w_o_mla[l], w_o_conv[l], w_o[l], norm2_g[l],
                    router_w[l], router_b[l], expert_w1[l], expert_b1[l], expert_w2[l], expert_b2[l], batch, seq)
    return x2.reshape(batch, seq, d)
```

```python
import functools
import math

import jax
import jax.numpy as jnp
from jax import lax
from jax.experimental import pallas as pl
from jax.experimental.pallas import tpu as pltpu

F32 = jnp.float32
BF16 = jnp.bfloat16

MLA_HEADS = 8
QK_NOPE = 64
QK_ROPE = 32
QK_HEAD = QK_NOPE + QK_ROPE
V_HEAD = 64
Q_LORA = 256
KV_LORA = 128
ROPE_THETA = 10000.0
CONV_WIDTH = 512
N_EXPERTS = 32
TOP_K = 4
SWIGLU_LIMIT = 7.0
SWIGLU_ALPHA = 1.702
EPS = 1e-6

LANE = 128
SUBLANE = 8
HEAD_SLOT = LANE
HALF_ROPE = QK_ROPE // 2
V_ROWS = V_HEAD + 16
OFFSET_FEATURE = QK_HEAD
SAFE_SCORE_RANGE = 100.0
VMEM_LIMIT = 56 * 1024 * 1024

TM_PROJ = 512
PROJ_SUBTILES = 2
TQ = 512
TKV = 512
ATTN_UNROLL = 4
TKV_BOUNDED = 512
BOUNDED_UNROLL = 16
TM_MIX = 512
MIX_SUBTILES = 1
TM_ROUTE = 512
EXPERT_BLOCK = 512
ISSUE_GROUP = 16
NEG_BIG = -1e30


def _load_token_tiles(ref, lead, rows):
    return jnp.concatenate([ref[lead + (pl.ds(c, rows, stride=SUBLANE), slice(None))] for c in range(SUBLANE)],
                           axis=1)


def _store_token_tiles(ref, value):
    rows = value.shape[0]
    for c in range(SUBLANE):
        ref[pl.ds(c, rows, stride=SUBLANE), :] = value[:, c * LANE:(c + 1) * LANE]


def _rms(x, g):
    return x * lax.rsqrt(jnp.mean(x * x, axis=-1, keepdims=True) + EPS) * g


def _inproj_kernel(x_ref, g1_ref, w_ref, gq_ref, gkv_ref, wuq_ref, wukv_ref, qgt_ref, kg_ref, kgt_ref,
                   qoff_ref, koff_ref, cost_ref, sint_ref, qt_ref, k_ref, vt_ref, cu_ref, bg_ref):
    tm = x_ref.shape[0] // PROJ_SUBTILES
    for part in range(PROJ_SUBTILES):
        _inproj_rows(slice(part * tm, (part + 1) * tm), x_ref, g1_ref, w_ref, gq_ref, gkv_ref, wuq_ref, wukv_ref,
                     qgt_ref, kg_ref, kgt_ref, qoff_ref, koff_ref, cost_ref, sint_ref,
                     qt_ref, k_ref, vt_ref, cu_ref, bg_ref)


def _inproj_rows(rows, x_ref, g1_ref, w_ref, gq_ref, gkv_ref, wuq_ref, wukv_ref, qgt_ref, kg_ref, kgt_ref,
                 qoff_ref, koff_ref, cost_ref, sint_ref, qt_ref, k_ref, vt_ref, cu_ref, bg_ref):
    x = x_ref[rows, :]
    h = _rms(x, g1_ref[...]).astype(BF16)
    proj = jnp.dot(h, w_ref[...], preferred_element_type=F32)
    c_q = proj[:, 0:Q_LORA]
    c_kv = proj[:, Q_LORA:Q_LORA + KV_LORA]
    kr = proj[:, Q_LORA + KV_LORA:Q_LORA + KV_LORA + LANE]
    o = Q_LORA + KV_LORA + LANE
    u = proj[:, o:o + CONV_WIDTH]
    c_gate = proj[:, o + CONV_WIDTH:o + 2 * CONV_WIDTH]
    b_gate = proj[:, o + 2 * CONV_WIDTH:o + 3 * CONV_WIDTH]
    cu_ref[rows, :] = c_gate * u
    bg_ref[rows, :] = b_gate

    kg = kg_ref[...]
    tm = x.shape[0]
    lane = lax.broadcasted_iota(jnp.int32, (tm, LANE), 1)

    q = jnp.dot(_rms(c_q, gq_ref[...]).astype(BF16), wuq_ref[...], preferred_element_type=F32)
    kv = jnp.dot(_rms(c_kv, gkv_ref[...]).astype(BF16), wukv_ref[...], preferred_element_type=F32)

    cos_c = cost_ref[:, rows]
    sin_c = sint_ref[:, rows]

    def rope_t(t):
        t1 = t[QK_NOPE:QK_NOPE + HALF_ROPE]
        t2 = t[QK_NOPE + HALF_ROPE:QK_HEAD]
        return jnp.concatenate([t[0:QK_NOPE], t1 * cos_c - t2 * sin_c, t1 * sin_c + t2 * cos_c, t[QK_HEAD:]], axis=0)

    ss_r = jnp.sum(kr * kr, axis=-1, keepdims=True)
    kr_roped = rope_t(kr.T * kgt_ref[...]).T
    qgt = qgt_ref[...]
    qoff = qoff_ref[...]
    koff = koff_ref[...]
    ones = jnp.ones((V_ROWS - V_HEAD, tm), BF16)
    for hd in range(MLA_HEADS):
        qht = q[:, hd * HEAD_SLOT:(hd + 1) * HEAD_SLOT].T
        r = lax.rsqrt(jnp.sum(qht * qht, axis=0, keepdims=True) * (1.0 / QK_HEAD) + EPS)
        qt_ref[0, hd, :, rows] = (rope_t(qht * r * qgt) + qoff).astype(BF16)

        kvh = kv[:, hd * HEAD_SLOT:(hd + 1) * HEAD_SLOT]
        knope = jnp.where(lane < QK_NOPE, kvh, 0.0)
        rk = lax.rsqrt((jnp.sum(knope * knope, axis=-1, keepdims=True) + ss_r) * (1.0 / QK_HEAD) + EPS)
        k_ref[0, hd, rows, :] = ((knope * kg + kr_roped) * rk + koff).astype(BF16)
        kvt = kvh.T
        vt_ref[0, hd, 0:V_HEAD, rows] = kvt[QK_NOPE:QK_NOPE + V_HEAD].astype(BF16)
        vt_ref[0, hd, V_HEAD:V_ROWS, rows] = ones


def _in_projection(x2, g1, w_a, gq, gkv, wuq, wukv, qgt, kg, kgt, qoff, koff, cos_c, sin_c, batch, seq):
    n, d = x2.shape
    tm = TM_PROJ
    spb = seq // tm
    full = lambda shp: pl.BlockSpec(shp, lambda i: (0,) * len(shp))
    return pl.pallas_call(
        _inproj_kernel,
        grid=(n // tm,),
        in_specs=[
            pl.BlockSpec((tm, d), lambda i: (i, 0)),
            full(g1.shape), full(w_a.shape), full(gq.shape), full(gkv.shape), full(wuq.shape), full(wukv.shape),
            full(qgt.shape), full(kg.shape), full(kgt.shape), full(qoff.shape), full(koff.shape),
            pl.BlockSpec((HALF_ROPE, tm), lambda i: (0, i)),
            pl.BlockSpec((HALF_ROPE, tm), lambda i: (0, i)),
        ],
        out_specs=[
            pl.BlockSpec((1, MLA_HEADS, HEAD_SLOT, tm), lambda i: (i // spb, 0, 0, i % spb)),
            pl.BlockSpec((1, MLA_HEADS, tm, HEAD_SLOT), lambda i: (i // spb, 0, i % spb, 0)),
            pl.BlockSpec((1, MLA_HEADS, V_ROWS, tm), lambda i: (i // spb, 0, 0, i % spb)),
            pl.BlockSpec((tm, CONV_WIDTH), lambda i: (i, 0)),
            pl.BlockSpec((tm, CONV_WIDTH), lambda i: (i, 0)),
        ],
        out_shape=[
            jax.ShapeDtypeStruct((batch, MLA_HEADS, HEAD_SLOT, seq), BF16),
            jax.ShapeDtypeStruct((batch, MLA_HEADS, seq, HEAD_SLOT), BF16),
            jax.ShapeDtypeStruct((batch, MLA_HEADS, V_ROWS, seq), BF16),
            jax.ShapeDtypeStruct((n, CONV_WIDTH), F32),
            jax.ShapeDtypeStruct((n, CONV_WIDTH), F32),
        ],
        compiler_params=pltpu.CompilerParams(dimension_semantics=("parallel",), vmem_limit_bytes=VMEM_LIMIT),
        name="in_projection",
    )(x2, g1, w_a, gq, gkv, wuq, wukv, qgt, kg, kgt, qoff, koff, cos_c, sin_c)


def _attn_kernel(qt_ref, k_ref, vt_ref, o_ref, sa_ref, sb_ref, m_ref, acc_ref, *, tq, tk):
    seq = k_ref.shape[2]
    nk = seq // tk
    bufs = (sa_ref, sb_ref)

    def query_tile(qi, carry):
        q0 = pl.multiple_of(qi * tq, tq)
        qt = qt_ref[0, 0, :, pl.ds(q0, tq)]

        def scores(c, s_ref):
            k0 = pl.multiple_of(c * tk, tk)
            s_ref[...] = jnp.dot(k_ref[0, 0, pl.ds(k0, tk), :], qt, preferred_element_type=F32)

        def accumulate(c, s_ref):
            k0 = pl.multiple_of(c * tk, tk)
            s = s_ref[...]
            m = m_ref[...]
            m_new = jnp.maximum(m, jnp.max(s, axis=0, keepdims=True))
            m_ref[...] = m_new
            p = jnp.exp2(s - m_new).astype(BF16)
            vs = vt_ref[0, 0, :, pl.ds(k0, tk)]
            acc_ref[...] = jnp.exp2(m - m_new) * acc_ref[...] + jnp.dot(vs, p, preferred_element_type=F32)

        m_ref[...] = jnp.full(m_ref.shape, NEG_BIG, F32)
        acc_ref[...] = jnp.zeros(acc_ref.shape, F32)
        scores(0, sa_ref)

        def group(j, c):
            base = ATTN_UNROLL * j
            for u in range(ATTN_UNROLL):
                scores(base + u + 1, bufs[(u + 1) % 2])
                accumulate(base + u, bufs[u % 2])
            return c

        lax.fori_loop(0, nk // ATTN_UNROLL - 1, group, 0)
        base = nk - ATTN_UNROLL
        for u in range(ATTN_UNROLL):
            if u + 1 < ATTN_UNROLL:
                scores(base + u + 1, bufs[(u + 1) % 2])
            accumulate(base + u, bufs[u % 2])
        acc = acc_ref[...]
        o_ref[0, :, pl.ds(q0, tq)] = (acc[0:V_HEAD] * (1.0 / acc[V_HEAD:V_HEAD + 1])).astype(BF16)
        return carry

    lax.fori_loop(0, seq // tq, query_tile, 0)


def _attn_bounded_kernel(qt_ref, k_ref, vt_ref, o_ref, sa_ref, sb_ref, acc_ref, *, tq, tk):
    seq = k_ref.shape[2]
    nk = seq // tk
    n_tiles = seq // tq
    n_groups = nk // BOUNDED_UNROLL
    bufs = (sa_ref, sb_ref)

    def load_qt(qi):
        return qt_ref[0, 0, :, pl.ds(pl.multiple_of(qi * tq, tq), tq)]

    def scores(c, qt, s_ref):
        k0 = pl.multiple_of(c * tk, tk)
        s_ref[...] = jnp.dot(k_ref[0, 0, pl.ds(k0, tk), :], qt, preferred_element_type=F32)

    def weighted_values(c, s_ref):
        k0 = pl.multiple_of(c * tk, tk)
        p = jnp.exp2(s_ref[...]).astype(BF16)
        return jnp.dot(vt_ref[0, 0, :, pl.ds(k0, tk)], p, preferred_element_type=F32)

    scores(0, load_qt(0), sa_ref)

    def query_tile(qi, carry):
        qt = load_qt(qi)
        qt_next = load_qt(jnp.minimum(qi + 1, n_tiles - 1))
        acc_ref[...] = jnp.zeros(acc_ref.shape, F32)

        def group(j, c):
            base = BOUNDED_UNROLL * j
            total = None
            for u in range(BOUNDED_UNROLL):
                if u + 1 < BOUNDED_UNROLL:
                    scores(base + u + 1, qt, bufs[(u + 1) % 2])
                else:
                    wraps = j == n_groups - 1
                    scores(jnp.where(wraps, 0, base + u + 1), jnp.where(wraps, qt_next, qt), bufs[(u + 1) % 2])
                part = weighted_values(base + u, bufs[u % 2])
                total = part if total is None else total + part
            acc_ref[...] += total
            return c

        lax.fori_loop(0, n_groups, group, 0)
        acc = acc_ref[...]
        o_ref[0, :, pl.ds(pl.multiple_of(qi * tq, tq), tq)] = (
            acc[0:V_HEAD] * (1.0 / acc[V_HEAD:V_HEAD + 1])).astype(BF16)
        return carry

    lax.fori_loop(0, n_tiles, query_tile, 0)


def _attention(qt, k, vt, bounded):
    batch, heads, _, seq = qt.shape
    assert ATTN_UNROLL % 2 == 0 and (seq // TKV) % ATTN_UNROLL == 0 and seq % TQ == 0
    in_specs = [
        pl.BlockSpec((1, 1, HEAD_SLOT, seq), lambda b, h, flag: (b, h, 0, 0)),
        pl.BlockSpec((1, 1, seq, HEAD_SLOT), lambda b, h, flag: (b, h, 0, 0)),
        pl.BlockSpec((1, 1, V_ROWS, seq), lambda b, h, flag: (b, h, 0, 0)),
    ]
    out_spec = pl.BlockSpec((1, V_HEAD, seq), lambda b, h, flag: (b, h, 0))
    out_shape = jax.ShapeDtypeStruct((batch, heads * V_HEAD, seq), BF16)
    grid_spec = pltpu.PrefetchScalarGridSpec(
        num_scalar_prefetch=1, grid=(batch, heads), in_specs=in_specs, out_specs=out_spec,
        scratch_shapes=[pltpu.VMEM((max(TKV, TKV_BOUNDED), TQ), F32), pltpu.VMEM((max(TKV, TKV_BOUNDED), TQ), F32),
                        pltpu.VMEM((1, TQ), F32), pltpu.VMEM((V_ROWS, TQ), F32)])
    return pl.pallas_call(
        _attn_select_kernel, grid_spec=grid_spec, out_shape=out_shape,
        compiler_params=pltpu.CompilerParams(dimension_semantics=("parallel", "parallel"),
                                             vmem_limit_bytes=VMEM_LIMIT),
        name="attention",
    )(bounded.astype(jnp.int32).reshape(1), qt, k, vt)


def _attn_select_kernel(bounded_ref, qt_ref, k_ref, vt_ref, o_ref, sa_ref, sb_ref, m_ref, acc_ref):
    @pl.when(bounded_ref[0] != 0)
    def _():
        _attn_bounded_kernel(qt_ref, k_ref, vt_ref, o_ref, sa_ref.at[0:TKV_BOUNDED], sb_ref.at[0:TKV_BOUNDED],
                             acc_ref, tq=TQ, tk=TKV_BOUNDED)

    @pl.when(bounded_ref[0] == 0)
    def _():
        _attn_kernel(qt_ref, k_ref, vt_ref, o_ref, sa_ref.at[0:TKV], sb_ref.at[0:TKV], m_ref, acc_ref,
                     tq=TQ, tk=TKV)


def _mix_kernel(x_ref, g1_ref, wg_ref, ot_ref, cu_ref, cup_ref, cun_ref, bg_ref, cw_ref,
                woa_ref, wob_ref, wo_ref, g2_ref, rw_ref, rb_ref,
                x1_ref, h2_ref, mi_ref, mw_ref, cnt_ref, tri_ref, carry_ref, *, steps_per_seq):
    i = pl.program_id(0)
    tm = x_ref.shape[0]
    ts = tm // MIX_SUBTILES

    @pl.when(i == 0)
    def _():
        r = lax.broadcasted_iota(jnp.int32, (ts, ts), 0)
        c = lax.broadcasted_iota(jnp.int32, (ts, ts), 1)
        tri_ref[...] = jnp.where(c < r, 1.0, 0.0).astype(BF16)
        carry_ref[...] = jnp.zeros_like(carry_ref)

    cu = cu_ref[...]
    row = lax.broadcasted_iota(jnp.int32, cu.shape, 0)
    s_in_seq = i % steps_per_seq
    prev_row = jnp.where(s_in_seq == 0, 0.0, cup_ref[7:8, :])
    next_row = jnp.where(s_in_seq == steps_per_seq - 1, 0.0, cun_ref[0:1, :])
    below = jnp.where(row == 0, prev_row, pltpu.roll(cu, 1, 0))
    above = jnp.where(row == tm - 1, next_row, pltpu.roll(cu, tm - 1, 0))
    cw = cw_ref[...]
    gated_conv = (bg_ref[...] * (cw[0:1, :] * below + cw[1:2, :] * cu + cw[2:3, :] * above)).astype(BF16)

    carry = carry_ref[...]
    for part in range(MIX_SUBTILES):
        carry = _mix_rows(part * ts, ts, gated_conv[part * ts:(part + 1) * ts], carry, x_ref, g1_ref, wg_ref,
                          ot_ref, woa_ref, wob_ref, wo_ref, g2_ref, rw_ref, rb_ref, tri_ref,
                          x1_ref, h2_ref, mi_ref, mw_ref)
    carry_ref[...] = carry
    cnt_ref[...] = jnp.broadcast_to(carry, cnt_ref.shape)


def _mix_rows(r0, ts, gated_conv, carry, x_ref, g1_ref, wg_ref, ot_ref, woa_ref, wob_ref, wo_ref, g2_ref,
              rw_ref, rb_ref, tri_ref, x1_ref, h2_ref, mi_ref, mw_ref):
    rows = slice(r0, r0 + ts)
    y_a = lax.dot_general(ot_ref[0, :, rows], woa_ref[...], (((0,), (0,)), ((), ())), preferred_element_type=F32)
    y_b = jnp.dot(gated_conv, wob_ref[...], preferred_element_type=F32)

    x = x_ref[rows, :]
    h = _rms(x, g1_ref[...]).astype(BF16)
    gates = jnp.dot(h, wg_ref[...], preferred_element_type=F32)
    d = x.shape[1]
    sig_a = 1.0 / (1.0 + jnp.exp(-gates[:, 0:d]))
    sig_b = 1.0 / (1.0 + jnp.exp(-gates[:, d:2 * d]))

    merged = (sig_a * y_a + sig_b * y_b).astype(BF16)
    x1 = x + jnp.dot(merged, wo_ref[...], preferred_element_type=F32)
    x1_ref[rows, :] = x1
    h2 = _rms(x1, g2_ref[...])
    _store_token_tiles(h2_ref.at[r0 * SUBLANE:(r0 + ts) * SUBLANE], h2)

    h2_hi = h2.astype(BF16)
    h2_lo = (h2 - h2_hi.astype(F32)).astype(BF16)
    rw = rw_ref[...]
    hi_terms = jnp.dot(h2_hi, rw, preferred_element_type=F32)
    logits = (hi_terms[:, 0:LANE] + hi_terms[:, LANE:2 * LANE]
              + jnp.dot(h2_lo, rw[:, 0:LANE], preferred_element_type=F32)
              + rb_ref[...])

    lane = lax.broadcasted_iota(jnp.int32, logits.shape, 1)
    lane_f = lane.astype(F32)
    work = logits
    vals, idxs, hits = [], [], []
    for _ in range(TOP_K):
        mx = jnp.max(work, axis=-1, keepdims=True)
        idx = jnp.min(jnp.where(work == mx, lane_f, float(LANE)), axis=-1, keepdims=True)
        hit = lane_f == idx
        work = jnp.where(hit, -jnp.inf, work)
        vals.append(mx)
        idxs.append(idx)
        hits.append(hit)
    exps = [jnp.exp(v - vals[0]) for v in vals]
    denom = exps[0] + exps[1] + exps[2] + exps[3]
    inv = 1.0 / denom

    sel = jnp.zeros(logits.shape, F32)
    for hit in hits:
        sel = sel + jnp.where(hit, 1.0, 0.0)
    before = jnp.dot(tri_ref[...], sel.astype(BF16), preferred_element_type=F32) + carry

    mi = jnp.zeros(logits.shape, F32)
    mw = jnp.zeros(logits.shape, F32)
    for k in range(TOP_K):
        rank = jnp.sum(jnp.where(hits[k], before, 0.0), axis=-1, keepdims=True)
        mi = jnp.where(lane == k, idxs[k], mi)
        mi = jnp.where(lane == TOP_K + k, rank, mi)
        mw = jnp.where(lane == k, exps[k] * inv, mw)
    mi_ref[:, rows] = mi.T[0:2 * TOP_K].astype(jnp.int32)
    mw_ref[rows, :] = mw
    return carry + jnp.sum(sel, axis=0, keepdims=True)


def _mix(x2, g1, w_g, ot, cu, bg, conv_w, woa, wob, wo, g2, rw_split, rb, batch, seq):
    n, d = x2.shape
    tm = TM_MIX
    spb = seq // tm
    r8 = tm // 8
    nsteps = n // tm
    full = lambda shp: pl.BlockSpec(shp, lambda i: (0,) * len(shp))
    return pl.pallas_call(
        functools.partial(_mix_kernel, steps_per_seq=spb),
        grid=(nsteps,),
        in_specs=[
            pl.BlockSpec((tm, d), lambda i: (i, 0)),
            full(g1.shape), full(w_g.shape),
            pl.BlockSpec((1, MLA_HEADS * V_HEAD, tm), lambda i: (i // spb, 0, i % spb)),
            pl.BlockSpec((tm, CONV_WIDTH), lambda i: (i, 0)),
            pl.BlockSpec((8, CONV_WIDTH), lambda i: (jnp.maximum(i * r8 - 1, 0), 0)),
            pl.BlockSpec((8, CONV_WIDTH), lambda i: (jnp.minimum((i + 1) * r8, nsteps * r8 - 1), 0)),
            pl.BlockSpec((tm, CONV_WIDTH), lambda i: (i, 0)),
            full(conv_w.shape), full(woa.shape), full(wob.shape), full(wo.shape), full(g2.shape),
            full(rw_split.shape), full(rb.shape),
        ],
        out_specs=[
            pl.BlockSpec((tm, d), lambda i: (i, 0)),
            pl.BlockSpec((tm * SUBLANE, LANE), lambda i: (i, 0)),
            pl.BlockSpec((2 * TOP_K, tm), lambda i: (0, i)),
            pl.BlockSpec((tm, LANE), lambda i: (i, 0)),
            pl.BlockSpec((8, LANE), lambda i: (0, 0)),
        ],
        out_shape=[
            jax.ShapeDtypeStruct((n, d), F32),
            jax.ShapeDtypeStruct((n * SUBLANE, LANE), F32),
            jax.ShapeDtypeStruct((2 * TOP_K, n), jnp.int32),
            jax.ShapeDtypeStruct((n, LANE), F32),
            jax.ShapeDtypeStruct((8, LANE), F32),
        ],
        scratch_shapes=[pltpu.VMEM((tm // MIX_SUBTILES, tm // MIX_SUBTILES), BF16), pltpu.VMEM((1, LANE), F32)],
        compiler_params=pltpu.CompilerParams(dimension_semantics=("arbitrary",), vmem_limit_bytes=VMEM_LIMIT),
        name="mix_route",
    )(x2, g1, w_g, ot, cu, cu, cu, bg, conv_w, woa, wob, wo, g2, rw_split, rb)


def _row_copy_wait(src_like, dst_like, sem, times):
    for _ in range(times):
        pltpu.make_async_copy(src_like, dst_like, sem).wait()


def _token(ref, idx):
    return ref.at[pl.ds(pl.multiple_of(idx * SUBLANE, SUBLANE), SUBLANE)]


def _dispatch_kernel(dest_ref, pe_ref, nused_ref, h2_ref, xs_ref, zero_ref, sem, zsem):
    tm = h2_ref.shape[0] // SUBLANE
    block_rows = EXPERT_BLOCK * SUBLANE
    n_blocks = xs_ref.shape[0] // block_rows

    @pl.when(pl.program_id(0) == 0)
    def _():
        zero_ref[...] = jnp.zeros_like(zero_ref)

        def zero_copy(first_token):
            start = pl.multiple_of(first_token * SUBLANE, block_rows)
            return pltpu.make_async_copy(zero_ref, xs_ref.at[pl.ds(start, block_rows)], zsem)

        def has_rows(e):
            return pe_ref[e] > jnp.where(e == 0, 0, pe_ref[jnp.maximum(e - 1, 0)])

        def start_expert(e, c):
            @pl.when(has_rows(e))
            def _():
                zero_copy(pe_ref[e] - EXPERT_BLOCK).start()
            return c

        def wait_expert(e, c):
            @pl.when(has_rows(e))
            def _():
                zero_copy(pe_ref[e] - EXPERT_BLOCK).wait()
            return c

        def start_tail(b, c):
            zero_copy(b * EXPERT_BLOCK).start()
            return c

        def wait_tail(b, c):
            zero_copy(b * EXPERT_BLOCK).wait()
            return c

        lax.fori_loop(0, N_EXPERTS, start_expert, 0)
        lax.fori_loop(nused_ref[0], n_blocks, start_tail, 0)
        lax.fori_loop(0, N_EXPERTS, wait_expert, 0)
        lax.fori_loop(nused_ref[0], n_blocks, wait_tail, 0)

    def issue(g, c):
        for u in range(ISSUE_GROUP):
            t = g * (ISSUE_GROUP // TOP_K) + u // TOP_K
            dest = dest_ref[(u % TOP_K) * tm + t]
            pltpu.make_async_copy(_token(h2_ref, t), _token(xs_ref, dest), sem).start(priority=u % 2)
        return c

    lax.fori_loop(0, tm * TOP_K // ISSUE_GROUP, issue, 0)
    _row_copy_wait(h2_ref, xs_ref.at[pl.ds(0, tm * SUBLANE)], sem, TOP_K)


def _dispatch(dest_flat, pad_end, n_used, h2t, rows):
    tm = TM_ROUTE
    n = h2t.shape[0] // SUBLANE
    return pl.pallas_call(
        _dispatch_kernel,
        grid=(n // tm,),
        in_specs=[
            pl.BlockSpec((tm * TOP_K,), lambda i: (i,), memory_space=pltpu.SMEM),
            pl.BlockSpec(memory_space=pltpu.SMEM),
            pl.BlockSpec(memory_space=pltpu.SMEM),
            pl.BlockSpec((tm * SUBLANE, LANE), lambda i: (i, 0)),
        ],
        out_specs=pl.BlockSpec(memory_space=pl.ANY),
        out_shape=jax.ShapeDtypeStruct((rows * SUBLANE, LANE), F32),
        scratch_shapes=[pltpu.VMEM((EXPERT_BLOCK * SUBLANE, LANE), F32), pltpu.SemaphoreType.DMA(()),
                        pltpu.SemaphoreType.DMA(())],
        compiler_params=pltpu.CompilerParams(dimension_semantics=("arbitrary",), vmem_limit_bytes=VMEM_LIMIT),
        name="dispatch",
    )(dest_flat, pad_end, n_used, h2t)


def _expert_kernel(bexp_ref, nused_ref, nexp_ref, slot_ref, nvalid_ref, xs_ref, w1_hbm, b1_ref, w2_hbm, b2_ref, ys_ref,
                   w1_ref, w2_ref, sem):
    i = pl.program_id(0)
    active = i < nused_ref[0]
    expert = bexp_ref[i]
    slot = slot_ref[i]
    prev = bexp_ref[jnp.maximum(i - 1, 0)]
    fresh = jnp.logical_or(i == 0, expert != prev)

    def weight_copies(e, s):
        return (pltpu.make_async_copy(w1_hbm.at[e], w1_ref.at[s], sem.at[0, s]),
                pltpu.make_async_copy(w2_hbm.at[e], w2_ref.at[s], sem.at[1, s]))

    @pl.when(jnp.logical_and(active, i == 0))
    def _():
        for cp in weight_copies(expert, slot):
            cp.start()

    @pl.when(jnp.logical_and(active, fresh))
    def _():
        for cp in weight_copies(expert, slot):
            cp.wait()

        @pl.when(nexp_ref[i] != expert)
        def _():
            for cp in weight_copies(nexp_ref[i], 1 - slot):
                cp.start()

    def ffn(rows):
        dff = w2_ref.shape[1]
        xb = _load_token_tiles(xs_ref, (), rows).astype(BF16)
        hm = jnp.dot(xb, w1_ref[slot].astype(BF16), preferred_element_type=F32) + b1_ref[0]
        gate = jnp.minimum(hm[:, 0:dff], SWIGLU_LIMIT)
        up = jnp.clip(hm[:, dff:2 * dff], -SWIGLU_LIMIT, SWIGLU_LIMIT)
        glu = gate * (1.0 / (1.0 + jnp.exp(-SWIGLU_ALPHA * gate)))
        act = ((up + 1.0) * glu).astype(BF16)
        _store_token_tiles(ys_ref.at[0:rows * SUBLANE],
                           jnp.dot(act, w2_ref[slot].astype(BF16), preferred_element_type=F32) + b2_ref[0])

    half = EXPERT_BLOCK // 2
    half_full = nvalid_ref[i] <= half

    @pl.when(jnp.logical_and(active, jnp.logical_not(half_full)))
    def _():
        ffn(EXPERT_BLOCK)

    @pl.when(jnp.logical_and(active, half_full))
    def _():
        ffn(half)
        ys_ref[half * SUBLANE:, :] = jnp.zeros((half * SUBLANE, LANE), F32)

    @pl.when(jnp.logical_not(active))
    def _():
        ys_ref[...] = jnp.zeros_like(ys_ref)


def _experts(block_exp, n_used, next_exp, weight_slot, block_valid, xs, w1, b1, w2, b2):
    d = w1.shape[1]
    assert d == SUBLANE * LANE
    block_rows = EXPERT_BLOCK * SUBLANE
    n_blocks = xs.shape[0] // block_rows
    dff2 = w1.shape[2]
    dff = w2.shape[1]
    grid_spec = pltpu.PrefetchScalarGridSpec(
        num_scalar_prefetch=5,
        grid=(n_blocks,),
        in_specs=[
            pl.BlockSpec((block_rows, LANE), lambda i, be, nu, ne, ws, nv: (jnp.minimum(i, nu[0] - 1), 0)),
            pl.BlockSpec(memory_space=pl.ANY),
            pl.BlockSpec((1, 1, dff2), lambda i, be, nu, ne, ws, nv: (be[i], 0, 0)),
            pl.BlockSpec(memory_space=pl.ANY),
            pl.BlockSpec((1, 1, d), lambda i, be, nu, ne, ws, nv: (be[i], 0, 0)),
        ],
        out_specs=pl.BlockSpec((block_rows, LANE), lambda i, be, nu, ne, ws, nv: (i, 0)),
        scratch_shapes=[pltpu.VMEM((2, d, dff2), F32), pltpu.VMEM((2, dff, d), F32),
                        pltpu.SemaphoreType.DMA((2, 2))],
    )
    return pl.pallas_call(
        _expert_kernel,
        grid_spec=grid_spec,
        out_shape=jax.ShapeDtypeStruct(xs.shape, F32),
        compiler_params=pltpu.CompilerParams(dimension_semantics=("arbitrary",), vmem_limit_bytes=VMEM_LIMIT),
        name="experts",
    )(block_exp, n_used, next_exp, weight_slot, block_valid, xs, w1, b1, w2, b2)


def _combine_kernel(dest_ref, dest_next_ref, x1_ref, mw_ref, ys_ref, out_ref, buf_ref, sem):
    i = pl.program_id(0)
    tm = x1_ref.shape[0]
    slot = lax.rem(i, 2)

    def gather(idx_ref, s):
        def issue(g, c):
            for u in range(ISSUE_GROUP):
                t = g * (ISSUE_GROUP // TOP_K) + u // TOP_K
                dest = idx_ref[(u % TOP_K) * tm + t]
                pltpu.make_async_copy(_token(ys_ref, dest), _token(buf_ref.at[s, u % TOP_K], t),
                                      sem.at[s]).start(priority=u % 2)
            return c

        lax.fori_loop(0, tm * TOP_K // ISSUE_GROUP, issue, 0)

    @pl.when(i == 0)
    def _():
        gather(dest_ref, 0)

    @pl.when(i + 1 < pl.num_programs(0))
    def _():
        gather(dest_next_ref, 1 - slot)

    _row_copy_wait(ys_ref.at[pl.ds(0, tm * SUBLANE)], buf_ref.at[slot, 0], sem.at[slot], TOP_K)
    acc = x1_ref[...]
    mw = mw_ref[...]
    for k in range(TOP_K):
        acc = acc + mw[:, k:k + 1] * _load_token_tiles(buf_ref, (slot, k), tm)
    out_ref[...] = acc


def _combine(dest_flat, x1, mw, ys):
    n, d = x1.shape
    tm = TM_ROUTE
    nsteps = n // tm
    return pl.pallas_call(
        _combine_kernel,
        grid=(nsteps,),
        in_specs=[
            pl.BlockSpec((tm * TOP_K,), lambda i: (i,), memory_space=pltpu.SMEM),
            pl.BlockSpec((tm * TOP_K,), lambda i: (jnp.minimum(i + 1, nsteps - 1),), memory_space=pltpu.SMEM),
            pl.BlockSpec((tm, d), lambda i: (i, 0)),
            pl.BlockSpec((tm, LANE), lambda i: (i, 0)),
            pl.BlockSpec(memory_space=pl.ANY),
        ],
        out_specs=pl.BlockSpec((tm, d), lambda i: (i, 0)),
        out_shape=jax.ShapeDtypeStruct((n, d), F32),
        scratch_shapes=[pltpu.VMEM((2, TOP_K, tm * SUBLANE, LANE), F32), pltpu.SemaphoreType.DMA((2,))],
        compiler_params=pltpu.CompilerParams(dimension_semantics=("arbitrary",), vmem_limit_bytes=VMEM_LIMIT),
        name="combine",
    )(dest_flat, dest_flat, x1, mw, ys)


def _pad_cols(w, width):
    return jnp.pad(w, ((0, 0), (0, width - w.shape[1])))


def _head_slots(w, per_head):
    rows = w.shape[0]
    w3 = w.reshape(rows, MLA_HEADS, per_head)
    return jnp.pad(w3, ((0, 0), (0, 0), (0, HEAD_SLOT - per_head))).reshape(rows, MLA_HEADS * HEAD_SLOT)


def _rope_tables(positions):
    inv_freq = ROPE_THETA ** (-jnp.arange(0, QK_ROPE, 2, dtype=F32) / QK_ROPE)
    ang = positions.astype(F32).reshape(-1, 1) * inv_freq
    return jnp.cos(ang).T, jnp.sin(ang).T


def _layer(x2, positions, norm1_g, w_in, q_a_norm_g, kv_a_norm_g, w_uq, w_ukv, q_norm_g, k_norm_g,
           conv_w, w_o_mla, w_o_conv, w_o, norm2_g, router_w, router_b,
           expert_w1, expert_b1, expert_w2, expert_b2, batch, seq):
    n, d = x2.shape
    o_kr = Q_LORA + KV_LORA
    o_u = o_kr + QK_ROPE
    o_g = o_u + 3 * CONV_WIDTH
    kr_cols = jnp.pad(w_in[:, o_kr:o_u], ((0, 0), (QK_NOPE, LANE - QK_HEAD)))
    w_a = jnp.concatenate([w_in[:, :o_kr], kr_cols, w_in[:, o_u:o_g]], axis=1).astype(BF16)
    w_g = w_in[:, o_g:].astype(BF16)
    row = lambda v: v.reshape(1, -1)
    cos_c, sin_c = _rope_tables(positions)
    q_scale = (QK_HEAD ** -0.5) * math.log2(math.e)
    gain_t = lambda g: jnp.broadcast_to(g.reshape(LANE, 1), (LANE, TM_PROJ // PROJ_SUBTILES))
    kg = _pad_cols(row(k_norm_g), LANE)
    score_bound = 1.02 * q_scale * QK_HEAD * jnp.max(jnp.abs(q_norm_g)) * jnp.max(jnp.abs(k_norm_g))
    bounded = 2.0 * score_bound <= SAFE_SCORE_RANGE
    offset = jnp.where(bounded, score_bound, 0.0)
    feature = jnp.arange(LANE) == OFFSET_FEATURE
    qoff = gain_t(jnp.where(feature, -offset, 0.0).astype(F32))
    koff = jnp.where(feature, 1.0, 0.0).astype(F32).reshape(1, LANE)

    qt, k, vt, cu, bg = _in_projection(
        x2, row(norm1_g), w_a, row(q_a_norm_g), row(kv_a_norm_g),
        _head_slots(w_uq, QK_HEAD).astype(BF16), w_ukv.astype(BF16),
        gain_t(_pad_cols(row(q_norm_g) * q_scale, LANE)), kg, gain_t(kg), qoff, koff,
        cos_c, sin_c, batch, seq)
    ot = _attention(qt, k, vt, bounded)

    rw = _pad_cols(router_w, LANE)
    rw_hi = rw.astype(BF16)
    rw_lo = (rw - rw_hi.astype(F32)).astype(BF16)
    rb = jnp.concatenate([row(router_b), jnp.full((1, LANE - N_EXPERTS), NEG_BIG, F32)], axis=1)
    x1, h2, mi, mw, cnt = _mix(
        x2, row(norm1_g), w_g, ot, cu, bg, conv_w, w_o_mla.astype(BF16), w_o_conv.astype(BF16),
        w_o.astype(BF16), row(norm2_g), jnp.concatenate([rw_hi, rw_lo], axis=1), rb, batch, seq)

    counts = cnt[0, :N_EXPERTS].astype(jnp.int32)
    padded = (counts + EXPERT_BLOCK - 1) // EXPERT_BLOCK * EXPERT_BLOCK
    experts = jnp.arange(N_EXPERTS, dtype=jnp.int32)
    pad_end = jnp.sum(jnp.where(experts[None, :] <= experts[:, None], padded[None, :], 0), axis=1)
    pad_start = (pad_end - padded).astype(jnp.int32)
    nk = n * TOP_K
    n_blocks = (nk + N_EXPERTS * (EXPERT_BLOCK - 1) + EXPERT_BLOCK - 1) // EXPERT_BLOCK
    rows = n_blocks * EXPERT_BLOCK
    block_first_row = jnp.arange(n_blocks, dtype=jnp.int32) * EXPERT_BLOCK
    block_exp = jnp.minimum(jnp.sum(pad_end[None, :] <= block_first_row[:, None], axis=1),
                            N_EXPERTS - 1).astype(jnp.int32)
    n_used = (pad_end[-1:] // EXPERT_BLOCK).astype(jnp.int32)
    group_end = jnp.sum(jnp.where(experts[None, :] == block_exp[:, None], pad_end[None, :], 0), axis=1)
    following = jnp.minimum(jnp.sum(pad_end[None, :] <= group_end[:, None], axis=1), N_EXPERTS - 1)
    next_exp = jnp.where(group_end < pad_end[-1], following, block_exp).astype(jnp.int32)
    ordinal = jnp.sum(jnp.where(experts[None, :] < experts[:, None], (padded > 0)[None, :], False), axis=1)
    weight_slot = jnp.sum(jnp.where(experts[None, :] == block_exp[:, None], (ordinal % 2)[None, :], 0),
                          axis=1).astype(jnp.int32)
    e_sel = mi[None, 0:TOP_K] == jnp.arange(N_EXPERTS, dtype=jnp.int32)[:, None, None]
    dest = jnp.sum(jnp.where(e_sel, pad_start[:, None, None], 0), axis=0) + mi[TOP_K:2 * TOP_K]
    dest_flat = dest.reshape(TOP_K, n // TM_ROUTE, TM_ROUTE).transpose(1, 0, 2).reshape(nk)

    xs = _dispatch(dest_flat, pad_end.astype(jnp.int32), n_used, h2, rows)
    rows_end = jnp.sum(jnp.where(experts[None, :] == block_exp[:, None], (pad_start + counts)[None, :], 0), axis=1)
    block_valid = jnp.clip(rows_end - block_first_row, 0, EXPERT_BLOCK).astype(jnp.int32)
    ys = _experts(block_exp, n_used, next_exp, weight_slot, block_valid, xs, expert_w1, expert_b1.reshape(N_EXPERTS, 1, -1),
                  expert_w2, expert_b2.reshape(N_EXPERTS, 1, -1))
    return _combine(dest_flat, x1, mw, ys)


def kernel(x, positions, norm1_g, w_in, q_a_norm_g, kv_a_norm_g, w_uq, w_ukv, q_norm_g, k_norm_g, conv_w,
           w_o_mla, w_o_conv, w_o, norm2_g, router_w, router_b, expert_w1, expert_b1, expert_w2, expert_b2):
    batch, seq, d = x.shape
    depth = norm1_g.shape[0]
    x2 = x.reshape(batch * seq, d)
    for l in range(depth):
        x2 = _layer(x2, positions, norm1_g[l], w_in[l], q_a_norm_g[l], kv_a_norm_g[l], w_uq[l], w_ukv[l],
                    q_norm_g[l], k_norm_g[l], conv_w[l], w_o_mla[l], w_o_conv[l], w_o[l], norm2_g[l],
                    router_w[l], router_b[l], expert_w1[l], expert_b1[l], expert_w2[l], expert_b2[l], batch, seq)
    return x2.reshape(batch, seq, d)
```

```python
import functools
import math

import jax
import jax.numpy as jnp
from jax import lax
from jax.experimental import pallas as pl
from jax.experimental.pallas import tpu as pltpu

F32 = jnp.float32
BF16 = jnp.bfloat16

MLA_HEADS = 8
QK_NOPE = 64
QK_ROPE = 32
QK_HEAD = QK_NOPE + QK_ROPE
V_HEAD = 64
Q_LORA = 256
KV_LORA = 128
ROPE_THETA = 10000.0
CONV_WIDTH = 512
N_EXPERTS = 32
TOP_K = 4
SWIGLU_LIMIT = 7.0
SWIGLU_ALPHA = 1.702
EPS = 1e-6

LANE = 128
SUBLANE = 8
HEAD_SLOT = LANE
HALF_ROPE = QK_ROPE // 2
V_ROWS = V_HEAD + 16
OFFSET_FEATURE = QK_HEAD
SAFE_SCORE_RANGE = 100.0
VMEM_LIMIT = 56 * 1024 * 1024

TM_PROJ = 512
PROJ_SUBTILES = 2
TQ = 512
TKV = 512
ATTN_UNROLL = 4
TKV_BOUNDED = 256
TM_MIX = 512
MIX_SUBTILES = 1
TM_ROUTE = 512
EXPERT_BLOCK = 512
EXPERT_ROW_GROUP = 128
ISSUE_GROUP = 16
NEG_BIG = -1e30


def _load_token_tiles(ref, lead, rows):
    return jnp.concatenate([ref[lead + (pl.ds(c, rows, stride=SUBLANE), slice(None))] for c in range(SUBLANE)],
                           axis=1)


def _store_token_tiles(ref, value):
    rows = value.shape[0]
    for c in range(SUBLANE):
        ref[pl.ds(c, rows, stride=SUBLANE), :] = value[:, c * LANE:(c + 1) * LANE]


def _rms(x, g):
    return x * lax.rsqrt(jnp.mean(x * x, axis=-1, keepdims=True) + EPS) * g


def _inproj_kernel(x_ref, g1_ref, w_ref, gq_ref, gkv_ref, wuq_ref, wukv_ref, qgt_ref, kg_ref, kgt_ref,
                   qoff_ref, koff_ref, cost_ref, sint_ref, qt_ref, k_ref, vt_ref, cu_ref, bg_ref):
    tm = x_ref.shape[0] // PROJ_SUBTILES
    for part in range(PROJ_SUBTILES):
        _inproj_rows(slice(part * tm, (part + 1) * tm), x_ref, g1_ref, w_ref, gq_ref, gkv_ref, wuq_ref, wukv_ref,
                     qgt_ref, kg_ref, kgt_ref, qoff_ref, koff_ref, cost_ref, sint_ref,
                     qt_ref, k_ref, vt_ref, cu_ref, bg_ref)


def _inproj_rows(rows, x_ref, g1_ref, w_ref, gq_ref, gkv_ref, wuq_ref, wukv_ref, qgt_ref, kg_ref, kgt_ref,
                 qoff_ref, koff_ref, cost_ref, sint_ref, qt_ref, k_ref, vt_ref, cu_ref, bg_ref):
    x = x_ref[rows, :]
    h = _rms(x, g1_ref[...]).astype(BF16)
    proj = jnp.dot(h, w_ref[...], preferred_element_type=F32)
    c_q = proj[:, 0:Q_LORA]
    c_kv = proj[:, Q_LORA:Q_LORA + KV_LORA]
    kr = proj[:, Q_LORA + KV_LORA:Q_LORA + KV_LORA + LANE]
    o = Q_LORA + KV_LORA + LANE
    u = proj[:, o:o + CONV_WIDTH]
    c_gate = proj[:, o + CONV_WIDTH:o + 2 * CONV_WIDTH]
    b_gate = proj[:, o + 2 * CONV_WIDTH:o + 3 * CONV_WIDTH]
    cu_ref[rows, :] = c_gate * u
    bg_ref[rows, :] = b_gate

    kg = kg_ref[...]
    tm = x.shape[0]
    lane = lax.broadcasted_iota(jnp.int32, (tm, LANE), 1)

    q = jnp.dot(_rms(c_q, gq_ref[...]).astype(BF16), wuq_ref[...], preferred_element_type=F32)
    kv = jnp.dot(_rms(c_kv, gkv_ref[...]).astype(BF16), wukv_ref[...], preferred_element_type=F32)

    cos_c = cost_ref[:, rows]
    sin_c = sint_ref[:, rows]

    def rope_t(t):
        t1 = t[QK_NOPE:QK_NOPE + HALF_ROPE]
        t2 = t[QK_NOPE + HALF_ROPE:QK_HEAD]
        return jnp.concatenate([t[0:QK_NOPE], t1 * cos_c - t2 * sin_c, t1 * sin_c + t2 * cos_c, t[QK_HEAD:]], axis=0)

    ss_r = jnp.sum(kr * kr, axis=-1, keepdims=True)
    kr_roped = rope_t(kr.T * kgt_ref[...]).T
    qgt = qgt_ref[...]
    qoff = qoff_ref[...]
    koff = koff_ref[...]
    ones = jnp.ones((V_ROWS - V_HEAD, tm), BF16)
    for hd in range(MLA_HEADS):
        qht = q[:, hd * HEAD_SLOT:(hd + 1) * HEAD_SLOT].T
        r = lax.rsqrt(jnp.sum(qht * qht, axis=0, keepdims=True) * (1.0 / QK_HEAD) + EPS)
        qt_ref[0, hd, :, rows] = (rope_t(qht * r * qgt) + qoff).astype(BF16)

        kvh = kv[:, hd * HEAD_SLOT:(hd + 1) * HEAD_SLOT]
        knope = jnp.where(lane < QK_NOPE, kvh, 0.0)
        rk = lax.rsqrt((jnp.sum(knope * knope, axis=-1, keepdims=True) + ss_r) * (1.0 / QK_HEAD) + EPS)
        k_ref[0, hd, rows, :] = ((knope * kg + kr_roped) * rk + koff).astype(BF16)
        kvt = kvh.T
        vt_ref[0, hd, 0:V_HEAD, rows] = kvt[QK_NOPE:QK_NOPE + V_HEAD].astype(BF16)
        vt_ref[0, hd, V_HEAD:V_ROWS, rows] = ones


def _in_projection(x2, g1, w_a, gq, gkv, wuq, wukv, qgt, kg, kgt, qoff, koff, cos_c, sin_c, batch, seq):
    n, d = x2.shape
    tm = TM_PROJ
    spb = seq // tm
    full = lambda shp: pl.BlockSpec(shp, lambda i: (0,) * len(shp))
    return pl.pallas_call(
        _inproj_kernel,
        grid=(n // tm,),
        in_specs=[
            pl.BlockSpec((tm, d), lambda i: (i, 0)),
            full(g1.shape), full(w_a.shape), full(gq.shape), full(gkv.shape), full(wuq.shape), full(wukv.shape),
            full(qgt.shape), full(kg.shape), full(kgt.shape), full(qoff.shape), full(koff.shape),
            pl.BlockSpec((HALF_ROPE, tm), lambda i: (0, i)),
            pl.BlockSpec((HALF_ROPE, tm), lambda i: (0, i)),
        ],
        out_specs=[
            pl.BlockSpec((1, MLA_HEADS, HEAD_SLOT, tm), lambda i: (i // spb, 0, 0, i % spb)),
            pl.BlockSpec((1, MLA_HEADS, tm, HEAD_SLOT), lambda i: (i // spb, 0, i % spb, 0)),
            pl.BlockSpec((1, MLA_HEADS, V_ROWS, tm), lambda i: (i // spb, 0, 0, i % spb)),
            pl.BlockSpec((tm, CONV_WIDTH), lambda i: (i, 0)),
            pl.BlockSpec((tm, CONV_WIDTH), lambda i: (i, 0)),
        ],
        out_shape=[
            jax.ShapeDtypeStruct((batch, MLA_HEADS, HEAD_SLOT, seq), BF16),
            jax.ShapeDtypeStruct((batch, MLA_HEADS, seq, HEAD_SLOT), BF16),
            jax.ShapeDtypeStruct((batch, MLA_HEADS, V_ROWS, seq), BF16),
            jax.ShapeDtypeStruct((n, CONV_WIDTH), F32),
            jax.ShapeDtypeStruct((n, CONV_WIDTH), F32),
        ],
        compiler_params=pltpu.CompilerParams(dimension_semantics=("parallel",), vmem_limit_bytes=VMEM_LIMIT),
        name="in_projection",
    )(x2, g1, w_a, gq, gkv, wuq, wukv, qgt, kg, kgt, qoff, koff, cos_c, sin_c)


def _attn_kernel(qt_ref, k_ref, vt_ref, o_ref, sa_ref, sb_ref, m_ref, acc_ref, *, tq, tk):
    seq = k_ref.shape[2]
    nk = seq // tk
    bufs = (sa_ref, sb_ref)

    def query_tile(qi, carry):
        q0 = pl.multiple_of(qi * tq, tq)
        qt = qt_ref[0, 0, :, pl.ds(q0, tq)]

        def scores(c, s_ref):
            k0 = pl.multiple_of(c * tk, tk)
            s_ref[...] = jnp.dot(k_ref[0, 0, pl.ds(k0, tk), :], qt, preferred_element_type=F32)

        def accumulate(c, s_ref):
            k0 = pl.multiple_of(c * tk, tk)
            s = s_ref[...]
            m = m_ref[...]
            m_new = jnp.maximum(m, jnp.max(s, axis=0, keepdims=True))
            m_ref[...] = m_new
            p = jnp.exp2(s - m_new).astype(BF16)
            vs = vt_ref[0, 0, :, pl.ds(k0, tk)]
            acc_ref[...] = jnp.exp2(m - m_new) * acc_ref[...] + jnp.dot(vs, p, preferred_element_type=F32)

        m_ref[...] = jnp.full(m_ref.shape, NEG_BIG, F32)
        acc_ref[...] = jnp.zeros(acc_ref.shape, F32)
        scores(0, sa_ref)

        def group(j, c):
            base = ATTN_UNROLL * j
            for u in range(ATTN_UNROLL):
                scores(base + u + 1, bufs[(u + 1) % 2])
                accumulate(base + u, bufs[u % 2])
            return c

        lax.fori_loop(0, nk // ATTN_UNROLL - 1, group, 0)
        base = nk - ATTN_UNROLL
        for u in range(ATTN_UNROLL):
            if u + 1 < ATTN_UNROLL:
                scores(base + u + 1, bufs[(u + 1) % 2])
            accumulate(base + u, bufs[u % 2])
        acc = acc_ref[...]
        o_ref[0, :, pl.ds(q0, tq)] = (acc[0:V_HEAD] * (1.0 / acc[V_HEAD:V_HEAD + 1])).astype(BF16)
        return carry

    lax.fori_loop(0, seq // tq, query_tile, 0)


def _attn_bounded_kernel(qt_ref, k_ref, vt_ref, o_ref, sa_ref, sb_ref, *, tq, tk):
    seq = k_ref.shape[2]
    nk = seq // tk
    n_tiles = seq // tq
    ahead_refs = (sa_ref, sb_ref)
    ahead = len(ahead_refs)

    def load_qt(qi):
        return qt_ref[0, 0, :, pl.ds(pl.multiple_of(qi * tq, tq), tq)]

    def scores(c, qt):
        return jnp.dot(k_ref[0, 0, c * tk:(c + 1) * tk, :], qt, preferred_element_type=F32)

    qt_first = load_qt(0)
    for a in range(ahead):
        ahead_refs[a][...] = scores(a, qt_first)

    def query_tile(qi, carry):
        qt = load_qt(qi)
        qt_next = load_qt(jnp.minimum(qi + 1, n_tiles - 1))
        pending = [ref[...] for ref in ahead_refs]
        total = None
        for c in range(nk):
            if c + ahead < nk:
                pending.append(scores(c + ahead, qt))
            else:
                ahead_refs[c + ahead - nk][...] = scores(c + ahead - nk, qt_next)
            p = jnp.exp2(pending.pop(0)).astype(BF16)
            part = jnp.dot(vt_ref[0, 0, :, c * tk:(c + 1) * tk], p, preferred_element_type=F32)
            total = part if total is None else total + part
        o_ref[0, :, pl.ds(pl.multiple_of(qi * tq, tq), tq)] = (
            total[0:V_HEAD] * (1.0 / total[V_HEAD:V_HEAD + 1])).astype(BF16)
        return carry

    lax.fori_loop(0, n_tiles, query_tile, 0)


def _attention(qt, k, vt, bounded):
    batch, heads, _, seq = qt.shape
    assert ATTN_UNROLL % 2 == 0 and (seq // TKV) % ATTN_UNROLL == 0 and seq % TQ == 0
    in_specs = [
        pl.BlockSpec((1, 1, HEAD_SLOT, seq), lambda b, h, flag: (b, h, 0, 0)),
        pl.BlockSpec((1, 1, seq, HEAD_SLOT), lambda b, h, flag: (b, h, 0, 0)),
        pl.BlockSpec((1, 1, V_ROWS, seq), lambda b, h, flag: (b, h, 0, 0)),
    ]
    out_spec = pl.BlockSpec((1, V_HEAD, seq), lambda b, h, flag: (b, h, 0))
    out_shape = jax.ShapeDtypeStruct((batch, heads * V_HEAD, seq), BF16)
    grid_spec = pltpu.PrefetchScalarGridSpec(
        num_scalar_prefetch=1, grid=(batch, heads), in_specs=in_specs, out_specs=out_spec,
        scratch_shapes=[pltpu.VMEM((max(TKV, TKV_BOUNDED), TQ), F32), pltpu.VMEM((max(TKV, TKV_BOUNDED), TQ), F32),
                        pltpu.VMEM((1, TQ), F32), pltpu.VMEM((V_ROWS, TQ), F32)])
    return pl.pallas_call(
        _attn_select_kernel, grid_spec=grid_spec, out_shape=out_shape,
        compiler_params=pltpu.CompilerParams(dimension_semantics=("parallel", "parallel"),
                                             vmem_limit_bytes=VMEM_LIMIT),
        name="attention",
    )(bounded.astype(jnp.int32).reshape(1), qt, k, vt)


def _attn_select_kernel(bounded_ref, qt_ref, k_ref, vt_ref, o_ref, sa_ref, sb_ref, m_ref, acc_ref):
    @pl.when(bounded_ref[0] != 0)
    def _():
        _attn_bounded_kernel(qt_ref, k_ref, vt_ref, o_ref, sa_ref.at[0:TKV_BOUNDED], sb_ref.at[0:TKV_BOUNDED],
                             tq=TQ, tk=TKV_BOUNDED)

    @pl.when(bounded_ref[0] == 0)
    def _():
        _attn_kernel(qt_ref, k_ref, vt_ref, o_ref, sa_ref.at[0:TKV], sb_ref.at[0:TKV], m_ref, acc_ref,
                     tq=TQ, tk=TKV)


def _mix_kernel(x_ref, g1_ref, wg_ref, ot_ref, cu_ref, cup_ref, cun_ref, bg_ref, cw_ref,
                woa_ref, wob_ref, wo_ref, g2_ref, rw_ref, rb_ref,
                x1_ref, h2_ref, mi_ref, mw_ref, cnt_ref, tri_ref, carry_ref, *, steps_per_seq):
    i = pl.program_id(0)
    tm = x_ref.shape[0]
    ts = tm // MIX_SUBTILES

    @pl.when(i == 0)
    def _():
        r = lax.broadcasted_iota(jnp.int32, (ts, ts), 0)
        c = lax.broadcasted_iota(jnp.int32, (ts, ts), 1)
        tri_ref[...] = jnp.where(c < r, 1.0, 0.0).astype(BF16)
        carry_ref[...] = jnp.zeros_like(carry_ref)

    cu = cu_ref[...]
    row = lax.broadcasted_iota(jnp.int32, cu.shape, 0)
    s_in_seq = i % steps_per_seq
    prev_row = jnp.where(s_in_seq == 0, 0.0, cup_ref[7:8, :])
    next_row = jnp.where(s_in_seq == steps_per_seq - 1, 0.0, cun_ref[0:1, :])
    below = jnp.where(row == 0, prev_row, pltpu.roll(cu, 1, 0))
    above = jnp.where(row == tm - 1, next_row, pltpu.roll(cu, tm - 1, 0))
    cw = cw_ref[...]
    gated_conv = (bg_ref[...] * (cw[0:1, :] * below + cw[1:2, :] * cu + cw[2:3, :] * above)).astype(BF16)

    carry = carry_ref[...]
    for part in range(MIX_SUBTILES):
        carry = _mix_rows(part * ts, ts, gated_conv[part * ts:(part + 1) * ts], carry, x_ref, g1_ref, wg_ref,
                          ot_ref, woa_ref, wob_ref, wo_ref, g2_ref, rw_ref, rb_ref, tri_ref,
                          x1_ref, h2_ref, mi_ref, mw_ref)
    carry_ref[...] = carry
    cnt_ref[...] = jnp.broadcast_to(carry, cnt_ref.shape)


def _mix_rows(r0, ts, gated_conv, carry, x_ref, g1_ref, wg_ref, ot_ref, woa_ref, wob_ref, wo_ref, g2_ref,
              rw_ref, rb_ref, tri_ref, x1_ref, h2_ref, mi_ref, mw_ref):
    rows = slice(r0, r0 + ts)
    y_a = lax.dot_general(ot_ref[0, :, rows], woa_ref[...], (((0,), (0,)), ((), ())), preferred_element_type=F32)
    y_b = jnp.dot(gated_conv, wob_ref[...], preferred_element_type=F32)

    x = x_ref[rows, :]
    h = _rms(x, g1_ref[...]).astype(BF16)
    gates = jnp.dot(h, wg_ref[...], preferred_element_type=F32)
    d = x.shape[1]
    sig_a = 1.0 / (1.0 + jnp.exp(-gates[:, 0:d]))
    sig_b = 1.0 / (1.0 + jnp.exp(-gates[:, d:2 * d]))

    merged = (sig_a * y_a + sig_b * y_b).astype(BF16)
    x1 = x + jnp.dot(merged, wo_ref[...], preferred_element_type=F32)
    x1_ref[rows, :] = x1
    h2 = _rms(x1, g2_ref[...])
    _store_token_tiles(h2_ref.at[r0 * SUBLANE:(r0 + ts) * SUBLANE], h2)

    h2_hi = h2.astype(BF16)
    h2_lo = (h2 - h2_hi.astype(F32)).astype(BF16)
    rw = rw_ref[...]
    hi_terms = jnp.dot(h2_hi, rw, preferred_element_type=F32)
    logits = (hi_terms[:, 0:LANE] + hi_terms[:, LANE:2 * LANE]
              + jnp.dot(h2_lo, rw[:, 0:LANE], preferred_element_type=F32)
              + rb_ref[...])

    lane = lax.broadcasted_iota(jnp.int32, logits.shape, 1)
    lane_f = lane.astype(F32)
    work = logits
    vals, idxs, hits = [], [], []
    for _ in range(TOP_K):
        mx = jnp.max(work, axis=-1, keepdims=True)
        idx = jnp.min(jnp.where(work == mx, lane_f, float(LANE)), axis=-1, keepdims=True)
        hit = lane_f == idx
        work = jnp.where(hit, -jnp.inf, work)
        vals.append(mx)
        idxs.append(idx)
        hits.append(hit)
    exps = [jnp.exp(v - vals[0]) for v in vals]
    denom = exps[0] + exps[1] + exps[2] + exps[3]
    inv = 1.0 / denom

    sel = jnp.zeros(logits.shape, F32)
    for hit in hits:
        sel = sel + jnp.where(hit, 1.0, 0.0)
    before = jnp.dot(tri_ref[...], sel.astype(BF16), preferred_element_type=F32) + carry

    mi = jnp.zeros(logits.shape, F32)
    mw = jnp.zeros(logits.shape, F32)
    for k in range(TOP_K):
        rank = jnp.sum(jnp.where(hits[k], before, 0.0), axis=-1, keepdims=True)
        mi = jnp.where(lane == k, idxs[k], mi)
        mi = jnp.where(lane == TOP_K + k, rank, mi)
        mw = jnp.where(lane == k, exps[k] * inv, mw)
    mi_ref[:, rows] = mi.T[0:2 * TOP_K].astype(jnp.int32)
    mw_ref[rows, :] = mw
    return carry + jnp.sum(sel, axis=0, keepdims=True)


def _mix(x2, g1, w_g, ot, cu, bg, conv_w, woa, wob, wo, g2, rw_split, rb, batch, seq):
    n, d = x2.shape
    tm = TM_MIX
    spb = seq // tm
    r8 = tm // 8
    nsteps = n // tm
    full = lambda shp: pl.BlockSpec(shp, lambda i: (0,) * len(shp))
    return pl.pallas_call(
        functools.partial(_mix_kernel, steps_per_seq=spb),
        grid=(nsteps,),
        in_specs=[
            pl.BlockSpec((tm, d), lambda i: (i, 0)),
            full(g1.shape), full(w_g.shape),
            pl.BlockSpec((1, MLA_HEADS * V_HEAD, tm), lambda i: (i // spb, 0, i % spb)),
            pl.BlockSpec((tm, CONV_WIDTH), lambda i: (i, 0)),
            pl.BlockSpec((8, CONV_WIDTH), lambda i: (jnp.maximum(i * r8 - 1, 0), 0)),
            pl.BlockSpec((8, CONV_WIDTH), lambda i: (jnp.minimum((i + 1) * r8, nsteps * r8 - 1), 0)),
            pl.BlockSpec((tm, CONV_WIDTH), lambda i: (i, 0)),
            full(conv_w.shape), full(woa.shape), full(wob.shape), full(wo.shape), full(g2.shape),
            full(rw_split.shape), full(rb.shape),
        ],
        out_specs=[
            pl.BlockSpec((tm, d), lambda i: (i, 0)),
            pl.BlockSpec((tm * SUBLANE, LANE), lambda i: (i, 0)),
            pl.BlockSpec((2 * TOP_K, tm), lambda i: (0, i)),
            pl.BlockSpec((tm, LANE), lambda i: (i, 0)),
            pl.BlockSpec((8, LANE), lambda i: (0, 0)),
        ],
        out_shape=[
            jax.ShapeDtypeStruct((n, d), F32),
            jax.ShapeDtypeStruct((n * SUBLANE, LANE), F32),
            jax.ShapeDtypeStruct((2 * TOP_K, n), jnp.int32),
            jax.ShapeDtypeStruct((n, LANE), F32),
            jax.ShapeDtypeStruct((8, LANE), F32),
        ],
        scratch_shapes=[pltpu.VMEM((tm // MIX_SUBTILES, tm // MIX_SUBTILES), BF16), pltpu.VMEM((1, LANE), F32)],
        compiler_params=pltpu.CompilerParams(dimension_semantics=("arbitrary",), vmem_limit_bytes=VMEM_LIMIT),
        name="mix_route",
    )(x2, g1, w_g, ot, cu, cu, cu, bg, conv_w, woa, wob, wo, g2, rw_split, rb)


def _row_copy_wait(src_like, dst_like, sem, times):
    for _ in range(times):
        pltpu.make_async_copy(src_like, dst_like, sem).wait()


def _token(ref, idx):
    return ref.at[pl.ds(pl.multiple_of(idx * SUBLANE, SUBLANE), SUBLANE)]


def _dispatch_kernel(dest_ref, pe_ref, nused_ref, h2_ref, xs_ref, zero_ref, sem, zsem):
    tm = h2_ref.shape[0] // SUBLANE
    block_rows = EXPERT_BLOCK * SUBLANE
    n_blocks = xs_ref.shape[0] // block_rows

    @pl.when(pl.program_id(0) == 0)
    def _():
        zero_ref[...] = jnp.zeros_like(zero_ref)

        def zero_copy(first_token):
            start = pl.multiple_of(first_token * SUBLANE, block_rows)
            return pltpu.make_async_copy(zero_ref, xs_ref.at[pl.ds(start, block_rows)], zsem)

        def has_rows(e):
            return pe_ref[e] > jnp.where(e == 0, 0, pe_ref[jnp.maximum(e - 1, 0)])

        def start_expert(e, c):
            @pl.when(has_rows(e))
            def _():
                zero_copy(pe_ref[e] - EXPERT_BLOCK).start()
            return c

        def wait_expert(e, c):
            @pl.when(has_rows(e))
            def _():
                zero_copy(pe_ref[e] - EXPERT_BLOCK).wait()
            return c

        def start_tail(b, c):
            zero_copy(b * EXPERT_BLOCK).start()
            return c

        def wait_tail(b, c):
            zero_copy(b * EXPERT_BLOCK).wait()
            return c

        lax.fori_loop(0, N_EXPERTS, start_expert, 0)
        lax.fori_loop(nused_ref[0], n_blocks, start_tail, 0)
        lax.fori_loop(0, N_EXPERTS, wait_expert, 0)
        lax.fori_loop(nused_ref[0], n_blocks, wait_tail, 0)

    def issue(g, c):
        for u in range(ISSUE_GROUP):
            t = g * (ISSUE_GROUP // TOP_K) + u // TOP_K
            dest = dest_ref[(u % TOP_K) * tm + t]
            pltpu.make_async_copy(_token(h2_ref, t), _token(xs_ref, dest), sem).start(priority=u % 2)
        return c

    lax.fori_loop(0, tm * TOP_K // ISSUE_GROUP, issue, 0)
    _row_copy_wait(h2_ref, xs_ref.at[pl.ds(0, tm * SUBLANE)], sem, TOP_K)


def _dispatch(dest_flat, pad_end, n_used, h2t, rows):
    tm = TM_ROUTE
    n = h2t.shape[0] // SUBLANE
    return pl.pallas_call(
        _dispatch_kernel,
        grid=(n // tm,),
        in_specs=[
            pl.BlockSpec((tm * TOP_K,), lambda i: (i,), memory_space=pltpu.SMEM),
            pl.BlockSpec(memory_space=pltpu.SMEM),
            pl.BlockSpec(memory_space=pltpu.SMEM),
            pl.BlockSpec((tm * SUBLANE, LANE), lambda i: (i, 0)),
        ],
        out_specs=pl.BlockSpec(memory_space=pl.ANY),
        out_shape=jax.ShapeDtypeStruct((rows * SUBLANE, LANE), F32),
        scratch_shapes=[pltpu.VMEM((EXPERT_BLOCK * SUBLANE, LANE), F32), pltpu.SemaphoreType.DMA(()),
                        pltpu.SemaphoreType.DMA(())],
        compiler_params=pltpu.CompilerParams(dimension_semantics=("arbitrary",), vmem_limit_bytes=VMEM_LIMIT),
        name="dispatch",
    )(dest_flat, pad_end, n_used, h2t)


def _expert_kernel(bexp_ref, nused_ref, nexp_ref, slot_ref, nvalid_ref, xs_ref, w1_hbm, b1_ref, w2_hbm, b2_ref, ys_ref,
                   w1_ref, w2_ref, sem):
    i = pl.program_id(0)
    active = i < nused_ref[0]
    expert = bexp_ref[i]
    slot = slot_ref[i]
    prev = bexp_ref[jnp.maximum(i - 1, 0)]
    fresh = jnp.logical_or(i == 0, expert != prev)

    def weight_copies(e, s):
        return (pltpu.make_async_copy(w1_hbm.at[e], w1_ref.at[s], sem.at[0, s]),
                pltpu.make_async_copy(w2_hbm.at[e], w2_ref.at[s], sem.at[1, s]))

    @pl.when(jnp.logical_and(active, i == 0))
    def _():
        for cp in weight_copies(expert, slot):
            cp.start()

    @pl.when(jnp.logical_and(active, fresh))
    def _():
        for cp in weight_copies(expert, slot):
            cp.wait()

        @pl.when(nexp_ref[i] != expert)
        def _():
            for cp in weight_copies(nexp_ref[i], 1 - slot):
                cp.start()

    def ffn(rows):
        dff = w2_ref.shape[1]
        xb = _load_token_tiles(xs_ref, (), rows).astype(BF16)
        hm = jnp.dot(xb, w1_ref[slot].astype(BF16), preferred_element_type=F32) + b1_ref[0]
        gate = jnp.minimum(hm[:, 0:dff], SWIGLU_LIMIT)
        up = jnp.clip(hm[:, dff:2 * dff], -SWIGLU_LIMIT, SWIGLU_LIMIT)
        glu = gate * (1.0 / (1.0 + jnp.exp(-SWIGLU_ALPHA * gate)))
        act = ((up + 1.0) * glu).astype(BF16)
        _store_token_tiles(ys_ref.at[0:rows * SUBLANE],
                           jnp.dot(act, w2_ref[slot].astype(BF16), preferred_element_type=F32) + b2_ref[0])

    groups = lax.div(nvalid_ref[i] + (EXPERT_ROW_GROUP - 1), EXPERT_ROW_GROUP)
    for g in range(1, EXPERT_BLOCK // EXPERT_ROW_GROUP + 1):
        rows = g * EXPERT_ROW_GROUP

        @pl.when(jnp.logical_and(active, groups == g))
        def _(rows=rows):
            ffn(rows)
            if rows < EXPERT_BLOCK:
                ys_ref[rows * SUBLANE:, :] = jnp.zeros(((EXPERT_BLOCK - rows) * SUBLANE, LANE), F32)

    @pl.when(jnp.logical_not(active))
    def _():
        ys_ref[...] = jnp.zeros_like(ys_ref)


def _experts(block_exp, n_used, next_exp, weight_slot, block_valid, xs, w1, b1, w2, b2):
    d = w1.shape[1]
    assert d == SUBLANE * LANE
    block_rows = EXPERT_BLOCK * SUBLANE
    n_blocks = xs.shape[0] // block_rows
    dff2 = w1.shape[2]
    dff = w2.shape[1]
    grid_spec = pltpu.PrefetchScalarGridSpec(
        num_scalar_prefetch=5,
        grid=(n_blocks,),
        in_specs=[
            pl.BlockSpec((block_rows, LANE), lambda i, be, nu, ne, ws, nv: (jnp.minimum(i, nu[0] - 1), 0)),
            pl.BlockSpec(memory_space=pl.ANY),
            pl.BlockSpec((1, 1, dff2), lambda i, be, nu, ne, ws, nv: (be[i], 0, 0)),
            pl.BlockSpec(memory_space=pl.ANY),
            pl.BlockSpec((1, 1, d), lambda i, be, nu, ne, ws, nv: (be[i], 0, 0)),
        ],
        out_specs=pl.BlockSpec((block_rows, LANE), lambda i, be, nu, ne, ws, nv: (i, 0)),
        scratch_shapes=[pltpu.VMEM((2, d, dff2), F32), pltpu.VMEM((2, dff, d), F32),
                        pltpu.SemaphoreType.DMA((2, 2))],
    )
    return pl.pallas_call(
        _expert_kernel,
        grid_spec=grid_spec,
        out_shape=jax.ShapeDtypeStruct(xs.shape, F32),
        compiler_params=pltpu.CompilerParams(dimension_semantics=("arbitrary",), vmem_limit_bytes=VMEM_LIMIT),
        name="experts",
    )(block_exp, n_used, next_exp, weight_slot, block_valid, xs, w1, b1, w2, b2)


def _combine_kernel(dest_ref, dest_next_ref, x1_ref, mw_ref, ys_ref, out_ref, buf_ref, sem):
    i = pl.program_id(0)
    tm = x1_ref.shape[0]
    slot = lax.rem(i, 2)

    def gather(idx_ref, s):
        def issue(g, c):
            for u in range(ISSUE_GROUP):
                t = g * (ISSUE_GROUP // TOP_K) + u // TOP_K
                dest = idx_ref[(u % TOP_K) * tm + t]
                pltpu.make_async_copy(_token(ys_ref, dest), _token(buf_ref.at[s, u % TOP_K], t),
                                      sem.at[s]).start(priority=u % 2)
            return c

        lax.fori_loop(0, tm * TOP_K // ISSUE_GROUP, issue, 0)

    @pl.when(i == 0)
    def _():
        gather(dest_ref, 0)

    @pl.when(i + 1 < pl.num_programs(0))
    def _():
        gather(dest_next_ref, 1 - slot)

    _row_copy_wait(ys_ref.at[pl.ds(0, tm * SUBLANE)], buf_ref.at[slot, 0], sem.at[slot], TOP_K)
    acc = x1_ref[...]
    mw = mw_ref[...]
    for k in range(TOP_K):
        acc = acc + mw[:, k:k + 1] * _load_token_tiles(buf_ref, (slot, k), tm)
    out_ref[...] = acc


def _combine(dest_flat, x1, mw, ys):
    n, d = x1.shape
    tm = TM_ROUTE
    nsteps = n // tm
    return pl.pallas_call(
        _combine_kernel,
        grid=(nsteps,),
        in_specs=[
            pl.BlockSpec((tm * TOP_K,), lambda i: (i,), memory_space=pltpu.SMEM),
            pl.BlockSpec((tm * TOP_K,), lambda i: (jnp.minimum(i + 1, nsteps - 1),), memory_space=pltpu.SMEM),
            pl.BlockSpec((tm, d), lambda i: (i, 0)),
            pl.BlockSpec((tm, LANE), lambda i: (i, 0)),
            pl.BlockSpec(memory_space=pl.ANY),
        ],
        out_specs=pl.BlockSpec((tm, d), lambda i: (i, 0)),
        out_shape=jax.ShapeDtypeStruct((n, d), F32),
        scratch_shapes=[pltpu.VMEM((2, TOP_K, tm * SUBLANE, LANE), F32), pltpu.SemaphoreType.DMA((2,))],
        compiler_params=pltpu.CompilerParams(dimension_semantics=("arbitrary",), vmem_limit_bytes=VMEM_LIMIT),
        name="combine",
    )(dest_flat, dest_flat, x1, mw, ys)


def _pad_cols(w, width):
    return jnp.pad(w, ((0, 0), (0, width - w.shape[1])))


def _head_slots(w, per_head):
    rows = w.shape[0]
    w3 = w.reshape(rows, MLA_HEADS, per_head)
    return jnp.pad(w3, ((0, 0), (0, 0), (0, HEAD_SLOT - per_head))).reshape(rows, MLA_HEADS * HEAD_SLOT)


def _rope_tables(positions):
    inv_freq = ROPE_THETA ** (-jnp.arange(0, QK_ROPE, 2, dtype=F32) / QK_ROPE)
    ang = positions.astype(F32).reshape(-1, 1) * inv_freq
    return jnp.cos(ang).T, jnp.sin(ang).T


def _layer(x2, positions, norm1_g, w_in, q_a_norm_g, kv_a_norm_g, w_uq, w_ukv, q_norm_g, k_norm_g,
           conv_w, w_o_mla, w_o_conv, w_o, norm2_g, router_w, router_b,
           expert_w1, expert_b1, expert_w2, expert_b2, batch, seq):
    n, d = x2.shape
    o_kr = Q_LORA + KV_LORA
    o_u = o_kr + QK_ROPE
    o_g = o_u + 3 * CONV_WIDTH
    kr_cols = jnp.pad(w_in[:, o_kr:o_u], ((0, 0), (QK_NOPE, LANE - QK_HEAD)))
    w_a = jnp.concatenate([w_in[:, :o_kr], kr_cols, w_in[:, o_u:o_g]], axis=1).astype(BF16)
    w_g = w_in[:, o_g:].astype(BF16)
    row = lambda v: v.reshape(1, -1)
    cos_c, sin_c = _rope_tables(positions)
    q_scale = (QK_HEAD ** -0.5) * math.log2(math.e)
    gain_t = lambda g: jnp.broadcast_to(g.reshape(LANE, 1), (LANE, TM_PROJ // PROJ_SUBTILES))
    kg = _pad_cols(row(k_norm_g), LANE)
    score_bound = 1.02 * q_scale * QK_HEAD * jnp.max(jnp.abs(q_norm_g)) * jnp.max(jnp.abs(k_norm_g))
    bounded = 2.0 * score_bound <= SAFE_SCORE_RANGE
    offset = jnp.where(bounded, score_bound, 0.0)
    feature = jnp.arange(LANE) == OFFSET_FEATURE
    qoff = gain_t(jnp.where(feature, -offset, 0.0).astype(F32))
    koff = jnp.where(feature, 1.0, 0.0).astype(F32).reshape(1, LANE)

    qt, k, vt, cu, bg = _in_projection(
        x2, row(norm1_g), w_a, row(q_a_norm_g), row(kv_a_norm_g),
        _head_slots(w_uq, QK_HEAD).astype(BF16), w_ukv.astype(BF16),
        gain_t(_pad_cols(row(q_norm_g) * q_scale, LANE)), kg, gain_t(kg), qoff, koff,
        cos_c, sin_c, batch, seq)
    ot = _attention(qt, k, vt, bounded)

    rw = _pad_cols(router_w, LANE)
    rw_hi = rw.astype(BF16)
    rw_lo = (rw - rw_hi.astype(F32)).astype(BF16)
    rb = jnp.concatenate([row(router_b), jnp.full((1, LANE - N_EXPERTS), NEG_BIG, F32)], axis=1)
    x1, h2, mi, mw, cnt = _mix(
        x2, row(norm1_g), w_g, ot, cu, bg, conv_w, w_o_mla.astype(BF16), w_o_conv.astype(BF16),
        w_o.astype(BF16), row(norm2_g), jnp.concatenate([rw_hi, rw_lo], axis=1), rb, batch, seq)

    counts = cnt[0, :N_EXPERTS].astype(jnp.int32)
    padded = (counts + EXPERT_BLOCK - 1) // EXPERT_BLOCK * EXPERT_BLOCK
    experts = jnp.arange(N_EXPERTS, dtype=jnp.int32)
    pad_end = jnp.sum(jnp.where(experts[None, :] <= experts[:, None], padded[None, :], 0), axis=1)
    pad_start = (pad_end - padded).astype(jnp.int32)
    nk = n * TOP_K
    n_blocks = (nk + N_EXPERTS * (EXPERT_BLOCK - 1) + EXPERT_BLOCK - 1) // EXPERT_BLOCK
    rows = n_blocks * EXPERT_BLOCK
    block_first_row = jnp.arange(n_blocks, dtype=jnp.int32) * EXPERT_BLOCK
    block_exp = jnp.minimum(jnp.sum(pad_end[None, :] <= block_first_row[:, None], axis=1),
                            N_EXPERTS - 1).astype(jnp.int32)
    n_used = (pad_end[-1:] // EXPERT_BLOCK).astype(jnp.int32)
    group_end = jnp.sum(jnp.where(experts[None, :] == block_exp[:, None], pad_end[None, :], 0), axis=1)
    following = jnp.minimum(jnp.sum(pad_end[None, :] <= group_end[:, None], axis=1), N_EXPERTS - 1)
    next_exp = jnp.where(group_end < pad_end[-1], following, block_exp).astype(jnp.int32)
    ordinal = jnp.sum(jnp.where(experts[None, :] < experts[:, None], (padded > 0)[None, :], False), axis=1)
    weight_slot = jnp.sum(jnp.where(experts[None, :] == block_exp[:, None], (ordinal % 2)[None, :], 0),
                          axis=1).astype(jnp.int32)
    e_sel = mi[None, 0:TOP_K] == jnp.arange(N_EXPERTS, dtype=jnp.int32)[:, None, None]
    dest = jnp.sum(jnp.where(e_sel, pad_start[:, None, None], 0), axis=0) + mi[TOP_K:2 * TOP_K]
    dest_flat = dest.reshape(TOP_K, n // TM_ROUTE, TM_ROUTE).transpose(1, 0, 2).reshape(nk)

    xs = _dispatch(dest_flat, pad_end.astype(jnp.int32), n_used, h2, rows)
    rows_end = jnp.sum(jnp.where(experts[None, :] == block_exp[:, None], (pad_start + counts)[None, :], 0), axis=1)
    block_valid = jnp.clip(rows_end - block_first_row, 0, EXPERT_BLOCK).astype(jnp.int32)
    ys = _experts(block_exp, n_used, next_exp, weight_slot, block_valid, xs, expert_w1, expert_b1.reshape(N_EXPERTS, 1, -1),
                  expert_w2, expert_b2.reshape(N_EXPERTS, 1, -1))
    return _combine(dest_flat, x1, mw, ys)


def kernel(x, positions, norm1_g, w_in, q_a_norm_g, kv_a_norm_g, w_uq, w_ukv, q_norm_g, k_norm_g, conv_w,
           w_o_mla, w_o_conv, w_o, norm2_g, router_w, router_b, expert_w1, expert_b1, expert_w2, expert_b2):
    batch, seq, d = x.shape
    depth = norm1_g.shape[0]
    x2 = x.reshape(batch * seq, d)
    for l in range(depth):
        x2 = _layer(x2, positions, norm1_g[l], w_in[l], q_a_norm_g[l], kv_a_norm_g[l], w_uq[l], w_ukv[l],
                    q_norm_g[l], k_norm_g[l], conv_w[l], w_o_mla[l], w_o_conv[l], w_o[l], norm2_g[l],
                    router_w[l], router_b[l], expert_w1[l], expert_b1[l], expert_w2[l], expert_b2[l], batch, seq)
    return x2.reshape(batch, seq, d)
```

```python
import functools
import math

import jax
import jax.numpy as jnp
from jax import lax
from jax.experimental import pallas as pl
from jax.experimental.pallas import tpu as pltpu

F32 = jnp.float32
BF16 = jnp.bfloat16

MLA_HEADS = 8
QK_NOPE = 64
QK_ROPE = 32
QK_HEAD = QK_NOPE + QK_ROPE
V_HEAD = 64
Q_LORA = 256
KV_LORA = 128
ROPE_THETA = 10000.0
CONV_WIDTH = 512
N_EXPERTS = 32
TOP_K = 4
SWIGLU_LIMIT = 7.0
SWIGLU_ALPHA = 1.702
EPS = 1e-6

LANE = 128
SUBLANE = 8
HEAD_SLOT = LANE
HALF_ROPE = QK_ROPE // 2
V_ROWS = V_HEAD + 16
OFFSET_FEATURE = QK_HEAD
SAFE_SCORE_RANGE = 100.0
VMEM_LIMIT = 56 * 1024 * 1024

TM_PROJ = 512
PROJ_SUBTILES = 2
TQ = 512
TKV = 512
ATTN_UNROLL = 4
TKV_BOUNDED = 256
TM_MIX = 512
MIX_SUBTILES = 1
TM_ROUTE = 512
EXPERT_BLOCK = 512
EXPERT_ROW_GROUP = 128
ISSUE_GROUP = 16
NEG_BIG = -1e30


def _load_token_tiles(ref, lead, rows):
    return jnp.concatenate([ref[lead + (pl.ds(c, rows, stride=SUBLANE), slice(None))] for c in range(SUBLANE)],
                           axis=1)


def _store_token_tiles(ref, value):
    rows = value.shape[0]
    for c in range(SUBLANE):
        ref[pl.ds(c, rows, stride=SUBLANE), :] = value[:, c * LANE:(c + 1) * LANE]


def _rms(x, g):
    return x * lax.rsqrt(jnp.mean(x * x, axis=-1, keepdims=True) + EPS) * g


def _inproj_kernel(x_ref, g1_ref, w_ref, gq_ref, gkv_ref, wuq_ref, wukv_ref, qgt_ref, kg_ref, kgt_ref,
                   qoff_ref, koff_ref, cost_ref, sint_ref, qt_ref, k_ref, vt_ref, cu_ref, bg_ref):
    tm = x_ref.shape[0] // PROJ_SUBTILES
    for part in range(PROJ_SUBTILES):
        _inproj_rows(slice(part * tm, (part + 1) * tm), x_ref, g1_ref, w_ref, gq_ref, gkv_ref, wuq_ref, wukv_ref,
                     qgt_ref, kg_ref, kgt_ref, qoff_ref, koff_ref, cost_ref, sint_ref,
                     qt_ref, k_ref, vt_ref, cu_ref, bg_ref)


def _inproj_rows(rows, x_ref, g1_ref, w_ref, gq_ref, gkv_ref, wuq_ref, wukv_ref, qgt_ref, kg_ref, kgt_ref,
                 qoff_ref, koff_ref, cost_ref, sint_ref, qt_ref, k_ref, vt_ref, cu_ref, bg_ref):
    x = x_ref[rows, :]
    h = _rms(x, g1_ref[...]).astype(BF16)
    proj = jnp.dot(h, w_ref[...], preferred_element_type=F32)
    c_q = proj[:, 0:Q_LORA]
    c_kv = proj[:, Q_LORA:Q_LORA + KV_LORA]
    kr = proj[:, Q_LORA + KV_LORA:Q_LORA + KV_LORA + LANE]
    o = Q_LORA + KV_LORA + LANE
    u = proj[:, o:o + CONV_WIDTH]
    c_gate = proj[:, o + CONV_WIDTH:o + 2 * CONV_WIDTH]
    b_gate = proj[:, o + 2 * CONV_WIDTH:o + 3 * CONV_WIDTH]
    cu_ref[rows, :] = c_gate * u
    bg_ref[rows, :] = b_gate

    kg = kg_ref[...]
    tm = x.shape[0]
    lane = lax.broadcasted_iota(jnp.int32, (tm, LANE), 1)

    q = jnp.dot(_rms(c_q, gq_ref[...]).astype(BF16), wuq_ref[...], preferred_element_type=F32)
    kv = jnp.dot(_rms(c_kv, gkv_ref[...]).astype(BF16), wukv_ref[...], preferred_element_type=F32)

    cos_c = cost_ref[:, rows]
    sin_c = sint_ref[:, rows]

    def rope_t(t):
        t1 = t[QK_NOPE:QK_NOPE + HALF_ROPE]
        t2 = t[QK_NOPE + HALF_ROPE:QK_HEAD]
        return jnp.concatenate([t[0:QK_NOPE], t1 * cos_c - t2 * sin_c, t1 * sin_c + t2 * cos_c, t[QK_HEAD:]], axis=0)

    ss_r = jnp.sum(kr * kr, axis=-1, keepdims=True)
    kr_roped = rope_t(kr.T * kgt_ref[...]).T
    qgt = qgt_ref[...]
    qoff = qoff_ref[...]
    koff = koff_ref[...]
    ones = jnp.ones((V_ROWS - V_HEAD, tm), BF16)
    for hd in range(MLA_HEADS):
        qht = q[:, hd * HEAD_SLOT:(hd + 1) * HEAD_SLOT].T
        r = lax.rsqrt(jnp.sum(qht * qht, axis=0, keepdims=True) * (1.0 / QK_HEAD) + EPS)
        qt_ref[0, hd, :, rows] = (rope_t(qht * r * qgt) + qoff).astype(BF16)

        kvh = kv[:, hd * HEAD_SLOT:(hd + 1) * HEAD_SLOT]
        knope = jnp.where(lane < QK_NOPE, kvh, 0.0)
        rk = lax.rsqrt((jnp.sum(knope * knope, axis=-1, keepdims=True) + ss_r) * (1.0 / QK_HEAD) + EPS)
        k_ref[0, hd, rows, :] = ((knope * kg + kr_roped) * rk + koff).astype(BF16)
        kvt = kvh.T
        vt_ref[0, hd, 0:V_HEAD, rows] = kvt[QK_NOPE:QK_NOPE + V_HEAD].astype(BF16)
        vt_ref[0, hd, V_HEAD:V_ROWS, rows] = ones


def _in_projection(x2, g1, w_a, gq, gkv, wuq, wukv, qgt, kg, kgt, qoff, koff, cos_c, sin_c, batch, seq):
    n, d = x2.shape
    tm = TM_PROJ
    spb = seq // tm
    full = lambda shp: pl.BlockSpec(shp, lambda i: (0,) * len(shp))
    return pl.pallas_call(
        _inproj_kernel,
        grid=(n // tm,),
        in_specs=[
            pl.BlockSpec((tm, d), lambda i: (i, 0)),
            full(g1.shape), full(w_a.shape), full(gq.shape), full(gkv.shape), full(wuq.shape), full(wukv.shape),
            full(qgt.shape), full(kg.shape), full(kgt.shape), full(qoff.shape), full(koff.shape),
            pl.BlockSpec((HALF_ROPE, tm), lambda i: (0, i)),
            pl.BlockSpec((HALF_ROPE, tm), lambda i: (0, i)),
        ],
        out_specs=[
            pl.BlockSpec((1, MLA_HEADS, HEAD_SLOT, tm), lambda i: (i // spb, 0, 0, i % spb)),
            pl.BlockSpec((1, MLA_HEADS, tm, HEAD_SLOT), lambda i: (i // spb, 0, i % spb, 0)),
            pl.BlockSpec((1, MLA_HEADS, V_ROWS, tm), lambda i: (i // spb, 0, 0, i % spb)),
            pl.BlockSpec((tm, CONV_WIDTH), lambda i: (i, 0)),
            pl.BlockSpec((tm, CONV_WIDTH), lambda i: (i, 0)),
        ],
        out_shape=[
            jax.ShapeDtypeStruct((batch, MLA_HEADS, HEAD_SLOT, seq), BF16),
            jax.ShapeDtypeStruct((batch, MLA_HEADS, seq, HEAD_SLOT), BF16),
            jax.ShapeDtypeStruct((batch, MLA_HEADS, V_ROWS, seq), BF16),
            jax.ShapeDtypeStruct((n, CONV_WIDTH), F32),
            jax.ShapeDtypeStruct((n, CONV_WIDTH), F32),
        ],
        compiler_params=pltpu.CompilerParams(dimension_semantics=("parallel",), vmem_limit_bytes=VMEM_LIMIT),
        name="in_projection",
    )(x2, g1, w_a, gq, gkv, wuq, wukv, qgt, kg, kgt, qoff, koff, cos_c, sin_c)


def _attn_kernel(qt_ref, k_ref, vt_ref, o_ref, sa_ref, sb_ref, m_ref, acc_ref, *, tq, tk):
    seq = k_ref.shape[2]
    nk = seq // tk
    bufs = (sa_ref, sb_ref)

    def query_tile(qi, carry):
        q0 = pl.multiple_of(qi * tq, tq)
        qt = qt_ref[0, 0, :, pl.ds(q0, tq)]

        def scores(c, s_ref):
            k0 = pl.multiple_of(c * tk, tk)
            s_ref[...] = jnp.dot(k_ref[0, 0, pl.ds(k0, tk), :], qt, preferred_element_type=F32)

        def accumulate(c, s_ref):
            k0 = pl.multiple_of(c * tk, tk)
            s = s_ref[...]
            m = m_ref[...]
            m_new = jnp.maximum(m, jnp.max(s, axis=0, keepdims=True))
            m_ref[...] = m_new
            p = jnp.exp2(s - m_new).astype(BF16)
            vs = vt_ref[0, 0, :, pl.ds(k0, tk)]
            acc_ref[...] = jnp.exp2(m - m_new) * acc_ref[...] + jnp.dot(vs, p, preferred_element_type=F32)

        m_ref[...] = jnp.full(m_ref.shape, NEG_BIG, F32)
        acc_ref[...] = jnp.zeros(acc_ref.shape, F32)
        scores(0, sa_ref)

        def group(j, c):
            base = ATTN_UNROLL * j
            for u in range(ATTN_UNROLL):
                scores(base + u + 1, bufs[(u + 1) % 2])
                accumulate(base + u, bufs[u % 2])
            return c

        lax.fori_loop(0, nk // ATTN_UNROLL - 1, group, 0)
        base = nk - ATTN_UNROLL
        for u in range(ATTN_UNROLL):
            if u + 1 < ATTN_UNROLL:
                scores(base + u + 1, bufs[(u + 1) % 2])
            accumulate(base + u, bufs[u % 2])
        acc = acc_ref[...]
        o_ref[0, :, pl.ds(q0, tq)] = (acc[0:V_HEAD] * (1.0 / acc[V_HEAD:V_HEAD + 1])).astype(BF16)
        return carry

    lax.fori_loop(0, seq // tq, query_tile, 0)


def _attn_bounded_kernel(qt_ref, k_ref, vt_ref, o_ref, sa_ref, sb_ref, *, tq, tk):
    seq = k_ref.shape[2]
    nk = seq // tk
    n_tiles = seq // tq
    ahead_refs = (sa_ref, sb_ref)
    ahead = len(ahead_refs)

    def load_qt(qi):
        return qt_ref[0, 0, :, pl.ds(pl.multiple_of(qi * tq, tq), tq)]

    def scores(c, qt):
        return jnp.dot(k_ref[0, 0, c * tk:(c + 1) * tk, :], qt, preferred_element_type=F32)

    qt_first = load_qt(0)
    for a in range(ahead):
        ahead_refs[a][...] = scores(a, qt_first)

    def query_tile(qi, carry):
        qt = load_qt(qi)
        qt_next = load_qt(jnp.minimum(qi + 1, n_tiles - 1))
        pending = [ref[...] for ref in ahead_refs]
        total = None
        for c in range(nk):
            if c + ahead < nk:
                pending.append(scores(c + ahead, qt))
            else:
                ahead_refs[c + ahead - nk][...] = scores(c + ahead - nk, qt_next)
            p = jnp.exp2(pending.pop(0)).astype(BF16)
            part = jnp.dot(vt_ref[0, 0, :, c * tk:(c + 1) * tk], p, preferred_element_type=F32)
            total = part if total is None else total + part
        o_ref[0, :, pl.ds(pl.multiple_of(qi * tq, tq), tq)] = (
            total[0:V_HEAD] * (1.0 / total[V_HEAD:V_HEAD + 1])).astype(BF16)
        return carry

    lax.fori_loop(0, n_tiles, query_tile, 0)


def _attention(qt, k, vt, bounded):
    batch, heads, _, seq = qt.shape
    assert ATTN_UNROLL % 2 == 0 and (seq // TKV) % ATTN_UNROLL == 0 and seq % TQ == 0
    in_specs = [
        pl.BlockSpec((1, 1, HEAD_SLOT, seq), lambda b, h, flag: (b, h, 0, 0)),
        pl.BlockSpec((1, 1, seq, HEAD_SLOT), lambda b, h, flag: (b, h, 0, 0)),
        pl.BlockSpec((1, 1, V_ROWS, seq), lambda b, h, flag: (b, h, 0, 0)),
    ]
    out_spec = pl.BlockSpec((1, V_HEAD, seq), lambda b, h, flag: (b, h, 0))
    out_shape = jax.ShapeDtypeStruct((batch, heads * V_HEAD, seq), BF16)
    grid_spec = pltpu.PrefetchScalarGridSpec(
        num_scalar_prefetch=1, grid=(batch, heads), in_specs=in_specs, out_specs=out_spec,
        scratch_shapes=[pltpu.VMEM((max(TKV, TKV_BOUNDED), TQ), F32), pltpu.VMEM((max(TKV, TKV_BOUNDED), TQ), F32),
                        pltpu.VMEM((1, TQ), F32), pltpu.VMEM((V_ROWS, TQ), F32)])
    return pl.pallas_call(
        _attn_select_kernel, grid_spec=grid_spec, out_shape=out_shape,
        compiler_params=pltpu.CompilerParams(dimension_semantics=("parallel", "parallel"),
                                             vmem_limit_bytes=VMEM_LIMIT),
        name="attention",
    )(bounded.astype(jnp.int32).reshape(1), qt, k, vt)


def _attn_select_kernel(bounded_ref, qt_ref, k_ref, vt_ref, o_ref, sa_ref, sb_ref, m_ref, acc_ref):
    @pl.when(bounded_ref[0] != 0)
    def _():
        _attn_bounded_kernel(qt_ref, k_ref, vt_ref, o_ref, sa_ref.at[0:TKV_BOUNDED], sb_ref.at[0:TKV_BOUNDED],
                             tq=TQ, tk=TKV_BOUNDED)

    @pl.when(bounded_ref[0] == 0)
    def _():
        _attn_kernel(qt_ref, k_ref, vt_ref, o_ref, sa_ref.at[0:TKV], sb_ref.at[0:TKV], m_ref, acc_ref,
                     tq=TQ, tk=TKV)


def _mix_kernel(x_ref, g1_ref, wg_ref, ot_ref, cu_ref, cup_ref, cun_ref, bg_ref, cw_ref,
                woa_ref, wob_ref, wo_ref, g2_ref, rw_ref, rb_ref,
                x1_ref, h2_ref, mi_ref, mw_ref, cnt_ref, tri_ref, carry_ref, *, steps_per_seq):
    i = pl.program_id(0)
    tm = x_ref.shape[0]
    ts = tm // MIX_SUBTILES

    @pl.when(i == 0)
    def _():
        r = lax.broadcasted_iota(jnp.int32, (ts, ts), 0)
        c = lax.broadcasted_iota(jnp.int32, (ts, ts), 1)
        tri_ref[...] = jnp.where(c < r, 1.0, 0.0).astype(BF16)
        carry_ref[...] = jnp.zeros_like(carry_ref)

    cu = cu_ref[...]
    row = lax.broadcasted_iota(jnp.int32, cu.shape, 0)
    s_in_seq = i % steps_per_seq
    prev_row = jnp.where(s_in_seq == 0, 0.0, cup_ref[7:8, :])
    next_row = jnp.where(s_in_seq == steps_per_seq - 1, 0.0, cun_ref[0:1, :])
    below = jnp.where(row == 0, prev_row, pltpu.roll(cu, 1, 0))
    above = jnp.where(row == tm - 1, next_row, pltpu.roll(cu, tm - 1, 0))
    cw = cw_ref[...]
    gated_conv = (bg_ref[...] * (cw[0:1, :] * below + cw[1:2, :] * cu + cw[2:3, :] * above)).astype(BF16)

    carry = carry_ref[...]
    for part in range(MIX_SUBTILES):
        carry = _mix_rows(part * ts, ts, gated_conv[part * ts:(part + 1) * ts], carry, x_ref, g1_ref, wg_ref,
                          ot_ref, woa_ref, wob_ref, wo_ref, g2_ref, rw_ref, rb_ref, tri_ref,
                          x1_ref, h2_ref, mi_ref, mw_ref)
    carry_ref[...] = carry
    cnt_ref[...] = jnp.broadcast_to(carry, cnt_ref.shape)


def _mix_rows(r0, ts, gated_conv, carry, x_ref, g1_ref, wg_ref, ot_ref, woa_ref, wob_ref, wo_ref, g2_ref,
              rw_ref, rb_ref, tri_ref, x1_ref, h2_ref, mi_ref, mw_ref):
    rows = slice(r0, r0 + ts)
    y_a = lax.dot_general(ot_ref[0, :, rows], woa_ref[...], (((0,), (0,)), ((), ())), preferred_element_type=F32)
    y_b = jnp.dot(gated_conv, wob_ref[...], preferred_element_type=F32)

    x = x_ref[rows, :]
    h = _rms(x, g1_ref[...]).astype(BF16)
    gates = jnp.dot(h, wg_ref[...], preferred_element_type=F32)
    d = x.shape[1]
    sig_a = 1.0 / (1.0 + jnp.exp(-gates[:, 0:d]))
    sig_b = 1.0 / (1.0 + jnp.exp(-gates[:, d:2 * d]))

    merged = (sig_a * y_a + sig_b * y_b).astype(BF16)
    x1 = x + jnp.dot(merged, wo_ref[...], preferred_element_type=F32)
    x1_ref[rows, :] = x1
    h2 = _rms(x1, g2_ref[...])
    _store_token_tiles(h2_ref.at[r0 * SUBLANE:(r0 + ts) * SUBLANE], h2)

    h2_hi = h2.astype(BF16)
    h2_lo = (h2 - h2_hi.astype(F32)).astype(BF16)
    rw = rw_ref[...]
    hi_terms = jnp.dot(h2_hi, rw, preferred_element_type=F32)
    logits = (hi_terms[:, 0:LANE] + hi_terms[:, LANE:2 * LANE]
              + jnp.dot(h2_lo, rw[:, 0:LANE], preferred_element_type=F32)
              + rb_ref[...])

    lane = lax.broadcasted_iota(jnp.int32, logits.shape, 1)
    lane_f = lane.astype(F32)
    work = logits
    vals, idxs, hits = [], [], []
    for _ in range(TOP_K):
        mx = jnp.max(work, axis=-1, keepdims=True)
        idx = jnp.min(jnp.where(work == mx, lane_f, float(LANE)), axis=-1, keepdims=True)
        hit = lane_f == idx
        work = jnp.where(hit, -jnp.inf, work)
        vals.append(mx)
        idxs.append(idx)
        hits.append(hit)
    exps = [jnp.exp(v - vals[0]) for v in vals]
    denom = exps[0] + exps[1] + exps[2] + exps[3]
    inv = 1.0 / denom

    sel = jnp.zeros(logits.shape, F32)
    for hit in hits:
        sel = sel + jnp.where(hit, 1.0, 0.0)
    before = jnp.dot(tri_ref[...], sel.astype(BF16), preferred_element_type=F32) + carry

    mi = jnp.zeros(logits.shape, F32)
    mw = jnp.zeros(logits.shape, F32)
    for k in range(TOP_K):
        rank = jnp.sum(jnp.where(hits[k], before, 0.0), axis=-1, keepdims=True)
        mi = jnp.where(lane == k, idxs[k], mi)
        mi = jnp.where(lane == TOP_K + k, rank, mi)
        mw = jnp.where(lane == k, exps[k] * inv, mw)
    mi_ref[:, rows] = mi.T[0:2 * TOP_K].astype(jnp.int32)
    mw_ref[rows, :] = mw
    return carry + jnp.sum(sel, axis=0, keepdims=True)


def _mix(x2, g1, w_g, ot, cu, bg, conv_w, woa, wob, wo, g2, rw_split, rb, batch, seq):
    n, d = x2.shape
    tm = TM_MIX
    spb = seq // tm
    r8 = tm // 8
    nsteps = n // tm
    full = lambda shp: pl.BlockSpec(shp, lambda i: (0,) * len(shp))
    return pl.pallas_call(
        functools.partial(_mix_kernel, steps_per_seq=spb),
        grid=(nsteps,),
        in_specs=[
            pl.BlockSpec((tm, d), lambda i: (i, 0)),
            full(g1.shape), full(w_g.shape),
            pl.BlockSpec((1, MLA_HEADS * V_HEAD, tm), lambda i: (i // spb, 0, i % spb)),
            pl.BlockSpec((tm, CONV_WIDTH), lambda i: (i, 0)),
            pl.BlockSpec((8, CONV_WIDTH), lambda i: (jnp.maximum(i * r8 - 1, 0), 0)),
            pl.BlockSpec((8, CONV_WIDTH), lambda i: (jnp.minimum((i + 1) * r8, nsteps * r8 - 1), 0)),
            pl.BlockSpec((tm, CONV_WIDTH), lambda i: (i, 0)),
            full(conv_w.shape), full(woa.shape), full(wob.shape), full(wo.shape), full(g2.shape),
            full(rw_split.shape), full(rb.shape),
        ],
        out_specs=[
            pl.BlockSpec((tm, d), lambda i: (i, 0)),
            pl.BlockSpec((tm * SUBLANE, LANE), lambda i: (i, 0)),
            pl.BlockSpec((2 * TOP_K, tm), lambda i: (0, i)),
            pl.BlockSpec((tm, LANE), lambda i: (i, 0)),
            pl.BlockSpec((8, LANE), lambda i: (0, 0)),
        ],
        out_shape=[
            jax.ShapeDtypeStruct((n, d), F32),
            jax.ShapeDtypeStruct((n * SUBLANE, LANE), F32),
            jax.ShapeDtypeStruct((2 * TOP_K, n), jnp.int32),
            jax.ShapeDtypeStruct((n, LANE), F32),
            jax.ShapeDtypeStruct((8, LANE), F32),
        ],
        scratch_shapes=[pltpu.VMEM((tm // MIX_SUBTILES, tm // MIX_SUBTILES), BF16), pltpu.VMEM((1, LANE), F32)],
        compiler_params=pltpu.CompilerParams(dimension_semantics=("arbitrary",), vmem_limit_bytes=VMEM_LIMIT),
        name="mix_route",
    )(x2, g1, w_g, ot, cu, cu, cu, bg, conv_w, woa, wob, wo, g2, rw_split, rb)


def _row_copy_wait(src_like, dst_like, sem, times):
    for _ in range(times):
        pltpu.make_async_copy(src_like, dst_like, sem).wait()


def _token(ref, idx):
    return ref.at[pl.ds(pl.multiple_of(idx * SUBLANE, SUBLANE), SUBLANE)]


def _dispatch_kernel(dest_ref, pe_ref, plen_ref, nused_ref, h2_ref, xs_ref, zero_ref, sem, zsem):
    tm = h2_ref.shape[0] // SUBLANE
    block_rows = EXPERT_BLOCK * SUBLANE
    n_blocks = xs_ref.shape[0] // block_rows
    step = pl.program_id(0)

    def zero_copies(act):
        def tail(b, c):
            start = pl.multiple_of(b * block_rows, block_rows)
            act(pltpu.make_async_copy(zero_ref, xs_ref.at[pl.ds(start, block_rows)], zsem))
            return c

        def expert(e, c):
            length = plen_ref[e]
            first = pe_ref[e] - length
            for bit in reversed(range(EXPERT_BLOCK.bit_length() - 1)):
                size = 1 << bit
                done = lax.bitwise_and(length, -2 * size)

                @pl.when(lax.bitwise_and(length, size) != 0)
                def _(size=size, done=done):
                    start = pl.multiple_of((first + done) * SUBLANE, SUBLANE)
                    act(pltpu.make_async_copy(zero_ref.at[0:size * SUBLANE],
                                              xs_ref.at[pl.ds(start, size * SUBLANE)], zsem))
            return c

        lax.fori_loop(0, N_EXPERTS, expert, 0)
        lax.fori_loop(nused_ref[0], n_blocks, tail, 0)

    @pl.when(step == 0)
    def _():
        zero_ref[...] = jnp.zeros_like(zero_ref)
        zero_copies(lambda cp: cp.start())

    def issue(g, c):
        for u in range(ISSUE_GROUP):
            t = g * (ISSUE_GROUP // TOP_K) + u // TOP_K
            dest = dest_ref[(u % TOP_K) * tm + t]
            pltpu.make_async_copy(_token(h2_ref, t), _token(xs_ref, dest), sem).start(priority=u % 2)
        return c

    lax.fori_loop(0, tm * TOP_K // ISSUE_GROUP, issue, 0)
    _row_copy_wait(h2_ref, xs_ref.at[pl.ds(0, tm * SUBLANE)], sem, TOP_K)

    @pl.when(step == pl.num_programs(0) - 1)
    def _():
        zero_copies(lambda cp: cp.wait())


def _dispatch(dest_flat, pad_end, pad_len, n_used, h2t, rows):
    tm = TM_ROUTE
    n = h2t.shape[0] // SUBLANE
    return pl.pallas_call(
        _dispatch_kernel,
        grid=(n // tm,),
        in_specs=[
            pl.BlockSpec((tm * TOP_K,), lambda i: (i,), memory_space=pltpu.SMEM),
            pl.BlockSpec(memory_space=pltpu.SMEM),
            pl.BlockSpec(memory_space=pltpu.SMEM),
            pl.BlockSpec(memory_space=pltpu.SMEM),
            pl.BlockSpec((tm * SUBLANE, LANE), lambda i: (i, 0)),
        ],
        out_specs=pl.BlockSpec(memory_space=pl.ANY),
        out_shape=jax.ShapeDtypeStruct((rows * SUBLANE, LANE), F32),
        scratch_shapes=[pltpu.VMEM((EXPERT_BLOCK * SUBLANE, LANE), F32), pltpu.SemaphoreType.DMA(()),
                        pltpu.SemaphoreType.DMA(())],
        compiler_params=pltpu.CompilerParams(dimension_semantics=("arbitrary",), vmem_limit_bytes=VMEM_LIMIT),
        name="dispatch",
    )(dest_flat, pad_end, pad_len, n_used, h2t)


def _expert_kernel(bexp_ref, nused_ref, nexp_ref, slot_ref, nvalid_ref, xs_ref, w1_hbm, b1_ref, w2_hbm, b2_ref, ys_ref,
                   w1_ref, w2_ref, sem):
    i = pl.program_id(0)
    active = i < nused_ref[0]
    expert = bexp_ref[i]
    slot = slot_ref[i]
    prev = bexp_ref[jnp.maximum(i - 1, 0)]
    fresh = jnp.logical_or(i == 0, expert != prev)

    def weight_copies(e, s):
        return (pltpu.make_async_copy(w1_hbm.at[e], w1_ref.at[s], sem.at[0, s]),
                pltpu.make_async_copy(w2_hbm.at[e], w2_ref.at[s], sem.at[1, s]))

    @pl.when(jnp.logical_and(active, i == 0))
    def _():
        for cp in weight_copies(expert, slot):
            cp.start()

    @pl.when(jnp.logical_and(active, fresh))
    def _():
        for cp in weight_copies(expert, slot):
            cp.wait()

        @pl.when(nexp_ref[i] != expert)
        def _():
            for cp in weight_copies(nexp_ref[i], 1 - slot):
                cp.start()

    def ffn(rows):
        dff = w2_ref.shape[1]
        xb = _load_token_tiles(xs_ref, (), rows).astype(BF16)
        hm = jnp.dot(xb, w1_ref[slot].astype(BF16), preferred_element_type=F32) + b1_ref[0]
        gate = jnp.minimum(hm[:, 0:dff], SWIGLU_LIMIT)
        up = jnp.clip(hm[:, dff:2 * dff], -SWIGLU_LIMIT, SWIGLU_LIMIT)
        glu = gate * (1.0 / (1.0 + jnp.exp(-SWIGLU_ALPHA * gate)))
        act = ((up + 1.0) * glu).astype(BF16)
        _store_token_tiles(ys_ref.at[0:rows * SUBLANE],
                           jnp.dot(act, w2_ref[slot].astype(BF16), preferred_element_type=F32) + b2_ref[0])

    groups = lax.div(nvalid_ref[i] + (EXPERT_ROW_GROUP - 1), EXPERT_ROW_GROUP)
    for g in range(1, EXPERT_BLOCK // EXPERT_ROW_GROUP + 1):
        rows = g * EXPERT_ROW_GROUP

        @pl.when(jnp.logical_and(active, groups == g))
        def _(rows=rows):
            ffn(rows)
            if rows < EXPERT_BLOCK:
                ys_ref[rows * SUBLANE:, :] = jnp.zeros(((EXPERT_BLOCK - rows) * SUBLANE, LANE), F32)

    @pl.when(jnp.logical_not(active))
    def _():
        ys_ref[...] = jnp.zeros_like(ys_ref)


def _experts(block_exp, n_used, next_exp, weight_slot, block_valid, xs, w1, b1, w2, b2):
    d = w1.shape[1]
    assert d == SUBLANE * LANE
    block_rows = EXPERT_BLOCK * SUBLANE
    n_blocks = xs.shape[0] // block_rows
    dff2 = w1.shape[2]
    dff = w2.shape[1]
    grid_spec = pltpu.PrefetchScalarGridSpec(
        num_scalar_prefetch=5,
        grid=(n_blocks,),
        in_specs=[
            pl.BlockSpec((block_rows, LANE), lambda i, be, nu, ne, ws, nv: (jnp.minimum(i, nu[0] - 1), 0)),
            pl.BlockSpec(memory_space=pl.ANY),
            pl.BlockSpec((1, 1, dff2), lambda i, be, nu, ne, ws, nv: (be[i], 0, 0)),
            pl.BlockSpec(memory_space=pl.ANY),
            pl.BlockSpec((1, 1, d), lambda i, be, nu, ne, ws, nv: (be[i], 0, 0)),
        ],
        out_specs=pl.BlockSpec((block_rows, LANE), lambda i, be, nu, ne, ws, nv: (i, 0)),
        scratch_shapes=[pltpu.VMEM((2, d, dff2), F32), pltpu.VMEM((2, dff, d), F32),
                        pltpu.SemaphoreType.DMA((2, 2))],
    )
    return pl.pallas_call(
        _expert_kernel,
        grid_spec=grid_spec,
        out_shape=jax.ShapeDtypeStruct(xs.shape, F32),
        compiler_params=pltpu.CompilerParams(dimension_semantics=("arbitrary",), vmem_limit_bytes=VMEM_LIMIT),
        name="experts",
    )(block_exp, n_used, next_exp, weight_slot, block_valid, xs, w1, b1, w2, b2)


def _combine_kernel(dest_ref, dest_next_ref, x1_ref, mw_ref, ys_ref, out_ref, buf_ref, sem):
    i = pl.program_id(0)
    tm = x1_ref.shape[0]
    slot = lax.rem(i, 2)

    def gather(idx_ref, s):
        def issue(g, c):
            for u in range(ISSUE_GROUP):
                t = g * (ISSUE_GROUP // TOP_K) + u // TOP_K
                dest = idx_ref[(u % TOP_K) * tm + t]
                pltpu.make_async_copy(_token(ys_ref, dest), _token(buf_ref.at[s, u % TOP_K], t),
                                      sem.at[s]).start(priority=u % 2)
            return c

        lax.fori_loop(0, tm * TOP_K // ISSUE_GROUP, issue, 0)

    @pl.when(i == 0)
    def _():
        gather(dest_ref, 0)

    @pl.when(i + 1 < pl.num_programs(0))
    def _():
        gather(dest_next_ref, 1 - slot)

    _row_copy_wait(ys_ref.at[pl.ds(0, tm * SUBLANE)], buf_ref.at[slot, 0], sem.at[slot], TOP_K)
    acc = x1_ref[...]
    mw = mw_ref[...]
    for k in range(TOP_K):
        acc = acc + mw[:, k:k + 1] * _load_token_tiles(buf_ref, (slot, k), tm)
    out_ref[...] = acc


def _combine(dest_flat, x1, mw, ys):
    n, d = x1.shape
    tm = TM_ROUTE
    nsteps = n // tm
    return pl.pallas_call(
        _combine_kernel,
        grid=(nsteps,),
        in_specs=[
            pl.BlockSpec((tm * TOP_K,), lambda i: (i,), memory_space=pltpu.SMEM),
            pl.BlockSpec((tm * TOP_K,), lambda i: (jnp.minimum(i + 1, nsteps - 1),), memory_space=pltpu.SMEM),
            pl.BlockSpec((tm, d), lambda i: (i, 0)),
            pl.BlockSpec((tm, LANE), lambda i: (i, 0)),
            pl.BlockSpec(memory_space=pl.ANY),
        ],
        out_specs=pl.BlockSpec((tm, d), lambda i: (i, 0)),
        out_shape=jax.ShapeDtypeStruct((n, d), F32),
        scratch_shapes=[pltpu.VMEM((2, TOP_K, tm * SUBLANE, LANE), F32), pltpu.SemaphoreType.DMA((2,))],
        compiler_params=pltpu.CompilerParams(dimension_semantics=("arbitrary",), vmem_limit_bytes=VMEM_LIMIT),
        name="combine",
    )(dest_flat, dest_flat, x1, mw, ys)


def _pad_cols(w, width):
    return jnp.pad(w, ((0, 0), (0, width - w.shape[1])))


def _head_slots(w, per_head):
    rows = w.shape[0]
    w3 = w.reshape(rows, MLA_HEADS, per_head)
    return jnp.pad(w3, ((0, 0), (0, 0), (0, HEAD_SLOT - per_head))).reshape(rows, MLA_HEADS * HEAD_SLOT)


def _rope_tables(positions):
    inv_freq = ROPE_THETA ** (-jnp.arange(0, QK_ROPE, 2, dtype=F32) / QK_ROPE)
    ang = positions.astype(F32).reshape(-1, 1) * inv_freq
    return jnp.cos(ang).T, jnp.sin(ang).T


def _layer(x2, positions, norm1_g, w_in, q_a_norm_g, kv_a_norm_g, w_uq, w_ukv, q_norm_g, k_norm_g,
           conv_w, w_o_mla, w_o_conv, w_o, norm2_g, router_w, router_b,
           expert_w1, expert_b1, expert_w2, expert_b2, batch, seq):
    n, d = x2.shape
    o_kr = Q_LORA + KV_LORA
    o_u = o_kr + QK_ROPE
    o_g = o_u + 3 * CONV_WIDTH
    kr_cols = jnp.pad(w_in[:, o_kr:o_u], ((0, 0), (QK_NOPE, LANE - QK_HEAD)))
    w_a = jnp.concatenate([w_in[:, :o_kr], kr_cols, w_in[:, o_u:o_g]], axis=1).astype(BF16)
    w_g = w_in[:, o_g:].astype(BF16)
    row = lambda v: v.reshape(1, -1)
    cos_c, sin_c = _rope_tables(positions)
    q_scale = (QK_HEAD ** -0.5) * math.log2(math.e)
    gain_t = lambda g: jnp.broadcast_to(g.reshape(LANE, 1), (LANE, TM_PROJ // PROJ_SUBTILES))
    kg = _pad_cols(row(k_norm_g), LANE)
    score_bound = 1.02 * q_scale * QK_HEAD * jnp.max(jnp.abs(q_norm_g)) * jnp.max(jnp.abs(k_norm_g))
    bounded = 2.0 * score_bound <= SAFE_SCORE_RANGE
    offset = jnp.where(bounded, score_bound, 0.0)
    feature = jnp.arange(LANE) == OFFSET_FEATURE
    qoff = gain_t(jnp.where(feature, -offset, 0.0).astype(F32))
    koff = jnp.where(feature, 1.0, 0.0).astype(F32).reshape(1, LANE)

    qt, k, vt, cu, bg = _in_projection(
        x2, row(norm1_g), w_a, row(q_a_norm_g), row(kv_a_norm_g),
        _head_slots(w_uq, QK_HEAD).astype(BF16), w_ukv.astype(BF16),
        gain_t(_pad_cols(row(q_norm_g) * q_scale, LANE)), kg, gain_t(kg), qoff, koff,
        cos_c, sin_c, batch, seq)
    ot = _attention(qt, k, vt, bounded)

    rw = _pad_cols(router_w, LANE)
    rw_hi = rw.astype(BF16)
    rw_lo = (rw - rw_hi.astype(F32)).astype(BF16)
    rb = jnp.concatenate([row(router_b), jnp.full((1, LANE - N_EXPERTS), NEG_BIG, F32)], axis=1)
    x1, h2, mi, mw, cnt = _mix(
        x2, row(norm1_g), w_g, ot, cu, bg, conv_w, w_o_mla.astype(BF16), w_o_conv.astype(BF16),
        w_o.astype(BF16), row(norm2_g), jnp.concatenate([rw_hi, rw_lo], axis=1), rb, batch, seq)

    counts = cnt[0, :N_EXPERTS].astype(jnp.int32)
    padded = (counts + EXPERT_BLOCK - 1) // EXPERT_BLOCK * EXPERT_BLOCK
    experts = jnp.arange(N_EXPERTS, dtype=jnp.int32)
    pad_end = jnp.sum(jnp.where(experts[None, :] <= experts[:, None], padded[None, :], 0), axis=1)
    pad_start = (pad_end - padded).astype(jnp.int32)
    nk = n * TOP_K
    n_blocks = (nk + N_EXPERTS * (EXPERT_BLOCK - 1) + EXPERT_BLOCK - 1) // EXPERT_BLOCK
    rows = n_blocks * EXPERT_BLOCK
    block_first_row = jnp.arange(n_blocks, dtype=jnp.int32) * EXPERT_BLOCK
    block_exp = jnp.minimum(jnp.sum(pad_end[None, :] <= block_first_row[:, None], axis=1),
                            N_EXPERTS - 1).astype(jnp.int32)
    n_used = (pad_end[-1:] // EXPERT_BLOCK).astype(jnp.int32)
    group_end = jnp.sum(jnp.where(experts[None, :] == block_exp[:, None], pad_end[None, :], 0), axis=1)
    following = jnp.minimum(jnp.sum(pad_end[None, :] <= group_end[:, None], axis=1), N_EXPERTS - 1)
    next_exp = jnp.where(group_end < pad_end[-1], following, block_exp).astype(jnp.int32)
    ordinal = jnp.sum(jnp.where(experts[None, :] < experts[:, None], (padded > 0)[None, :], False), axis=1)
    weight_slot = jnp.sum(jnp.where(experts[None, :] == block_exp[:, None], (ordinal % 2)[None, :], 0),
                          axis=1).astype(jnp.int32)
    e_sel = mi[None, 0:TOP_K] == jnp.arange(N_EXPERTS, dtype=jnp.int32)[:, None, None]
    dest = jnp.sum(jnp.where(e_sel, pad_start[:, None, None], 0), axis=0) + mi[TOP_K:2 * TOP_K]
    dest_flat = dest.reshape(TOP_K, n // TM_ROUTE, TM_ROUTE).transpose(1, 0, 2).reshape(nk)

    xs = _dispatch(dest_flat, pad_end.astype(jnp.int32), (padded - counts).astype(jnp.int32), n_used, h2, rows)
    rows_end = jnp.sum(jnp.where(experts[None, :] == block_exp[:, None], (pad_start + counts)[None, :], 0), axis=1)
    block_valid = jnp.clip(rows_end - block_first_row, 0, EXPERT_BLOCK).astype(jnp.int32)
    ys = _experts(block_exp, n_used, next_exp, weight_slot, block_valid, xs, expert_w1, expert_b1.reshape(N_EXPERTS, 1, -1),
                  expert_w2, expert_b2.reshape(N_EXPERTS, 1, -1))
    return _combine(dest_flat, x1, mw, ys)


def kernel(x, positions, norm1_g, w_in, q_a_norm_g, kv_a_norm_g, w_uq, w_ukv, q_norm_g, k_norm_g, conv_w,
           w_o_mla, w_o_conv, w_o, norm2_g, router_w, router_b, expert_w1, expert_b1, expert_w2, expert_b2):
    batch, seq, d = x.shape
    depth = norm1_g.shape[0]
    x2 = x.reshape(batch * seq, d)
    for l in range(depth):
        x2 = _layer(x2, positions, norm1_g[l], w_in[l], q_a_norm_g[l], kv_a_norm_g[l], w_uq[l], w_ukv[l],
                    q_norm_g[l], k_norm_g[l], conv_w[l], w_o_mla[l], w_o_conv[l], w_o[l], norm2_g[l],
                    router_w[l], router_b[l], expert_w1[l], expert_b1[l], expert_w2[l], expert_b2[l], batch, seq)
    return x2.reshape(batch, seq, d)
```

```python
import functools
import math

import jax
import jax.numpy as jnp
from jax import lax
from jax.experimental import pallas as pl
from jax.experimental.pallas import tpu as pltpu

F32 = jnp.float32
BF16 = jnp.bfloat16

MLA_HEADS = 8
QK_NOPE = 64
QK_ROPE = 32
QK_HEAD = QK_NOPE + QK_ROPE
V_HEAD = 64
Q_LORA = 256
KV_LORA = 128
ROPE_THETA = 10000.0
CONV_WIDTH = 512
N_EXPERTS = 32
TOP_K = 4
SWIGLU_LIMIT = 7.0
SWIGLU_ALPHA = 1.702
EPS = 1e-6

LANE = 128
SUBLANE = 8
HEAD_SLOT = LANE
HALF_ROPE = QK_ROPE // 2
V_ROWS = V_HEAD + 16
OFFSET_FEATURE = QK_HEAD
SAFE_SCORE_RANGE = 100.0
VMEM_LIMIT = 56 * 1024 * 1024

TM_PROJ = 512
PROJ_SUBTILES = 2
TQ = 512
TKV = 512
ATTN_UNROLL = 4
TKV_BOUNDED = 256
TM_MIX = 512
MIX_SUBTILES = 1
TM_DISPATCH = 1024
TM_COMBINE = 256
EXPERT_BLOCK = 512
EXPERT_ROW_GROUP = 128
ISSUE_GROUP = 16
NEG_BIG = -1e30


def _load_token_tiles(ref, lead, rows):
    return jnp.concatenate([ref[lead + (pl.ds(c, rows, stride=SUBLANE), slice(None))] for c in range(SUBLANE)],
                           axis=1)


def _store_token_tiles(ref, value):
    rows = value.shape[0]
    for c in range(SUBLANE):
        ref[pl.ds(c, rows, stride=SUBLANE), :] = value[:, c * LANE:(c + 1) * LANE]


def _rms(x, g):
    return x * lax.rsqrt(jnp.mean(x * x, axis=-1, keepdims=True) + EPS) * g


def _inproj_kernel(x_ref, g1_ref, w_ref, gq_ref, gkv_ref, wuq_ref, wukv_ref, qgt_ref, kg_ref, kgt_ref,
                   qoff_ref, koff_ref, cost_ref, sint_ref, qt_ref, k_ref, vt_ref, cu_ref, bg_ref):
    tm = x_ref.shape[0] // PROJ_SUBTILES
    for part in range(PROJ_SUBTILES):
        _inproj_rows(slice(part * tm, (part + 1) * tm), x_ref, g1_ref, w_ref, gq_ref, gkv_ref, wuq_ref, wukv_ref,
                     qgt_ref, kg_ref, kgt_ref, qoff_ref, koff_ref, cost_ref, sint_ref,
                     qt_ref, k_ref, vt_ref, cu_ref, bg_ref)


def _inproj_rows(rows, x_ref, g1_ref, w_ref, gq_ref, gkv_ref, wuq_ref, wukv_ref, qgt_ref, kg_ref, kgt_ref,
                 qoff_ref, koff_ref, cost_ref, sint_ref, qt_ref, k_ref, vt_ref, cu_ref, bg_ref):
    x = x_ref[rows, :]
    h = _rms(x, g1_ref[...]).astype(BF16)
    proj = jnp.dot(h, w_ref[...], preferred_element_type=F32)
    c_q = proj[:, 0:Q_LORA]
    c_kv = proj[:, Q_LORA:Q_LORA + KV_LORA]
    kr = proj[:, Q_LORA + KV_LORA:Q_LORA + KV_LORA + LANE]
    o = Q_LORA + KV_LORA + LANE
    u = proj[:, o:o + CONV_WIDTH]
    c_gate = proj[:, o + CONV_WIDTH:o + 2 * CONV_WIDTH]
    b_gate = proj[:, o + 2 * CONV_WIDTH:o + 3 * CONV_WIDTH]
    cu_ref[rows, :] = c_gate * u
    bg_ref[rows, :] = b_gate

    kg = kg_ref[...]
    tm = x.shape[0]
    lane = lax.broadcasted_iota(jnp.int32, (tm, LANE), 1)

    q = jnp.dot(_rms(c_q, gq_ref[...]).astype(BF16), wuq_ref[...], preferred_element_type=F32)
    kv = jnp.dot(_rms(c_kv, gkv_ref[...]).astype(BF16), wukv_ref[...], preferred_element_type=F32)

    cos_c = cost_ref[:, rows]
    sin_c = sint_ref[:, rows]

    def rope_t(t):
        t1 = t[QK_NOPE:QK_NOPE + HALF_ROPE]
        t2 = t[QK_NOPE + HALF_ROPE:QK_HEAD]
        return jnp.concatenate([t[0:QK_NOPE], t1 * cos_c - t2 * sin_c, t1 * sin_c + t2 * cos_c, t[QK_HEAD:]], axis=0)

    ss_r = jnp.sum(kr * kr, axis=-1, keepdims=True)
    kr_roped = rope_t(kr.T * kgt_ref[...]).T
    qgt = qgt_ref[...]
    qoff = qoff_ref[...]
    koff = koff_ref[...]
    ones = jnp.ones((V_ROWS - V_HEAD, tm), BF16)
    for hd in range(MLA_HEADS):
        qht = q[:, hd * HEAD_SLOT:(hd + 1) * HEAD_SLOT].T
        r = lax.rsqrt(jnp.sum(qht * qht, axis=0, keepdims=True) * (1.0 / QK_HEAD) + EPS)
        qt_ref[0, hd, :, rows] = (rope_t(qht * r * qgt) + qoff).astype(BF16)

        kvh = kv[:, hd * HEAD_SLOT:(hd + 1) * HEAD_SLOT]
        knope = jnp.where(lane < QK_NOPE, kvh, 0.0)
        rk = lax.rsqrt((jnp.sum(knope * knope, axis=-1, keepdims=True) + ss_r) * (1.0 / QK_HEAD) + EPS)
        k_ref[0, hd, rows, :] = ((knope * kg + kr_roped) * rk + koff).astype(BF16)
        kvt = kvh.T
        vt_ref[0, hd, 0:V_HEAD, rows] = kvt[QK_NOPE:QK_NOPE + V_HEAD].astype(BF16)
        vt_ref[0, hd, V_HEAD:V_ROWS, rows] = ones


def _in_projection(x2, g1, w_a, gq, gkv, wuq, wukv, qgt, kg, kgt, qoff, koff, cos_c, sin_c, batch, seq):
    n, d = x2.shape
    tm = TM_PROJ
    spb = seq // tm
    full = lambda shp: pl.BlockSpec(shp, lambda i: (0,) * len(shp))
    return pl.pallas_call(
        _inproj_kernel,
        grid=(n // tm,),
        in_specs=[
            pl.BlockSpec((tm, d), lambda i: (i, 0)),
            full(g1.shape), full(w_a.shape), full(gq.shape), full(gkv.shape), full(wuq.shape), full(wukv.shape),
            full(qgt.shape), full(kg.shape), full(kgt.shape), full(qoff.shape), full(koff.shape),
            pl.BlockSpec((HALF_ROPE, tm), lambda i: (0, i)),
            pl.BlockSpec((HALF_ROPE, tm), lambda i: (0, i)),
        ],
        out_specs=[
            pl.BlockSpec((1, MLA_HEADS, HEAD_SLOT, tm), lambda i: (i // spb, 0, 0, i % spb)),
            pl.BlockSpec((1, MLA_HEADS, tm, HEAD_SLOT), lambda i: (i // spb, 0, i % spb, 0)),
            pl.BlockSpec((1, MLA_HEADS, V_ROWS, tm), lambda i: (i // spb, 0, 0, i % spb)),
            pl.BlockSpec((tm, CONV_WIDTH), lambda i: (i, 0)),
            pl.BlockSpec((tm, CONV_WIDTH), lambda i: (i, 0)),
        ],
        out_shape=[
            jax.ShapeDtypeStruct((batch, MLA_HEADS, HEAD_SLOT, seq), BF16),
            jax.ShapeDtypeStruct((batch, MLA_HEADS, seq, HEAD_SLOT), BF16),
            jax.ShapeDtypeStruct((batch, MLA_HEADS, V_ROWS, seq), BF16),
            jax.ShapeDtypeStruct((n, CONV_WIDTH), F32),
            jax.ShapeDtypeStruct((n, CONV_WIDTH), F32),
        ],
        compiler_params=pltpu.CompilerParams(dimension_semantics=("parallel",), vmem_limit_bytes=VMEM_LIMIT),
        name="in_projection",
    )(x2, g1, w_a, gq, gkv, wuq, wukv, qgt, kg, kgt, qoff, koff, cos_c, sin_c)


def _attn_kernel(qt_ref, k_ref, vt_ref, o_ref, sa_ref, sb_ref, m_ref, acc_ref, *, tq, tk):
    seq = k_ref.shape[2]
    nk = seq // tk
    bufs = (sa_ref, sb_ref)

    def query_tile(qi, carry):
        q0 = pl.multiple_of(qi * tq, tq)
        qt = qt_ref[0, 0, :, pl.ds(q0, tq)]

        def scores(c, s_ref):
            k0 = pl.multiple_of(c * tk, tk)
            s_ref[...] = jnp.dot(k_ref[0, 0, pl.ds(k0, tk), :], qt, preferred_element_type=F32)

        def accumulate(c, s_ref):
            k0 = pl.multiple_of(c * tk, tk)
            s = s_ref[...]
            m = m_ref[...]
            m_new = jnp.maximum(m, jnp.max(s, axis=0, keepdims=True))
            m_ref[...] = m_new
            p = jnp.exp2(s - m_new).astype(BF16)
            vs = vt_ref[0, 0, :, pl.ds(k0, tk)]
            acc_ref[...] = jnp.exp2(m - m_new) * acc_ref[...] + jnp.dot(vs, p, preferred_element_type=F32)

        m_ref[...] = jnp.full(m_ref.shape, NEG_BIG, F32)
        acc_ref[...] = jnp.zeros(acc_ref.shape, F32)
        scores(0, sa_ref)

        def group(j, c):
            base = ATTN_UNROLL * j
            for u in range(ATTN_UNROLL):
                scores(base + u + 1, bufs[(u + 1) % 2])
                accumulate(base + u, bufs[u % 2])
            return c

        lax.fori_loop(0, nk // ATTN_UNROLL - 1, group, 0)
        base = nk - ATTN_UNROLL
        for u in range(ATTN_UNROLL):
            if u + 1 < ATTN_UNROLL:
                scores(base + u + 1, bufs[(u + 1) % 2])
            accumulate(base + u, bufs[u % 2])
        acc = acc_ref[...]
        o_ref[0, :, pl.ds(q0, tq)] = (acc[0:V_HEAD] * (1.0 / acc[V_HEAD:V_HEAD + 1])).astype(BF16)
        return carry

    lax.fori_loop(0, seq // tq, query_tile, 0)


def _attn_bounded_kernel(qt_ref, k_ref, vt_ref, o_ref, sa_ref, sb_ref, *, tq, tk):
    seq = k_ref.shape[2]
    nk = seq // tk
    n_tiles = seq // tq
    ahead_refs = (sa_ref, sb_ref)
    ahead = len(ahead_refs)

    def load_qt(qi):
        return qt_ref[0, 0, :, pl.ds(pl.multiple_of(qi * tq, tq), tq)]

    def scores(c, qt):
        return jnp.dot(k_ref[0, 0, c * tk:(c + 1) * tk, :], qt, preferred_element_type=F32)

    qt_first = load_qt(0)
    for a in range(ahead):
        ahead_refs[a][...] = scores(a, qt_first)

    def query_tile(qi, carry):
        qt = load_qt(qi)
        qt_next = load_qt(jnp.minimum(qi + 1, n_tiles - 1))
        pending = [ref[...] for ref in ahead_refs]
        total = None
        for c in range(nk):
            if c + ahead < nk:
                pending.append(scores(c + ahead, qt))
            else:
                ahead_refs[c + ahead - nk][...] = scores(c + ahead - nk, qt_next)
            p = jnp.exp2(pending.pop(0)).astype(BF16)
            part = jnp.dot(vt_ref[0, 0, :, c * tk:(c + 1) * tk], p, preferred_element_type=F32)
            total = part if total is None else total + part
        o_ref[0, :, pl.ds(pl.multiple_of(qi * tq, tq), tq)] = (
            total[0:V_HEAD] * (1.0 / total[V_HEAD:V_HEAD + 1])).astype(BF16)
        return carry

    lax.fori_loop(0, n_tiles, query_tile, 0)


def _attention(qt, k, vt, bounded):
    batch, heads, _, seq = qt.shape
    assert ATTN_UNROLL % 2 == 0 and (seq // TKV) % ATTN_UNROLL == 0 and seq % TQ == 0
    in_specs = [
        pl.BlockSpec((1, 1, HEAD_SLOT, seq), lambda b, h, flag: (b, h, 0, 0)),
        pl.BlockSpec((1, 1, seq, HEAD_SLOT), lambda b, h, flag: (b, h, 0, 0)),
        pl.BlockSpec((1, 1, V_ROWS, seq), lambda b, h, flag: (b, h, 0, 0)),
    ]
    out_spec = pl.BlockSpec((1, V_HEAD, seq), lambda b, h, flag: (b, h, 0))
    out_shape = jax.ShapeDtypeStruct((batch, heads * V_HEAD, seq), BF16)
    grid_spec = pltpu.PrefetchScalarGridSpec(
        num_scalar_prefetch=1, grid=(batch, heads), in_specs=in_specs, out_specs=out_spec,
        scratch_shapes=[pltpu.VMEM((max(TKV, TKV_BOUNDED), TQ), F32), pltpu.VMEM((max(TKV, TKV_BOUNDED), TQ), F32),
                        pltpu.VMEM((1, TQ), F32), pltpu.VMEM((V_ROWS, TQ), F32)])
    return pl.pallas_call(
        _attn_select_kernel, grid_spec=grid_spec, out_shape=out_shape,
        compiler_params=pltpu.CompilerParams(dimension_semantics=("parallel", "parallel"),
                                             vmem_limit_bytes=VMEM_LIMIT),
        name="attention",
    )(bounded.astype(jnp.int32).reshape(1), qt, k, vt)


def _attn_select_kernel(bounded_ref, qt_ref, k_ref, vt_ref, o_ref, sa_ref, sb_ref, m_ref, acc_ref):
    @pl.when(bounded_ref[0] != 0)
    def _():
        _attn_bounded_kernel(qt_ref, k_ref, vt_ref, o_ref, sa_ref.at[0:TKV_BOUNDED], sb_ref.at[0:TKV_BOUNDED],
                             tq=TQ, tk=TKV_BOUNDED)

    @pl.when(bounded_ref[0] == 0)
    def _():
        _attn_kernel(qt_ref, k_ref, vt_ref, o_ref, sa_ref.at[0:TKV], sb_ref.at[0:TKV], m_ref, acc_ref,
                     tq=TQ, tk=TKV)


def _mix_kernel(x_ref, g1_ref, wg_ref, ot_ref, cu_ref, cup_ref, cun_ref, bg_ref, cw_ref,
                woa_ref, wob_ref, wo_ref, g2_ref, rw_ref, rb_ref,
                x1_ref, h2_ref, mi_ref, mw_ref, cnt_ref, tri_ref, carry_ref, *, steps_per_seq):
    i = pl.program_id(0)
    tm = x_ref.shape[0]
    ts = tm // MIX_SUBTILES

    @pl.when(i == 0)
    def _():
        r = lax.broadcasted_iota(jnp.int32, (ts, ts), 0)
        c = lax.broadcasted_iota(jnp.int32, (ts, ts), 1)
        tri_ref[...] = jnp.where(c < r, 1.0, 0.0).astype(BF16)
        carry_ref[...] = jnp.zeros_like(carry_ref)

    cu = cu_ref[...]
    row = lax.broadcasted_iota(jnp.int32, cu.shape, 0)
    s_in_seq = i % steps_per_seq
    prev_row = jnp.where(s_in_seq == 0, 0.0, cup_ref[7:8, :])
    next_row = jnp.where(s_in_seq == steps_per_seq - 1, 0.0, cun_ref[0:1, :])
    below = jnp.where(row == 0, prev_row, pltpu.roll(cu, 1, 0))
    above = jnp.where(row == tm - 1, next_row, pltpu.roll(cu, tm - 1, 0))
    cw = cw_ref[...]
    gated_conv = (bg_ref[...] * (cw[0:1, :] * below + cw[1:2, :] * cu + cw[2:3, :] * above)).astype(BF16)

    carry = carry_ref[...]
    for part in range(MIX_SUBTILES):
        carry = _mix_rows(part * ts, ts, gated_conv[part * ts:(part + 1) * ts], carry, x_ref, g1_ref, wg_ref,
                          ot_ref, woa_ref, wob_ref, wo_ref, g2_ref, rw_ref, rb_ref, tri_ref,
                          x1_ref, h2_ref, mi_ref, mw_ref)
    carry_ref[...] = carry
    cnt_ref[...] = jnp.broadcast_to(carry, cnt_ref.shape)


def _mix_rows(r0, ts, gated_conv, carry, x_ref, g1_ref, wg_ref, ot_ref, woa_ref, wob_ref, wo_ref, g2_ref,
              rw_ref, rb_ref, tri_ref, x1_ref, h2_ref, mi_ref, mw_ref):
    rows = slice(r0, r0 + ts)
    y_a = lax.dot_general(ot_ref[0, :, rows], woa_ref[...], (((0,), (0,)), ((), ())), preferred_element_type=F32)
    y_b = jnp.dot(gated_conv, wob_ref[...], preferred_element_type=F32)

    x = x_ref[rows, :]
    h = _rms(x, g1_ref[...]).astype(BF16)
    gates = jnp.dot(h, wg_ref[...], preferred_element_type=F32)
    d = x.shape[1]
    sig_a = 1.0 / (1.0 + jnp.exp(-gates[:, 0:d]))
    sig_b = 1.0 / (1.0 + jnp.exp(-gates[:, d:2 * d]))

    merged = (sig_a * y_a + sig_b * y_b).astype(BF16)
    x1 = x + jnp.dot(merged, wo_ref[...], preferred_element_type=F32)
    x1_ref[rows, :] = x1
    h2 = _rms(x1, g2_ref[...])
    _store_token_tiles(h2_ref.at[r0 * SUBLANE:(r0 + ts) * SUBLANE], h2)

    h2_hi = h2.astype(BF16)
    h2_lo = (h2 - h2_hi.astype(F32)).astype(BF16)
    rw = rw_ref[...]
    hi_terms = jnp.dot(h2_hi, rw, preferred_element_type=F32)
    logits = (hi_terms[:, 0:LANE] + hi_terms[:, LANE:2 * LANE]
              + jnp.dot(h2_lo, rw[:, 0:LANE], preferred_element_type=F32)
              + rb_ref[...])

    lane = lax.broadcasted_iota(jnp.int32, logits.shape, 1)
    lane_f = lane.astype(F32)
    work = logits
    vals, idxs, hits = [], [], []
    for _ in range(TOP_K):
        mx = jnp.max(work, axis=-1, keepdims=True)
        idx = jnp.min(jnp.where(work == mx, lane_f, float(LANE)), axis=-1, keepdims=True)
        hit = lane_f == idx
        work = jnp.where(hit, -jnp.inf, work)
        vals.append(mx)
        idxs.append(idx)
        hits.append(hit)
    exps = [jnp.exp(v - vals[0]) for v in vals]
    denom = exps[0] + exps[1] + exps[2] + exps[3]
    inv = 1.0 / denom

    sel = jnp.zeros(logits.shape, F32)
    for hit in hits:
        sel = sel + jnp.where(hit, 1.0, 0.0)
    before = jnp.dot(tri_ref[...], sel.astype(BF16), preferred_element_type=F32) + carry

    mi = jnp.zeros(logits.shape, F32)
    mw = jnp.zeros(logits.shape, F32)
    for k in range(TOP_K):
        rank = jnp.sum(jnp.where(hits[k], before, 0.0), axis=-1, keepdims=True)
        mi = jnp.where(lane == k, idxs[k], mi)
        mi = jnp.where(lane == TOP_K + k, rank, mi)
        mw = jnp.where(lane == k, exps[k] * inv, mw)
    mi_ref[:, rows] = mi.T[0:2 * TOP_K].astype(jnp.int32)
    mw_ref[rows, :] = mw
    return carry + jnp.sum(sel, axis=0, keepdims=True)


def _mix(x2, g1, w_g, ot, cu, bg, conv_w, woa, wob, wo, g2, rw_split, rb, batch, seq):
    n, d = x2.shape
    tm = TM_MIX
    spb = seq // tm
    r8 = tm // 8
    nsteps = n // tm
    full = lambda shp: pl.BlockSpec(shp, lambda i: (0,) * len(shp))
    return pl.pallas_call(
        functools.partial(_mix_kernel, steps_per_seq=spb),
        grid=(nsteps,),
        in_specs=[
            pl.BlockSpec((tm, d), lambda i: (i, 0)),
            full(g1.shape), full(w_g.shape),
            pl.BlockSpec((1, MLA_HEADS * V_HEAD, tm), lambda i: (i // spb, 0, i % spb)),
            pl.BlockSpec((tm, CONV_WIDTH), lambda i: (i, 0)),
            pl.BlockSpec((8, CONV_WIDTH), lambda i: (jnp.maximum(i * r8 - 1, 0), 0)),
            pl.BlockSpec((8, CONV_WIDTH), lambda i: (jnp.minimum((i + 1) * r8, nsteps * r8 - 1), 0)),
            pl.BlockSpec((tm, CONV_WIDTH), lambda i: (i, 0)),
            full(conv_w.shape), full(woa.shape), full(wob.shape), full(wo.shape), full(g2.shape),
            full(rw_split.shape), full(rb.shape),
        ],
        out_specs=[
            pl.BlockSpec((tm, d), lambda i: (i, 0)),
            pl.BlockSpec((tm * SUBLANE, LANE), lambda i: (i, 0)),
            pl.BlockSpec((2 * TOP_K, tm), lambda i: (0, i)),
            pl.BlockSpec((tm, LANE), lambda i: (i, 0)),
            pl.BlockSpec((8, LANE), lambda i: (0, 0)),
        ],
        out_shape=[
            jax.ShapeDtypeStruct((n, d), F32),
            jax.ShapeDtypeStruct((n * SUBLANE, LANE), F32),
            jax.ShapeDtypeStruct((2 * TOP_K, n), jnp.int32),
            jax.ShapeDtypeStruct((n, LANE), F32),
            jax.ShapeDtypeStruct((8, LANE), F32),
        ],
        scratch_shapes=[pltpu.VMEM((tm // MIX_SUBTILES, tm // MIX_SUBTILES), BF16), pltpu.VMEM((1, LANE), F32)],
        compiler_params=pltpu.CompilerParams(dimension_semantics=("arbitrary",), vmem_limit_bytes=VMEM_LIMIT),
        name="mix_route",
    )(x2, g1, w_g, ot, cu, cu, cu, bg, conv_w, woa, wob, wo, g2, rw_split, rb)


def _row_copy_wait(src_like, dst_like, sem, times):
    for _ in range(times):
        pltpu.make_async_copy(src_like, dst_like, sem).wait()


def _token(ref, idx):
    return ref.at[pl.ds(pl.multiple_of(idx * SUBLANE, SUBLANE), SUBLANE)]


def _dispatch_kernel(dest_ref, pe_ref, plen_ref, nused_ref, h2_ref, xs_ref, zero_ref, sem, zsem):
    tm = h2_ref.shape[0] // SUBLANE
    block_rows = EXPERT_BLOCK * SUBLANE
    n_blocks = xs_ref.shape[0] // block_rows
    step = pl.program_id(0)

    def zero_copies(act):
        def tail(b, c):
            start = pl.multiple_of(b * block_rows, block_rows)
            act(pltpu.make_async_copy(zero_ref, xs_ref.at[pl.ds(start, block_rows)], zsem))
            return c

        def expert(e, c):
            length = plen_ref[e]
            first = pe_ref[e] - length
            for bit in reversed(range(EXPERT_BLOCK.bit_length() - 1)):
                size = 1 << bit
                done = lax.bitwise_and(length, -2 * size)

                @pl.when(lax.bitwise_and(length, size) != 0)
                def _(size=size, done=done):
                    start = pl.multiple_of((first + done) * SUBLANE, SUBLANE)
                    act(pltpu.make_async_copy(zero_ref.at[0:size * SUBLANE],
                                              xs_ref.at[pl.ds(start, size * SUBLANE)], zsem))
            return c

        lax.fori_loop(0, N_EXPERTS, expert, 0)
        lax.fori_loop(nused_ref[0], n_blocks, tail, 0)

    @pl.when(step == 0)
    def _():
        zero_ref[...] = jnp.zeros_like(zero_ref)
        zero_copies(lambda cp: cp.start())

    def issue(g, c):
        for u in range(ISSUE_GROUP):
            t = g * (ISSUE_GROUP // TOP_K) + u // TOP_K
            dest = dest_ref[(u % TOP_K) * tm + t]
            pltpu.make_async_copy(_token(h2_ref, t), _token(xs_ref, dest), sem).start(priority=u % 2)
        return c

    lax.fori_loop(0, tm * TOP_K // ISSUE_GROUP, issue, 0)
    _row_copy_wait(h2_ref, xs_ref.at[pl.ds(0, tm * SUBLANE)], sem, TOP_K)

    @pl.when(step == pl.num_programs(0) - 1)
    def _():
        zero_copies(lambda cp: cp.wait())


def _dispatch(dest_flat, pad_end, pad_len, n_used, h2t, rows):
    tm = TM_DISPATCH
    n = h2t.shape[0] // SUBLANE
    return pl.pallas_call(
        _dispatch_kernel,
        grid=(n // tm,),
        in_specs=[
            pl.BlockSpec((tm * TOP_K,), lambda i: (i,), memory_space=pltpu.SMEM),
            pl.BlockSpec(memory_space=pltpu.SMEM),
            pl.BlockSpec(memory_space=pltpu.SMEM),
            pl.BlockSpec(memory_space=pltpu.SMEM),
            pl.BlockSpec((tm * SUBLANE, LANE), lambda i: (i, 0)),
        ],
        out_specs=pl.BlockSpec(memory_space=pl.ANY),
        out_shape=jax.ShapeDtypeStruct((rows * SUBLANE, LANE), F32),
        scratch_shapes=[pltpu.VMEM((EXPERT_BLOCK * SUBLANE, LANE), F32), pltpu.SemaphoreType.DMA(()),
                        pltpu.SemaphoreType.DMA(())],
        compiler_params=pltpu.CompilerParams(dimension_semantics=("arbitrary",), vmem_limit_bytes=VMEM_LIMIT),
        name="dispatch",
    )(dest_flat, pad_end, pad_len, n_used, h2t)


def _expert_kernel(bexp_ref, nused_ref, nexp_ref, slot_ref, nvalid_ref, xs_ref, w1_hbm, b1_ref, w2_hbm, b2_ref, ys_ref,
                   w1_ref, w2_ref, sem):
    i = pl.program_id(0)
    active = i < nused_ref[0]
    expert = bexp_ref[i]
    slot = slot_ref[i]
    prev = bexp_ref[jnp.maximum(i - 1, 0)]
    fresh = jnp.logical_or(i == 0, expert != prev)

    def weight_copies(e, s):
        return (pltpu.make_async_copy(w1_hbm.at[e], w1_ref.at[s], sem.at[0, s]),
                pltpu.make_async_copy(w2_hbm.at[e], w2_ref.at[s], sem.at[1, s]))

    @pl.when(jnp.logical_and(active, i == 0))
    def _():
        for cp in weight_copies(expert, slot):
            cp.start()

    @pl.when(jnp.logical_and(active, fresh))
    def _():
        for cp in weight_copies(expert, slot):
            cp.wait()

        @pl.when(nexp_ref[i] != expert)
        def _():
            for cp in weight_copies(nexp_ref[i], 1 - slot):
                cp.start()

    def ffn(rows):
        dff = w2_ref.shape[1]
        xb = _load_token_tiles(xs_ref, (), rows).astype(BF16)
        hm = jnp.dot(xb, w1_ref[slot].astype(BF16), preferred_element_type=F32) + b1_ref[0]
        gate = jnp.minimum(hm[:, 0:dff], SWIGLU_LIMIT)
        up = jnp.clip(hm[:, dff:2 * dff], -SWIGLU_LIMIT, SWIGLU_LIMIT)
        glu = gate * (1.0 / (1.0 + jnp.exp(-SWIGLU_ALPHA * gate)))
        act = ((up + 1.0) * glu).astype(BF16)
        _store_token_tiles(ys_ref.at[0:rows * SUBLANE],
                           jnp.dot(act, w2_ref[slot].astype(BF16), preferred_element_type=F32) + b2_ref[0])

    groups = lax.div(nvalid_ref[i] + (EXPERT_ROW_GROUP - 1), EXPERT_ROW_GROUP)
    for g in range(1, EXPERT_BLOCK // EXPERT_ROW_GROUP + 1):
        rows = g * EXPERT_ROW_GROUP

        @pl.when(jnp.logical_and(active, groups == g))
        def _(rows=rows):
            ffn(rows)
            if rows < EXPERT_BLOCK:
                ys_ref[rows * SUBLANE:, :] = jnp.zeros(((EXPERT_BLOCK - rows) * SUBLANE, LANE), F32)

    @pl.when(jnp.logical_not(active))
    def _():
        ys_ref[...] = jnp.zeros_like(ys_ref)


def _experts(block_exp, n_used, next_exp, weight_slot, block_valid, xs, w1, b1, w2, b2):
    d = w1.shape[1]
    assert d == SUBLANE * LANE
    block_rows = EXPERT_BLOCK * SUBLANE
    n_blocks = xs.shape[0] // block_rows
    dff2 = w1.shape[2]
    dff = w2.shape[1]
    grid_spec = pltpu.PrefetchScalarGridSpec(
        num_scalar_prefetch=5,
        grid=(n_blocks,),
        in_specs=[
            pl.BlockSpec((block_rows, LANE), lambda i, be, nu, ne, ws, nv: (jnp.minimum(i, nu[0] - 1), 0)),
            pl.BlockSpec(memory_space=pl.ANY),
            pl.BlockSpec((1, 1, dff2), lambda i, be, nu, ne, ws, nv: (be[i], 0, 0)),
            pl.BlockSpec(memory_space=pl.ANY),
            pl.BlockSpec((1, 1, d), lambda i, be, nu, ne, ws, nv: (be[i], 0, 0)),
        ],
        out_specs=pl.BlockSpec((block_rows, LANE), lambda i, be, nu, ne, ws, nv: (i, 0)),
        scratch_shapes=[pltpu.VMEM((2, d, dff2), F32), pltpu.VMEM((2, dff, d), F32),
                        pltpu.SemaphoreType.DMA((2, 2))],
    )
    return pl.pallas_call(
        _expert_kernel,
        grid_spec=grid_spec,
        out_shape=jax.ShapeDtypeStruct(xs.shape, F32),
        compiler_params=pltpu.CompilerParams(dimension_semantics=("arbitrary",), vmem_limit_bytes=VMEM_LIMIT),
        name="experts",
    )(block_exp, n_used, next_exp, weight_slot, block_valid, xs, w1, b1, w2, b2)


def _combine_kernel(dest_ref, dest_next_ref, x1_ref, mw_ref, ys_ref, out_ref, buf_ref, sem):
    i = pl.program_id(0)
    tm = x1_ref.shape[0]
    slot = lax.rem(i, 2)

    def gather(idx_ref, s):
        def issue(g, c):
            for u in range(ISSUE_GROUP):
                t = g * (ISSUE_GROUP // TOP_K) + u // TOP_K
                dest = idx_ref[(u % TOP_K) * tm + t]
                pltpu.make_async_copy(_token(ys_ref, dest), _token(buf_ref.at[s, u % TOP_K], t),
                                      sem.at[s]).start(priority=u % 2)
            return c

        lax.fori_loop(0, tm * TOP_K // ISSUE_GROUP, issue, 0)

    @pl.when(i == 0)
    def _():
        gather(dest_ref, 0)

    @pl.when(i + 1 < pl.num_programs(0))
    def _():
        gather(dest_next_ref, 1 - slot)

    _row_copy_wait(ys_ref.at[pl.ds(0, tm * SUBLANE)], buf_ref.at[slot, 0], sem.at[slot], TOP_K)
    acc = x1_ref[...]
    mw = mw_ref[...]
    for k in range(TOP_K):
        acc = acc + mw[:, k:k + 1] * _load_token_tiles(buf_ref, (slot, k), tm)
    out_ref[...] = acc


def _combine(dest_flat, x1, mw, ys):
    n, d = x1.shape
    tm = TM_COMBINE
    nsteps = n // tm
    return pl.pallas_call(
        _combine_kernel,
        grid=(nsteps,),
        in_specs=[
            pl.BlockSpec((tm * TOP_K,), lambda i: (i,), memory_space=pltpu.SMEM),
            pl.BlockSpec((tm * TOP_K,), lambda i: (jnp.minimum(i + 1, nsteps - 1),), memory_space=pltpu.SMEM),
            pl.BlockSpec((tm, d), lambda i: (i, 0)),
            pl.BlockSpec((tm, LANE), lambda i: (i, 0)),
            pl.BlockSpec(memory_space=pl.ANY),
        ],
        out_specs=pl.BlockSpec((tm, d), lambda i: (i, 0)),
        out_shape=jax.ShapeDtypeStruct((n, d), F32),
        scratch_shapes=[pltpu.VMEM((2, TOP_K, tm * SUBLANE, LANE), F32), pltpu.SemaphoreType.DMA((2,))],
        compiler_params=pltpu.CompilerParams(dimension_semantics=("arbitrary",), vmem_limit_bytes=VMEM_LIMIT),
        name="combine",
    )(dest_flat, dest_flat, x1, mw, ys)


def _pad_cols(w, width):
    return jnp.pad(w, ((0, 0), (0, width - w.shape[1])))


def _head_slots(w, per_head):
    rows = w.shape[0]
    w3 = w.reshape(rows, MLA_HEADS, per_head)
    return jnp.pad(w3, ((0, 0), (0, 0), (0, HEAD_SLOT - per_head))).reshape(rows, MLA_HEADS * HEAD_SLOT)


def _rope_tables(positions):
    inv_freq = ROPE_THETA ** (-jnp.arange(0, QK_ROPE, 2, dtype=F32) / QK_ROPE)
    ang = positions.astype(F32).reshape(-1, 1) * inv_freq
    return jnp.cos(ang).T, jnp.sin(ang).T


def _layer(x2, positions, norm1_g, w_in, q_a_norm_g, kv_a_norm_g, w_uq, w_ukv, q_norm_g, k_norm_g,
           conv_w, w_o_mla, w_o_conv, w_o, norm2_g, router_w, router_b,
           expert_w1, expert_b1, expert_w2, expert_b2, batch, seq):
    n, d = x2.shape
    o_kr = Q_LORA + KV_LORA
    o_u = o_kr + QK_ROPE
    o_g = o_u + 3 * CONV_WIDTH
    kr_cols = jnp.pad(w_in[:, o_kr:o_u], ((0, 0), (QK_NOPE, LANE - QK_HEAD)))
    w_a = jnp.concatenate([w_in[:, :o_kr], kr_cols, w_in[:, o_u:o_g]], axis=1).astype(BF16)
    w_g = w_in[:, o_g:].astype(BF16)
    row = lambda v: v.reshape(1, -1)
    cos_c, sin_c = _rope_tables(positions)
    q_scale = (QK_HEAD ** -0.5) * math.log2(math.e)
    gain_t = lambda g: jnp.broadcast_to(g.reshape(LANE, 1), (LANE, TM_PROJ // PROJ_SUBTILES))
    kg = _pad_cols(row(k_norm_g), LANE)
    score_bound = 1.02 * q_scale * QK_HEAD * jnp.max(jnp.abs(q_norm_g)) * jnp.max(jnp.abs(k_norm_g))
    bounded = 2.0 * score_bound <= SAFE_SCORE_RANGE
    offset = jnp.where(bounded, score_bound, 0.0)
    feature = jnp.arange(LANE) == OFFSET_FEATURE
    qoff = gain_t(jnp.where(feature, -offset, 0.0).astype(F32))
    koff = jnp.where(feature, 1.0, 0.0).astype(F32).reshape(1, LANE)

    qt, k, vt, cu, bg = _in_projection(
        x2, row(norm1_g), w_a, row(q_a_norm_g), row(kv_a_norm_g),
        _head_slots(w_uq, QK_HEAD).astype(BF16), w_ukv.astype(BF16),
        gain_t(_pad_cols(row(q_norm_g) * q_scale, LANE)), kg, gain_t(kg), qoff, koff,
        cos_c, sin_c, batch, seq)
    ot = _attention(qt, k, vt, bounded)

    rw = _pad_cols(router_w, LANE)
    rw_hi = rw.astype(BF16)
    rw_lo = (rw - rw_hi.astype(F32)).astype(BF16)
    rb = jnp.concatenate([row(router_b), jnp.full((1, LANE - N_EXPERTS), NEG_BIG, F32)], axis=1)
    x1, h2, mi, mw, cnt = _mix(
        x2, row(norm1_g), w_g, ot, cu, bg, conv_w, w_o_mla.astype(BF16), w_o_conv.astype(BF16),
        w_o.astype(BF16), row(norm2_g), jnp.concatenate([rw_hi, rw_lo], axis=1), rb, batch, seq)

    counts = cnt[0, :N_EXPERTS].astype(jnp.int32)
    padded = (counts + EXPERT_BLOCK - 1) // EXPERT_BLOCK * EXPERT_BLOCK
    experts = jnp.arange(N_EXPERTS, dtype=jnp.int32)
    pad_end = jnp.sum(jnp.where(experts[None, :] <= experts[:, None], padded[None, :], 0), axis=1)
    pad_start = (pad_end - padded).astype(jnp.int32)
    nk = n * TOP_K
    n_blocks = (nk + N_EXPERTS * (EXPERT_BLOCK - 1) + EXPERT_BLOCK - 1) // EXPERT_BLOCK
    rows = n_blocks * EXPERT_BLOCK
    block_first_row = jnp.arange(n_blocks, dtype=jnp.int32) * EXPERT_BLOCK
    block_exp = jnp.minimum(jnp.sum(pad_end[None, :] <= block_first_row[:, None], axis=1),
                            N_EXPERTS - 1).astype(jnp.int32)
    n_used = (pad_end[-1:] // EXPERT_BLOCK).astype(jnp.int32)
    group_end = jnp.sum(jnp.where(experts[None, :] == block_exp[:, None], pad_end[None, :], 0), axis=1)
    following = jnp.minimum(jnp.sum(pad_end[None, :] <= group_end[:, None], axis=1), N_EXPERTS - 1)
    next_exp = jnp.where(group_end < pad_end[-1], following, block_exp).astype(jnp.int32)
    ordinal = jnp.sum(jnp.where(experts[None, :] < experts[:, None], (padded > 0)[None, :], False), axis=1)
    weight_slot = jnp.sum(jnp.where(experts[None, :] == block_exp[:, None], (ordinal % 2)[None, :], 0),
                          axis=1).astype(jnp.int32)
    e_sel = mi[None, 0:TOP_K] == jnp.arange(N_EXPERTS, dtype=jnp.int32)[:, None, None]
    dest = jnp.sum(jnp.where(e_sel, pad_start[:, None, None], 0), axis=0) + mi[TOP_K:2 * TOP_K]
    tiled = lambda tile: dest.reshape(TOP_K, n // tile, tile).transpose(1, 0, 2).reshape(nk)

    xs = _dispatch(tiled(TM_DISPATCH), pad_end.astype(jnp.int32), (padded - counts).astype(jnp.int32), n_used,
                   h2, rows)
    rows_end = jnp.sum(jnp.where(experts[None, :] == block_exp[:, None], (pad_start + counts)[None, :], 0), axis=1)
    block_valid = jnp.clip(rows_end - block_first_row, 0, EXPERT_BLOCK).astype(jnp.int32)
    ys = _experts(block_exp, n_used, next_exp, weight_slot, block_valid, xs, expert_w1, expert_b1.reshape(N_EXPERTS, 1, -1),
                  expert_w2, expert_b2.reshape(N_EXPERTS, 1, -1))
    return _combine(tiled(TM_COMBINE), x1, mw, ys)


def kernel(x, positions, norm1_g, w_in, q_a_norm_g, kv_a_norm_g, w_uq, w_ukv, q_norm_g, k_norm_g, conv_w,
           w_o_mla, w_o_conv, w_o, norm2_g, router_w, router_b, expert_w1, expert_b1, expert_w2, expert_b2):
    batch, seq, d = x.shape
    depth = norm1_g.shape[0]
    x2 = x.reshape(batch * seq, d)
    for l in range(depth):
        x2 = _layer(x2, positions, norm1_g[l], w_in[l], q_a_norm_g[l], kv_a_norm_g[l], w_uq[l], w_ukv[l],
                    q_norm_g[l], k_norm_g[l], conv_w[l], w_o_mla[l], w_o_conv[l], w_o[l], norm2_g[l],
                    router_w[l], router_b[l], expert_w1[l], expert_b1[l], expert_w2[l], expert_b2[l], batch, seq)
    return x2.reshape(batch, seq, d)
```

```python
import functools
import math

import jax
import jax.numpy as jnp
from jax import lax
from jax.experimental import pallas as pl
from jax.experimental.pallas import tpu as pltpu

F32 = jnp.float32
BF16 = jnp.bfloat16

MLA_HEADS = 8
QK_NOPE = 64
QK_ROPE = 32
QK_HEAD = QK_NOPE + QK_ROPE
V_HEAD = 64
Q_LORA = 256
KV_LORA = 128
ROPE_THETA = 10000.0
CONV_WIDTH = 512
N_EXPERTS = 32
TOP_K = 4
SWIGLU_LIMIT = 7.0
SWIGLU_ALPHA = 1.702
EPS = 1e-6

LANE = 128
SUBLANE = 8
HEAD_SLOT = LANE
HALF_ROPE = QK_ROPE // 2
V_ROWS = V_HEAD + 16
OFFSET_FEATURE = QK_HEAD
SAFE_SCORE_RANGE = 100.0
VMEM_LIMIT = 56 * 1024 * 1024

TM_PROJ = 512
PROJ_SUBTILES = 2
TQ = 512
TKV = 512
ATTN_UNROLL = 4
TKV_BOUNDED = 256
TM_MIX = 512
MIX_SUBTILES = 1
TM_DISPATCH = 2048
TM_COMBINE = 256
EXPERT_BLOCK = 512
EXPERT_ROW_GROUP = 128
ISSUE_GROUP = 16
NEG_BIG = -1e30


def _load_token_tiles(ref, lead, rows):
    return jnp.concatenate([ref[lead + (pl.ds(c, rows, stride=SUBLANE), slice(None))] for c in range(SUBLANE)],
                           axis=1)


def _store_token_tiles(ref, value):
    rows = value.shape[0]
    for c in range(SUBLANE):
        ref[pl.ds(c, rows, stride=SUBLANE), :] = value[:, c * LANE:(c + 1) * LANE]


def _rms(x, g):
    return x * lax.rsqrt(jnp.mean(x * x, axis=-1, keepdims=True) + EPS) * g


def _inproj_kernel(x_ref, g1_ref, w_ref, gq_ref, gkv_ref, wuq_ref, wukv_ref, qgt_ref, kg_ref, kgt_ref,
                   qoff_ref, koff_ref, cost_ref, sint_ref, qt_ref, k_ref, vt_ref, cu_ref, bg_ref):
    tm = x_ref.shape[0] // PROJ_SUBTILES
    for part in range(PROJ_SUBTILES):
        _inproj_rows(slice(part * tm, (part + 1) * tm), x_ref, g1_ref, w_ref, gq_ref, gkv_ref, wuq_ref, wukv_ref,
                     qgt_ref, kg_ref, kgt_ref, qoff_ref, koff_ref, cost_ref, sint_ref,
                     qt_ref, k_ref, vt_ref, cu_ref, bg_ref)


def _inproj_rows(rows, x_ref, g1_ref, w_ref, gq_ref, gkv_ref, wuq_ref, wukv_ref, qgt_ref, kg_ref, kgt_ref,
                 qoff_ref, koff_ref, cost_ref, sint_ref, qt_ref, k_ref, vt_ref, cu_ref, bg_ref):
    x = x_ref[rows, :]
    h = _rms(x, g1_ref[...]).astype(BF16)
    proj = jnp.dot(h, w_ref[...], preferred_element_type=F32)
    c_q = proj[:, 0:Q_LORA]
    c_kv = proj[:, Q_LORA:Q_LORA + KV_LORA]
    kr = proj[:, Q_LORA + KV_LORA:Q_LORA + KV_LORA + LANE]
    o = Q_LORA + KV_LORA + LANE
    u = proj[:, o:o + CONV_WIDTH]
    c_gate = proj[:, o + CONV_WIDTH:o + 2 * CONV_WIDTH]
    b_gate = proj[:, o + 2 * CONV_WIDTH:o + 3 * CONV_WIDTH]
    cu_ref[rows, :] = c_gate * u
    bg_ref[rows, :] = b_gate

    kg = kg_ref[...]
    tm = x.shape[0]
    lane = lax.broadcasted_iota(jnp.int32, (tm, LANE), 1)

    q = jnp.dot(_rms(c_q, gq_ref[...]).astype(BF16), wuq_ref[...], preferred_element_type=F32)
    kv = jnp.dot(_rms(c_kv, gkv_ref[...]).astype(BF16), wukv_ref[...], preferred_element_type=F32)

    cos_c = cost_ref[:, rows]
    sin_c = sint_ref[:, rows]

    def rope_t(t):
        t1 = t[QK_NOPE:QK_NOPE + HALF_ROPE]
        t2 = t[QK_NOPE + HALF_ROPE:QK_HEAD]
        return jnp.concatenate([t[0:QK_NOPE], t1 * cos_c - t2 * sin_c, t1 * sin_c + t2 * cos_c, t[QK_HEAD:]], axis=0)

    ss_r = jnp.sum(kr * kr, axis=-1, keepdims=True)
    kr_roped = rope_t(kr.T * kgt_ref[...]).T
    qgt = qgt_ref[...]
    qoff = qoff_ref[...]
    koff = koff_ref[...]
    ones = jnp.ones((V_ROWS - V_HEAD, tm), BF16)
    for hd in range(MLA_HEADS):
        qht = q[:, hd * HEAD_SLOT:(hd + 1) * HEAD_SLOT].T
        r = lax.rsqrt(jnp.sum(qht * qht, axis=0, keepdims=True) * (1.0 / QK_HEAD) + EPS)
        qt_ref[0, hd, :, rows] = (rope_t(qht * r * qgt) + qoff).astype(BF16)

        kvh = kv[:, hd * HEAD_SLOT:(hd + 1) * HEAD_SLOT]
        knope = jnp.where(lane < QK_NOPE, kvh, 0.0)
        rk = lax.rsqrt((jnp.sum(knope * knope, axis=-1, keepdims=True) + ss_r) * (1.0 / QK_HEAD) + EPS)
        k_ref[0, hd, rows, :] = ((knope * kg + kr_roped) * rk + koff).astype(BF16)
        kvt = kvh.T
        vt_ref[0, hd, 0:V_HEAD, rows] = kvt[QK_NOPE:QK_NOPE + V_HEAD].astype(BF16)
        vt_ref[0, hd, V_HEAD:V_ROWS, rows] = ones


def _in_projection(x2, g1, w_a, gq, gkv, wuq, wukv, qgt, kg, kgt, qoff, koff, cos_c, sin_c, batch, seq):
    n, d = x2.shape
    tm = TM_PROJ
    spb = seq // tm
    full = lambda shp: pl.BlockSpec(shp, lambda i: (0,) * len(shp))
    return pl.pallas_call(
        _inproj_kernel,
        grid=(n // tm,),
        in_specs=[
            pl.BlockSpec((tm, d), lambda i: (i, 0)),
            full(g1.shape), full(w_a.shape), full(gq.shape), full(gkv.shape), full(wuq.shape), full(wukv.shape),
            full(qgt.shape), full(kg.shape), full(kgt.shape), full(qoff.shape), full(koff.shape),
            pl.BlockSpec((HALF_ROPE, tm), lambda i: (0, i)),
            pl.BlockSpec((HALF_ROPE, tm), lambda i: (0, i)),
        ],
        out_specs=[
            pl.BlockSpec((1, MLA_HEADS, HEAD_SLOT, tm), lambda i: (i // spb, 0, 0, i % spb)),
            pl.BlockSpec((1, MLA_HEADS, tm, HEAD_SLOT), lambda i: (i // spb, 0, i % spb, 0)),
            pl.BlockSpec((1, MLA_HEADS, V_ROWS, tm), lambda i: (i // spb, 0, 0, i % spb)),
            pl.BlockSpec((tm, CONV_WIDTH), lambda i: (i, 0)),
            pl.BlockSpec((tm, CONV_WIDTH), lambda i: (i, 0)),
        ],
        out_shape=[
            jax.ShapeDtypeStruct((batch, MLA_HEADS, HEAD_SLOT, seq), BF16),
            jax.ShapeDtypeStruct((batch, MLA_HEADS, seq, HEAD_SLOT), BF16),
            jax.ShapeDtypeStruct((batch, MLA_HEADS, V_ROWS, seq), BF16),
            jax.ShapeDtypeStruct((n, CONV_WIDTH), F32),
            jax.ShapeDtypeStruct((n, CONV_WIDTH), F32),
        ],
        compiler_params=pltpu.CompilerParams(dimension_semantics=("parallel",), vmem_limit_bytes=VMEM_LIMIT),
        name="in_projection",
    )(x2, g1, w_a, gq, gkv, wuq, wukv, qgt, kg, kgt, qoff, koff, cos_c, sin_c)


def _attn_kernel(qt_ref, k_ref, vt_ref, o_ref, sa_ref, sb_ref, m_ref, acc_ref, *, tq, tk):
    seq = k_ref.shape[2]
    nk = seq // tk
    bufs = (sa_ref, sb_ref)

    def query_tile(qi, carry):
        q0 = pl.multiple_of(qi * tq, tq)
        qt = qt_ref[0, 0, :, pl.ds(q0, tq)]

        def scores(c, s_ref):
            k0 = pl.multiple_of(c * tk, tk)
            s_ref[...] = jnp.dot(k_ref[0, 0, pl.ds(k0, tk), :], qt, preferred_element_type=F32)

        def accumulate(c, s_ref):
            k0 = pl.multiple_of(c * tk, tk)
            s = s_ref[...]
            m = m_ref[...]
            m_new = jnp.maximum(m, jnp.max(s, axis=0, keepdims=True))
            m_ref[...] = m_new
            p = jnp.exp2(s - m_new).astype(BF16)
            vs = vt_ref[0, 0, :, pl.ds(k0, tk)]
            acc_ref[...] = jnp.exp2(m - m_new) * acc_ref[...] + jnp.dot(vs, p, preferred_element_type=F32)

        m_ref[...] = jnp.full(m_ref.shape, NEG_BIG, F32)
        acc_ref[...] = jnp.zeros(acc_ref.shape, F32)
        scores(0, sa_ref)

        def group(j, c):
            base = ATTN_UNROLL * j
            for u in range(ATTN_UNROLL):
                scores(base + u + 1, bufs[(u + 1) % 2])
                accumulate(base + u, bufs[u % 2])
            return c

        lax.fori_loop(0, nk // ATTN_UNROLL - 1, group, 0)
        base = nk - ATTN_UNROLL
        for u in range(ATTN_UNROLL):
            if u + 1 < ATTN_UNROLL:
                scores(base + u + 1, bufs[(u + 1) % 2])
            accumulate(base + u, bufs[u % 2])
        acc = acc_ref[...]
        o_ref[0, :, pl.ds(q0, tq)] = (acc[0:V_HEAD] * (1.0 / acc[V_HEAD:V_HEAD + 1])).astype(BF16)
        return carry

    lax.fori_loop(0, seq // tq, query_tile, 0)


def _attn_bounded_kernel(qt_ref, k_ref, vt_ref, o_ref, sa_ref, sb_ref, *, tq, tk):
    seq = k_ref.shape[2]
    nk = seq // tk
    n_tiles = seq // tq
    ahead_refs = (sa_ref, sb_ref)
    ahead = len(ahead_refs)

    def load_qt(qi):
        return qt_ref[0, 0, :, pl.ds(pl.multiple_of(qi * tq, tq), tq)]

    def scores(c, qt):
        return jnp.dot(k_ref[0, 0, c * tk:(c + 1) * tk, :], qt, preferred_element_type=F32)

    qt_first = load_qt(0)
    for a in range(ahead):
        ahead_refs[a][...] = scores(a, qt_first)

    def query_tile(qi, carry):
        qt = load_qt(qi)
        qt_next = load_qt(jnp.minimum(qi + 1, n_tiles - 1))
        pending = [ref[...] for ref in ahead_refs]
        total = None
        for c in range(nk):
            if c + ahead < nk:
                pending.append(scores(c + ahead, qt))
            else:
                ahead_refs[c + ahead - nk][...] = scores(c + ahead - nk, qt_next)
            p = jnp.exp2(pending.pop(0)).astype(BF16)
            part = jnp.dot(vt_ref[0, 0, :, c * tk:(c + 1) * tk], p, preferred_element_type=F32)
            total = part if total is None else total + part
        o_ref[0, :, pl.ds(pl.multiple_of(qi * tq, tq), tq)] = (
            total[0:V_HEAD] * (1.0 / total[V_HEAD:V_HEAD + 1])).astype(BF16)
        return carry

    lax.fori_loop(0, n_tiles, query_tile, 0)


def _attention(qt, k, vt, bounded):
    batch, heads, _, seq = qt.shape
    assert ATTN_UNROLL % 2 == 0 and (seq // TKV) % ATTN_UNROLL == 0 and seq % TQ == 0
    in_specs = [
        pl.BlockSpec((1, 1, HEAD_SLOT, seq), lambda b, h, flag: (b, h, 0, 0)),
        pl.BlockSpec((1, 1, seq, HEAD_SLOT), lambda b, h, flag: (b, h, 0, 0)),
        pl.BlockSpec((1, 1, V_ROWS, seq), lambda b, h, flag: (b, h, 0, 0)),
    ]
    out_spec = pl.BlockSpec((1, V_HEAD, seq), lambda b, h, flag: (b, h, 0))
    out_shape = jax.ShapeDtypeStruct((batch, heads * V_HEAD, seq), BF16)
    grid_spec = pltpu.PrefetchScalarGridSpec(
        num_scalar_prefetch=1, grid=(batch, heads), in_specs=in_specs, out_specs=out_spec,
        scratch_shapes=[pltpu.VMEM((max(TKV, TKV_BOUNDED), TQ), F32), pltpu.VMEM((max(TKV, TKV_BOUNDED), TQ), F32),
                        pltpu.VMEM((1, TQ), F32), pltpu.VMEM((V_ROWS, TQ), F32)])
    return pl.pallas_call(
        _attn_select_kernel, grid_spec=grid_spec, out_shape=out_shape,
        compiler_params=pltpu.CompilerParams(dimension_semantics=("parallel", "parallel"),
                                             vmem_limit_bytes=VMEM_LIMIT),
        name="attention",
    )(bounded.astype(jnp.int32).reshape(1), qt, k, vt)


def _attn_select_kernel(bounded_ref, qt_ref, k_ref, vt_ref, o_ref, sa_ref, sb_ref, m_ref, acc_ref):
    @pl.when(bounded_ref[0] != 0)
    def _():
        _attn_bounded_kernel(qt_ref, k_ref, vt_ref, o_ref, sa_ref.at[0:TKV_BOUNDED], sb_ref.at[0:TKV_BOUNDED],
                             tq=TQ, tk=TKV_BOUNDED)

    @pl.when(bounded_ref[0] == 0)
    def _():
        _attn_kernel(qt_ref, k_ref, vt_ref, o_ref, sa_ref.at[0:TKV], sb_ref.at[0:TKV], m_ref, acc_ref,
                     tq=TQ, tk=TKV)


def _mix_kernel(x_ref, g1_ref, wg_ref, ot_ref, cu_ref, cup_ref, cun_ref, bg_ref, cw_ref,
                woa_ref, wob_ref, wo_ref, g2_ref, rw_ref, rb_ref,
                x1_ref, h2_ref, mi_ref, mw_ref, cnt_ref, tri_ref, carry_ref, *, steps_per_seq):
    i = pl.program_id(0)
    tm = x_ref.shape[0]
    ts = tm // MIX_SUBTILES

    @pl.when(i == 0)
    def _():
        r = lax.broadcasted_iota(jnp.int32, (ts, ts), 0)
        c = lax.broadcasted_iota(jnp.int32, (ts, ts), 1)
        tri_ref[...] = jnp.where(c < r, 1.0, 0.0).astype(BF16)
        carry_ref[...] = jnp.zeros_like(carry_ref)

    cu = cu_ref[...]
    row = lax.broadcasted_iota(jnp.int32, cu.shape, 0)
    s_in_seq = i % steps_per_seq
    prev_row = jnp.where(s_in_seq == 0, 0.0, cup_ref[7:8, :])
    next_row = jnp.where(s_in_seq == steps_per_seq - 1, 0.0, cun_ref[0:1, :])
    below = jnp.where(row == 0, prev_row, pltpu.roll(cu, 1, 0))
    above = jnp.where(row == tm - 1, next_row, pltpu.roll(cu, tm - 1, 0))
    cw = cw_ref[...]
    gated_conv = (bg_ref[...] * (cw[0:1, :] * below + cw[1:2, :] * cu + cw[2:3, :] * above)).astype(BF16)

    carry = carry_ref[...]
    for part in range(MIX_SUBTILES):
        carry = _mix_rows(part * ts, ts, gated_conv[part * ts:(part + 1) * ts], carry, x_ref, g1_ref, wg_ref,
                          ot_ref, woa_ref, wob_ref, wo_ref, g2_ref, rw_ref, rb_ref, tri_ref,
                          x1_ref, h2_ref, mi_ref, mw_ref)
    carry_ref[...] = carry
    cnt_ref[...] = jnp.broadcast_to(carry, cnt_ref.shape)


def _mix_rows(r0, ts, gated_conv, carry, x_ref, g1_ref, wg_ref, ot_ref, woa_ref, wob_ref, wo_ref, g2_ref,
              rw_ref, rb_ref, tri_ref, x1_ref, h2_ref, mi_ref, mw_ref):
    rows = slice(r0, r0 + ts)
    y_a = lax.dot_general(ot_ref[0, :, rows], woa_ref[...], (((0,), (0,)), ((), ())), preferred_element_type=F32)
    y_b = jnp.dot(gated_conv, wob_ref[...], preferred_element_type=F32)

    x = x_ref[rows, :]
    h = _rms(x, g1_ref[...]).astype(BF16)
    gates = jnp.dot(h, wg_ref[...], preferred_element_type=F32)
    d = x.shape[1]
    sig_a = 1.0 / (1.0 + jnp.exp(-gates[:, 0:d]))
    sig_b = 1.0 / (1.0 + jnp.exp(-gates[:, d:2 * d]))

    merged = (sig_a * y_a + sig_b * y_b).astype(BF16)
    x1 = x + jnp.dot(merged, wo_ref[...], preferred_element_type=F32)
    x1_ref[rows, :] = x1
    h2 = _rms(x1, g2_ref[...])
    _store_token_tiles(h2_ref.at[r0 * SUBLANE:(r0 + ts) * SUBLANE], h2)

    h2_hi = h2.astype(BF16)
    h2_lo = (h2 - h2_hi.astype(F32)).astype(BF16)
    rw = rw_ref[...]
    hi_terms = jnp.dot(h2_hi, rw, preferred_element_type=F32)
    logits = (hi_terms[:, 0:LANE] + hi_terms[:, LANE:2 * LANE]
              + jnp.dot(h2_lo, rw[:, 0:LANE], preferred_element_type=F32)
              + rb_ref[...])

    lane = lax.broadcasted_iota(jnp.int32, logits.shape, 1)
    lane_f = lane.astype(F32)
    work = logits
    vals, idxs, hits = [], [], []
    for _ in range(TOP_K):
        mx = jnp.max(work, axis=-1, keepdims=True)
        idx = jnp.min(jnp.where(work == mx, lane_f, float(LANE)), axis=-1, keepdims=True)
        hit = lane_f == idx
        work = jnp.where(hit, -jnp.inf, work)
        vals.append(mx)
        idxs.append(idx)
        hits.append(hit)
    exps = [jnp.exp(v - vals[0]) for v in vals]
    denom = exps[0] + exps[1] + exps[2] + exps[3]
    inv = 1.0 / denom

    sel = jnp.zeros(logits.shape, F32)
    for hit in hits:
        sel = sel + jnp.where(hit, 1.0, 0.0)
    before = jnp.dot(tri_ref[...], sel.astype(BF16), preferred_element_type=F32) + carry

    mi = jnp.zeros(logits.shape, F32)
    mw = jnp.zeros(logits.shape, F32)
    for k in range(TOP_K):
        rank = jnp.sum(jnp.where(hits[k], before, 0.0), axis=-1, keepdims=True)
        mi = jnp.where(lane == k, idxs[k], mi)
        mi = jnp.where(lane == TOP_K + k, rank, mi)
        mw = jnp.where(lane == k, exps[k] * inv, mw)
    mi_ref[:, rows] = mi.T[0:2 * TOP_K].astype(jnp.int32)
    mw_ref[rows, :] = mw
    return carry + jnp.sum(sel, axis=0, keepdims=True)


def _mix(x2, g1, w_g, ot, cu, bg, conv_w, woa, wob, wo, g2, rw_split, rb, batch, seq):
    n, d = x2.shape
    tm = TM_MIX
    spb = seq // tm
    r8 = tm // 8
    nsteps = n // tm
    full = lambda shp: pl.BlockSpec(shp, lambda i: (0,) * len(shp))
    return pl.pallas_call(
        functools.partial(_mix_kernel, steps_per_seq=spb),
        grid=(nsteps,),
        in_specs=[
            pl.BlockSpec((tm, d), lambda i: (i, 0)),
            full(g1.shape), full(w_g.shape),
            pl.BlockSpec((1, MLA_HEADS * V_HEAD, tm), lambda i: (i // spb, 0, i % spb)),
            pl.BlockSpec((tm, CONV_WIDTH), lambda i: (i, 0)),
            pl.BlockSpec((8, CONV_WIDTH), lambda i: (jnp.maximum(i * r8 - 1, 0), 0)),
            pl.BlockSpec((8, CONV_WIDTH), lambda i: (jnp.minimum((i + 1) * r8, nsteps * r8 - 1), 0)),
            pl.BlockSpec((tm, CONV_WIDTH), lambda i: (i, 0)),
            full(conv_w.shape), full(woa.shape), full(wob.shape), full(wo.shape), full(g2.shape),
            full(rw_split.shape), full(rb.shape),
        ],
        out_specs=[
            pl.BlockSpec((tm, d), lambda i: (i, 0)),
            pl.BlockSpec((tm * SUBLANE, LANE), lambda i: (i, 0)),
            pl.BlockSpec((2 * TOP_K, tm), lambda i: (0, i)),
            pl.BlockSpec((tm, LANE), lambda i: (i, 0)),
            pl.BlockSpec((8, LANE), lambda i: (0, 0)),
        ],
        out_shape=[
            jax.ShapeDtypeStruct((n, d), F32),
            jax.ShapeDtypeStruct((n * SUBLANE, LANE), F32),
            jax.ShapeDtypeStruct((2 * TOP_K, n), jnp.int32),
            jax.ShapeDtypeStruct((n, LANE), F32),
            jax.ShapeDtypeStruct((8, LANE), F32),
        ],
        scratch_shapes=[pltpu.VMEM((tm // MIX_SUBTILES, tm // MIX_SUBTILES), BF16), pltpu.VMEM((1, LANE), F32)],
        compiler_params=pltpu.CompilerParams(dimension_semantics=("arbitrary",), vmem_limit_bytes=VMEM_LIMIT),
        name="mix_route",
    )(x2, g1, w_g, ot, cu, cu, cu, bg, conv_w, woa, wob, wo, g2, rw_split, rb)


def _row_copy_wait(src_like, dst_like, sem, times):
    for _ in range(times):
        pltpu.make_async_copy(src_like, dst_like, sem).wait()


def _token(ref, idx):
    return ref.at[pl.ds(pl.multiple_of(idx * SUBLANE, SUBLANE), SUBLANE)]


def _dispatch_kernel(dest_ref, pe_ref, plen_ref, nused_ref, h2_ref, xs_ref, zero_ref, sem, zsem):
    tm = h2_ref.shape[0] // SUBLANE
    block_rows = EXPERT_BLOCK * SUBLANE
    n_blocks = xs_ref.shape[0] // block_rows
    step = pl.program_id(0)

    def zero_copies(act):
        def tail(b, c):
            start = pl.multiple_of(b * block_rows, block_rows)
            act(pltpu.make_async_copy(zero_ref, xs_ref.at[pl.ds(start, block_rows)], zsem))
            return c

        def expert(e, c):
            length = plen_ref[e]
            first = pe_ref[e] - length
            for bit in reversed(range(EXPERT_BLOCK.bit_length() - 1)):
                size = 1 << bit
                done = lax.bitwise_and(length, -2 * size)

                @pl.when(lax.bitwise_and(length, size) != 0)
                def _(size=size, done=done):
                    start = pl.multiple_of((first + done) * SUBLANE, SUBLANE)
                    act(pltpu.make_async_copy(zero_ref.at[0:size * SUBLANE],
                                              xs_ref.at[pl.ds(start, size * SUBLANE)], zsem))
            return c

        lax.fori_loop(0, N_EXPERTS, expert, 0)
        lax.fori_loop(nused_ref[0], n_blocks, tail, 0)

    @pl.when(step == 0)
    def _():
        zero_ref[...] = jnp.zeros_like(zero_ref)
        zero_copies(lambda cp: cp.start())

    def issue(g, c):
        for u in range(ISSUE_GROUP):
            t = g * (ISSUE_GROUP // TOP_K) + u // TOP_K
            dest = dest_ref[(u % TOP_K) * tm + t]
            pltpu.make_async_copy(_token(h2_ref, t), _token(xs_ref, dest), sem).start(priority=u % 2)
        return c

    lax.fori_loop(0, tm * TOP_K // ISSUE_GROUP, issue, 0)
    _row_copy_wait(h2_ref, xs_ref.at[pl.ds(0, tm * SUBLANE)], sem, TOP_K)

    @pl.when(step == pl.num_programs(0) - 1)
    def _():
        zero_copies(lambda cp: cp.wait())


def _dispatch(dest_flat, pad_end, pad_len, n_used, h2t, rows):
    tm = TM_DISPATCH
    n = h2t.shape[0] // SUBLANE
    return pl.pallas_call(
        _dispatch_kernel,
        grid=(n // tm,),
        in_specs=[
            pl.BlockSpec((tm * TOP_K,), lambda i: (i,), memory_space=pltpu.SMEM),
            pl.BlockSpec(memory_space=pltpu.SMEM),
            pl.BlockSpec(memory_space=pltpu.SMEM),
            pl.BlockSpec(memory_space=pltpu.SMEM),
            pl.BlockSpec((tm * SUBLANE, LANE), lambda i: (i, 0)),
        ],
        out_specs=pl.BlockSpec(memory_space=pl.ANY),
        out_shape=jax.ShapeDtypeStruct((rows * SUBLANE, LANE), F32),
        scratch_shapes=[pltpu.VMEM((EXPERT_BLOCK * SUBLANE, LANE), F32), pltpu.SemaphoreType.DMA(()),
                        pltpu.SemaphoreType.DMA(())],
        compiler_params=pltpu.CompilerParams(dimension_semantics=("arbitrary",), vmem_limit_bytes=VMEM_LIMIT),
        name="dispatch",
    )(dest_flat, pad_end, pad_len, n_used, h2t)


def _expert_kernel(bexp_ref, nused_ref, nexp_ref, slot_ref, nvalid_ref, xs_ref, w1_hbm, b1_ref, w2_hbm, b2_ref, ys_ref,
                   w1_ref, w2_ref, sem):
    i = pl.program_id(0)
    active = i < nused_ref[0]
    expert = bexp_ref[i]
    slot = slot_ref[i]
    prev = bexp_ref[jnp.maximum(i - 1, 0)]
    fresh = jnp.logical_or(i == 0, expert != prev)

    def weight_copies(e, s):
        return (pltpu.make_async_copy(w1_hbm.at[e], w1_ref.at[s], sem.at[0, s]),
                pltpu.make_async_copy(w2_hbm.at[e], w2_ref.at[s], sem.at[1, s]))

    @pl.when(jnp.logical_and(active, i == 0))
    def _():
        for cp in weight_copies(expert, slot):
            cp.start()

    @pl.when(jnp.logical_and(active, fresh))
    def _():
        for cp in weight_copies(expert, slot):
            cp.wait()

        @pl.when(nexp_ref[i] != expert)
        def _():
            for cp in weight_copies(nexp_ref[i], 1 - slot):
                cp.start()

    def ffn(rows):
        dff = w2_ref.shape[1]
        xb = _load_token_tiles(xs_ref, (), rows).astype(BF16)
        hm = jnp.dot(xb, w1_ref[slot].astype(BF16), preferred_element_type=F32) + b1_ref[0]
        gate = jnp.minimum(hm[:, 0:dff], SWIGLU_LIMIT)
        up = jnp.clip(hm[:, dff:2 * dff], -SWIGLU_LIMIT, SWIGLU_LIMIT)
        glu = gate * (1.0 / (1.0 + jnp.exp(-SWIGLU_ALPHA * gate)))
        act = ((up + 1.0) * glu).astype(BF16)
        _store_token_tiles(ys_ref.at[0:rows * SUBLANE],
                           jnp.dot(act, w2_ref[slot].astype(BF16), preferred_element_type=F32) + b2_ref[0])

    groups = lax.div(nvalid_ref[i] + (EXPERT_ROW_GROUP - 1), EXPERT_ROW_GROUP)
    for g in range(1, EXPERT_BLOCK // EXPERT_ROW_GROUP + 1):
        rows = g * EXPERT_ROW_GROUP

        @pl.when(jnp.logical_and(active, groups == g))
        def _(rows=rows):
            ffn(rows)
            if rows < EXPERT_BLOCK:
                ys_ref[rows * SUBLANE:, :] = jnp.zeros(((EXPERT_BLOCK - rows) * SUBLANE, LANE), F32)

    @pl.when(jnp.logical_not(active))
    def _():
        ys_ref[...] = jnp.zeros_like(ys_ref)


def _experts(block_exp, n_used, next_exp, weight_slot, block_valid, xs, w1, b1, w2, b2):
    d = w1.shape[1]
    assert d == SUBLANE * LANE
    block_rows = EXPERT_BLOCK * SUBLANE
    n_blocks = xs.shape[0] // block_rows
    dff2 = w1.shape[2]
    dff = w2.shape[1]
    grid_spec = pltpu.PrefetchScalarGridSpec(
        num_scalar_prefetch=5,
        grid=(n_blocks,),
        in_specs=[
            pl.BlockSpec((block_rows, LANE), lambda i, be, nu, ne, ws, nv: (jnp.minimum(i, nu[0] - 1), 0)),
            pl.BlockSpec(memory_space=pl.ANY),
            pl.BlockSpec((1, 1, dff2), lambda i, be, nu, ne, ws, nv: (be[i], 0, 0)),
            pl.BlockSpec(memory_space=pl.ANY),
            pl.BlockSpec((1, 1, d), lambda i, be, nu, ne, ws, nv: (be[i], 0, 0)),
        ],
        out_specs=pl.BlockSpec((block_rows, LANE), lambda i, be, nu, ne, ws, nv: (i, 0)),
        scratch_shapes=[pltpu.VMEM((2, d, dff2), F32), pltpu.VMEM((2, dff, d), F32),
                        pltpu.SemaphoreType.DMA((2, 2))],
    )
    return pl.pallas_call(
        _expert_kernel,
        grid_spec=grid_spec,
        out_shape=jax.ShapeDtypeStruct(xs.shape, F32),
        compiler_params=pltpu.CompilerParams(dimension_semantics=("arbitrary",), vmem_limit_bytes=VMEM_LIMIT),
        name="experts",
    )(block_exp, n_used, next_exp, weight_slot, block_valid, xs, w1, b1, w2, b2)


def _combine_kernel(dest_ref, dest_next_ref, x1_ref, mw_ref, ys_ref, out_ref, buf_ref, sem):
    i = pl.program_id(0)
    tm = x1_ref.shape[0]
    slot = lax.rem(i, 2)

    def gather(idx_ref, s):
        def issue(g, c):
            for u in range(ISSUE_GROUP):
                t = g * (ISSUE_GROUP // TOP_K) + u // TOP_K
                dest = idx_ref[(u % TOP_K) * tm + t]
                pltpu.make_async_copy(_token(ys_ref, dest), _token(buf_ref.at[s, u % TOP_K], t),
                                      sem.at[s]).start(priority=u % 2)
            return c

        lax.fori_loop(0, tm * TOP_K // ISSUE_GROUP, issue, 0)

    @pl.when(i == 0)
    def _():
        gather(dest_ref, 0)

    @pl.when(i + 1 < pl.num_programs(0))
    def _():
        gather(dest_next_ref, 1 - slot)

    _row_copy_wait(ys_ref.at[pl.ds(0, tm * SUBLANE)], buf_ref.at[slot, 0], sem.at[slot], TOP_K)
    acc = x1_ref[...]
    mw = mw_ref[...]
    for k in range(TOP_K):
        acc = acc + mw[:, k:k + 1] * _load_token_tiles(buf_ref, (slot, k), tm)
    out_ref[...] = acc


def _combine(dest_flat, x1, mw, ys):
    n, d = x1.shape
    tm = TM_COMBINE
    nsteps = n // tm
    return pl.pallas_call(
        _combine_kernel,
        grid=(nsteps,),
        in_specs=[
            pl.BlockSpec((tm * TOP_K,), lambda i: (i,), memory_space=pltpu.SMEM),
            pl.BlockSpec((tm * TOP_K,), lambda i: (jnp.minimum(i + 1, nsteps - 1),), memory_space=pltpu.SMEM),
            pl.BlockSpec((tm, d), lambda i: (i, 0)),
            pl.BlockSpec((tm, LANE), lambda i: (i, 0)),
            pl.BlockSpec(memory_space=pl.ANY),
        ],
        out_specs=pl.BlockSpec((tm, d), lambda i: (i, 0)),
        out_shape=jax.ShapeDtypeStruct((n, d), F32),
        scratch_shapes=[pltpu.VMEM((2, TOP_K, tm * SUBLANE, LANE), F32), pltpu.SemaphoreType.DMA((2,))],
        compiler_params=pltpu.CompilerParams(dimension_semantics=("arbitrary",), vmem_limit_bytes=VMEM_LIMIT),
        name="combine",
    )(dest_flat, dest_flat, x1, mw, ys)


def _pad_cols(w, width):
    return jnp.pad(w, ((0, 0), (0, width - w.shape[1])))


def _head_slots(w, per_head):
    rows = w.shape[0]
    w3 = w.reshape(rows, MLA_HEADS, per_head)
    return jnp.pad(w3, ((0, 0), (0, 0), (0, HEAD_SLOT - per_head))).reshape(rows, MLA_HEADS * HEAD_SLOT)


def _rope_tables(positions):
    inv_freq = ROPE_THETA ** (-jnp.arange(0, QK_ROPE, 2, dtype=F32) / QK_ROPE)
    ang = positions.astype(F32).reshape(-1, 1) * inv_freq
    return jnp.cos(ang).T, jnp.sin(ang).T


def _layer(x2, positions, norm1_g, w_in, q_a_norm_g, kv_a_norm_g, w_uq, w_ukv, q_norm_g, k_norm_g,
           conv_w, w_o_mla, w_o_conv, w_o, norm2_g, router_w, router_b,
           expert_w1, expert_b1, expert_w2, expert_b2, batch, seq):
    n, d = x2.shape
    o_kr = Q_LORA + KV_LORA
    o_u = o_kr + QK_ROPE
    o_g = o_u + 3 * CONV_WIDTH
    kr_cols = jnp.pad(w_in[:, o_kr:o_u], ((0, 0), (QK_NOPE, LANE - QK_HEAD)))
    w_a = jnp.concatenate([w_in[:, :o_kr], kr_cols, w_in[:, o_u:o_g]], axis=1).astype(BF16)
    w_g = w_in[:, o_g:].astype(BF16)
    row = lambda v: v.reshape(1, -1)
    cos_c, sin_c = _rope_tables(positions)
    q_scale = (QK_HEAD ** -0.5) * math.log2(math.e)
    gain_t = lambda g: jnp.broadcast_to(g.reshape(LANE, 1), (LANE, TM_PROJ // PROJ_SUBTILES))
    kg = _pad_cols(row(k_norm_g), LANE)
    score_bound = 1.02 * q_scale * QK_HEAD * jnp.max(jnp.abs(q_norm_g)) * jnp.max(jnp.abs(k_norm_g))
    bounded = 2.0 * score_bound <= SAFE_SCORE_RANGE
    offset = jnp.where(bounded, score_bound, 0.0)
    feature = jnp.arange(LANE) == OFFSET_FEATURE
    qoff = gain_t(jnp.where(feature, -offset, 0.0).astype(F32))
    koff = jnp.where(feature, 1.0, 0.0).astype(F32).reshape(1, LANE)

    qt, k, vt, cu, bg = _in_projection(
        x2, row(norm1_g), w_a, row(q_a_norm_g), row(kv_a_norm_g),
        _head_slots(w_uq, QK_HEAD).astype(BF16), w_ukv.astype(BF16),
        gain_t(_pad_cols(row(q_norm_g) * q_scale, LANE)), kg, gain_t(kg), qoff, koff,
        cos_c, sin_c, batch, seq)
    ot = _attention(qt, k, vt, bounded)

    rw = _pad_cols(router_w, LANE)
    rw_hi = rw.astype(BF16)
    rw_lo = (rw - rw_hi.astype(F32)).astype(BF16)
    rb = jnp.concatenate([row(router_b), jnp.full((1, LANE - N_EXPERTS), NEG_BIG, F32)], axis=1)
    x1, h2, mi, mw, cnt = _mix(
        x2, row(norm1_g), w_g, ot, cu, bg, conv_w, w_o_mla.astype(BF16), w_o_conv.astype(BF16),
        w_o.astype(BF16), row(norm2_g), jnp.concatenate([rw_hi, rw_lo], axis=1), rb, batch, seq)

    counts = cnt[0, :N_EXPERTS].astype(jnp.int32)
    padded = (counts + EXPERT_BLOCK - 1) // EXPERT_BLOCK * EXPERT_BLOCK
    experts = jnp.arange(N_EXPERTS, dtype=jnp.int32)
    pad_end = jnp.sum(jnp.where(experts[None, :] <= experts[:, None], padded[None, :], 0), axis=1)
    pad_start = (pad_end - padded).astype(jnp.int32)
    nk = n * TOP_K
    n_blocks = (nk + N_EXPERTS * (EXPERT_BLOCK - 1) + EXPERT_BLOCK - 1) // EXPERT_BLOCK
    rows = n_blocks * EXPERT_BLOCK
    block_first_row = jnp.arange(n_blocks, dtype=jnp.int32) * EXPERT_BLOCK
    block_exp = jnp.minimum(jnp.sum(pad_end[None, :] <= block_first_row[:, None], axis=1),
                            N_EXPERTS - 1).astype(jnp.int32)
    n_used = (pad_end[-1:] // EXPERT_BLOCK).astype(jnp.int32)
    group_end = jnp.sum(jnp.where(experts[None, :] == block_exp[:, None], pad_end[None, :], 0), axis=1)
    following = jnp.minimum(jnp.sum(pad_end[None, :] <= group_end[:, None], axis=1), N_EXPERTS - 1)
    next_exp = jnp.where(group_end < pad_end[-1], following, block_exp).astype(jnp.int32)
    ordinal = jnp.sum(jnp.where(experts[None, :] < experts[:, None], (padded > 0)[None, :], False), axis=1)
    weight_slot = jnp.sum(jnp.where(experts[None, :] == block_exp[:, None], (ordinal % 2)[None, :], 0),
                          axis=1).astype(jnp.int32)
    e_sel = mi[None, 0:TOP_K] == jnp.arange(N_EXPERTS, dtype=jnp.int32)[:, None, None]
    dest = jnp.sum(jnp.where(e_sel, pad_start[:, None, None], 0), axis=0) + mi[TOP_K:2 * TOP_K]
    tiled = lambda tile: dest.reshape(TOP_K, n // tile, tile).transpose(1, 0, 2).reshape(nk)

    xs = _dispatch(tiled(TM_DISPATCH), pad_end.astype(jnp.int32), (padded - counts).astype(jnp.int32), n_used,
                   h2, rows)
    rows_end = jnp.sum(jnp.where(experts[None, :] == block_exp[:, None], (pad_start + counts)[None, :], 0), axis=1)
    block_valid = jnp.clip(rows_end - block_first_row, 0, EXPERT_BLOCK).astype(jnp.int32)
    ys = _experts(block_exp, n_used, next_exp, weight_slot, block_valid, xs, expert_w1, expert_b1.reshape(N_EXPERTS, 1, -1),
                  expert_w2, expert_b2.reshape(N_EXPERTS, 1, -1))
    return _combine(tiled(TM_COMBINE), x1, mw, ys)


def kernel(x, positions, norm1_g, w_in, q_a_norm_g, kv_a_norm_g, w_uq, w_ukv, q_norm_g, k_norm_g, conv_w,
           w_o_mla, w_o_conv, w_o, norm2_g, router_w, router_b, expert_w1, expert_b1, expert_w2, expert_b2):
    batch, seq, d = x.shape
    depth = norm1_g.shape[0]
    x2 = x.reshape(batch * seq, d)
    for l in range(depth):
        x2 = _layer(x2, positions, norm1_g[l], w_in[l], q_a_norm_g[l], kv_a_norm_g[l], w_uq[l], w_ukv[l],
                    q_norm_g[l], k_norm_g[l], conv_w[l], w_o_mla[l], w_o_conv[l], w_o[l], norm2_g[l],
                    router_w[l], router_b[l], expert_w1[l], expert_b1[l], expert_w2[l], expert_b2[l], batch, seq)
    return x2.reshape(batch, seq, d)
```

```python
import functools
import math

import jax
import jax.numpy as jnp
from jax import lax
from jax.experimental import pallas as pl
from jax.experimental.pallas import tpu as pltpu

F32 = jnp.float32
BF16 = jnp.bfloat16

MLA_HEADS = 8
QK_NOPE = 64
QK_ROPE = 32
QK_HEAD = QK_NOPE + QK_ROPE
V_HEAD = 64
Q_LORA = 256
KV_LORA = 128
ROPE_THETA = 10000.0
CONV_WIDTH = 512
N_EXPERTS = 32
TOP_K = 4
SWIGLU_LIMIT = 7.0
SWIGLU_ALPHA = 1.702
EPS = 1e-6

LANE = 128
SUBLANE = 8
HEAD_SLOT = LANE
HALF_ROPE = QK_ROPE // 2
V_ROWS = V_HEAD + 16
OFFSET_FEATURE = QK_HEAD
SAFE_SCORE_RANGE = 100.0
VMEM_LIMIT = 56 * 1024 * 1024

TM_PROJ = 512
PROJ_SUBTILES = 2
TQ = 512
TKV = 512
ATTN_UNROLL = 4
TKV_BOUNDED = 256
BOUNDED_TILES_PER_TRIP = 4
TM_MIX = 512
MIX_SUBTILES = 1
TM_DISPATCH = 2048
TM_COMBINE = 256
EXPERT_BLOCK = 512
EXPERT_ROW_GROUP = 128
ISSUE_GROUP = 16
NEG_BIG = -1e30


def _load_token_tiles(ref, lead, rows):
    return jnp.concatenate([ref[lead + (pl.ds(c, rows, stride=SUBLANE), slice(None))] for c in range(SUBLANE)],
                           axis=1)


def _store_token_tiles(ref, value):
    rows = value.shape[0]
    for c in range(SUBLANE):
        ref[pl.ds(c, rows, stride=SUBLANE), :] = value[:, c * LANE:(c + 1) * LANE]


def _rms(x, g):
    return x * lax.rsqrt(jnp.mean(x * x, axis=-1, keepdims=True) + EPS) * g


def _inproj_kernel(x_ref, g1_ref, w_ref, gq_ref, gkv_ref, wuq_ref, wukv_ref, qgt_ref, kg_ref, kgt_ref,
                   qoff_ref, koff_ref, cost_ref, sint_ref, qt_ref, k_ref, vt_ref, cu_ref, bg_ref):
    tm = x_ref.shape[0] // PROJ_SUBTILES
    for part in range(PROJ_SUBTILES):
        _inproj_rows(slice(part * tm, (part + 1) * tm), x_ref, g1_ref, w_ref, gq_ref, gkv_ref, wuq_ref, wukv_ref,
                     qgt_ref, kg_ref, kgt_ref, qoff_ref, koff_ref, cost_ref, sint_ref,
                     qt_ref, k_ref, vt_ref, cu_ref, bg_ref)


def _inproj_rows(rows, x_ref, g1_ref, w_ref, gq_ref, gkv_ref, wuq_ref, wukv_ref, qgt_ref, kg_ref, kgt_ref,
                 qoff_ref, koff_ref, cost_ref, sint_ref, qt_ref, k_ref, vt_ref, cu_ref, bg_ref):
    x = x_ref[rows, :]
    h = _rms(x, g1_ref[...]).astype(BF16)
    proj = jnp.dot(h, w_ref[...], preferred_element_type=F32)
    c_q = proj[:, 0:Q_LORA]
    c_kv = proj[:, Q_LORA:Q_LORA + KV_LORA]
    kr = proj[:, Q_LORA + KV_LORA:Q_LORA + KV_LORA + LANE]
    o = Q_LORA + KV_LORA + LANE
    u = proj[:, o:o + CONV_WIDTH]
    c_gate = proj[:, o + CONV_WIDTH:o + 2 * CONV_WIDTH]
    b_gate = proj[:, o + 2 * CONV_WIDTH:o + 3 * CONV_WIDTH]
    cu_ref[rows, :] = c_gate * u
    bg_ref[rows, :] = b_gate

    kg = kg_ref[...]
    tm = x.shape[0]
    lane = lax.broadcasted_iota(jnp.int32, (tm, LANE), 1)

    q = jnp.dot(_rms(c_q, gq_ref[...]).astype(BF16), wuq_ref[...], preferred_element_type=F32)
    kv = jnp.dot(_rms(c_kv, gkv_ref[...]).astype(BF16), wukv_ref[...], preferred_element_type=F32)

    cos_c = cost_ref[:, rows]
    sin_c = sint_ref[:, rows]

    def rope_t(t):
        t1 = t[QK_NOPE:QK_NOPE + HALF_ROPE]
        t2 = t[QK_NOPE + HALF_ROPE:QK_HEAD]
        return jnp.concatenate([t[0:QK_NOPE], t1 * cos_c - t2 * sin_c, t1 * sin_c + t2 * cos_c, t[QK_HEAD:]], axis=0)

    ss_r = jnp.sum(kr * kr, axis=-1, keepdims=True)
    kr_roped = rope_t(kr.T * kgt_ref[...]).T
    qgt = qgt_ref[...]
    qoff = qoff_ref[...]
    koff = koff_ref[...]
    ones = jnp.ones((V_ROWS - V_HEAD, tm), BF16)
    for hd in range(MLA_HEADS):
        qht = q[:, hd * HEAD_SLOT:(hd + 1) * HEAD_SLOT].T
        r = lax.rsqrt(jnp.sum(qht * qht, axis=0, keepdims=True) * (1.0 / QK_HEAD) + EPS)
        qt_ref[0, hd, :, rows] = (rope_t(qht * r * qgt) + qoff).astype(BF16)

        kvh = kv[:, hd * HEAD_SLOT:(hd + 1) * HEAD_SLOT]
        knope = jnp.where(lane < QK_NOPE, kvh, 0.0)
        rk = lax.rsqrt((jnp.sum(knope * knope, axis=-1, keepdims=True) + ss_r) * (1.0 / QK_HEAD) + EPS)
        k_ref[0, hd, rows, :] = ((knope * kg + kr_roped) * rk + koff).astype(BF16)
        kvt = kvh.T
        vt_ref[0, hd, 0:V_HEAD, rows] = kvt[QK_NOPE:QK_NOPE + V_HEAD].astype(BF16)
        vt_ref[0, hd, V_HEAD:V_ROWS, rows] = ones


def _in_projection(x2, g1, w_a, gq, gkv, wuq, wukv, qgt, kg, kgt, qoff, koff, cos_c, sin_c, batch, seq):
    n, d = x2.shape
    tm = TM_PROJ
    spb = seq // tm
    full = lambda shp: pl.BlockSpec(shp, lambda i: (0,) * len(shp))
    return pl.pallas_call(
        _inproj_kernel,
        grid=(n // tm,),
        in_specs=[
            pl.BlockSpec((tm, d), lambda i: (i, 0)),
            full(g1.shape), full(w_a.shape), full(gq.shape), full(gkv.shape), full(wuq.shape), full(wukv.shape),
            full(qgt.shape), full(kg.shape), full(kgt.shape), full(qoff.shape), full(koff.shape),
            pl.BlockSpec((HALF_ROPE, tm), lambda i: (0, i)),
            pl.BlockSpec((HALF_ROPE, tm), lambda i: (0, i)),
        ],
        out_specs=[
            pl.BlockSpec((1, MLA_HEADS, HEAD_SLOT, tm), lambda i: (i // spb, 0, 0, i % spb)),
            pl.BlockSpec((1, MLA_HEADS, tm, HEAD_SLOT), lambda i: (i // spb, 0, i % spb, 0)),
            pl.BlockSpec((1, MLA_HEADS, V_ROWS, tm), lambda i: (i // spb, 0, 0, i % spb)),
            pl.BlockSpec((tm, CONV_WIDTH), lambda i: (i, 0)),
            pl.BlockSpec((tm, CONV_WIDTH), lambda i: (i, 0)),
        ],
        out_shape=[
            jax.ShapeDtypeStruct((batch, MLA_HEADS, HEAD_SLOT, seq), BF16),
            jax.ShapeDtypeStruct((batch, MLA_HEADS, seq, HEAD_SLOT), BF16),
            jax.ShapeDtypeStruct((batch, MLA_HEADS, V_ROWS, seq), BF16),
            jax.ShapeDtypeStruct((n, CONV_WIDTH), F32),
            jax.ShapeDtypeStruct((n, CONV_WIDTH), F32),
        ],
        compiler_params=pltpu.CompilerParams(dimension_semantics=("parallel",), vmem_limit_bytes=VMEM_LIMIT),
        name="in_projection",
    )(x2, g1, w_a, gq, gkv, wuq, wukv, qgt, kg, kgt, qoff, koff, cos_c, sin_c)


def _attn_kernel(qt_ref, k_ref, vt_ref, o_ref, sa_ref, sb_ref, m_ref, acc_ref, *, tq, tk):
    seq = k_ref.shape[2]
    nk = seq // tk
    bufs = (sa_ref, sb_ref)

    def query_tile(qi, carry):
        q0 = pl.multiple_of(qi * tq, tq)
        qt = qt_ref[0, 0, :, pl.ds(q0, tq)]

        def scores(c, s_ref):
            k0 = pl.multiple_of(c * tk, tk)
            s_ref[...] = jnp.dot(k_ref[0, 0, pl.ds(k0, tk), :], qt, preferred_element_type=F32)

        def accumulate(c, s_ref):
            k0 = pl.multiple_of(c * tk, tk)
            s = s_ref[...]
            m = m_ref[...]
            m_new = jnp.maximum(m, jnp.max(s, axis=0, keepdims=True))
            m_ref[...] = m_new
            p = jnp.exp2(s - m_new).astype(BF16)
            vs = vt_ref[0, 0, :, pl.ds(k0, tk)]
            acc_ref[...] = jnp.exp2(m - m_new) * acc_ref[...] + jnp.dot(vs, p, preferred_element_type=F32)

        m_ref[...] = jnp.full(m_ref.shape, NEG_BIG, F32)
        acc_ref[...] = jnp.zeros(acc_ref.shape, F32)
        scores(0, sa_ref)

        def group(j, c):
            base = ATTN_UNROLL * j
            for u in range(ATTN_UNROLL):
                scores(base + u + 1, bufs[(u + 1) % 2])
                accumulate(base + u, bufs[u % 2])
            return c

        lax.fori_loop(0, nk // ATTN_UNROLL - 1, group, 0)
        base = nk - ATTN_UNROLL
        for u in range(ATTN_UNROLL):
            if u + 1 < ATTN_UNROLL:
                scores(base + u + 1, bufs[(u + 1) % 2])
            accumulate(base + u, bufs[u % 2])
        acc = acc_ref[...]
        o_ref[0, :, pl.ds(q0, tq)] = (acc[0:V_HEAD] * (1.0 / acc[V_HEAD:V_HEAD + 1])).astype(BF16)
        return carry

    lax.fori_loop(0, seq // tq, query_tile, 0)


def _attn_bounded_kernel(qt_ref, k_ref, vt_ref, o_ref, sa_ref, sb_ref, *, tq, tk):
    seq = k_ref.shape[2]
    nk = seq // tk
    n_tiles = seq // tq
    ahead_refs = (sa_ref, sb_ref)
    ahead = len(ahead_refs)

    def load_qt(qi):
        return qt_ref[0, 0, :, pl.ds(pl.multiple_of(qi * tq, tq), tq)]

    def scores(c, qt):
        return jnp.dot(k_ref[0, 0, c * tk:(c + 1) * tk, :], qt, preferred_element_type=F32)

    qt_first = load_qt(0)
    for a in range(ahead):
        ahead_refs[a][...] = scores(a, qt_first)

    def query_tile(qi, carry):
        qt = load_qt(qi)
        qt_next = load_qt(jnp.minimum(qi + 1, n_tiles - 1))
        pending = [ref[...] for ref in ahead_refs]
        total = None
        for c in range(nk):
            if c + ahead < nk:
                pending.append(scores(c + ahead, qt))
            else:
                ahead_refs[c + ahead - nk][...] = scores(c + ahead - nk, qt_next)
            p = jnp.exp2(pending.pop(0)).astype(BF16)
            part = jnp.dot(vt_ref[0, 0, :, c * tk:(c + 1) * tk], p, preferred_element_type=F32)
            total = part if total is None else total + part
        o_ref[0, :, pl.ds(pl.multiple_of(qi * tq, tq), tq)] = (
            total[0:V_HEAD] * (1.0 / total[V_HEAD:V_HEAD + 1])).astype(BF16)
        return carry

    def tile_group(j, carry):
        for t in range(BOUNDED_TILES_PER_TRIP):
            query_tile(BOUNDED_TILES_PER_TRIP * j + t, carry)
        return carry

    lax.fori_loop(0, n_tiles // BOUNDED_TILES_PER_TRIP, tile_group, 0)


def _attention(qt, k, vt, bounded):
    batch, heads, _, seq = qt.shape
    assert ATTN_UNROLL % 2 == 0 and (seq // TKV) % ATTN_UNROLL == 0 and seq % TQ == 0
    in_specs = [
        pl.BlockSpec((1, 1, HEAD_SLOT, seq), lambda b, h, flag: (b, h, 0, 0)),
        pl.BlockSpec((1, 1, seq, HEAD_SLOT), lambda b, h, flag: (b, h, 0, 0)),
        pl.BlockSpec((1, 1, V_ROWS, seq), lambda b, h, flag: (b, h, 0, 0)),
    ]
    out_spec = pl.BlockSpec((1, V_HEAD, seq), lambda b, h, flag: (b, h, 0))
    out_shape = jax.ShapeDtypeStruct((batch, heads * V_HEAD, seq), BF16)
    grid_spec = pltpu.PrefetchScalarGridSpec(
        num_scalar_prefetch=1, grid=(batch, heads), in_specs=in_specs, out_specs=out_spec,
        scratch_shapes=[pltpu.VMEM((max(TKV, TKV_BOUNDED), TQ), F32), pltpu.VMEM((max(TKV, TKV_BOUNDED), TQ), F32),
                        pltpu.VMEM((1, TQ), F32), pltpu.VMEM((V_ROWS, TQ), F32)])
    return pl.pallas_call(
        _attn_select_kernel, grid_spec=grid_spec, out_shape=out_shape,
        compiler_params=pltpu.CompilerParams(dimension_semantics=("parallel", "parallel"),
                                             vmem_limit_bytes=VMEM_LIMIT),
        name="attention",
    )(bounded.astype(jnp.int32).reshape(1), qt, k, vt)


def _attn_select_kernel(bounded_ref, qt_ref, k_ref, vt_ref, o_ref, sa_ref, sb_ref, m_ref, acc_ref):
    @pl.when(bounded_ref[0] != 0)
    def _():
        _attn_bounded_kernel(qt_ref, k_ref, vt_ref, o_ref, sa_ref.at[0:TKV_BOUNDED], sb_ref.at[0:TKV_BOUNDED],
                             tq=TQ, tk=TKV_BOUNDED)

    @pl.when(bounded_ref[0] == 0)
    def _():
        _attn_kernel(qt_ref, k_ref, vt_ref, o_ref, sa_ref.at[0:TKV], sb_ref.at[0:TKV], m_ref, acc_ref,
                     tq=TQ, tk=TKV)


def _mix_kernel(x_ref, g1_ref, wg_ref, ot_ref, cu_ref, cup_ref, cun_ref, bg_ref, cw_ref,
                woa_ref, wob_ref, wo_ref, g2_ref, rw_ref, rb_ref,
                x1_ref, h2_ref, mi_ref, mw_ref, cnt_ref, tri_ref, carry_ref, *, steps_per_seq):
    i = pl.program_id(0)
    tm = x_ref.shape[0]
    ts = tm // MIX_SUBTILES

    @pl.when(i == 0)
    def _():
        r = lax.broadcasted_iota(jnp.int32, (ts, ts), 0)
        c = lax.broadcasted_iota(jnp.int32, (ts, ts), 1)
        tri_ref[...] = jnp.where(c < r, 1.0, 0.0).astype(BF16)
        carry_ref[...] = jnp.zeros_like(carry_ref)

    cu = cu_ref[...]
    row = lax.broadcasted_iota(jnp.int32, cu.shape, 0)
    s_in_seq = i % steps_per_seq
    prev_row = jnp.where(s_in_seq == 0, 0.0, cup_ref[7:8, :])
    next_row = jnp.where(s_in_seq == steps_per_seq - 1, 0.0, cun_ref[0:1, :])
    below = jnp.where(row == 0, prev_row, pltpu.roll(cu, 1, 0))
    above = jnp.where(row == tm - 1, next_row, pltpu.roll(cu, tm - 1, 0))
    cw = cw_ref[...]
    gated_conv = (bg_ref[...] * (cw[0:1, :] * below + cw[1:2, :] * cu + cw[2:3, :] * above)).astype(BF16)

    carry = carry_ref[...]
    for part in range(MIX_SUBTILES):
        carry = _mix_rows(part * ts, ts, gated_conv[part * ts:(part + 1) * ts], carry, x_ref, g1_ref, wg_ref,
                          ot_ref, woa_ref, wob_ref, wo_ref, g2_ref, rw_ref, rb_ref, tri_ref,
                          x1_ref, h2_ref, mi_ref, mw_ref)
    carry_ref[...] = carry
    cnt_ref[...] = jnp.broadcast_to(carry, cnt_ref.shape)


def _mix_rows(r0, ts, gated_conv, carry, x_ref, g1_ref, wg_ref, ot_ref, woa_ref, wob_ref, wo_ref, g2_ref,
              rw_ref, rb_ref, tri_ref, x1_ref, h2_ref, mi_ref, mw_ref):
    rows = slice(r0, r0 + ts)
    y_a = lax.dot_general(ot_ref[0, :, rows], woa_ref[...], (((0,), (0,)), ((), ())), preferred_element_type=F32)
    y_b = jnp.dot(gated_conv, wob_ref[...], preferred_element_type=F32)

    x = x_ref[rows, :]
    h = _rms(x, g1_ref[...]).astype(BF16)
    gates = jnp.dot(h, wg_ref[...], preferred_element_type=F32)
    d = x.shape[1]
    sig_a = 1.0 / (1.0 + jnp.exp(-gates[:, 0:d]))
    sig_b = 1.0 / (1.0 + jnp.exp(-gates[:, d:2 * d]))

    merged = (sig_a * y_a + sig_b * y_b).astype(BF16)
    x1 = x + jnp.dot(merged, wo_ref[...], preferred_element_type=F32)
    x1_ref[rows, :] = x1
    h2 = _rms(x1, g2_ref[...])
    _store_token_tiles(h2_ref.at[r0 * SUBLANE:(r0 + ts) * SUBLANE], h2)

    h2_hi = h2.astype(BF16)
    h2_lo = (h2 - h2_hi.astype(F32)).astype(BF16)
    rw = rw_ref[...]
    hi_terms = jnp.dot(h2_hi, rw, preferred_element_type=F32)
    logits = (hi_terms[:, 0:LANE] + hi_terms[:, LANE:2 * LANE]
              + jnp.dot(h2_lo, rw[:, 0:LANE], preferred_element_type=F32)
              + rb_ref[...])

    lane = lax.broadcasted_iota(jnp.int32, logits.shape, 1)
    lane_f = lane.astype(F32)
    work = logits
    vals, idxs, hits = [], [], []
    for _ in range(TOP_K):
        mx = jnp.max(work, axis=-1, keepdims=True)
        idx = jnp.min(jnp.where(work == mx, lane_f, float(LANE)), axis=-1, keepdims=True)
        hit = lane_f == idx
        work = jnp.where(hit, -jnp.inf, work)
        vals.append(mx)
        idxs.append(idx)
        hits.append(hit)
    exps = [jnp.exp(v - vals[0]) for v in vals]
    denom = exps[0] + exps[1] + exps[2] + exps[3]
    inv = 1.0 / denom

    sel = jnp.zeros(logits.shape, F32)
    for hit in hits:
        sel = sel + jnp.where(hit, 1.0, 0.0)
    before = jnp.dot(tri_ref[...], sel.astype(BF16), preferred_element_type=F32) + carry

    mi = jnp.zeros(logits.shape, F32)
    mw = jnp.zeros(logits.shape, F32)
    for k in range(TOP_K):
        rank = jnp.sum(jnp.where(hits[k], before, 0.0), axis=-1, keepdims=True)
        mi = jnp.where(lane == k, idxs[k], mi)
        mi = jnp.where(lane == TOP_K + k, rank, mi)
        mw = jnp.where(lane == k, exps[k] * inv, mw)
    mi_ref[:, rows] = mi.T[0:2 * TOP_K].astype(jnp.int32)
    mw_ref[rows, :] = mw
    return carry + jnp.sum(sel, axis=0, keepdims=True)


def _mix(x2, g1, w_g, ot, cu, bg, conv_w, woa, wob, wo, g2, rw_split, rb, batch, seq):
    n, d = x2.shape
    tm = TM_MIX
    spb = seq // tm
    r8 = tm // 8
    nsteps = n // tm
    full = lambda shp: pl.BlockSpec(shp, lambda i: (0,) * len(shp))
    return pl.pallas_call(
        functools.partial(_mix_kernel, steps_per_seq=spb),
        grid=(nsteps,),
        in_specs=[
            pl.BlockSpec((tm, d), lambda i: (i, 0)),
            full(g1.shape), full(w_g.shape),
            pl.BlockSpec((1, MLA_HEADS * V_HEAD, tm), lambda i: (i // spb, 0, i % spb)),
            pl.BlockSpec((tm, CONV_WIDTH), lambda i: (i, 0)),
            pl.BlockSpec((8, CONV_WIDTH), lambda i: (jnp.maximum(i * r8 - 1, 0), 0)),
            pl.BlockSpec((8, CONV_WIDTH), lambda i: (jnp.minimum((i + 1) * r8, nsteps * r8 - 1), 0)),
            pl.BlockSpec((tm, CONV_WIDTH), lambda i: (i, 0)),
            full(conv_w.shape), full(woa.shape), full(wob.shape), full(wo.shape), full(g2.shape),
            full(rw_split.shape), full(rb.shape),
        ],
        out_specs=[
            pl.BlockSpec((tm, d), lambda i: (i, 0)),
            pl.BlockSpec((tm * SUBLANE, LANE), lambda i: (i, 0)),
            pl.BlockSpec((2 * TOP_K, tm), lambda i: (0, i)),
            pl.BlockSpec((tm, LANE), lambda i: (i, 0)),
            pl.BlockSpec((8, LANE), lambda i: (0, 0)),
        ],
        out_shape=[
            jax.ShapeDtypeStruct((n, d), F32),
            jax.ShapeDtypeStruct((n * SUBLANE, LANE), F32),
            jax.ShapeDtypeStruct((2 * TOP_K, n), jnp.int32),
            jax.ShapeDtypeStruct((n, LANE), F32),
            jax.ShapeDtypeStruct((8, LANE), F32),
        ],
        scratch_shapes=[pltpu.VMEM((tm // MIX_SUBTILES, tm // MIX_SUBTILES), BF16), pltpu.VMEM((1, LANE), F32)],
        compiler_params=pltpu.CompilerParams(dimension_semantics=("arbitrary",), vmem_limit_bytes=VMEM_LIMIT),
        name="mix_route",
    )(x2, g1, w_g, ot, cu, cu, cu, bg, conv_w, woa, wob, wo, g2, rw_split, rb)


def _row_copy_wait(src_like, dst_like, sem, times):
    for _ in range(times):
        pltpu.make_async_copy(src_like, dst_like, sem).wait()


def _token(ref, idx):
    return ref.at[pl.ds(pl.multiple_of(idx * SUBLANE, SUBLANE), SUBLANE)]


def _dispatch_kernel(dest_ref, pe_ref, plen_ref, nused_ref, h2_ref, xs_ref, zero_ref, sem, zsem):
    tm = h2_ref.shape[0] // SUBLANE
    block_rows = EXPERT_BLOCK * SUBLANE
    n_blocks = xs_ref.shape[0] // block_rows
    step = pl.program_id(0)

    def zero_copies(act):
        def tail(b, c):
            start = pl.multiple_of(b * block_rows, block_rows)
            act(pltpu.make_async_copy(zero_ref, xs_ref.at[pl.ds(start, block_rows)], zsem))
            return c

        def expert(e, c):
            length = plen_ref[e]
            first = pe_ref[e] - length
            for bit in reversed(range(EXPERT_BLOCK.bit_length() - 1)):
                size = 1 << bit
                done = lax.bitwise_and(length, -2 * size)

                @pl.when(lax.bitwise_and(length, size) != 0)
                def _(size=size, done=done):
                    start = pl.multiple_of((first + done) * SUBLANE, SUBLANE)
                    act(pltpu.make_async_copy(zero_ref.at[0:size * SUBLANE],
                                              xs_ref.at[pl.ds(start, size * SUBLANE)], zsem))
            return c

        lax.fori_loop(0, N_EXPERTS, expert, 0)
        lax.fori_loop(nused_ref[0], n_blocks, tail, 0)

    @pl.when(step == 0)
    def _():
        zero_ref[...] = jnp.zeros_like(zero_ref)
        zero_copies(lambda cp: cp.start())

    def issue(g, c):
        for u in range(ISSUE_GROUP):
            t = g * (ISSUE_GROUP // TOP_K) + u // TOP_K
            dest = dest_ref[(u % TOP_K) * tm + t]
            pltpu.make_async_copy(_token(h2_ref, t), _token(xs_ref, dest), sem).start(priority=u % 2)
        return c

    lax.fori_loop(0, tm * TOP_K // ISSUE_GROUP, issue, 0)
    _row_copy_wait(h2_ref, xs_ref.at[pl.ds(0, tm * SUBLANE)], sem, TOP_K)

    @pl.when(step == pl.num_programs(0) - 1)
    def _():
        zero_copies(lambda cp: cp.wait())


def _dispatch(dest_flat, pad_end, pad_len, n_used, h2t, rows):
    tm = TM_DISPATCH
    n = h2t.shape[0] // SUBLANE
    return pl.pallas_call(
        _dispatch_kernel,
        grid=(n // tm,),
        in_specs=[
            pl.BlockSpec((tm * TOP_K,), lambda i: (i,), memory_space=pltpu.SMEM),
            pl.BlockSpec(memory_space=pltpu.SMEM),
            pl.BlockSpec(memory_space=pltpu.SMEM),
            pl.BlockSpec(memory_space=pltpu.SMEM),
            pl.BlockSpec((tm * SUBLANE, LANE), lambda i: (i, 0)),
        ],
        out_specs=pl.BlockSpec(memory_space=pl.ANY),
        out_shape=jax.ShapeDtypeStruct((rows * SUBLANE, LANE), F32),
        scratch_shapes=[pltpu.VMEM((EXPERT_BLOCK * SUBLANE, LANE), F32), pltpu.SemaphoreType.DMA(()),
                        pltpu.SemaphoreType.DMA(())],
        compiler_params=pltpu.CompilerParams(dimension_semantics=("arbitrary",), vmem_limit_bytes=VMEM_LIMIT),
        name="dispatch",
    )(dest_flat, pad_end, pad_len, n_used, h2t)


def _expert_kernel(bexp_ref, nused_ref, nexp_ref, slot_ref, nvalid_ref, xs_ref, w1_hbm, b1_ref, w2_hbm, b2_ref, ys_ref,
                   w1_ref, w2_ref, sem):
    i = pl.program_id(0)
    active = i < nused_ref[0]
    expert = bexp_ref[i]
    slot = slot_ref[i]
    prev = bexp_ref[jnp.maximum(i - 1, 0)]
    fresh = jnp.logical_or(i == 0, expert != prev)

    def weight_copies(e, s):
        return (pltpu.make_async_copy(w1_hbm.at[e], w1_ref.at[s], sem.at[0, s]),
                pltpu.make_async_copy(w2_hbm.at[e], w2_ref.at[s], sem.at[1, s]))

    @pl.when(jnp.logical_and(active, i == 0))
    def _():
        for cp in weight_copies(expert, slot):
            cp.start()

    @pl.when(jnp.logical_and(active, fresh))
    def _():
        for cp in weight_copies(expert, slot):
            cp.wait()

        @pl.when(nexp_ref[i] != expert)
        def _():
            for cp in weight_copies(nexp_ref[i], 1 - slot):
                cp.start()

    def ffn(rows):
        dff = w2_ref.shape[1]
        xb = _load_token_tiles(xs_ref, (), rows).astype(BF16)
        hm = jnp.dot(xb, w1_ref[slot].astype(BF16), preferred_element_type=F32) + b1_ref[0]
        gate = jnp.minimum(hm[:, 0:dff], SWIGLU_LIMIT)
        up = jnp.clip(hm[:, dff:2 * dff], -SWIGLU_LIMIT, SWIGLU_LIMIT)
        glu = gate * (1.0 / (1.0 + jnp.exp(-SWIGLU_ALPHA * gate)))
        act = ((up + 1.0) * glu).astype(BF16)
        _store_token_tiles(ys_ref.at[0:rows * SUBLANE],
                           jnp.dot(act, w2_ref[slot].astype(BF16), preferred_element_type=F32) + b2_ref[0])

    groups = lax.div(nvalid_ref[i] + (EXPERT_ROW_GROUP - 1), EXPERT_ROW_GROUP)
    for g in range(1, EXPERT_BLOCK // EXPERT_ROW_GROUP + 1):
        rows = g * EXPERT_ROW_GROUP

        @pl.when(jnp.logical_and(active, groups == g))
        def _(rows=rows):
            ffn(rows)
            if rows < EXPERT_BLOCK:
                ys_ref[rows * SUBLANE:, :] = jnp.zeros(((EXPERT_BLOCK - rows) * SUBLANE, LANE), F32)

    @pl.when(jnp.logical_not(active))
    def _():
        ys_ref[...] = jnp.zeros_like(ys_ref)


def _experts(block_exp, n_used, next_exp, weight_slot, block_valid, xs, w1, b1, w2, b2):
    d = w1.shape[1]
    assert d == SUBLANE * LANE
    block_rows = EXPERT_BLOCK * SUBLANE
    n_blocks = xs.shape[0] // block_rows
    dff2 = w1.shape[2]
    dff = w2.shape[1]
    grid_spec = pltpu.PrefetchScalarGridSpec(
        num_scalar_prefetch=5,
        grid=(n_blocks,),
        in_specs=[
            pl.BlockSpec((block_rows, LANE), lambda i, be, nu, ne, ws, nv: (jnp.minimum(i, nu[0] - 1), 0)),
            pl.BlockSpec(memory_space=pl.ANY),
            pl.BlockSpec((1, 1, dff2), lambda i, be, nu, ne, ws, nv: (be[i], 0, 0)),
            pl.BlockSpec(memory_space=pl.ANY),
            pl.BlockSpec((1, 1, d), lambda i, be, nu, ne, ws, nv: (be[i], 0, 0)),
        ],
        out_specs=pl.BlockSpec((block_rows, LANE), lambda i, be, nu, ne, ws, nv: (i, 0)),
        scratch_shapes=[pltpu.VMEM((2, d, dff2), F32), pltpu.VMEM((2, dff, d), F32),
                        pltpu.SemaphoreType.DMA((2, 2))],
    )
    return pl.pallas_call(
        _expert_kernel,
        grid_spec=grid_spec,
        out_shape=jax.ShapeDtypeStruct(xs.shape, F32),
        compiler_params=pltpu.CompilerParams(dimension_semantics=("arbitrary",), vmem_limit_bytes=VMEM_LIMIT),
        name="experts",
    )(block_exp, n_used, next_exp, weight_slot, block_valid, xs, w1, b1, w2, b2)


def _combine_kernel(dest_ref, dest_next_ref, x1_ref, mw_ref, ys_ref, out_ref, buf_ref, sem):
    i = pl.program_id(0)
    tm = x1_ref.shape[0]
    slot = lax.rem(i, 2)

    def gather(idx_ref, s):
        def issue(g, c):
            for u in range(ISSUE_GROUP):
                t = g * (ISSUE_GROUP // TOP_K) + u // TOP_K
                dest = idx_ref[(u % TOP_K) * tm + t]
                pltpu.make_async_copy(_token(ys_ref, dest), _token(buf_ref.at[s, u % TOP_K], t),
                                      sem.at[s]).start(priority=u % 2)
            return c

        lax.fori_loop(0, tm * TOP_K // ISSUE_GROUP, issue, 0)

    @pl.when(i == 0)
    def _():
        gather(dest_ref, 0)

    @pl.when(i + 1 < pl.num_programs(0))
    def _():
        gather(dest_next_ref, 1 - slot)

    _row_copy_wait(ys_ref.at[pl.ds(0, tm * SUBLANE)], buf_ref.at[slot, 0], sem.at[slot], TOP_K)
    acc = x1_ref[...]
    mw = mw_ref[...]
    for k in range(TOP_K):
        acc = acc + mw[:, k:k + 1] * _load_token_tiles(buf_ref, (slot, k), tm)
    out_ref[...] = acc


def _combine(dest_flat, x1, mw, ys):
    n, d = x1.shape
    tm = TM_COMBINE
    nsteps = n // tm
    return pl.pallas_call(
        _combine_kernel,
        grid=(nsteps,),
        in_specs=[
            pl.BlockSpec((tm * TOP_K,), lambda i: (i,), memory_space=pltpu.SMEM),
            pl.BlockSpec((tm * TOP_K,), lambda i: (jnp.minimum(i + 1, nsteps - 1),), memory_space=pltpu.SMEM),
            pl.BlockSpec((tm, d), lambda i: (i, 0)),
            pl.BlockSpec((tm, LANE), lambda i: (i, 0)),
            pl.BlockSpec(memory_space=pl.ANY),
        ],
        out_specs=pl.BlockSpec((tm, d), lambda i: (i, 0)),
        out_shape=jax.ShapeDtypeStruct((n, d), F32),
        scratch_shapes=[pltpu.VMEM((2, TOP_K, tm * SUBLANE, LANE), F32), pltpu.SemaphoreType.DMA((2,))],
        compiler_params=pltpu.CompilerParams(dimension_semantics=("arbitrary",), vmem_limit_bytes=VMEM_LIMIT),
        name="combine",
    )(dest_flat, dest_flat, x1, mw, ys)


def _pad_cols(w, width):
    return jnp.pad(w, ((0, 0), (0, width - w.shape[1])))


def _head_slots(w, per_head):
    rows = w.shape[0]
    w3 = w.reshape(rows, MLA_HEADS, per_head)
    return jnp.pad(w3, ((0, 0), (0, 0), (0, HEAD_SLOT - per_head))).reshape(rows, MLA_HEADS * HEAD_SLOT)


def _rope_tables(positions):
    inv_freq = ROPE_THETA ** (-jnp.arange(0, QK_ROPE, 2, dtype=F32) / QK_ROPE)
    ang = positions.astype(F32).reshape(-1, 1) * inv_freq
    return jnp.cos(ang).T, jnp.sin(ang).T


def _layer(x2, positions, norm1_g, w_in, q_a_norm_g, kv_a_norm_g, w_uq, w_ukv, q_norm_g, k_norm_g,
           conv_w, w_o_mla, w_o_conv, w_o, norm2_g, router_w, router_b,
           expert_w1, expert_b1, expert_w2, expert_b2, batch, seq):
    n, d = x2.shape
    o_kr = Q_LORA + KV_LORA
    o_u = o_kr + QK_ROPE
    o_g = o_u + 3 * CONV_WIDTH
    kr_cols = jnp.pad(w_in[:, o_kr:o_u], ((0, 0), (QK_NOPE, LANE - QK_HEAD)))
    w_a = jnp.concatenate([w_in[:, :o_kr], kr_cols, w_in[:, o_u:o_g]], axis=1).astype(BF16)
    w_g = w_in[:, o_g:].astype(BF16)
    row = lambda v: v.reshape(1, -1)
    cos_c, sin_c = _rope_tables(positions)
    q_scale = (QK_HEAD ** -0.5) * math.log2(math.e)
    gain_t = lambda g: jnp.broadcast_to(g.reshape(LANE, 1), (LANE, TM_PROJ // PROJ_SUBTILES))
    kg = _pad_cols(row(k_norm_g), LANE)
    score_bound = 1.02 * q_scale * QK_HEAD * jnp.max(jnp.abs(q_norm_g)) * jnp.max(jnp.abs(k_norm_g))
    bounded = 2.0 * score_bound <= SAFE_SCORE_RANGE
    offset = jnp.where(bounded, score_bound, 0.0)
    feature = jnp.arange(LANE) == OFFSET_FEATURE
    qoff = gain_t(jnp.where(feature, -offset, 0.0).astype(F32))
    koff = jnp.where(feature, 1.0, 0.0).astype(F32).reshape(1, LANE)

    qt, k, vt, cu, bg = _in_projection(
        x2, row(norm1_g), w_a, row(q_a_norm_g), row(kv_a_norm_g),
        _head_slots(w_uq, QK_HEAD).astype(BF16), w_ukv.astype(BF16),
        gain_t(_pad_cols(row(q_norm_g) * q_scale, LANE)), kg, gain_t(kg), qoff, koff,
        cos_c, sin_c, batch, seq)
    ot = _attention(qt, k, vt, bounded)

    rw = _pad_cols(router_w, LANE)
    rw_hi = rw.astype(BF16)
    rw_lo = (rw - rw_hi.astype(F32)).astype(BF16)
    rb = jnp.concatenate([row(router_b), jnp.full((1, LANE - N_EXPERTS), NEG_BIG, F32)], axis=1)
    x1, h2, mi, mw, cnt = _mix(
        x2, row(norm1_g), w_g, ot, cu, bg, conv_w, w_o_mla.astype(BF16), w_o_conv.astype(BF16),
        w_o.astype(BF16), row(norm2_g), jnp.concatenate([rw_hi, rw_lo], axis=1), rb, batch, seq)

    counts = cnt[0, :N_EXPERTS].astype(jnp.int32)
    padded = (counts + EXPERT_BLOCK - 1) // EXPERT_BLOCK * EXPERT_BLOCK
    experts = jnp.arange(N_EXPERTS, dtype=jnp.int32)
    pad_end = jnp.sum(jnp.where(experts[None, :] <= experts[:, None], padded[None, :], 0), axis=1)
    pad_start = (pad_end - padded).astype(jnp.int32)
    nk = n * TOP_K
    n_blocks = (nk + N_EXPERTS * (EXPERT_BLOCK - 1) + EXPERT_BLOCK - 1) // EXPERT_BLOCK
    rows = n_blocks * EXPERT_BLOCK
    block_first_row = jnp.arange(n_blocks, dtype=jnp.int32) * EXPERT_BLOCK
    block_exp = jnp.minimum(jnp.sum(pad_end[None, :] <= block_first_row[:, None], axis=1),
                            N_EXPERTS - 1).astype(jnp.int32)
    n_used = (pad_end[-1:] // EXPERT_BLOCK).astype(jnp.int32)
    group_end = jnp.sum(jnp.where(experts[None, :] == block_exp[:, None], pad_end[None, :], 0), axis=1)
    following = jnp.minimum(jnp.sum(pad_end[None, :] <= group_end[:, None], axis=1), N_EXPERTS - 1)
    next_exp = jnp.where(group_end < pad_end[-1], following, block_exp).astype(jnp.int32)
    ordinal = jnp.sum(jnp.where(experts[None, :] < experts[:, None], (padded > 0)[None, :], False), axis=1)
    weight_slot = jnp.sum(jnp.where(experts[None, :] == block_exp[:, None], (ordinal % 2)[None, :], 0),
                          axis=1).astype(jnp.int32)
    e_sel = mi[None, 0:TOP_K] == jnp.arange(N_EXPERTS, dtype=jnp.int32)[:, None, None]
    dest = jnp.sum(jnp.where(e_sel, pad_start[:, None, None], 0), axis=0) + mi[TOP_K:2 * TOP_K]
    tiled = lambda tile: dest.reshape(TOP_K, n // tile, tile).transpose(1, 0, 2).reshape(nk)

    xs = _dispatch(tiled(TM_DISPATCH), pad_end.astype(jnp.int32), (padded - counts).astype(jnp.int32), n_used,
                   h2, rows)
    rows_end = jnp.sum(jnp.where(experts[None, :] == block_exp[:, None], (pad_start + counts)[None, :], 0), axis=1)
    block_valid = jnp.clip(rows_end - block_first_row, 0, EXPERT_BLOCK).astype(jnp.int32)
    ys = _experts(block_exp, n_used, next_exp, weight_slot, block_valid, xs, expert_w1, expert_b1.reshape(N_EXPERTS, 1, -1),
                  expert_w2, expert_b2.reshape(N_EXPERTS, 1, -1))
    return _combine(tiled(TM_COMBINE), x1, mw, ys)


def kernel(x, positions, norm1_g, w_in, q_a_norm_g, kv_a_norm_g, w_uq, w_ukv, q_norm_g, k_norm_g, conv_w,
           w_o_mla, w_o_conv, w_o, norm2_g, router_w, router_b, expert_w1, expert_b1, expert_w2, expert_b2):
    batch, seq, d = x.shape
    depth = norm1_g.shape[0]
    x2 = x.reshape(batch * seq, d)
    for l in range(depth):
        x2 = _layer(x2, positions, norm1_g[l], w_in[l], q_a_norm_g[l], kv_a_norm_g[l], w_uq[l], w_ukv[l],
                    q_norm_g[l], k_norm_g[l], conv_w[l], w_o_mla[l], w_o_conv[l], w_o[l], norm2_g[l],
                    router_w[l], router_b[l], expert_w1[l], expert_b1[l], expert_w2[l], expert_b2[l], batch, seq)
    return x2.reshape(batch, seq, d)
```
